```python
import math
import jax
import jax.numpy as jnp
from jax import lax
import numpy as np

D_MODEL = 4096
BATCH = 1
SEQ = 8192
DEPTH = 2

HEAD_DIM = 128
BLOCK_Q = 128
NEG_INF = -1e30
RMS_EPS = 1e-6

SB_HEADS = 16
SB_WIDTH = SB_HEADS * HEAD_DIM

NSA_HEADS = 16
NSA_KV_HEADS = 4
NSA_CMP_LEN = 32
NSA_CMP_STRIDE = 16
NSA_CMP_HIDDEN = 256
NSA_SEL_LEN = 64
NSA_SEL_TOPK = 16
NSA_WINDOW = 512
NSA_FORCE_SCORE = 1e6
NSA_Q_WIDTH = NSA_HEADS * HEAD_DIM
NSA_KV_WIDTH = NSA_KV_HEADS * HEAD_DIM

SG_WIDTH = D_MODEL // 2
SG_GROUPS = 8
SG_CHUNK = 128

SWA_HEADS = 16
SWA_KV_HEADS = 2
SWA_WINDOW = 128
SWA_Q_WIDTH = SWA_HEADS * HEAD_DIM
SWA_KV_WIDTH = SWA_KV_HEADS * HEAD_DIM

REL_BUCKETS = 32
REL_MAX_DIST = 1024
REL_HEADS = 16

D_FF = 14336
N_EXPERTS = 8
TOP_K = 2
D_FF_EXPERT = 3584

AB_IN = 3 * SB_WIDTH + NSA_Q_WIDTH + 6 * NSA_KV_WIDTH + 3 * NSA_HEADS
AB_OUT = SB_WIDTH + NSA_Q_WIDTH
CD_IN = 2 * SG_WIDTH + SWA_Q_WIDTH + 2 * SWA_KV_WIDTH
CD_OUT = SG_WIDTH + SWA_Q_WIDTH

kernel_name = 'hybrid_sb_nsa_sgmlp_swa_moe'


def rmsnorm(x, g):
    xf = x.astype(jnp.float32)
    y = xf * lax.rsqrt(jnp.mean(xf * xf, axis=-1, keepdims=True) + RMS_EPS)
    return (y * g.astype(jnp.float32)).astype(x.dtype)


def layernorm(x, g, b):
    xf = x.astype(jnp.float32)
    mu = jnp.mean(xf, axis=-1, keepdims=True)
    var = jnp.mean(jnp.square(xf - mu), axis=-1, keepdims=True)
    y = (xf - mu) * lax.rsqrt(var + RMS_EPS) * g.astype(jnp.float32) + b.astype(jnp.float32)
    return y.astype(x.dtype)


def split_cols(z, sizes):
    return jnp.split(z, np.cumsum(sizes)[:-1].tolist(), axis=-1)


def rel_bucket(dist):
    max_exact = REL_BUCKETS // 2
    d = jnp.maximum(dist, 0)
    log_ratio = jnp.log(jnp.maximum(d, 1).astype(jnp.float32) / max_exact) / math.log(REL_MAX_DIST / max_exact)
    far = jnp.minimum(max_exact + (log_ratio * (REL_BUCKETS - max_exact)).astype(jnp.int32), REL_BUCKETS - 1)
    return jnp.where(d < max_exact, d, far)


def rel_bias(table, dist, n_kv):
    b = jnp.moveaxis(table.astype(jnp.float32)[rel_bucket(dist)], -1, -3)
    return b.reshape(*dist.shape[:-2], n_kv, -1, *dist.shape[-2:])


def stick_breaking_attention(q, k, v):
    B, T, H, dh = q.shape
    nb = T // BLOCK_Q
    kf, vf = k.astype(jnp.float32), v.astype(jnp.float32)
    qb = jnp.moveaxis(q.reshape(B, nb, BLOCK_Q, H, dh), 1, 0)
    k_pos = jnp.arange(T)

    def block(args):
        q_blk, start = args
        z = jnp.einsum('bqhd,bkhd->bhqk', q_blk.astype(jnp.float32), kf) * (dh ** -0.5)
        q_pos = start + jnp.arange(BLOCK_Q)
        past = k_pos[None, :] < q_pos[:, None]
        log_stay = jnp.where(past, -jax.nn.softplus(z), 0.0)
        later = lax.cumsum(log_stay, axis=3, reverse=True) - log_stay
        w = jnp.where(past, jnp.exp(jax.nn.log_sigmoid(z) + later), 0.0)
        return jnp.einsum('bhqk,bkhd->bqhd', w, vf)

    o = lax.map(block, (qb, jnp.arange(nb) * BLOCK_Q))
    return jnp.moveaxis(o, 0, 1).reshape(B, T, H, dh)


def banded_attention(q, k, v, window, rel_table, sinks=None):
    B, T, Hkv, G, dh = q.shape
    nb = T // BLOCK_Q
    n_prev = -(-window // BLOCK_Q)
    n_keys = (n_prev + 1) * BLOCK_Q
    qb = q.reshape(B, nb, BLOCK_Q, Hkv, G, dh).astype(jnp.float32)

    def band(a):
        ab = jnp.pad(a.reshape(B, nb, BLOCK_Q, Hkv, dh), ((0, 0), (n_prev, 0), (0, 0), (0, 0), (0, 0)))
        return jnp.concatenate([ab[:, i:i + nb] for i in range(n_prev + 1)], axis=2).astype(jnp.float32)

    kb, vb = band(k), band(v)
    q_pos = jnp.arange(T).reshape(nb, BLOCK_Q)
    k_pos = q_pos[:, :1] - n_prev * BLOCK_Q + jnp.arange(n_keys)[None, :]
    dist = q_pos[:, :, None] - k_pos[:, None, :]
    valid = (dist >= 0) & (dist < window) & (k_pos[:, None, :] >= 0)
    s = jnp.einsum('bnqhgd,bnkhd->bnhgqk', qb, kb) * (dh ** -0.5) + rel_bias(rel_table, dist, Hkv)[None]
    s = jnp.where(valid[None, :, None, None], s, NEG_INF)
    if sinks is None:
        p = jax.nn.softmax(s, axis=-1)
    else:
        sk = sinks.astype(jnp.float32).reshape(Hkv, G)[None, None, :, :, None, None]
        m = jnp.maximum(jnp.max(s, axis=-1, keepdims=True), sk)
        e = jnp.exp(s - m)
        p = e / (jnp.sum(e, axis=-1, keepdims=True) + jnp.exp(sk - m))
    o = jnp.einsum('bnhgqk,bnkhd->bnqhgd', p, vb)
    return o.reshape(B, T, Hkv, G, dh)


def compress_blocks(a, pe, w1, b1, w2):
    B, T, Hkv, dh = a.shape
    n_cmp = (T - NSA_CMP_LEN) // NSA_CMP_STRIDE + 1
    idx = jnp.arange(n_cmp)[:, None] * NSA_CMP_STRIDE + jnp.arange(NSA_CMP_LEN)[None, :]
    blk = a[:, idx] + pe[None, None, :, None, :]
    blk = jnp.moveaxis(blk, 3, 2).reshape(B, n_cmp, Hkv, NSA_CMP_LEN * dh)
    return jax.nn.gelu(blk @ w1 + b1) @ w2


def nsa_compressed_selected(q, kc, vc, ks, vs, rel_table):
    B, T, Hkv, G, dh = q.shape
    nb = T // BLOCK_Q
    n_cmp = kc.shape[1]
    n_sel = T // NSA_SEL_LEN
    top_k = min(NSA_SEL_TOPK, n_sel)
    scale = dh ** -0.5
    cmp_end = jnp.arange(n_cmp) * NSA_CMP_STRIDE + NSA_CMP_LEN - 1
    cmp_start = cmp_end - (NSA_CMP_LEN - 1)
    sel_start = jnp.arange(n_sel) * NSA_SEL_LEN
    overlap = jnp.minimum(cmp_end[:, None], sel_start[None, :] + NSA_SEL_LEN - 1) - jnp.maximum(cmp_start[:, None], sel_start[None, :]) + 1
    cmp_to_sel = jnp.clip(overlap, 0).astype(jnp.float32) / NSA_CMP_LEN
    kcf, vcf = kc.astype(jnp.float32), vc.astype(jnp.float32)
    ksb = jnp.moveaxis(ks.reshape(B, n_sel, NSA_SEL_LEN, Hkv, dh), 3, 1)
    vsb = jnp.moveaxis(vs.reshape(B, n_sel, NSA_SEL_LEN, Hkv, dh), 3, 1)
    qb = jnp.moveaxis(q.reshape(B, nb, BLOCK_Q, Hkv, G, dh), 1, 0)
    tab = rel_table.astype(jnp.float32).reshape(REL_BUCKETS, Hkv, G)
    b_idx = jnp.arange(B)[:, None, None, None]
    h_idx = jnp.arange(Hkv)[None, :, None, None]
    blk_id = jnp.arange(n_sel)

    def block(args):
        q_blk, start = args
        qf = q_blk.astype(jnp.float32)
        q_pos = start + jnp.arange(BLOCK_Q)
        dist_c = q_pos[:, None] - cmp_end[None, :]
        vis_c = dist_c >= 0
        s_c = jnp.einsum('bqhgd,bchd->bhgqc', qf, kcf) * scale + rel_bias(rel_table, dist_c, Hkv)
        p_c = jnp.where(vis_c, jax.nn.softmax(jnp.where(vis_c, s_c, NEG_INF), axis=-1), 0.0)
        o_c = jnp.einsum('bhgqc,bchd->bqhgd', p_c, vcf)
        imp = jnp.einsum('bhgqc,cn->bhqn', p_c, cmp_to_sel)
        cur = q_pos // NSA_SEL_LEN
        forced = (blk_id[None, :] == 0) | (blk_id[None, :] == cur[:, None]) | (blk_id[None, :] == cur[:, None] - 1)
        causal_blk = sel_start[None, :] <= q_pos[:, None]
        imp = jnp.where(forced, NSA_FORCE_SCORE, jnp.where(causal_blk, imp, -1.0))
        _, sel = lax.top_k(imp, top_k)
        k_sel = ksb[b_idx, h_idx, sel].astype(jnp.float32)
        v_sel = vsb[b_idx, h_idx, sel].astype(jnp.float32)
        key_pos = sel[..., None] * NSA_SEL_LEN + jnp.arange(NSA_SEL_LEN)
        dist_s = q_pos[None, None, :, None, None] - key_pos
        bias_s = jnp.moveaxis(tab[rel_bucket(dist_s), jnp.arange(Hkv)[None, :, None, None, None]], -1, 2)
        s_s = jnp.einsum('bqhgd,bhqnld->bhgqnl', qf, k_sel) * scale + bias_s
        s_s = jnp.where((dist_s >= 0)[:, :, None], s_s, NEG_INF)
        p_s = jax.nn.softmax(s_s, axis=(-2, -1))
        o_s = jnp.einsum('bhgqnl,bhqnld->bqhgd', p_s, v_sel)
        return o_c, o_s

    o_c, o_s = lax.map(block, (qb, jnp.arange(nb) * BLOCK_Q))
    unblock = lambda o: jnp.moveaxis(o, 0, 1).reshape(B, T, Hkv, G, dh)
    return unblock(o_c), unblock(o_s)


def mixer_ab(h, w_in, w_out, cmp_k, cmp_v, rel_table):
    B, T, _ = h.shape
    sizes = [SB_WIDTH] * 3 + [NSA_Q_WIDTH] + [NSA_KV_WIDTH] * 6 + [3 * NSA_HEADS]
    q_sb, k_sb, v_sb, q_n, k_c, v_c, k_s, v_s, k_w, v_w, g = split_cols(h @ w_in, sizes)
    sb = lambda a: a.reshape(B, T, SB_HEADS, HEAD_DIM)
    o_sb = stick_breaking_attention(sb(q_sb), sb(k_sb), sb(v_sb)).reshape(B, T, SB_WIDTH)
    G = NSA_HEADS // NSA_KV_HEADS
    q_n = q_n.reshape(B, T, NSA_KV_HEADS, G, HEAD_DIM)
    kv = lambda a: a.reshape(B, T, NSA_KV_HEADS, HEAD_DIM)
    kc = compress_blocks(kv(k_c), *cmp_k)
    vc = compress_blocks(kv(v_c), *cmp_v)
    o_cmp, o_sel = nsa_compressed_selected(q_n, kc, vc, kv(k_s), kv(v_s), rel_table)
    o_win = banded_attention(q_n, kv(k_w), kv(v_w), NSA_WINDOW, rel_table)
    gate = jax.nn.sigmoid(g.astype(jnp.float32)).reshape(B, T, NSA_KV_HEADS, G, 3, 1)
    o_nsa = gate[..., 0, :] * o_cmp + gate[..., 1, :] * o_sel + gate[..., 2, :] * o_win
    mixed = jnp.concatenate([o_sb, o_nsa.reshape(B, T, NSA_Q_WIDTH)], axis=-1).astype(h.dtype)
    return mixed @ w_out


def mixer_cd(h, w_in, w_out, ln_g, ln_b, sg_w, sg_b, sinks, rel_table):
    B, T, _ = h.shape
    uv, q, k, v = split_cols(h @ w_in, [2 * SG_WIDTH, SWA_Q_WIDTH, SWA_KV_WIDTH, SWA_KV_WIDTH])
    u, gv = jnp.split(jax.nn.gelu(uv), 2, axis=-1)
    gv = layernorm(gv, ln_g, ln_b).reshape(B, T // SG_CHUNK, SG_CHUNK, SG_GROUPS, SG_WIDTH // SG_GROUPS)
    w_s = jnp.tril(sg_w)
    spatial = jnp.einsum('gts,bnsge->bntge', w_s, gv) + jnp.transpose(sg_b)[None, None, :, :, None]
    o_sg = u * spatial.reshape(B, T, SG_WIDTH)
    G = SWA_HEADS // SWA_KV_HEADS
    o_swa = banded_attention(q.reshape(B, T, SWA_KV_HEADS, G, HEAD_DIM),
                             k.reshape(B, T, SWA_KV_HEADS, HEAD_DIM),
                             v.reshape(B, T, SWA_KV_HEADS, HEAD_DIM),
                             SWA_WINDOW, rel_table, sinks)
    mixed = jnp.concatenate([o_sg.astype(h.dtype), o_swa.reshape(B, T, SWA_Q_WIDTH).astype(h.dtype)], axis=-1)
    return mixed @ w_out


def swiglu(h, w_gate, w_up, w_down):
    return (jax.nn.silu(h @ w_gate) * (h @ w_up)) @ w_down


def moe_swiglu(h, w_router, w_gate, w_up, w_down):
    logits = (h @ w_router).astype(jnp.float32)
    top_val, top_idx = lax.top_k(logits, TOP_K)
    top_w = jax.nn.softmax(top_val, axis=-1)
    gate = jnp.sum(jax.nn.one_hot(top_idx, N_EXPERTS, dtype=jnp.float32) * top_w[..., None], axis=-2)
    out = jnp.zeros_like(h)
    for e in range(N_EXPERTS):
        y = (jax.nn.silu(h @ w_gate[e]) * (h @ w_up[e])) @ w_down[e]
        out = out + gate[..., e:e + 1].astype(h.dtype) * y
    return out


def setup_inputs(seed: int = 0) -> dict:
    key = jax.random.key(seed)
    keys = iter(jax.random.split(key, 32))
    n_even = (DEPTH + 1) // 2
    n_odd = DEPTH // 2

    def nrm(shape, scale):
        return jax.random.normal(next(keys), shape, jnp.float32) * scale

    def gain(shape):
        return 1.0 + nrm(shape, 0.05)

    cmp_in = NSA_CMP_LEN * HEAD_DIM
    return {
        'x': nrm((BATCH, SEQ, D_MODEL), 1.0),
        'rel_table': nrm((REL_BUCKETS, REL_HEADS), 0.5),
        'norm_mix': gain((DEPTH, D_MODEL)),
        'norm_ffn': gain((DEPTH, D_MODEL)),
        'norm_final': gain((D_MODEL,)),
        'ab_w_in': nrm((n_even, D_MODEL, AB_IN), D_MODEL ** -0.5),
        'ab_w_out': nrm((n_even, AB_OUT, D_MODEL), AB_OUT ** -0.5),
        'cmp_k_pe': nrm((n_even, NSA_CMP_LEN, HEAD_DIM), 0.1),
        'cmp_k_w1': nrm((n_even, cmp_in, NSA_CMP_HIDDEN), cmp_in ** -0.5),
        'cmp_k_b1': nrm((n_even, NSA_CMP_HIDDEN), 0.01),
        'cmp_k_w2': nrm((n_even, NSA_CMP_HIDDEN, HEAD_DIM), NSA_CMP_HIDDEN ** -0.5),
        'cmp_v_pe': nrm((n_even, NSA_CMP_LEN, HEAD_DIM), 0.1),
        'cmp_v_w1': nrm((n_even, cmp_in, NSA_CMP_HIDDEN), cmp_in ** -0.5),
        'cmp_v_b1': nrm((n_even, NSA_CMP_HIDDEN), 0.01),
        'cmp_v_w2': nrm((n_even, NSA_CMP_HIDDEN, HEAD_DIM), NSA_CMP_HIDDEN ** -0.5),
        'ffn_w_gate': nrm((n_even, D_MODEL, D_FF), D_MODEL ** -0.5),
        'ffn_w_up': nrm((n_even, D_MODEL, D_FF), D_MODEL ** -0.5),
        'ffn_w_down': nrm((n_even, D_FF, D_MODEL), D_FF ** -0.5),
        'cd_w_in': nrm((n_odd, D_MODEL, CD_IN), D_MODEL ** -0.5),
        'cd_w_out': nrm((n_odd, CD_OUT, D_MODEL), CD_OUT ** -0.5),
        'sg_ln_g': gain((n_odd, SG_WIDTH)),
        'sg_ln_b': nrm((n_odd, SG_WIDTH), 0.01),
        'sg_w': nrm((n_odd, SG_GROUPS, SG_CHUNK, SG_CHUNK), SG_CHUNK ** -0.5),
        'sg_b': 1.0 + nrm((n_odd, SG_GROUPS, SG_CHUNK), 0.05),
        'swa_sinks': nrm((n_odd, SWA_HEADS), 0.5),
        'moe_w_router': nrm((n_odd, D_MODEL, N_EXPERTS), D_MODEL ** -0.5),
        'moe_w_gate': nrm((n_odd, N_EXPERTS, D_MODEL, D_FF_EXPERT), D_MODEL ** -0.5),
        'moe_w_up': nrm((n_odd, N_EXPERTS, D_MODEL, D_FF_EXPERT), D_MODEL ** -0.5),
        'moe_w_down': nrm((n_odd, N_EXPERTS, D_FF_EXPERT, D_MODEL), D_FF_EXPERT ** -0.5),
    }


def reference(x, rel_table, norm_mix, norm_ffn, norm_final,
              ab_w_in, ab_w_out,
              cmp_k_pe, cmp_k_w1, cmp_k_b1, cmp_k_w2,
              cmp_v_pe, cmp_v_w1, cmp_v_b1, cmp_v_w2,
              ffn_w_gate, ffn_w_up, ffn_w_down,
              cd_w_in, cd_w_out, sg_ln_g, sg_ln_b, sg_w, sg_b, swa_sinks,
              moe_w_router, moe_w_gate, moe_w_up, moe_w_down):
    for layer in range(DEPTH):
        i = layer // 2
        h = rmsnorm(x, norm_mix[layer])
        if layer % 2 == 0:
            x = x + mixer_ab(h, ab_w_in[i], ab_w_out[i],
                             (cmp_k_pe[i], cmp_k_w1[i], cmp_k_b1[i], cmp_k_w2[i]),
                             (cmp_v_pe[i], cmp_v_w1[i], cmp_v_b1[i], cmp_v_w2[i]),
                             rel_table)
            h = rmsnorm(x, norm_ffn[layer])
            x = x + swiglu(h, ffn_w_gate[i], ffn_w_up[i], ffn_w_down[i])
        else:
            x = x + mixer_cd(h, cd_w_in[i], cd_w_out[i], sg_ln_g[i], sg_ln_b[i],
                             sg_w[i], sg_b[i], swa_sinks[i], rel_table)
            h = rmsnorm(x, norm_ffn[layer])
            x = x + moe_swiglu(h, moe_w_router[i], moe_w_gate[i], moe_w_up[i], moe_w_down[i])
    return rmsnorm(x, norm_final)
```

```python
import functools
import math

import numpy as np
import jax
import jax.numpy as jnp
from jax import lax
from jax.experimental import pallas as pl
from jax.experimental.pallas import tpu as pltpu

F32 = jnp.float32
BF16 = jnp.bfloat16

LANE = 128
VMEM_LIMIT = 56 * 1024 * 1024

HEAD_DIM = 128
BLK = 128
NEG_INF = -1e30
RMS_EPS = 1e-6
SCALE = HEAD_DIM ** -0.5

SB_HEADS = 16
NSA_HEADS = 16
NSA_KV_HEADS = 4
NSA_G = NSA_HEADS // NSA_KV_HEADS
NSA_CMP_LEN = 32
NSA_CMP_STRIDE = 16
NSA_SEL_LEN = 64
NSA_SEL_TOPK = 16
NSA_WINDOW = 512
NSA_FORCE_SCORE = 1e6
SG_GROUPS = 8
SG_CHUNK = 128
SWA_HEADS = 16
SWA_KV_HEADS = 2
SWA_WINDOW = 128
REL_BUCKETS = 32
REL_MAX_DIST = 1024
REL_HEADS = 16
N_EXPERTS = 8

AB_Q_SB, AB_K_SB, AB_V_SB = 0, 16, 32
AB_Q_N = 48
AB_K_C, AB_V_C, AB_K_S, AB_V_S, AB_K_W, AB_V_W = 64, 68, 72, 76, 80, 84
AB_MAIN_BLOCKS = 88
CD_Q, CD_K, CD_V = 32, 48, 50

SEL_NEAR_BLOCKS = 12
CMP_PAD = 120
SB_SKIP_LOG = -110.0
M_INIT = -5e29


def _cparams(sem):
    return pltpu.CompilerParams(dimension_semantics=sem, vmem_limit_bytes=VMEM_LIMIT)


def _dot(a, b):
    return jnp.dot(a, b, preferred_element_type=F32)


def _dot_nt(a, b):
    return lax.dot_general(a, b, (((1,), (1,)), ((), ())), preferred_element_type=F32)


def _sigmoid(x):
    return 1.0 / (1.0 + jnp.exp(-x))


def _gelu_tanh(x):
    return 0.5 * x * (1.0 + jnp.tanh(math.sqrt(2.0 / math.pi) * (x + 0.044715 * (x * x * x))))


def _rmsnorm_kernel(x_ref, g_ref, o_ref):
    x = x_ref[...]
    y = x * lax.rsqrt(jnp.mean(x * x, axis=-1, keepdims=True) + RMS_EPS)
    o_ref[...] = (y * g_ref[...]).astype(o_ref.dtype)


def rmsnorm(x, g, out_dtype):
    t, d = x.shape
    tm = 256
    return pl.pallas_call(
        _rmsnorm_kernel,
        out_shape=jax.ShapeDtypeStruct((t, d), out_dtype),
        grid=(t // tm,),
        in_specs=[pl.BlockSpec((tm, d), lambda i: (i, 0)), pl.BlockSpec((1, d), lambda i: (0, 0))],
        out_specs=pl.BlockSpec((tm, d), lambda i: (i, 0)),
        compiler_params=_cparams(("parallel",)),
        name="rmsnorm",
    )(x, g.reshape(1, d))


def _rmsnorm_router_kernel(x_ref, g_ref, wr_ref, o_ref, gate_ref):
    x = x_ref[...]
    y = x * lax.rsqrt(jnp.mean(x * x, axis=-1, keepdims=True) + RMS_EPS)
    y = y * g_ref[...]
    o_ref[...] = y.astype(o_ref.dtype)
    logits = jnp.dot(y, wr_ref[...], preferred_element_type=F32, precision=lax.Precision.HIGHEST)
    lane = lax.broadcasted_iota(jnp.int32, logits.shape, 1)
    logits = jnp.where(lane < N_EXPERTS, logits, NEG_INF)
    m1 = jnp.max(logits, axis=-1, keepdims=True)
    i1 = jnp.min(jnp.where(logits == m1, lane, LANE), axis=-1, keepdims=True)
    rest = jnp.where(lane == i1, NEG_INF, logits)
    m2 = jnp.max(rest, axis=-1, keepdims=True)
    i2 = jnp.min(jnp.where(rest == m2, lane, LANE), axis=-1, keepdims=True)
    e2 = jnp.exp(m2 - m1)
    denom = 1.0 + e2
    gate_ref[...] = jnp.where(lane == i1, 1.0 / denom, 0.0) + jnp.where(lane == i2, e2 / denom, 0.0)


def rmsnorm_router(x, g, w_router):
    t, d = x.shape
    tm = 256
    wr = jnp.pad(w_router, ((0, 0), (0, LANE - w_router.shape[1])))
    return pl.pallas_call(
        _rmsnorm_router_kernel,
        out_shape=(jax.ShapeDtypeStruct((t, d), BF16), jax.ShapeDtypeStruct((t, LANE), F32)),
        grid=(t // tm,),
        in_specs=[pl.BlockSpec((tm, d), lambda i: (i, 0)), pl.BlockSpec((1, d), lambda i: (0, 0)),
                  pl.BlockSpec((d, LANE), lambda i: (0, 0))],
        out_specs=(pl.BlockSpec((tm, d), lambda i: (i, 0)), pl.BlockSpec((tm, LANE), lambda i: (i, 0))),
        compiler_params=_cparams(("parallel",)),
        name="rmsnorm_router",
    )(x, g.reshape(1, d), wr)


def _mm_kernel(*refs, n_w, has_res, has_scale, act, nk):
    a_ref = refs[0]
    w_refs = refs[1:1 + n_w]
    pos = 1 + n_w
    res_ref = refs[pos] if has_res else None
    pos += int(has_res)
    scale_ref = refs[pos] if has_scale else None
    pos += int(has_scale)
    o_ref = refs[pos]
    acc_refs = refs[pos + 1:]

    def finalize(parts):
        r = parts[0]
        if act == "swiglu":
            r = (r * _sigmoid(r)) * parts[1]
        if has_scale:
            r = scale_ref[...] * r
        if has_res:
            r = res_ref[...] + r
        o_ref[...] = r.astype(o_ref.dtype)

    a = a_ref[...]
    parts = [_dot(a, w[...].astype(BF16)) for w in w_refs]
    if nk == 1:
        finalize(parts)
        return
    k = pl.program_id(2)

    @pl.when(k == 0)
    def _():
        for acc, p in zip(acc_refs, parts):
            acc[...] = p

    @pl.when(k > 0)
    def _():
        for acc, p in zip(acc_refs, parts):
            acc[...] += p

    @pl.when(k == nk - 1)
    def _():
        finalize([acc[...] for acc in acc_refs])


def matmul(a, ws, lead, n_cols, *, out_dtype, res=None, scale=None, act=None, tm=1024, tn=512, tk=2048):
    m, kdim = a.shape
    tk = min(tk, kdim)
    nk = kdim // tk
    assert m % tm == 0 and n_cols % tn == 0 and kdim % tk == 0
    lead = tuple(lead)
    in_specs = [pl.BlockSpec((tm, tk), lambda j, i, k: (i, k))]
    for _ in ws:
        in_specs.append(pl.BlockSpec((None,) * len(lead) + (tk, tn), lambda j, i, k: lead + (k, j)))
    args = [a, *ws]
    if res is not None:
        in_specs.append(pl.BlockSpec((tm, tn), lambda j, i, k: (i, j)))
        args.append(res)
    if scale is not None:
        in_specs.append(pl.BlockSpec((tm, 1), lambda j, i, k: (i, 0)))
        args.append(scale)
    kern = functools.partial(_mm_kernel, n_w=len(ws), has_res=res is not None, has_scale=scale is not None,
                             act=act, nk=nk)
    return pl.pallas_call(
        kern,
        out_shape=jax.ShapeDtypeStruct((m, n_cols), out_dtype),
        grid=(n_cols // tn, m // tm, nk),
        in_specs=in_specs,
        out_specs=pl.BlockSpec((tm, tn), lambda j, i, k: (i, j)),
        scratch_shapes=[pltpu.VMEM((tm, tn), F32) for _ in ws] if nk > 1 else [],
        compiler_params=_cparams(("parallel", "parallel", "arbitrary")),
        name="matmul",
    )(*args)


def _bucket_thresholds():
    d = np.arange(0, 4 * REL_MAX_DIST)
    max_exact = REL_BUCKETS // 2
    log_ratio = np.log(np.maximum(d, 1).astype(np.float64) / max_exact) / math.log(REL_MAX_DIST / max_exact)
    far = np.minimum(max_exact + (log_ratio * (REL_BUCKETS - max_exact)).astype(np.int32), REL_BUCKETS - 1)
    bucket = np.where(d < max_exact, d, far)
    assert np.all(np.diff(bucket) >= 0) and bucket[-1] == REL_BUCKETS - 1
    return [int(np.argmax(bucket >= b)) for b in range(REL_BUCKETS)]


_BUCKET_THR = _bucket_thresholds()
REL_FAR_DIST = _BUCKET_THR[REL_BUCKETS - 1]
assert REL_FAR_DIST <= (SEL_NEAR_BLOCKS - 4) * BLK - (BLK - 1)
assert REL_FAR_DIST <= NSA_CMP_STRIDE * (CMP_PAD + 1) - (NSA_CMP_LEN - 1)


def _bias_kernel(tab_ref, raw_ref, sel_ref, cmp_ref):
    h = pl.program_id(0)
    delta = pl.program_id(1)
    row = lax.broadcasted_iota(jnp.int32, (BLK, BLK), 0)
    col = lax.broadcasted_iota(jnp.int32, (BLK, BLK), 1)

    def lookup(dist):
        dd = jnp.maximum(dist, 0)
        b = jnp.full(dist.shape, tab_ref[0, h], F32)
        for bk in range(1, REL_BUCKETS):
            b = jnp.where(dd >= _BUCKET_THR[bk], tab_ref[bk, h], b)
        return b

    far = tab_ref[REL_BUCKETS - 1, h]
    dist = delta * BLK + row - col
    raw = lookup(dist)
    raw_ref[...] = raw
    sel_ref[...] = jnp.where(dist < 0, NEG_INF, raw - far)

    @pl.when(delta == 0)
    def _():
        dist_c = row - NSA_CMP_STRIDE * (col - CMP_PAD) - (NSA_CMP_LEN - 1)
        cmp_ref[...] = lookup(dist_c) - far


def bias_tiles(rel_table):
    nd = SEL_NEAR_BLOCKS
    tile = lambda: pl.BlockSpec((None, None, BLK, BLK), lambda h, d: (d, h, 0, 0))
    return pl.pallas_call(
        _bias_kernel,
        out_shape=(jax.ShapeDtypeStruct((nd, REL_HEADS, BLK, BLK), F32),
                   jax.ShapeDtypeStruct((nd, REL_HEADS, BLK, BLK), F32),
                   jax.ShapeDtypeStruct((REL_HEADS, BLK, BLK), F32)),
        grid=(REL_HEADS, nd),
        in_specs=[pl.BlockSpec(memory_space=pltpu.SMEM)],
        out_specs=(tile(), tile(), pl.BlockSpec((None, BLK, BLK), lambda h, d: (h, 0, 0))),
        compiler_params=_cparams(("arbitrary", "arbitrary")),
        name="bias_tiles",
    )(rel_table)


SB_BQ = 256


def _sb_kernel(q_ref, k_ref, v_ref, o_ref):
    qb = pl.program_id(1)
    q = q_ref[...]
    row = lax.broadcasted_iota(jnp.int32, (SB_BQ, SB_BQ), 0)
    col = lax.broadcasted_iota(jnp.int32, (SB_BQ, SB_BQ), 1)
    upper = (row > col).astype(BF16)

    def block(kb, past, carry):
        ks = pl.multiple_of(kb * SB_BQ, SB_BQ)
        z = _dot_nt(q, k_ref[pl.ds(ks, SB_BQ), :]) * SCALE
        sp = jnp.maximum(z, 0.0) + jnp.log(1.0 + jnp.exp(-jnp.abs(z)))
        log_stay = -sp
        if past is not None:
            log_stay = jnp.where(past, log_stay, 0.0)
        hi = log_stay.astype(BF16)
        lo = (log_stay - hi.astype(F32)).astype(BF16)
        later = _dot(hi, upper) + _dot(lo, upper)
        w = jnp.exp((z - sp) + later + carry)
        if past is not None:
            w = jnp.where(past, w, 0.0)
        contrib = _dot(w.astype(BF16), v_ref[pl.ds(ks, SB_BQ), :])
        return contrib, jnp.sum(log_stay, axis=-1, keepdims=True)

    acc0, c0 = block(qb, col < row, jnp.zeros((SB_BQ, 1), F32))

    def cond(state):
        kb, c, _ = state
        return jnp.logical_and(kb >= 0, jnp.max(c) > SB_SKIP_LOG)

    def body(state):
        kb, c, acc = state
        contrib, tot = block(kb, None, c)
        return kb - 1, c + tot, acc + contrib

    _, _, acc = lax.while_loop(cond, body, (qb - 1, c0, acc0))
    o_ref[...] = acc.astype(o_ref.dtype)


def sb_attention(z, t):
    return pl.pallas_call(
        _sb_kernel,
        out_shape=jax.ShapeDtypeStruct((t, SB_HEADS * HEAD_DIM), BF16),
        grid=(SB_HEADS, t // SB_BQ),
        in_specs=[pl.BlockSpec((SB_BQ, HEAD_DIM), lambda h, i: (i, AB_Q_SB + h)),
                  pl.BlockSpec((t, HEAD_DIM), lambda h, i: (0, AB_K_SB + h)),
                  pl.BlockSpec((t, HEAD_DIM), lambda h, i: (0, AB_V_SB + h))],
        out_specs=pl.BlockSpec((SB_BQ, HEAD_DIM), lambda h, i: (i, h)),
        compiler_params=_cparams(("parallel", "arbitrary")),
        name="sb_attention",
    )(z, z, z)


def _compress_kernel(blk_ref, pe_ref, w1_ref, b1_ref, w2_ref, o_ref, *, n_rows):
    x = (blk_ref[...].astype(F32) + pe_ref[...]).astype(BF16)
    hid = _gelu_tanh(_dot(x, w1_ref[...].astype(BF16)) + b1_ref[...])
    out = _dot(hid.astype(BF16), w2_ref[...].astype(BF16))
    o_ref[0:CMP_PAD, :] = jnp.zeros((CMP_PAD, HEAD_DIM), o_ref.dtype)
    o_ref[CMP_PAD:CMP_PAD + n_rows, :] = out.astype(o_ref.dtype)
    o_ref[CMP_PAD + n_rows:, :] = jnp.zeros((o_ref.shape[0] - CMP_PAD - n_rows, HEAD_DIM), o_ref.dtype)


def compress(z, col_block, pe, w1, b1, w2, t):
    n_chunk = t // NSA_CMP_STRIDE
    a = z[:, col_block * LANE:(col_block + NSA_KV_HEADS) * LANE]
    a = a.reshape(n_chunk, NSA_CMP_STRIDE, NSA_KV_HEADS, HEAD_DIM).transpose(2, 0, 1, 3)
    a = a.reshape(NSA_KV_HEADS, n_chunk, NSA_CMP_STRIDE * HEAD_DIM)
    nxt = jnp.concatenate([a[:, 1:], jnp.zeros_like(a[:, :1])], axis=1)
    blk = jnp.concatenate([a, nxt], axis=-1)
    width = NSA_CMP_LEN * HEAD_DIM
    hidden = w1.shape[-1]
    rows = CMP_PAD + n_chunk + 8
    return pl.pallas_call(
        functools.partial(_compress_kernel, n_rows=n_chunk),
        out_shape=jax.ShapeDtypeStruct((NSA_KV_HEADS, rows, HEAD_DIM), BF16),
        grid=(NSA_KV_HEADS,),
        in_specs=[pl.BlockSpec((None, n_chunk, width), lambda h: (h, 0, 0)),
                  pl.BlockSpec((1, width), lambda h: (0, 0)),
                  pl.BlockSpec((width, hidden), lambda h: (0, 0)),
                  pl.BlockSpec((1, hidden), lambda h: (0, 0)),
                  pl.BlockSpec((hidden, HEAD_DIM), lambda h: (0, 0))],
        out_specs=pl.BlockSpec((None, rows, HEAD_DIM), lambda h: (h, 0, 0)),
        compiler_params=_cparams(("parallel",)),
        name="compress",
    )(blk, pe.reshape(1, width), w1, b1.reshape(1, hidden), w2)


def _cmp_to_sel_padded(t):
    n_cmp = (t - NSA_CMP_LEN) // NSA_CMP_STRIDE + 1
    n_sel = t // NSA_SEL_LEN
    cmp_end = np.arange(n_cmp) * NSA_CMP_STRIDE + NSA_CMP_LEN - 1
    cmp_start = cmp_end - (NSA_CMP_LEN - 1)
    sel_start = np.arange(n_sel) * NSA_SEL_LEN
    overlap = (np.minimum(cmp_end[:, None], sel_start[None, :] + NSA_SEL_LEN - 1)
               - np.maximum(cmp_start[:, None], sel_start[None, :]) + 1)
    c2s = np.clip(overlap, 0, None).astype(np.float32) / NSA_CMP_LEN
    out = np.zeros((CMP_PAD + n_cmp + 9, n_sel), np.float32)
    out[CMP_PAD:CMP_PAD + n_cmp] = c2s
    return out


def _cmp_kernel(q_ref, kc_ref, vc_ref, c2s_ref, bias_ref, o_ref, pen_ref, *, n_far):
    qb = pl.program_id(1)
    r0 = pl.multiple_of(qb * (BLK // NSA_CMP_STRIDE), 8)
    kc_far = kc_ref[0:n_far, :]
    kc_near = kc_ref[pl.ds(r0, BLK), :]
    vs_far = jnp.concatenate([vc_ref[0:n_far, :], c2s_ref[0:n_far, :]], axis=1)
    vs_near = jnp.concatenate([vc_ref[pl.ds(r0, BLK), :], c2s_ref[pl.ds(r0, BLK), :]], axis=1)
    r_far = lax.broadcasted_iota(jnp.int32, (BLK, n_far), 1)
    far_ok = jnp.logical_and(r_far >= CMP_PAD, r_far < r0)
    row = lax.broadcasted_iota(jnp.int32, (BLK, BLK), 0)
    col = lax.broadcasted_iota(jnp.int32, (BLK, BLK), 1)
    dist_c = row - NSA_CMP_STRIDE * (col - CMP_PAD) - (NSA_CMP_LEN - 1)
    near_ok = jnp.logical_and(dist_c >= 0, col + r0 >= CMP_PAD)

    n_sel = c2s_ref.shape[1]
    imp = jnp.zeros((BLK, n_sel), F32)
    for g in range(NSA_G):
        q = q_ref[:, g * HEAD_DIM:(g + 1) * HEAD_DIM]
        s_far = jnp.where(far_ok, _dot_nt(q, kc_far) * SCALE, NEG_INF)
        s_near = jnp.where(near_ok, _dot_nt(q, kc_near) * SCALE + bias_ref[g], NEG_INF)
        m = jnp.maximum(jnp.max(s_far, axis=-1, keepdims=True), jnp.max(s_near, axis=-1, keepdims=True))
        e_far = jnp.where(far_ok, jnp.exp(s_far - m), 0.0)
        e_near = jnp.where(near_ok, jnp.exp(s_near - m), 0.0)
        l = jnp.sum(e_far, axis=-1, keepdims=True) + jnp.sum(e_near, axis=-1, keepdims=True)
        inv = 1.0 / jnp.where(l > 0.0, l, 1.0)
        res = (_dot(e_far.astype(BF16), vs_far) + _dot(e_near.astype(BF16), vs_near)) * inv
        o_ref[:, g * HEAD_DIM:(g + 1) * HEAD_DIM] = res[:, :HEAD_DIM].astype(o_ref.dtype)
        imp = imp + res[:, HEAD_DIM:]

    blk = lax.broadcasted_iota(jnp.int32, (BLK, n_sel), 1)
    t_pos = qb * BLK + lax.broadcasted_iota(jnp.int32, (BLK, n_sel), 0)
    cur = t_pos // NSA_SEL_LEN
    forced = jnp.logical_or(blk == 0, jnp.logical_or(blk == cur, blk == cur - 1))
    score = jnp.where(forced, NSA_FORCE_SCORE, jnp.where(blk <= cur, imp, -1.0))
    chosen = jnp.zeros((BLK, n_sel), jnp.bool_)
    for _ in range(min(NSA_SEL_TOPK, n_sel)):
        top = jnp.max(score, axis=-1, keepdims=True)
        first = jnp.min(jnp.where(score == top, blk, n_sel), axis=-1, keepdims=True)
        hit = blk == first
        chosen = jnp.logical_or(chosen, hit)
        score = jnp.where(hit, -2.0, score)
    pen_ref[...] = jnp.where(chosen, 0.0, NEG_INF).astype(pen_ref.dtype)


def cmp_select(z, kc, vc, bias_cmp, t):
    n_sel = t // NSA_SEL_LEN
    assert n_sel == LANE, "selection mask rides in one 128-lane tile"
    c2s = jnp.asarray(_cmp_to_sel_padded(t), BF16)
    rows = kc.shape[1]
    assert c2s.shape[0] == rows
    n_far = t // NSA_CMP_STRIDE
    qw = NSA_G * HEAD_DIM
    return pl.pallas_call(
        functools.partial(_cmp_kernel, n_far=n_far),
        out_shape=(jax.ShapeDtypeStruct((t, NSA_HEADS * HEAD_DIM), BF16),
                   jax.ShapeDtypeStruct((NSA_KV_HEADS, t, n_sel), BF16)),
        grid=(NSA_KV_HEADS, t // BLK),
        in_specs=[pl.BlockSpec((BLK, qw), lambda h, i: (i, AB_Q_N // NSA_G + h)),
                  pl.BlockSpec((None, rows, HEAD_DIM), lambda h, i: (h, 0, 0)),
                  pl.BlockSpec((None, rows, HEAD_DIM), lambda h, i: (h, 0, 0)),
                  pl.BlockSpec((rows, n_sel), lambda h, i: (0, 0)),
                  pl.BlockSpec((NSA_G, BLK, BLK), lambda h, i: (h, 0, 0))],
        out_specs=(pl.BlockSpec((BLK, qw), lambda h, i: (i, h)),
                   pl.BlockSpec((None, BLK, n_sel), lambda h, i: (h, i, 0))),
        compiler_params=_cparams(("parallel", "arbitrary")),
        name="cmp_select",
    )(z, kc, vc, c2s, bias_cmp)


SEL_FAR_CHUNK = 4 * BLK


def _sel_kernel(q_ref, pen_ref, k_ref, v_ref, e_ref, bias_ref, o_ref, m_ref, l_ref, acc_ref):
    qb = pl.program_id(1)
    rows = NSA_G * BLK
    pen = pen_ref[...]
    qa = jnp.concatenate(
        [jnp.concatenate([(q_ref[:, g * HEAD_DIM:(g + 1) * HEAD_DIM].astype(F32) * SCALE).astype(BF16), pen], axis=1)
         for g in range(NSA_G)], axis=0)
    m_ref[...] = jnp.full((rows, 1), M_INIT, F32)
    l_ref[...] = jnp.zeros((rows, 1), F32)
    acc_ref[...] = jnp.zeros((rows, HEAD_DIM), F32)

    def step(ks, size, bias):
        kaug = jnp.concatenate([k_ref[pl.ds(ks, size), :], e_ref[pl.ds(ks, size), :]], axis=1)
        s = _dot_nt(qa, kaug)
        if bias is not None:
            s = s + bias
        m_old = m_ref[...]
        m_new = jnp.maximum(m_old, jnp.max(s, axis=-1, keepdims=True))
        alpha = jnp.exp(m_old - m_new)
        p = jnp.exp(s - m_new)
        l_ref[...] = alpha * l_ref[...] + jnp.sum(p, axis=-1, keepdims=True)
        acc_ref[...] = alpha * acc_ref[...] + _dot(p.astype(BF16), v_ref[pl.ds(ks, size), :])
        m_ref[...] = m_new

    n_far = jnp.maximum(qb - (SEL_NEAR_BLOCKS - 5), 0) // 4

    def far_body(c, carry):
        step(pl.multiple_of(c * SEL_FAR_CHUNK, SEL_FAR_CHUNK), SEL_FAR_CHUNK, None)
        return carry

    lax.fori_loop(0, n_far, far_body, 0)

    def near_body(kb, carry):
        step(pl.multiple_of(kb * BLK, BLK), BLK, bias_ref[qb - kb].reshape(rows, BLK))
        return carry

    lax.fori_loop(n_far * 4, qb + 1, near_body, 0)

    out = acc_ref[...] / l_ref[...]
    for g in range(NSA_G):
        o_ref[:, g * HEAD_DIM:(g + 1) * HEAD_DIM] = out[g * BLK:(g + 1) * BLK, :].astype(o_ref.dtype)


def sel_attention(z, pen, bias_sel, t):
    n_sel = t // NSA_SEL_LEN
    onehot = (np.arange(t)[:, None] // NSA_SEL_LEN == np.arange(n_sel)[None, :]).astype(np.float32)
    qw = NSA_G * HEAD_DIM
    rows = NSA_G * BLK
    return pl.pallas_call(
        _sel_kernel,
        out_shape=jax.ShapeDtypeStruct((t, NSA_HEADS * HEAD_DIM), BF16),
        grid=(NSA_KV_HEADS, t // BLK),
        in_specs=[pl.BlockSpec((BLK, qw), lambda h, i: (i, AB_Q_N // NSA_G + h)),
                  pl.BlockSpec((None, BLK, n_sel), lambda h, i: (h, i, 0)),
                  pl.BlockSpec((t, HEAD_DIM), lambda h, i: (0, AB_K_S + h)),
                  pl.BlockSpec((t, HEAD_DIM), lambda h, i: (0, AB_V_S + h)),
                  pl.BlockSpec((t, n_sel), lambda h, i: (0, 0)),
                  pl.BlockSpec((SEL_NEAR_BLOCKS, NSA_G, BLK, BLK), lambda h, i: (0, h, 0, 0))],
        out_specs=pl.BlockSpec((BLK, qw), lambda h, i: (i, h)),
        scratch_shapes=[pltpu.VMEM((rows, 1), F32), pltpu.VMEM((rows, 1), F32), pltpu.VMEM((rows, HEAD_DIM), F32)],
        compiler_params=_cparams(("parallel", "arbitrary")),
        name="sel_attention",
    )(z, pen, z, z, jnp.asarray(onehot, BF16), bias_sel)


def _banded_kernel(*refs, group, n_prev, window, has_sinks):
    if has_sinks:
        sink_ref, q_ref, k_ref, v_ref, bias_ref, o_ref = refs
    else:
        q_ref, k_ref, v_ref, bias_ref, o_ref = refs
    kv = pl.program_id(0)
    qb = pl.program_id(1)
    row = lax.broadcasted_iota(jnp.int32, (BLK, BLK), 0)
    col = lax.broadcasted_iota(jnp.int32, (BLK, BLK), 1)
    starts, valids = [], []
    for delta in range(n_prev + 1):
        kb = qb - delta
        starts.append(pl.multiple_of(jnp.maximum(kb, 0) * BLK, BLK))
        dist = delta * BLK + row - col
        limit = jnp.where(kb >= 0, window, 0)
        valids.append(jnp.logical_and(dist >= 0, dist < limit))
    for g in range(group):
        q = q_ref[:, g * HEAD_DIM:(g + 1) * HEAD_DIM]
        s = [jnp.where(valids[d], _dot_nt(q, k_ref[pl.ds(starts[d], BLK), :]) * SCALE + bias_ref[d, g], NEG_INF)
             for d in range(n_prev + 1)]
        m = functools.reduce(jnp.maximum, [jnp.max(x, axis=-1, keepdims=True) for x in s])
        if has_sinks:
            sink = sink_ref[kv * group + g]
            m = jnp.maximum(m, sink)
        e = [jnp.exp(x - m) for x in s]
        l = functools.reduce(jnp.add, [jnp.sum(x, axis=-1, keepdims=True) for x in e])
        if has_sinks:
            l = l + jnp.exp(sink - m)
        o = functools.reduce(jnp.add, [_dot(e[d].astype(BF16), v_ref[pl.ds(starts[d], BLK), :])
                                       for d in range(n_prev + 1)])
        o_ref[:, g * HEAD_DIM:(g + 1) * HEAD_DIM] = (o / l).astype(o_ref.dtype)


def banded_attention(z, bias_raw, t, *, q_block, k_block, v_block, n_kv, group, window, sinks=None):
    n_prev = -(-window // BLK)
    qw = group * HEAD_DIM
    assert q_block % group == 0
    in_specs = [pl.BlockSpec((BLK, qw), lambda h, i: (i, q_block // group + h)),
                pl.BlockSpec((t, HEAD_DIM), lambda h, i: (0, k_block + h)),
                pl.BlockSpec((t, HEAD_DIM), lambda h, i: (0, v_block + h)),
                pl.BlockSpec((n_prev + 1, group, BLK, BLK), lambda h, i: (0, h, 0, 0))]
    args = [z, z, z, bias_raw]
    if sinks is not None:
        in_specs.insert(0, pl.BlockSpec(memory_space=pltpu.SMEM))
        args.insert(0, sinks)
    return pl.pallas_call(
        functools.partial(_banded_kernel, group=group, n_prev=n_prev, window=window, has_sinks=sinks is not None),
        out_shape=jax.ShapeDtypeStruct((t, n_kv * qw), BF16),
        grid=(n_kv, t // BLK),
        in_specs=in_specs,
        out_specs=pl.BlockSpec((BLK, qw), lambda h, i: (i, h)),
        compiler_params=_cparams(("parallel", "arbitrary")),
        name="banded_attention",
    )(*args)


def _combine_kernel(g_ref, c_ref, s_ref, w_ref, o_ref):
    gates = _sigmoid(g_ref[...])
    for h in range(NSA_HEADS):
        sl = slice(h * HEAD_DIM, (h + 1) * HEAD_DIM)
        o = (gates[:, 3 * h:3 * h + 1] * c_ref[:, sl].astype(F32)
             + gates[:, 3 * h + 1:3 * h + 2] * s_ref[:, sl].astype(F32)
             + gates[:, 3 * h + 2:3 * h + 3] * w_ref[:, sl].astype(F32))
        o_ref[:, sl] = o.astype(o_ref.dtype)


def nsa_combine(gate_logits, o_cmp, o_sel, o_win):
    t, w = o_cmp.shape
    tm = 256
    row = lambda width: pl.BlockSpec((tm, width), lambda i: (i, 0))
    return pl.pallas_call(
        _combine_kernel,
        out_shape=jax.ShapeDtypeStruct((t, w), BF16),
        grid=(t // tm,),
        in_specs=[row(LANE), row(w), row(w), row(w)],
        out_specs=row(w),
        compiler_params=_cparams(("parallel",)),
        name="nsa_combine",
    )(gate_logits, o_cmp, o_sel, o_win)


def _sg_kernel(u_ref, v_ref, lng_ref, lnb_ref, w_ref, b_ref, o_ref):
    u = _gelu_tanh(u_ref[...].astype(F32))
    gv = _gelu_tanh(v_ref[...].astype(F32))
    mu = jnp.mean(gv, axis=-1, keepdims=True)
    cen = gv - mu
    var = jnp.mean(cen * cen, axis=-1, keepdims=True)
    gv = (cen * lax.rsqrt(var + RMS_EPS) * lng_ref[...] + lnb_ref[...]).astype(BF16)
    row = lax.broadcasted_iota(jnp.int32, (SG_CHUNK, SG_CHUNK), 0)
    col = lax.broadcasted_iota(jnp.int32, (SG_CHUNK, SG_CHUNK), 1)
    gw = u.shape[1] // SG_GROUPS
    for g in range(SG_GROUPS):
        sl = slice(g * gw, (g + 1) * gw)
        w = jnp.where(row >= col, w_ref[g], 0.0).astype(BF16)
        spatial = _dot(w, gv[:, sl]) + b_ref[:, g:g + 1]
        o_ref[:, sl] = (u[:, sl] * spatial).astype(o_ref.dtype)


def spatial_gating(z1, ln_g, ln_b, sg_w, sg_b, t):
    width = ln_g.shape[0]
    nblk = width // LANE
    assert width % (SG_GROUPS * LANE) == 0
    return pl.pallas_call(
        _sg_kernel,
        out_shape=jax.ShapeDtypeStruct((t, width), BF16),
        grid=(t // SG_CHUNK,),
        in_specs=[pl.BlockSpec((SG_CHUNK, width), lambda i: (i, 0)),
                  pl.BlockSpec((SG_CHUNK, width), lambda i: (i, 1)),
                  pl.BlockSpec((1, width), lambda i: (0, 0)),
                  pl.BlockSpec((1, width), lambda i: (0, 0)),
                  pl.BlockSpec((SG_GROUPS, SG_CHUNK, SG_CHUNK), lambda i: (0, 0, 0)),
                  pl.BlockSpec((SG_CHUNK, SG_GROUPS), lambda i: (0, 0))],
        out_specs=pl.BlockSpec((SG_CHUNK, width), lambda i: (i, 0)),
        compiler_params=_cparams(("parallel",)),
        name="spatial_gating",
    )(z1, z1, ln_g.reshape(1, width), ln_b.reshape(1, width), sg_w, jnp.transpose(sg_b))


def kernel(x, rel_table, norm_mix, norm_ffn, norm_final, ab_w_in, ab_w_out, cmp_k_pe, cmp_k_w1, cmp_k_b1, cmp_k_w2, cmp_v_pe, cmp_v_w1, cmp_v_b1, cmp_v_w2, ffn_w_gate, ffn_w_up, ffn_w_down, cd_w_in, cd_w_out, sg_ln_g, sg_ln_b, sg_w, sg_b, swa_sinks, moe_w_router, moe_w_gate, moe_w_up, moe_w_down):
    b, t, d = x.shape
    assert b == 1
    xs = x.reshape(t, d)
    bias_raw, bias_sel, bias_cmp = bias_tiles(rel_table)

    h = rmsnorm(xs, norm_mix[0], BF16)
    ab_main = AB_MAIN_BLOCKS * LANE
    z = matmul(h, [ab_w_in], (0,), ab_main, out_dtype=BF16)
    w_gate_cols = jnp.pad(ab_w_in[0, :, ab_main:], ((0, 0), (0, LANE - (ab_w_in.shape[2] - ab_main))))
    gate_logits = matmul(h, [w_gate_cols], (), LANE, out_dtype=F32, tn=LANE)
    o_sb = sb_attention(z, t)
    kc = compress(z, AB_K_C, cmp_k_pe[0], cmp_k_w1[0], cmp_k_b1[0], cmp_k_w2[0], t)
    vc = compress(z, AB_V_C, cmp_v_pe[0], cmp_v_w1[0], cmp_v_b1[0], cmp_v_w2[0], t)
    o_cmp, pen = cmp_select(z, kc, vc, bias_cmp, t)
    o_sel = sel_attention(z, pen, bias_sel, t)
    o_win = banded_attention(z, bias_raw, t, q_block=AB_Q_N, k_block=AB_K_W, v_block=AB_V_W,
                             n_kv=NSA_KV_HEADS, group=NSA_G, window=NSA_WINDOW)
    o_nsa = nsa_combine(gate_logits, o_cmp, o_sel, o_win)
    mixed = jnp.concatenate([o_sb, o_nsa], axis=1)
    xs = matmul(mixed, [ab_w_out], (0,), d, out_dtype=F32, res=xs)
    h = rmsnorm(xs, norm_ffn[0], BF16)
    hid = matmul(h, [ffn_w_gate, ffn_w_up], (0,), ffn_w_gate.shape[2], out_dtype=BF16, act="swiglu")
    xs = matmul(hid, [ffn_w_down], (0,), d, out_dtype=F32, res=xs)

    h = rmsnorm(xs, norm_mix[1], BF16)
    z1 = matmul(h, [cd_w_in], (0,), cd_w_in.shape[2], out_dtype=BF16)
    o_sg = spatial_gating(z1, sg_ln_g[0], sg_ln_b[0], sg_w[0], sg_b[0], t)
    o_swa = banded_attention(z1, bias_raw, t, q_block=CD_Q, k_block=CD_K, v_block=CD_V,
                             n_kv=SWA_KV_HEADS, group=SWA_HEADS // SWA_KV_HEADS, window=SWA_WINDOW,
                             sinks=swa_sinks[0])
    mixed = jnp.concatenate([o_sg, o_swa], axis=1)
    xs = matmul(mixed, [cd_w_out], (0,), d, out_dtype=F32, res=xs)
    h, gate = rmsnorm_router(xs, norm_ffn[1], moe_w_router[0])
    d_ff_e = moe_w_gate.shape[3]
    for e in range(N_EXPERTS):
        hid = matmul(h, [moe_w_gate, moe_w_up], (0, e), d_ff_e, out_dtype=BF16, act="swiglu")
        xs = matmul(hid, [moe_w_down], (0, e), d, out_dtype=F32, res=xs, scale=gate[:, e:e + 1], tk=d_ff_e // 2)
    return rmsnorm(xs, norm_final, F32).reshape(b, t, d)
```

```python
import functools
import math

import numpy as np
import jax
import jax.numpy as jnp
from jax import lax
from jax.experimental import pallas as pl
from jax.experimental.pallas import tpu as pltpu

F32 = jnp.float32
BF16 = jnp.bfloat16

LANE = 128
VMEM_LIMIT = 56 * 1024 * 1024

HEAD_DIM = 128
BLK = 128
NEG_INF = -1e30
RMS_EPS = 1e-6
SCALE = HEAD_DIM ** -0.5

SB_HEADS = 16
NSA_HEADS = 16
NSA_KV_HEADS = 4
NSA_G = NSA_HEADS // NSA_KV_HEADS
NSA_CMP_LEN = 32
NSA_CMP_STRIDE = 16
NSA_SEL_LEN = 64
NSA_SEL_TOPK = 16
NSA_WINDOW = 512
NSA_FORCE_SCORE = 1e6
SG_GROUPS = 8
SG_CHUNK = 128
SWA_HEADS = 16
SWA_KV_HEADS = 2
SWA_WINDOW = 128
REL_BUCKETS = 32
REL_MAX_DIST = 1024
REL_HEADS = 16
N_EXPERTS = 8

AB_Q_SB, AB_K_SB, AB_V_SB = 0, 16, 32
AB_Q_N = 48
AB_K_C, AB_V_C, AB_K_S, AB_V_S, AB_K_W, AB_V_W = 64, 68, 72, 76, 80, 84
AB_MAIN_BLOCKS = 88
CD_Q, CD_K, CD_V = 32, 48, 50

SEL_CHUNK_BLOCKS = 4
SEL_FAR_BLOCKS = 8
SEL_NEG_BLOCKS = SEL_CHUNK_BLOCKS - 1
SEL_TILES = SEL_NEG_BLOCKS + SEL_FAR_BLOCKS + SEL_CHUNK_BLOCKS - 1
RAW_TILES = NSA_WINDOW // BLK + 1
CMP_PAD = 120
SB_SKIP_LOG = -110.0
M_INIT = -5e29


def _cparams(sem):
    return pltpu.CompilerParams(dimension_semantics=sem, vmem_limit_bytes=VMEM_LIMIT)


def _dot(a, b):
    return jnp.dot(a, b, preferred_element_type=F32)


def _dot_nt(a, b):
    return lax.dot_general(a, b, (((1,), (1,)), ((), ())), preferred_element_type=F32)


def _sigmoid(x):
    return 1.0 / (1.0 + jnp.exp(-x))


def _gelu_tanh(x):
    return 0.5 * x * (1.0 + jnp.tanh(math.sqrt(2.0 / math.pi) * (x + 0.044715 * (x * x * x))))


def _rmsnorm_kernel(x_ref, g_ref, o_ref):
    x = x_ref[...]
    y = x * lax.rsqrt(jnp.mean(x * x, axis=-1, keepdims=True) + RMS_EPS)
    o_ref[...] = (y * g_ref[...]).astype(o_ref.dtype)


def rmsnorm(x, g, out_dtype):
    t, d = x.shape
    tm = 256
    return pl.pallas_call(
        _rmsnorm_kernel,
        out_shape=jax.ShapeDtypeStruct((t, d), out_dtype),
        grid=(t // tm,),
        in_specs=[pl.BlockSpec((tm, d), lambda i: (i, 0)), pl.BlockSpec((1, d), lambda i: (0, 0))],
        out_specs=pl.BlockSpec((tm, d), lambda i: (i, 0)),
        compiler_params=_cparams(("parallel",)),
        name="rmsnorm",
    )(x, g.reshape(1, d))


def _rmsnorm_router_kernel(x_ref, g_ref, wr_ref, o_ref, gate_ref, idx_ref):
    x = x_ref[...]
    y = x * lax.rsqrt(jnp.mean(x * x, axis=-1, keepdims=True) + RMS_EPS)
    y = y * g_ref[...]
    o_ref[...] = y.astype(o_ref.dtype)
    logits = jnp.dot(y, wr_ref[...], preferred_element_type=F32, precision=lax.Precision.HIGHEST)
    lane = lax.broadcasted_iota(jnp.int32, logits.shape, 1)
    logits = jnp.where(lane < N_EXPERTS, logits, NEG_INF)
    m1 = jnp.max(logits, axis=-1, keepdims=True)
    i1 = jnp.min(jnp.where(logits == m1, lane, LANE), axis=-1, keepdims=True)
    rest = jnp.where(lane == i1, NEG_INF, logits)
    m2 = jnp.max(rest, axis=-1, keepdims=True)
    i2 = jnp.min(jnp.where(rest == m2, lane, LANE), axis=-1, keepdims=True)
    e2 = jnp.exp(m2 - m1)
    denom = 1.0 + e2
    gate_ref[...] = jnp.where(lane == 0, 1.0 / denom, jnp.where(lane == 1, e2 / denom, 0.0))
    idx_ref[...] = jnp.where(lane == 0, i1, jnp.where(lane == 1, i2, 0))


def rmsnorm_router(x, g, w_router):
    t, d = x.shape
    tm = 256
    wr = jnp.pad(w_router, ((0, 0), (0, LANE - w_router.shape[1])))
    return pl.pallas_call(
        _rmsnorm_router_kernel,
        out_shape=(jax.ShapeDtypeStruct((t, d), BF16), jax.ShapeDtypeStruct((t, LANE), F32),
                   jax.ShapeDtypeStruct((t, LANE), jnp.int32)),
        grid=(t // tm,),
        in_specs=[pl.BlockSpec((tm, d), lambda i: (i, 0)), pl.BlockSpec((1, d), lambda i: (0, 0)),
                  pl.BlockSpec((d, LANE), lambda i: (0, 0))],
        out_specs=(pl.BlockSpec((tm, d), lambda i: (i, 0)), pl.BlockSpec((tm, LANE), lambda i: (i, 0)),
                   pl.BlockSpec((tm, LANE), lambda i: (i, 0))),
        compiler_params=_cparams(("parallel",)),
        name="rmsnorm_router",
    )(x, g.reshape(1, d), wr)


def _mm_epilogue(parts, res_ref, o_ref, act):
    r = parts[0]
    if act == "swiglu":
        r = (r * _sigmoid(r)) * parts[1]
    if res_ref is not None:
        r = res_ref[...] + r
    o_ref[...] = r.astype(o_ref.dtype)


def _mm_resident_kernel(*refs, n_w, has_res, act):
    a_ref = refs[0]
    w_refs = refs[1:1 + n_w]
    res_ref = refs[1 + n_w] if has_res else None
    o_ref = refs[1 + n_w + int(has_res)]
    wbf_refs = refs[2 + n_w + int(has_res):]

    @pl.when(pl.program_id(1) == 0)
    def _():
        for w, wbf in zip(w_refs, wbf_refs):
            wbf[...] = w[...].astype(BF16)

    a = a_ref[...]
    _mm_epilogue([_dot(a, wbf[...]) for wbf in wbf_refs], res_ref, o_ref, act)


def _mm_ktiled_kernel(a_ref, w_ref, res_ref, o_ref, acc_ref, *, nk):
    k = pl.program_id(2)
    part = _dot(a_ref[...], w_ref[...].astype(BF16))

    @pl.when(k == 0)
    def _():
        acc_ref[...] = part

    @pl.when(k > 0)
    def _():
        acc_ref[...] += part

    @pl.when(k == nk - 1)
    def _():
        _mm_epilogue([acc_ref[...]], res_ref, o_ref, None)


MM_RESIDENT_MAX_K = 4096


def matmul(a, ws, lead, n_cols, *, out_dtype, res=None, act=None):
    m, kdim = a.shape
    lead = tuple(lead)
    squeeze = (None,) * len(lead)
    if kdim <= MM_RESIDENT_MAX_K:
        tm = 1024
        tn = min(n_cols, 512 // len(ws))
        assert m % tm == 0 and n_cols % tn == 0
        in_specs = [pl.BlockSpec((tm, kdim), lambda j, i: (i, 0))]
        in_specs += [pl.BlockSpec(squeeze + (kdim, tn), lambda j, i: lead + (0, j)) for _ in ws]
        args = [a, *ws]
        if res is not None:
            in_specs.append(pl.BlockSpec((tm, tn), lambda j, i: (i, j)))
            args.append(res)
        return pl.pallas_call(
            functools.partial(_mm_resident_kernel, n_w=len(ws), has_res=res is not None, act=act),
            out_shape=jax.ShapeDtypeStruct((m, n_cols), out_dtype),
            grid=(n_cols // tn, m // tm),
            in_specs=in_specs,
            out_specs=pl.BlockSpec((tm, tn), lambda j, i: (i, j)),
            scratch_shapes=[pltpu.VMEM((kdim, tn), BF16) for _ in ws],
            compiler_params=_cparams(("parallel", "arbitrary")),
            name="matmul_resident",
        )(*args)
    assert len(ws) == 1 and act is None and res is not None
    tm, tn, tk = 2048, 512, 2048
    assert m % tm == 0 and n_cols % tn == 0 and kdim % tk == 0
    nk = kdim // tk
    return pl.pallas_call(
        functools.partial(_mm_ktiled_kernel, nk=nk),
        out_shape=jax.ShapeDtypeStruct((m, n_cols), out_dtype),
        grid=(n_cols // tn, m // tm, nk),
        in_specs=[pl.BlockSpec((tm, tk), lambda j, i, k: (i, k)),
                  pl.BlockSpec(squeeze + (tk, tn), lambda j, i, k: lead + (k, j)),
                  pl.BlockSpec((tm, tn), lambda j, i, k: (i, j))],
        out_specs=pl.BlockSpec((tm, tn), lambda j, i, k: (i, j)),
        scratch_shapes=[pltpu.VMEM((tm, tn), F32)],
        compiler_params=_cparams(("parallel", "parallel", "arbitrary")),
        name="matmul_ktiled",
    )(a, ws[0], res)


def _bucket_thresholds():
    d = np.arange(0, 4 * REL_MAX_DIST)
    max_exact = REL_BUCKETS // 2
    log_ratio = np.log(np.maximum(d, 1).astype(np.float64) / max_exact) / math.log(REL_MAX_DIST / max_exact)
    far = np.minimum(max_exact + (log_ratio * (REL_BUCKETS - max_exact)).astype(np.int32), REL_BUCKETS - 1)
    bucket = np.where(d < max_exact, d, far)
    assert np.all(np.diff(bucket) >= 0) and bucket[-1] == REL_BUCKETS - 1
    return [int(np.argmax(bucket >= b)) for b in range(REL_BUCKETS)]


_BUCKET_THR = _bucket_thresholds()
REL_FAR_DIST = _BUCKET_THR[REL_BUCKETS - 1]
assert REL_FAR_DIST <= SEL_FAR_BLOCKS * BLK - (BLK - 1)


def _raw_tile_index(step):
    return jnp.clip(step - SEL_NEG_BLOCKS, 0, RAW_TILES - 1)
assert REL_FAR_DIST <= NSA_CMP_STRIDE * (CMP_PAD + 1) - (NSA_CMP_LEN - 1)


def _bias_kernel(tab_ref, raw_ref, sel_ref, cmp_ref):
    h = pl.program_id(0)
    step = pl.program_id(1)
    row = lax.broadcasted_iota(jnp.int32, (BLK, BLK), 0)
    col = lax.broadcasted_iota(jnp.int32, (BLK, BLK), 1)

    def lookup(dist):
        dd = jnp.maximum(dist, 0)
        b = jnp.full(dist.shape, tab_ref[0, h], F32)
        for bk in range(1, REL_BUCKETS):
            b = jnp.where(dd >= _BUCKET_THR[bk], tab_ref[bk, h], b)
        return b

    far = tab_ref[REL_BUCKETS - 1, h]
    raw_ref[...] = lookup(_raw_tile_index(step) * BLK + row - col)
    dist = (step - SEL_NEG_BLOCKS) * BLK + row - col
    sel_ref[...] = jnp.where(dist < 0, NEG_INF, lookup(dist) - far)

    @pl.when(step == 0)
    def _():
        dist_c = row - NSA_CMP_STRIDE * (col - CMP_PAD) - (NSA_CMP_LEN - 1)
        cmp_ref[...] = lookup(dist_c) - far


def bias_tiles(rel_table):
    tile = lambda d_of_step: pl.BlockSpec((None, None, BLK, BLK), lambda h, s: (d_of_step(s), h, 0, 0))
    return pl.pallas_call(
        _bias_kernel,
        out_shape=(jax.ShapeDtypeStruct((RAW_TILES, REL_HEADS, BLK, BLK), F32),
                   jax.ShapeDtypeStruct((SEL_TILES, REL_HEADS, BLK, BLK), F32),
                   jax.ShapeDtypeStruct((REL_HEADS, BLK, BLK), F32)),
        grid=(REL_HEADS, SEL_TILES),
        in_specs=[pl.BlockSpec(memory_space=pltpu.SMEM)],
        out_specs=(tile(_raw_tile_index), tile(lambda s: s), pl.BlockSpec((None, BLK, BLK), lambda h, s: (h, 0, 0))),
        compiler_params=_cparams(("arbitrary", "arbitrary")),
        name="bias_tiles",
    )(rel_table)


SB_BQ = 256


def _sb_kernel(q_ref, k_ref, v_ref, o_ref):
    qb = pl.program_id(1)
    q = q_ref[...]
    row = lax.broadcasted_iota(jnp.int32, (SB_BQ, SB_BQ), 0)
    col = lax.broadcasted_iota(jnp.int32, (SB_BQ, SB_BQ), 1)
    upper = (row > col).astype(BF16)

    def block(kb, past, carry):
        ks = pl.multiple_of(kb * SB_BQ, SB_BQ)
        z = _dot_nt(q, k_ref[pl.ds(ks, SB_BQ), :]) * SCALE
        sp = jnp.maximum(z, 0.0) + jnp.log(1.0 + jnp.exp(-jnp.abs(z)))
        log_stay = -sp
        if past is not None:
            log_stay = jnp.where(past, log_stay, 0.0)
        hi = log_stay.astype(BF16)
        lo = (log_stay - hi.astype(F32)).astype(BF16)
        later = _dot(hi, upper) + _dot(lo, upper)
        w = jnp.exp((z - sp) + later + carry)
        if past is not None:
            w = jnp.where(past, w, 0.0)
        contrib = _dot(w.astype(BF16), v_ref[pl.ds(ks, SB_BQ), :])
        return contrib, jnp.sum(log_stay, axis=-1, keepdims=True)

    acc0, c0 = block(qb, col < row, jnp.zeros((SB_BQ, 1), F32))

    def cond(state):
        kb, c, _ = state
        return jnp.logical_and(kb >= 0, jnp.max(c) > SB_SKIP_LOG)

    def body(state):
        kb, c, acc = state
        contrib, tot = block(kb, None, c)
        return kb - 1, c + tot, acc + contrib

    _, _, acc = lax.while_loop(cond, body, (qb - 1, c0, acc0))
    o_ref[...] = acc.astype(o_ref.dtype)


def sb_attention(z, t):
    return pl.pallas_call(
        _sb_kernel,
        out_shape=jax.ShapeDtypeStruct((t, SB_HEADS * HEAD_DIM), BF16),
        grid=(SB_HEADS, t // SB_BQ),
        in_specs=[pl.BlockSpec((SB_BQ, HEAD_DIM), lambda h, i: (i, AB_Q_SB + h)),
                  pl.BlockSpec((t, HEAD_DIM), lambda h, i: (0, AB_K_SB + h)),
                  pl.BlockSpec((t, HEAD_DIM), lambda h, i: (0, AB_V_SB + h))],
        out_specs=pl.BlockSpec((SB_BQ, HEAD_DIM), lambda h, i: (i, h)),
        compiler_params=_cparams(("parallel", "arbitrary")),
        name="sb_attention",
    )(z, z, z)


def _compress_kernel(blk_ref, pe_ref, w1_ref, b1_ref, w2_ref, o_ref, *, n_rows):
    x = (blk_ref[...].astype(F32) + pe_ref[...]).astype(BF16)
    hid = _gelu_tanh(_dot(x, w1_ref[...].astype(BF16)) + b1_ref[...])
    out = _dot(hid.astype(BF16), w2_ref[...].astype(BF16))
    o_ref[0:CMP_PAD, :] = jnp.zeros((CMP_PAD, HEAD_DIM), o_ref.dtype)
    o_ref[CMP_PAD:CMP_PAD + n_rows, :] = out.astype(o_ref.dtype)
    o_ref[CMP_PAD + n_rows:, :] = jnp.zeros((o_ref.shape[0] - CMP_PAD - n_rows, HEAD_DIM), o_ref.dtype)


def compress(z, col_block, pe, w1, b1, w2, t):
    n_chunk = t // NSA_CMP_STRIDE
    a = z[:, col_block * LANE:(col_block + NSA_KV_HEADS) * LANE]
    a = a.reshape(n_chunk, NSA_CMP_STRIDE, NSA_KV_HEADS, HEAD_DIM).transpose(2, 0, 1, 3)
    a = a.reshape(NSA_KV_HEADS, n_chunk, NSA_CMP_STRIDE * HEAD_DIM)
    nxt = jnp.concatenate([a[:, 1:], jnp.zeros_like(a[:, :1])], axis=1)
    blk = jnp.concatenate([a, nxt], axis=-1)
    width = NSA_CMP_LEN * HEAD_DIM
    hidden = w1.shape[-1]
    rows = CMP_PAD + n_chunk + 8
    return pl.pallas_call(
        functools.partial(_compress_kernel, n_rows=n_chunk),
        out_shape=jax.ShapeDtypeStruct((NSA_KV_HEADS, rows, HEAD_DIM), BF16),
        grid=(NSA_KV_HEADS,),
        in_specs=[pl.BlockSpec((None, n_chunk, width), lambda h: (h, 0, 0)),
                  pl.BlockSpec((1, width), lambda h: (0, 0)),
                  pl.BlockSpec((width, hidden), lambda h: (0, 0)),
                  pl.BlockSpec((1, hidden), lambda h: (0, 0)),
                  pl.BlockSpec((hidden, HEAD_DIM), lambda h: (0, 0))],
        out_specs=pl.BlockSpec((None, rows, HEAD_DIM), lambda h: (h, 0, 0)),
        compiler_params=_cparams(("parallel",)),
        name="compress",
    )(blk, pe.reshape(1, width), w1, b1.reshape(1, hidden), w2)


def _cmp_to_sel_padded(t):
    n_cmp = (t - NSA_CMP_LEN) // NSA_CMP_STRIDE + 1
    n_sel = t // NSA_SEL_LEN
    cmp_end = np.arange(n_cmp) * NSA_CMP_STRIDE + NSA_CMP_LEN - 1
    cmp_start = cmp_end - (NSA_CMP_LEN - 1)
    sel_start = np.arange(n_sel) * NSA_SEL_LEN
    overlap = (np.minimum(cmp_end[:, None], sel_start[None, :] + NSA_SEL_LEN - 1)
               - np.maximum(cmp_start[:, None], sel_start[None, :]) + 1)
    c2s = np.clip(overlap, 0, None).astype(np.float32) / NSA_CMP_LEN
    out = np.zeros((CMP_PAD + n_cmp + 9, n_sel), np.float32)
    out[CMP_PAD:CMP_PAD + n_cmp] = c2s
    return out


def _cmp_kernel(q_ref, kc_ref, vc_ref, c2s_ref, bias_ref, o_ref, pen_ref, *, n_far):
    qb = pl.program_id(1)
    r0 = pl.multiple_of(qb * (BLK // NSA_CMP_STRIDE), 8)
    kc_far = kc_ref[0:n_far, :]
    kc_near = kc_ref[pl.ds(r0, BLK), :]
    vs_far = jnp.concatenate([vc_ref[0:n_far, :], c2s_ref[0:n_far, :]], axis=1)
    vs_near = jnp.concatenate([vc_ref[pl.ds(r0, BLK), :], c2s_ref[pl.ds(r0, BLK), :]], axis=1)
    r_far = lax.broadcasted_iota(jnp.int32, (BLK, n_far), 1)
    far_ok = jnp.logical_and(r_far >= CMP_PAD, r_far < r0)
    row = lax.broadcasted_iota(jnp.int32, (BLK, BLK), 0)
    col = lax.broadcasted_iota(jnp.int32, (BLK, BLK), 1)
    dist_c = row - NSA_CMP_STRIDE * (col - CMP_PAD) - (NSA_CMP_LEN - 1)
    near_ok = jnp.logical_and(dist_c >= 0, col + r0 >= CMP_PAD)

    n_sel = c2s_ref.shape[1]
    imp = jnp.zeros((BLK, n_sel), F32)
    for g in range(NSA_G):
        q = q_ref[:, g * HEAD_DIM:(g + 1) * HEAD_DIM]
        s_far = jnp.where(far_ok, _dot_nt(q, kc_far) * SCALE, NEG_INF)
        s_near = jnp.where(near_ok, _dot_nt(q, kc_near) * SCALE + bias_ref[g], NEG_INF)
        m = jnp.maximum(jnp.max(s_far, axis=-1, keepdims=True), jnp.max(s_near, axis=-1, keepdims=True))
        e_far = jnp.where(far_ok, jnp.exp(s_far - m), 0.0)
        e_near = jnp.where(near_ok, jnp.exp(s_near - m), 0.0)
        l = jnp.sum(e_far, axis=-1, keepdims=True) + jnp.sum(e_near, axis=-1, keepdims=True)
        inv = 1.0 / jnp.where(l > 0.0, l, 1.0)
        res = (_dot(e_far.astype(BF16), vs_far) + _dot(e_near.astype(BF16), vs_near)) * inv
        o_ref[:, g * HEAD_DIM:(g + 1) * HEAD_DIM] = res[:, :HEAD_DIM].astype(o_ref.dtype)
        imp = imp + res[:, HEAD_DIM:]

    blk = lax.broadcasted_iota(jnp.int32, (BLK, n_sel), 1)
    t_pos = qb * BLK + lax.broadcasted_iota(jnp.int32, (BLK, n_sel), 0)
    cur = t_pos // NSA_SEL_LEN
    forced = jnp.logical_or(blk == 0, jnp.logical_or(blk == cur, blk == cur - 1))
    score = jnp.where(forced, NSA_FORCE_SCORE, jnp.where(blk <= cur, imp, -1.0))
    chosen = jnp.zeros((BLK, n_sel), jnp.bool_)
    for _ in range(min(NSA_SEL_TOPK, n_sel)):
        top = jnp.max(score, axis=-1, keepdims=True)
        first = jnp.min(jnp.where(score == top, blk, n_sel), axis=-1, keepdims=True)
        hit = blk == first
        chosen = jnp.logical_or(chosen, hit)
        score = jnp.where(hit, -2.0, score)
    pen_ref[...] = jnp.where(chosen, 0.0, NEG_INF).astype(pen_ref.dtype)


def cmp_select(z, kc, vc, bias_cmp, t):
    n_sel = t // NSA_SEL_LEN
    assert n_sel == LANE, "selection mask rides in one 128-lane tile"
    c2s = jnp.asarray(_cmp_to_sel_padded(t), BF16)
    rows = kc.shape[1]
    assert c2s.shape[0] == rows
    n_far = t // NSA_CMP_STRIDE
    qw = NSA_G * HEAD_DIM
    return pl.pallas_call(
        functools.partial(_cmp_kernel, n_far=n_far),
        out_shape=(jax.ShapeDtypeStruct((t, NSA_HEADS * HEAD_DIM), BF16),
                   jax.ShapeDtypeStruct((NSA_KV_HEADS, t, n_sel), BF16)),
        grid=(NSA_KV_HEADS, t // BLK),
        in_specs=[pl.BlockSpec((BLK, qw), lambda h, i: (i, AB_Q_N // NSA_G + h)),
                  pl.BlockSpec((None, rows, HEAD_DIM), lambda h, i: (h, 0, 0)),
                  pl.BlockSpec((None, rows, HEAD_DIM), lambda h, i: (h, 0, 0)),
                  pl.BlockSpec((rows, n_sel), lambda h, i: (0, 0)),
                  pl.BlockSpec((NSA_G, BLK, BLK), lambda h, i: (h, 0, 0))],
        out_specs=(pl.BlockSpec((BLK, qw), lambda h, i: (i, h)),
                   pl.BlockSpec((None, BLK, n_sel), lambda h, i: (h, i, 0))),
        compiler_params=_cparams(("parallel", "arbitrary")),
        name="cmp_select",
    )(z, kc, vc, c2s, bias_cmp)


SEL_CHUNK = SEL_CHUNK_BLOCKS * BLK


def _sel_kernel(q_ref, pen_ref, kaug_ref, vaug_ref, bias_ref, o_ref, qa_ref, m_ref, acc_ref):
    qb = pl.program_id(1)
    pen = pen_ref[...]
    for g in range(NSA_G):
        rows = slice(g * BLK, (g + 1) * BLK)
        qa_ref[rows, :HEAD_DIM] = (q_ref[:, g * HEAD_DIM:(g + 1) * HEAD_DIM].astype(F32) * SCALE).astype(BF16)
        qa_ref[rows, HEAD_DIM:] = pen
    m_ref[...] = jnp.full(m_ref.shape, M_INIT, F32)
    acc_ref[...] = jnp.zeros(acc_ref.shape, F32)

    def chunk(c, with_bias):
        ks = pl.multiple_of(c * SEL_CHUNK, SEL_CHUNK)
        s_all = _dot_nt(qa_ref[...], kaug_ref[pl.ds(ks, SEL_CHUNK), :])
        va = vaug_ref[pl.ds(ks, SEL_CHUNK), :]
        for g in range(NSA_G):
            rows = slice(g * BLK, (g + 1) * BLK)
            tiles = [s_all[rows, j * BLK:(j + 1) * BLK] for j in range(SEL_CHUNK_BLOCKS)]
            if with_bias:
                first = qb - c * SEL_CHUNK_BLOCKS + SEL_NEG_BLOCKS
                tiles = [tile + bias_ref[first - j, g] for j, tile in enumerate(tiles)]
            m_old = m_ref[rows, :]
            m_new = jnp.maximum(m_old, jnp.max(functools.reduce(jnp.maximum, tiles), axis=-1, keepdims=True))
            alpha = jnp.exp(m_old - m_new)
            p = jnp.concatenate([jnp.exp(tile - m_new) for tile in tiles], axis=1).astype(BF16)
            acc_ref[rows, :] = jnp.concatenate([alpha, alpha], axis=1) * acc_ref[rows, :] + _dot(p, va)
            m_ref[rows, :] = m_new

    n_far = jnp.maximum(qb - (SEL_FAR_BLOCKS - 1), 0) // SEL_CHUNK_BLOCKS

    def far_body(c, carry):
        chunk(c, False)
        return carry

    def near_body(c, carry):
        chunk(c, True)
        return carry

    lax.fori_loop(0, n_far, far_body, 0)
    lax.fori_loop(n_far, qb // SEL_CHUNK_BLOCKS + 1, near_body, 0)

    for g in range(NSA_G):
        rows = slice(g * BLK, (g + 1) * BLK)
        o_ref[:, g * HEAD_DIM:(g + 1) * HEAD_DIM] = (acc_ref[rows, :HEAD_DIM] / acc_ref[rows, HEAD_DIM:]
                                                      ).astype(o_ref.dtype)


def sel_attention(z, pen, bias_sel, t):
    n_sel = t // NSA_SEL_LEN
    assert t % SEL_CHUNK == 0
    onehot = jnp.asarray(np.arange(t)[:, None] // NSA_SEL_LEN == np.arange(n_sel)[None, :], BF16)
    kv_cols = lambda blk: z[:, blk * LANE:(blk + NSA_KV_HEADS) * LANE].reshape(t, NSA_KV_HEADS, HEAD_DIM)
    side = lambda a: jnp.broadcast_to(a[:, None, :], (t, NSA_KV_HEADS, a.shape[1]))
    kaug = jnp.concatenate([kv_cols(AB_K_S), side(onehot)], axis=2).transpose(1, 0, 2)
    vaug = jnp.concatenate([kv_cols(AB_V_S), side(jnp.ones((t, HEAD_DIM), BF16))], axis=2).transpose(1, 0, 2)
    qw = NSA_G * HEAD_DIM
    rows = NSA_G * BLK
    return pl.pallas_call(
        _sel_kernel,
        out_shape=jax.ShapeDtypeStruct((t, NSA_HEADS * HEAD_DIM), BF16),
        grid=(NSA_KV_HEADS, t // BLK),
        in_specs=[pl.BlockSpec((BLK, qw), lambda h, i: (i, AB_Q_N // NSA_G + h)),
                  pl.BlockSpec((None, BLK, n_sel), lambda h, i: (h, i, 0)),
                  pl.BlockSpec((None, t, HEAD_DIM + n_sel), lambda h, i: (h, 0, 0)),
                  pl.BlockSpec((None, t, 2 * HEAD_DIM), lambda h, i: (h, 0, 0)),
                  pl.BlockSpec((SEL_TILES, NSA_G, BLK, BLK), lambda h, i: (0, h, 0, 0))],
        out_specs=pl.BlockSpec((BLK, qw), lambda h, i: (i, h)),
        scratch_shapes=[pltpu.VMEM((rows, HEAD_DIM + n_sel), BF16), pltpu.VMEM((rows, LANE), F32),
                        pltpu.VMEM((rows, 2 * HEAD_DIM), F32)],
        compiler_params=_cparams(("parallel", "arbitrary")),
        name="sel_attention",
    )(z, pen, kaug, vaug, bias_sel)


def _banded_kernel(*refs, group, n_prev, window, has_sinks):
    if has_sinks:
        sink_ref, q_ref, k_ref, v_ref, bias_ref, o_ref = refs
    else:
        q_ref, k_ref, v_ref, bias_ref, o_ref = refs
    kv = pl.program_id(0)
    qb = pl.program_id(1)
    row = lax.broadcasted_iota(jnp.int32, (BLK, BLK), 0)
    col = lax.broadcasted_iota(jnp.int32, (BLK, BLK), 1)
    starts, valids = [], []
    for delta in range(n_prev + 1):
        kb = qb - delta
        starts.append(pl.multiple_of(jnp.maximum(kb, 0) * BLK, BLK))
        dist = delta * BLK + row - col
        limit = jnp.where(kb >= 0, window, 0)
        valids.append(jnp.logical_and(dist >= 0, dist < limit))
    for g in range(group):
        q = q_ref[:, g * HEAD_DIM:(g + 1) * HEAD_DIM]
        s = [jnp.where(valids[d], _dot_nt(q, k_ref[pl.ds(starts[d], BLK), :]) * SCALE + bias_ref[d, g], NEG_INF)
             for d in range(n_prev + 1)]
        m = functools.reduce(jnp.maximum, [jnp.max(x, axis=-1, keepdims=True) for x in s])
        if has_sinks:
            sink = sink_ref[kv * group + g]
            m = jnp.maximum(m, sink)
        e = [jnp.exp(x - m) for x in s]
        l = functools.reduce(jnp.add, [jnp.sum(x, axis=-1, keepdims=True) for x in e])
        if has_sinks:
            l = l + jnp.exp(sink - m)
        o = functools.reduce(jnp.add, [_dot(e[d].astype(BF16), v_ref[pl.ds(starts[d], BLK), :])
                                       for d in range(n_prev + 1)])
        o_ref[:, g * HEAD_DIM:(g + 1) * HEAD_DIM] = (o / l).astype(o_ref.dtype)


def banded_attention(z, bias_raw, t, *, q_block, k_block, v_block, n_kv, group, window, sinks=None):
    n_prev = -(-window // BLK)
    qw = group * HEAD_DIM
    assert q_block % group == 0
    in_specs = [pl.BlockSpec((BLK, qw), lambda h, i: (i, q_block // group + h)),
                pl.BlockSpec((t, HEAD_DIM), lambda h, i: (0, k_block + h)),
                pl.BlockSpec((t, HEAD_DIM), lambda h, i: (0, v_block + h)),
                pl.BlockSpec((n_prev + 1, group, BLK, BLK), lambda h, i: (0, h, 0, 0))]
    args = [z, z, z, bias_raw]
    if sinks is not None:
        in_specs.insert(0, pl.BlockSpec(memory_space=pltpu.SMEM))
        args.insert(0, sinks)
    return pl.pallas_call(
        functools.partial(_banded_kernel, group=group, n_prev=n_prev, window=window, has_sinks=sinks is not None),
        out_shape=jax.ShapeDtypeStruct((t, n_kv * qw), BF16),
        grid=(n_kv, t // BLK),
        in_specs=in_specs,
        out_specs=pl.BlockSpec((BLK, qw), lambda h, i: (i, h)),
        compiler_params=_cparams(("parallel", "arbitrary")),
        name="banded_attention",
    )(*args)


def _combine_kernel(g_ref, c_ref, s_ref, w_ref, o_ref):
    gates = _sigmoid(g_ref[...])
    for h in range(NSA_HEADS):
        sl = slice(h * HEAD_DIM, (h + 1) * HEAD_DIM)
        o = (gates[:, 3 * h:3 * h + 1] * c_ref[:, sl].astype(F32)
             + gates[:, 3 * h + 1:3 * h + 2] * s_ref[:, sl].astype(F32)
             + gates[:, 3 * h + 2:3 * h + 3] * w_ref[:, sl].astype(F32))
        o_ref[:, sl] = o.astype(o_ref.dtype)


def nsa_combine(gate_logits, o_cmp, o_sel, o_win):
    t, w = o_cmp.shape
    tm = 256
    row = lambda width: pl.BlockSpec((tm, width), lambda i: (i, 0))
    return pl.pallas_call(
        _combine_kernel,
        out_shape=jax.ShapeDtypeStruct((t, w), BF16),
        grid=(t // tm,),
        in_specs=[row(LANE), row(w), row(w), row(w)],
        out_specs=row(w),
        compiler_params=_cparams(("parallel",)),
        name="nsa_combine",
    )(gate_logits, o_cmp, o_sel, o_win)


def _sg_kernel(u_ref, v_ref, lng_ref, lnb_ref, w_ref, b_ref, o_ref):
    u = _gelu_tanh(u_ref[...].astype(F32))
    gv = _gelu_tanh(v_ref[...].astype(F32))
    mu = jnp.mean(gv, axis=-1, keepdims=True)
    cen = gv - mu
    var = jnp.mean(cen * cen, axis=-1, keepdims=True)
    gv = (cen * lax.rsqrt(var + RMS_EPS) * lng_ref[...] + lnb_ref[...]).astype(BF16)
    row = lax.broadcasted_iota(jnp.int32, (SG_CHUNK, SG_CHUNK), 0)
    col = lax.broadcasted_iota(jnp.int32, (SG_CHUNK, SG_CHUNK), 1)
    gw = u.shape[1] // SG_GROUPS
    for g in range(SG_GROUPS):
        sl = slice(g * gw, (g + 1) * gw)
        w = jnp.where(row >= col, w_ref[g], 0.0).astype(BF16)
        spatial = _dot(w, gv[:, sl]) + b_ref[:, g:g + 1]
        o_ref[:, sl] = (u[:, sl] * spatial).astype(o_ref.dtype)


def spatial_gating(z1, ln_g, ln_b, sg_w, sg_b, t):
    width = ln_g.shape[0]
    nblk = width // LANE
    assert width % (SG_GROUPS * LANE) == 0
    return pl.pallas_call(
        _sg_kernel,
        out_shape=jax.ShapeDtypeStruct((t, width), BF16),
        grid=(t // SG_CHUNK,),
        in_specs=[pl.BlockSpec((SG_CHUNK, width), lambda i: (i, 0)),
                  pl.BlockSpec((SG_CHUNK, width), lambda i: (i, 1)),
                  pl.BlockSpec((1, width), lambda i: (0, 0)),
                  pl.BlockSpec((1, width), lambda i: (0, 0)),
                  pl.BlockSpec((SG_GROUPS, SG_CHUNK, SG_CHUNK), lambda i: (0, 0, 0)),
                  pl.BlockSpec((SG_CHUNK, SG_GROUPS), lambda i: (0, 0))],
        out_specs=pl.BlockSpec((SG_CHUNK, width), lambda i: (i, 0)),
        compiler_params=_cparams(("parallel",)),
        name="spatial_gating",
    )(z1, z1, ln_g.reshape(1, width), ln_b.reshape(1, width), sg_w, jnp.transpose(sg_b))


MOE_TM = 512
MOE_TOP_K = 2


def moe_plan(idx, gate, t):
    n_rows = MOE_TOP_K * t + N_EXPERTS * MOE_TM
    n_blocks = n_rows // MOE_TM
    pair_e = idx[:, :MOE_TOP_K].reshape(-1)
    pair_w = gate[:, :MOE_TOP_K].reshape(-1)
    onehot = (pair_e[:, None] == jnp.arange(N_EXPERTS, dtype=jnp.int32)[None, :]).astype(jnp.int32)
    before = jnp.cumsum(onehot, axis=0) - onehot
    count = jnp.sum(onehot, axis=0)
    padded = (count + MOE_TM - 1) // MOE_TM * MOE_TM
    group_end = jnp.cumsum(padded)
    group_start = group_end - padded
    pair_row = jnp.sum(onehot * (before + group_start[None, :]), axis=1)
    row_tok = jnp.zeros((n_rows,), jnp.int32).at[pair_row].set(jnp.arange(MOE_TOP_K * t, dtype=jnp.int32) // MOE_TOP_K)
    row_w = jnp.zeros((n_rows,), F32).at[pair_row].set(pair_w)
    block_start = jnp.arange(n_blocks, dtype=jnp.int32) * MOE_TM
    block_e = jnp.minimum(jnp.sum(block_start[:, None] >= group_end[None, :], axis=1), N_EXPERTS - 1).astype(jnp.int32)
    n_used = (group_end[-1] // MOE_TM).astype(jnp.int32).reshape(1)
    return row_tok, row_w.reshape(n_rows, 1), block_e, n_used, pair_row.astype(jnp.int32)


def _gather_rows_kernel(tok_ref, src_ref, dst_ref, sem, *, rows_per_step):
    base = pl.program_id(0) * rows_per_step

    def row_copy(k):
        return pltpu.make_async_copy(src_ref.at[tok_ref[k]], dst_ref.at[base + k], sem)

    def start(k, carry):
        row_copy(k).start()
        return carry

    def wait(k, carry):
        row_copy(k).wait()
        return carry

    lax.fori_loop(0, rows_per_step, start, 0)
    lax.fori_loop(0, rows_per_step, wait, 0)


def gather_rows(src, row_tok):
    n_rows = row_tok.shape[0]
    rows_per_step = MOE_TM
    return pl.pallas_call(
        functools.partial(_gather_rows_kernel, rows_per_step=rows_per_step),
        out_shape=jax.ShapeDtypeStruct((n_rows,) + src.shape[1:], src.dtype),
        grid=(n_rows // rows_per_step,),
        in_specs=[pl.BlockSpec((rows_per_step,), lambda b: (b,), memory_space=pltpu.SMEM),
                  pl.BlockSpec(memory_space=pl.ANY)],
        out_specs=pl.BlockSpec(memory_space=pl.ANY),
        scratch_shapes=[pltpu.SemaphoreType.DMA(())],
        compiler_params=_cparams(("arbitrary",)),
        name="moe_gather",
    )(row_tok, src)


def _expert_changed(b, block_e_ref):
    prev = jnp.maximum(b - 1, 0)
    return jnp.logical_or(b == 0, block_e_ref[b] != block_e_ref[prev])


def _moe_up_kernel(block_e_ref, n_used_ref, x_ref, wg_ref, wu_ref, o_ref, wg_bf, wu_bf):
    b = pl.program_id(1)
    live = b < n_used_ref[0]

    @pl.when(jnp.logical_and(live, _expert_changed(b, block_e_ref)))
    def _():
        wg_bf[...] = wg_ref[...].astype(BF16)
        wu_bf[...] = wu_ref[...].astype(BF16)

    @pl.when(live)
    def _():
        x = x_ref[...]
        gate = _dot(x, wg_bf[...])
        o_ref[...] = ((gate * _sigmoid(gate)) * _dot(x, wu_bf[...])).astype(o_ref.dtype)

    @pl.when(jnp.logical_not(live))
    def _():
        o_ref[...] = jnp.zeros(o_ref.shape, o_ref.dtype)


def _moe_down_kernel(block_e_ref, n_used_ref, h_ref, wd_ref, w_ref, o_ref, wd_bf):
    b = pl.program_id(1)
    live = b < n_used_ref[0]

    @pl.when(jnp.logical_and(live, _expert_changed(b, block_e_ref)))
    def _():
        wd_bf[...] = wd_ref[...].astype(BF16)

    @pl.when(live)
    def _():
        o_ref[...] = w_ref[...] * _dot(h_ref[...], wd_bf[...])

    @pl.when(jnp.logical_not(live))
    def _():
        o_ref[...] = jnp.zeros(o_ref.shape, o_ref.dtype)


def moe_experts(xs, row_w, block_e, n_used, w_gate, w_up, w_down):
    n_rows, d = xs.shape
    d_ff = w_gate.shape[3]
    n_blocks = n_rows // MOE_TM
    tn_up, tn_down = 256, 512
    blk = lambda b, nu: jnp.minimum(b, nu[0] - 1)
    hid = pl.pallas_call(
        _moe_up_kernel,
        out_shape=jax.ShapeDtypeStruct((n_rows, d_ff), BF16),
        grid_spec=pltpu.PrefetchScalarGridSpec(
            num_scalar_prefetch=2,
            grid=(d_ff // tn_up, n_blocks),
            in_specs=[pl.BlockSpec((MOE_TM, d), lambda j, b, be, nu: (blk(b, nu), 0)),
                      pl.BlockSpec((None, None, d, tn_up), lambda j, b, be, nu: (0, be[blk(b, nu)], 0, j)),
                      pl.BlockSpec((None, None, d, tn_up), lambda j, b, be, nu: (0, be[blk(b, nu)], 0, j))],
            out_specs=pl.BlockSpec((MOE_TM, tn_up), lambda j, b, be, nu: (b, j)),
            scratch_shapes=[pltpu.VMEM((d, tn_up), BF16), pltpu.VMEM((d, tn_up), BF16)]),
        compiler_params=_cparams(("arbitrary", "arbitrary")),
        name="moe_up",
    )(block_e, n_used, xs, w_gate, w_up)
    return pl.pallas_call(
        _moe_down_kernel,
        out_shape=jax.ShapeDtypeStruct((n_rows, d), F32),
        grid_spec=pltpu.PrefetchScalarGridSpec(
            num_scalar_prefetch=2,
            grid=(d // tn_down, n_blocks),
            in_specs=[pl.BlockSpec((MOE_TM, d_ff), lambda j, b, be, nu: (blk(b, nu), 0)),
                      pl.BlockSpec((None, None, d_ff, tn_down), lambda j, b, be, nu: (0, be[blk(b, nu)], 0, j)),
                      pl.BlockSpec((MOE_TM, 1), lambda j, b, be, nu: (blk(b, nu), 0))],
            out_specs=pl.BlockSpec((MOE_TM, tn_down), lambda j, b, be, nu: (b, j)),
            scratch_shapes=[pltpu.VMEM((d_ff, tn_down), BF16)]),
        compiler_params=_cparams(("arbitrary", "arbitrary")),
        name="moe_down",
    )(block_e, n_used, hid, w_down, row_w)


MOE_TC = 128


def _moe_combine_kernel(rows_ref, x_ref, g_ref, y_ref, o_ref, buf, sem):
    def row_copy(k, choice):
        row = rows_ref[MOE_TOP_K * k + choice]
        return pltpu.make_async_copy(y_ref.at[pl.ds(row, 1), :], buf.at[choice, pl.ds(k, 1), :], sem)

    def start(k, carry):
        for choice in range(MOE_TOP_K):
            row_copy(k, choice).start()
        return carry

    def wait(k, carry):
        for choice in range(MOE_TOP_K):
            row_copy(k, choice).wait()
        return carry

    lax.fori_loop(0, MOE_TC, start, 0)
    lax.fori_loop(0, MOE_TC, wait, 0)
    x = x_ref[...]
    for choice in range(MOE_TOP_K):
        x = x + buf[choice]
    y = x * lax.rsqrt(jnp.mean(x * x, axis=-1, keepdims=True) + RMS_EPS)
    o_ref[...] = (y * g_ref[...]).astype(o_ref.dtype)


def moe_combine_rmsnorm(x, ys, pair_row, g):
    t, d = x.shape
    return pl.pallas_call(
        _moe_combine_kernel,
        out_shape=jax.ShapeDtypeStruct((t, d), F32),
        grid=(t // MOE_TC,),
        in_specs=[pl.BlockSpec((MOE_TOP_K * MOE_TC,), lambda i: (i,), memory_space=pltpu.SMEM),
                  pl.BlockSpec((MOE_TC, d), lambda i: (i, 0)),
                  pl.BlockSpec((1, d), lambda i: (0, 0)),
                  pl.BlockSpec(memory_space=pl.ANY)],
        out_specs=pl.BlockSpec((MOE_TC, d), lambda i: (i, 0)),
        scratch_shapes=[pltpu.VMEM((MOE_TOP_K, MOE_TC, d), F32), pltpu.SemaphoreType.DMA(())],
        compiler_params=_cparams(("arbitrary",)),
        name="moe_combine",
    )(pair_row, x, g.reshape(1, d), ys)


def kernel(x, rel_table, norm_mix, norm_ffn, norm_final, ab_w_in, ab_w_out, cmp_k_pe, cmp_k_w1, cmp_k_b1, cmp_k_w2, cmp_v_pe, cmp_v_w1, cmp_v_b1, cmp_v_w2, ffn_w_gate, ffn_w_up, ffn_w_down, cd_w_in, cd_w_out, sg_ln_g, sg_ln_b, sg_w, sg_b, swa_sinks, moe_w_router, moe_w_gate, moe_w_up, moe_w_down):
    b, t, d = x.shape
    assert b == 1
    xs = x.reshape(t, d)
    bias_raw, bias_sel, bias_cmp = bias_tiles(rel_table)

    h = rmsnorm(xs, norm_mix[0], BF16)
    ab_main = AB_MAIN_BLOCKS * LANE
    z = matmul(h, [ab_w_in], (0,), ab_main, out_dtype=BF16)
    w_gate_cols = jnp.pad(ab_w_in[0, :, ab_main:], ((0, 0), (0, LANE - (ab_w_in.shape[2] - ab_main))))
    gate_logits = matmul(h, [w_gate_cols], (), LANE, out_dtype=F32)
    o_sb = sb_attention(z, t)
    kc = compress(z, AB_K_C, cmp_k_pe[0], cmp_k_w1[0], cmp_k_b1[0], cmp_k_w2[0], t)
    vc = compress(z, AB_V_C, cmp_v_pe[0], cmp_v_w1[0], cmp_v_b1[0], cmp_v_w2[0], t)
    o_cmp, pen = cmp_select(z, kc, vc, bias_cmp, t)
    o_sel = sel_attention(z, pen, bias_sel, t)
    o_win = banded_attention(z, bias_raw, t, q_block=AB_Q_N, k_block=AB_K_W, v_block=AB_V_W,
                             n_kv=NSA_KV_HEADS, group=NSA_G, window=NSA_WINDOW)
    o_nsa = nsa_combine(gate_logits, o_cmp, o_sel, o_win)
    mixed = jnp.concatenate([o_sb, o_nsa], axis=1)
    xs = matmul(mixed, [ab_w_out], (0,), d, out_dtype=F32, res=xs)
    h = rmsnorm(xs, norm_ffn[0], BF16)
    hid = matmul(h, [ffn_w_gate, ffn_w_up], (0,), ffn_w_gate.shape[2], out_dtype=BF16, act="swiglu")
    xs = matmul(hid, [ffn_w_down], (0,), d, out_dtype=F32, res=xs)

    h = rmsnorm(xs, norm_mix[1], BF16)
    z1 = matmul(h, [cd_w_in], (0,), cd_w_in.shape[2], out_dtype=BF16)
    o_sg = spatial_gating(z1, sg_ln_g[0], sg_ln_b[0], sg_w[0], sg_b[0], t)
    o_swa = banded_attention(z1, bias_raw, t, q_block=CD_Q, k_block=CD_K, v_block=CD_V,
                             n_kv=SWA_KV_HEADS, group=SWA_HEADS // SWA_KV_HEADS, window=SWA_WINDOW,
                             sinks=swa_sinks[0])
    mixed = jnp.concatenate([o_sg, o_swa], axis=1)
    xs = matmul(mixed, [cd_w_out], (0,), d, out_dtype=F32, res=xs)
    h, gate, idx = rmsnorm_router(xs, norm_ffn[1], moe_w_router[0])
    row_tok, row_w, block_e, n_used, pair_row = moe_plan(idx, gate, t)
    rows = gather_rows(h.reshape(t, d // LANE, LANE), row_tok).reshape(-1, d)
    ys = moe_experts(rows, row_w, block_e, n_used, moe_w_gate, moe_w_up, moe_w_down)
    return moe_combine_rmsnorm(xs, ys, pair_row, norm_final).reshape(b, t, d)
```

```python
import functools
import math

import numpy as np
import jax
import jax.numpy as jnp
from jax import lax
from jax.experimental import pallas as pl
from jax.experimental.pallas import tpu as pltpu

F32 = jnp.float32
BF16 = jnp.bfloat16

LANE = 128
VMEM_LIMIT = 56 * 1024 * 1024

HEAD_DIM = 128
BLK = 128
NEG_INF = -1e30
RMS_EPS = 1e-6
SCALE = HEAD_DIM ** -0.5

SB_HEADS = 16
NSA_HEADS = 16
NSA_KV_HEADS = 4
NSA_G = NSA_HEADS // NSA_KV_HEADS
NSA_CMP_LEN = 32
NSA_CMP_STRIDE = 16
NSA_SEL_LEN = 64
NSA_SEL_TOPK = 16
NSA_WINDOW = 512
NSA_FORCE_SCORE = 1e6
SG_GROUPS = 8
SG_CHUNK = 128
SWA_HEADS = 16
SWA_KV_HEADS = 2
SWA_WINDOW = 128
REL_BUCKETS = 32
REL_MAX_DIST = 1024
REL_HEADS = 16
N_EXPERTS = 8

AB_Q_SB, AB_K_SB, AB_V_SB = 0, 16, 32
AB_Q_N = 48
AB_K_C, AB_V_C, AB_K_S, AB_V_S, AB_K_W, AB_V_W = 64, 68, 72, 76, 80, 84
AB_MAIN_BLOCKS = 88
CD_Q, CD_K, CD_V = 32, 48, 50

SEL_CHUNK_BLOCKS = 4
SEL_FAR_BLOCKS = 8
SEL_NEG_BLOCKS = SEL_CHUNK_BLOCKS - 1
SEL_TILES = SEL_NEG_BLOCKS + SEL_FAR_BLOCKS + SEL_CHUNK_BLOCKS - 1
RAW_TILES = NSA_WINDOW // BLK + 1
CMP_PAD = 120
SB_SKIP_LOG = -110.0
M_INIT = -5e29


def _cparams(sem):
    return pltpu.CompilerParams(dimension_semantics=sem, vmem_limit_bytes=VMEM_LIMIT)


def _dot(a, b):
    return jnp.dot(a, b, preferred_element_type=F32)


def _dot_nt(a, b):
    return lax.dot_general(a, b, (((1,), (1,)), ((), ())), preferred_element_type=F32)


def _sigmoid(x):
    return 1.0 / (1.0 + jnp.exp(-x))


def _gelu_tanh(x):
    return 0.5 * x * (1.0 + jnp.tanh(math.sqrt(2.0 / math.pi) * (x + 0.044715 * (x * x * x))))


def _rmsnorm_kernel(x_ref, g_ref, o_ref):
    x = x_ref[...]
    y = x * lax.rsqrt(jnp.mean(x * x, axis=-1, keepdims=True) + RMS_EPS)
    o_ref[...] = (y * g_ref[...]).astype(o_ref.dtype)


def rmsnorm(x, g, out_dtype):
    t, d = x.shape
    tm = 256
    return pl.pallas_call(
        _rmsnorm_kernel,
        out_shape=jax.ShapeDtypeStruct((t, d), out_dtype),
        grid=(t // tm,),
        in_specs=[pl.BlockSpec((tm, d), lambda i: (i, 0)), pl.BlockSpec((1, d), lambda i: (0, 0))],
        out_specs=pl.BlockSpec((tm, d), lambda i: (i, 0)),
        compiler_params=_cparams(("parallel",)),
        name="rmsnorm",
    )(x, g.reshape(1, d))


def _rmsnorm_router_kernel(x_ref, g_ref, wr_ref, o_ref, gate_ref, idx_ref):
    x = x_ref[...]
    y = x * lax.rsqrt(jnp.mean(x * x, axis=-1, keepdims=True) + RMS_EPS)
    y = y * g_ref[...]
    o_ref[...] = y.astype(o_ref.dtype)
    logits = jnp.dot(y, wr_ref[...], preferred_element_type=F32, precision=lax.Precision.HIGHEST)
    lane = lax.broadcasted_iota(jnp.int32, logits.shape, 1)
    logits = jnp.where(lane < N_EXPERTS, logits, NEG_INF)
    m1 = jnp.max(logits, axis=-1, keepdims=True)
    i1 = jnp.min(jnp.where(logits == m1, lane, LANE), axis=-1, keepdims=True)
    rest = jnp.where(lane == i1, NEG_INF, logits)
    m2 = jnp.max(rest, axis=-1, keepdims=True)
    i2 = jnp.min(jnp.where(rest == m2, lane, LANE), axis=-1, keepdims=True)
    e2 = jnp.exp(m2 - m1)
    denom = 1.0 + e2
    gate_ref[...] = jnp.where(lane == 0, 1.0 / denom, jnp.where(lane == 1, e2 / denom, 0.0))
    idx_ref[...] = jnp.where(lane == 0, i1, jnp.where(lane == 1, i2, 0))


def rmsnorm_router(x, g, w_router):
    t, d = x.shape
    tm = 256
    wr = jnp.pad(w_router, ((0, 0), (0, LANE - w_router.shape[1])))
    return pl.pallas_call(
        _rmsnorm_router_kernel,
        out_shape=(jax.ShapeDtypeStruct((t, d), F32), jax.ShapeDtypeStruct((t, LANE), F32),
                   jax.ShapeDtypeStruct((t, LANE), jnp.int32)),
        grid=(t // tm,),
        in_specs=[pl.BlockSpec((tm, d), lambda i: (i, 0)), pl.BlockSpec((1, d), lambda i: (0, 0)),
                  pl.BlockSpec((d, LANE), lambda i: (0, 0))],
        out_specs=(pl.BlockSpec((tm, d), lambda i: (i, 0)), pl.BlockSpec((tm, LANE), lambda i: (i, 0)),
                   pl.BlockSpec((tm, LANE), lambda i: (i, 0))),
        compiler_params=_cparams(("parallel",)),
        name="rmsnorm_router",
    )(x, g.reshape(1, d), wr)


def _mm_epilogue(parts, res_ref, o_ref, act):
    r = parts[0]
    if act == "swiglu":
        r = (r * _sigmoid(r)) * parts[1]
    if res_ref is not None:
        r = res_ref[...] + r
    o_ref[...] = r.astype(o_ref.dtype)


def _mm_resident_kernel(*refs, n_w, has_res, act):
    a_ref = refs[0]
    w_refs = refs[1:1 + n_w]
    res_ref = refs[1 + n_w] if has_res else None
    o_ref = refs[1 + n_w + int(has_res)]
    wbf_refs = refs[2 + n_w + int(has_res):]

    @pl.when(pl.program_id(1) == 0)
    def _():
        for w, wbf in zip(w_refs, wbf_refs):
            wbf[...] = w[...].astype(BF16)

    a = a_ref[...]
    _mm_epilogue([_dot(a, wbf[...]) for wbf in wbf_refs], res_ref, o_ref, act)


def _mm_ktiled_kernel(a_ref, w_ref, res_ref, o_ref, acc_ref, *, nk):
    k = pl.program_id(2)
    part = _dot(a_ref[...], w_ref[...].astype(BF16))

    @pl.when(k == 0)
    def _():
        acc_ref[...] = part

    @pl.when(k > 0)
    def _():
        acc_ref[...] += part

    @pl.when(k == nk - 1)
    def _():
        _mm_epilogue([acc_ref[...]], res_ref, o_ref, None)


MM_RESIDENT_MAX_K = 4096


def matmul(a, ws, lead, n_cols, *, out_dtype, res=None, act=None):
    m, kdim = a.shape
    lead = tuple(lead)
    squeeze = (None,) * len(lead)
    if kdim <= MM_RESIDENT_MAX_K:
        tm = 1024
        tn = min(n_cols, 512 // len(ws))
        assert m % tm == 0 and n_cols % tn == 0
        in_specs = [pl.BlockSpec((tm, kdim), lambda j, i: (i, 0))]
        in_specs += [pl.BlockSpec(squeeze + (kdim, tn), lambda j, i: lead + (0, j)) for _ in ws]
        args = [a, *ws]
        if res is not None:
            in_specs.append(pl.BlockSpec((tm, tn), lambda j, i: (i, j)))
            args.append(res)
        return pl.pallas_call(
            functools.partial(_mm_resident_kernel, n_w=len(ws), has_res=res is not None, act=act),
            out_shape=jax.ShapeDtypeStruct((m, n_cols), out_dtype),
            grid=(n_cols // tn, m // tm),
            in_specs=in_specs,
            out_specs=pl.BlockSpec((tm, tn), lambda j, i: (i, j)),
            scratch_shapes=[pltpu.VMEM((kdim, tn), BF16) for _ in ws],
            compiler_params=_cparams(("parallel", "arbitrary")),
            name="matmul_resident",
        )(*args)
    assert len(ws) == 1 and act is None and res is not None
    tm, tn, tk = 2048, 512, 2048
    assert m % tm == 0 and n_cols % tn == 0 and kdim % tk == 0
    nk = kdim // tk
    return pl.pallas_call(
        functools.partial(_mm_ktiled_kernel, nk=nk),
        out_shape=jax.ShapeDtypeStruct((m, n_cols), out_dtype),
        grid=(n_cols // tn, m // tm, nk),
        in_specs=[pl.BlockSpec((tm, tk), lambda j, i, k: (i, k)),
                  pl.BlockSpec(squeeze + (tk, tn), lambda j, i, k: lead + (k, j)),
                  pl.BlockSpec((tm, tn), lambda j, i, k: (i, j))],
        out_specs=pl.BlockSpec((tm, tn), lambda j, i, k: (i, j)),
        scratch_shapes=[pltpu.VMEM((tm, tn), F32)],
        compiler_params=_cparams(("parallel", "parallel", "arbitrary")),
        name="matmul_ktiled",
    )(a, ws[0], res)


def _bucket_thresholds():
    d = np.arange(0, 4 * REL_MAX_DIST)
    max_exact = REL_BUCKETS // 2
    log_ratio = np.log(np.maximum(d, 1).astype(np.float64) / max_exact) / math.log(REL_MAX_DIST / max_exact)
    far = np.minimum(max_exact + (log_ratio * (REL_BUCKETS - max_exact)).astype(np.int32), REL_BUCKETS - 1)
    bucket = np.where(d < max_exact, d, far)
    assert np.all(np.diff(bucket) >= 0) and bucket[-1] == REL_BUCKETS - 1
    return [int(np.argmax(bucket >= b)) for b in range(REL_BUCKETS)]


_BUCKET_THR = _bucket_thresholds()
REL_FAR_DIST = _BUCKET_THR[REL_BUCKETS - 1]
assert REL_FAR_DIST <= SEL_FAR_BLOCKS * BLK - (BLK - 1)


def _raw_tile_index(step):
    return jnp.clip(step - SEL_NEG_BLOCKS, 0, RAW_TILES - 1)
assert REL_FAR_DIST <= NSA_CMP_STRIDE * (CMP_PAD + 1) - (NSA_CMP_LEN - 1)


def _bias_kernel(tab_ref, raw_ref, sel_ref, cmp_ref):
    h = pl.program_id(0)
    step = pl.program_id(1)
    row = lax.broadcasted_iota(jnp.int32, (BLK, BLK), 0)
    col = lax.broadcasted_iota(jnp.int32, (BLK, BLK), 1)

    def lookup(dist):
        dd = jnp.maximum(dist, 0)
        b = jnp.full(dist.shape, tab_ref[0, h], F32)
        for bk in range(1, REL_BUCKETS):
            b = jnp.where(dd >= _BUCKET_THR[bk], tab_ref[bk, h], b)
        return b

    far = tab_ref[REL_BUCKETS - 1, h]
    raw_ref[...] = lookup(_raw_tile_index(step) * BLK + row - col)
    dist = (step - SEL_NEG_BLOCKS) * BLK + row - col
    sel_ref[...] = jnp.where(dist < 0, NEG_INF, lookup(dist) - far)

    @pl.when(step == 0)
    def _():
        dist_c = row - NSA_CMP_STRIDE * (col - CMP_PAD) - (NSA_CMP_LEN - 1)
        cmp_ref[...] = lookup(dist_c) - far


def bias_tiles(rel_table):
    tile = lambda d_of_step: pl.BlockSpec((None, None, BLK, BLK), lambda h, s: (d_of_step(s), h, 0, 0))
    return pl.pallas_call(
        _bias_kernel,
        out_shape=(jax.ShapeDtypeStruct((RAW_TILES, REL_HEADS, BLK, BLK), F32),
                   jax.ShapeDtypeStruct((SEL_TILES, REL_HEADS, BLK, BLK), F32),
                   jax.ShapeDtypeStruct((REL_HEADS, BLK, BLK), F32)),
        grid=(REL_HEADS, SEL_TILES),
        in_specs=[pl.BlockSpec(memory_space=pltpu.SMEM)],
        out_specs=(tile(_raw_tile_index), tile(lambda s: s), pl.BlockSpec((None, BLK, BLK), lambda h, s: (h, 0, 0))),
        compiler_params=_cparams(("arbitrary", "arbitrary")),
        name="bias_tiles",
    )(rel_table)


SB_BQ = 256


SB_GROUP = 4


def _sb_kernel(q_ref, k_ref, v_ref, o_ref, c_ref, acc_ref):
    qb = pl.program_id(1)
    row = lax.broadcasted_iota(jnp.int32, (SB_BQ, SB_BQ), 0)
    col = lax.broadcasted_iota(jnp.int32, (SB_BQ, SB_BQ), 1)
    upper = (row > col).astype(BF16)

    def block(h, kb, past):
        cols = slice(h * HEAD_DIM, (h + 1) * HEAD_DIM)
        ks = pl.multiple_of(kb * SB_BQ, SB_BQ)
        z = _dot_nt(q_ref[:, cols], k_ref[pl.ds(ks, SB_BQ), cols]) * SCALE
        sp = jnp.maximum(z, 0.0) + jnp.log(1.0 + jnp.exp(-jnp.abs(z)))
        log_stay = -sp
        if past is not None:
            log_stay = jnp.where(past, log_stay, 0.0)
        hi = log_stay.astype(BF16)
        lo = (log_stay - hi.astype(F32)).astype(BF16)
        later = _dot(hi, upper) + _dot(lo, upper)
        tot = jnp.broadcast_to(jnp.sum(log_stay, axis=-1, keepdims=True), (SB_BQ, HEAD_DIM))
        if past is not None:
            w = jnp.where(past, jnp.exp((z - sp) + later), 0.0)
            c_new = tot
        else:
            carry = c_ref[:, cols]
            w = jnp.exp((z - sp) + later + jnp.concatenate([carry] * (SB_BQ // HEAD_DIM), axis=1))
            c_new = carry + tot
        contrib = _dot(w.astype(BF16), v_ref[pl.ds(ks, SB_BQ), cols])
        if past is not None:
            acc_ref[:, cols] = contrib
        else:
            acc_ref[:, cols] += contrib
        c_ref[:, cols] = c_new
        return jnp.max(c_new)

    def sweep(kb, past):
        return functools.reduce(jnp.maximum, [block(h, kb, past) for h in range(SB_GROUP)])

    def cond(state):
        kb, c_max = state
        return jnp.logical_and(kb >= 0, c_max > SB_SKIP_LOG)

    def body(state):
        kb, _ = state
        return kb - 1, sweep(kb, None)

    lax.while_loop(cond, body, (qb - 1, sweep(qb, col < row)))
    o_ref[...] = acc_ref[...].astype(o_ref.dtype)


def sb_attention(z, t):
    width = SB_GROUP * HEAD_DIM
    assert SB_HEADS % SB_GROUP == 0 and AB_K_SB % SB_GROUP == 0 and AB_V_SB % SB_GROUP == 0
    return pl.pallas_call(
        _sb_kernel,
        out_shape=jax.ShapeDtypeStruct((t, SB_HEADS * HEAD_DIM), BF16),
        grid=(SB_HEADS // SB_GROUP, t // SB_BQ),
        in_specs=[pl.BlockSpec((SB_BQ, width), lambda h, i: (i, AB_Q_SB // SB_GROUP + h)),
                  pl.BlockSpec((t, width), lambda h, i: (0, AB_K_SB // SB_GROUP + h)),
                  pl.BlockSpec((t, width), lambda h, i: (0, AB_V_SB // SB_GROUP + h))],
        out_specs=pl.BlockSpec((SB_BQ, width), lambda h, i: (i, h)),
        scratch_shapes=[pltpu.VMEM((SB_BQ, width), F32), pltpu.VMEM((SB_BQ, width), F32)],
        compiler_params=_cparams(("parallel", "arbitrary")),
        name="sb_attention",
    )(z, z, z)


def _compress_kernel(blk_ref, pe_ref, w1_ref, b1_ref, w2_ref, o_ref, *, n_rows):
    x = (blk_ref[...].astype(F32) + pe_ref[...]).astype(BF16)
    hid = _gelu_tanh(_dot(x, w1_ref[...].astype(BF16)) + b1_ref[...])
    out = _dot(hid.astype(BF16), w2_ref[...].astype(BF16))
    o_ref[0:CMP_PAD, :] = jnp.zeros((CMP_PAD, HEAD_DIM), o_ref.dtype)
    o_ref[CMP_PAD:CMP_PAD + n_rows, :] = out.astype(o_ref.dtype)
    o_ref[CMP_PAD + n_rows:, :] = jnp.zeros((o_ref.shape[0] - CMP_PAD - n_rows, HEAD_DIM), o_ref.dtype)


def compress(z, col_block, pe, w1, b1, w2, t):
    n_chunk = t // NSA_CMP_STRIDE
    a = z[:, col_block * LANE:(col_block + NSA_KV_HEADS) * LANE]
    a = a.reshape(n_chunk, NSA_CMP_STRIDE, NSA_KV_HEADS, HEAD_DIM).transpose(2, 0, 1, 3)
    a = a.reshape(NSA_KV_HEADS, n_chunk, NSA_CMP_STRIDE * HEAD_DIM)
    nxt = jnp.concatenate([a[:, 1:], jnp.zeros_like(a[:, :1])], axis=1)
    blk = jnp.concatenate([a, nxt], axis=-1)
    width = NSA_CMP_LEN * HEAD_DIM
    hidden = w1.shape[-1]
    rows = CMP_PAD + n_chunk + 8
    return pl.pallas_call(
        functools.partial(_compress_kernel, n_rows=n_chunk),
        out_shape=jax.ShapeDtypeStruct((NSA_KV_HEADS, rows, HEAD_DIM), BF16),
        grid=(NSA_KV_HEADS,),
        in_specs=[pl.BlockSpec((None, n_chunk, width), lambda h: (h, 0, 0)),
                  pl.BlockSpec((1, width), lambda h: (0, 0)),
                  pl.BlockSpec((width, hidden), lambda h: (0, 0)),
                  pl.BlockSpec((1, hidden), lambda h: (0, 0)),
                  pl.BlockSpec((hidden, HEAD_DIM), lambda h: (0, 0))],
        out_specs=pl.BlockSpec((None, rows, HEAD_DIM), lambda h: (h, 0, 0)),
        compiler_params=_cparams(("parallel",)),
        name="compress",
    )(blk, pe.reshape(1, width), w1, b1.reshape(1, hidden), w2)


def _cmp_to_sel_padded(t):
    n_cmp = (t - NSA_CMP_LEN) // NSA_CMP_STRIDE + 1
    n_sel = t // NSA_SEL_LEN
    cmp_end = np.arange(n_cmp) * NSA_CMP_STRIDE + NSA_CMP_LEN - 1
    cmp_start = cmp_end - (NSA_CMP_LEN - 1)
    sel_start = np.arange(n_sel) * NSA_SEL_LEN
    overlap = (np.minimum(cmp_end[:, None], sel_start[None, :] + NSA_SEL_LEN - 1)
               - np.maximum(cmp_start[:, None], sel_start[None, :]) + 1)
    c2s = np.clip(overlap, 0, None).astype(np.float32) / NSA_CMP_LEN
    out = np.zeros((CMP_PAD + n_cmp + 9, n_sel), np.float32)
    out[CMP_PAD:CMP_PAD + n_cmp] = c2s
    return out


def _cmp_kernel(q_ref, kc_ref, vc_ref, c2s_ref, bias_ref, o_ref, pen_ref, *, n_far):
    qb = pl.program_id(1)
    r0 = pl.multiple_of(qb * (BLK // NSA_CMP_STRIDE), 8)
    kc_far = kc_ref[0:n_far, :]
    kc_near = kc_ref[pl.ds(r0, BLK), :]
    vs_far = jnp.concatenate([vc_ref[0:n_far, :], c2s_ref[0:n_far, :]], axis=1)
    vs_near = jnp.concatenate([vc_ref[pl.ds(r0, BLK), :], c2s_ref[pl.ds(r0, BLK), :]], axis=1)
    r_far = lax.broadcasted_iota(jnp.int32, (BLK, n_far), 1)
    far_ok = jnp.logical_and(r_far >= CMP_PAD, r_far < r0)
    row = lax.broadcasted_iota(jnp.int32, (BLK, BLK), 0)
    col = lax.broadcasted_iota(jnp.int32, (BLK, BLK), 1)
    dist_c = row - NSA_CMP_STRIDE * (col - CMP_PAD) - (NSA_CMP_LEN - 1)
    near_ok = jnp.logical_and(dist_c >= 0, col + r0 >= CMP_PAD)

    n_sel = c2s_ref.shape[1]
    imp = jnp.zeros((BLK, n_sel), F32)
    for g in range(NSA_G):
        q = q_ref[:, g * HEAD_DIM:(g + 1) * HEAD_DIM]
        s_far = jnp.where(far_ok, _dot_nt(q, kc_far) * SCALE, NEG_INF)
        s_near = jnp.where(near_ok, _dot_nt(q, kc_near) * SCALE + bias_ref[g], NEG_INF)
        m = jnp.maximum(jnp.max(s_far, axis=-1, keepdims=True), jnp.max(s_near, axis=-1, keepdims=True))
        e_far = jnp.where(far_ok, jnp.exp(s_far - m), 0.0)
        e_near = jnp.where(near_ok, jnp.exp(s_near - m), 0.0)
        l = jnp.sum(e_far, axis=-1, keepdims=True) + jnp.sum(e_near, axis=-1, keepdims=True)
        inv = 1.0 / jnp.where(l > 0.0, l, 1.0)
        res = (_dot(e_far.astype(BF16), vs_far) + _dot(e_near.astype(BF16), vs_near)) * inv
        o_ref[:, g * HEAD_DIM:(g + 1) * HEAD_DIM] = res[:, :HEAD_DIM].astype(o_ref.dtype)
        imp = imp + res[:, HEAD_DIM:]

    blk = lax.broadcasted_iota(jnp.int32, (n_sel, BLK), 0)
    t_pos = qb * BLK + lax.broadcasted_iota(jnp.int32, (n_sel, BLK), 1)
    cur = t_pos // NSA_SEL_LEN
    forced = jnp.logical_or(blk == 0, jnp.logical_or(blk == cur, blk == cur - 1))
    score = jnp.where(forced, NSA_FORCE_SCORE, jnp.where(blk <= cur, imp.T, -1.0))
    pen = jnp.full((n_sel, BLK), NEG_INF, F32)
    for _ in range(min(NSA_SEL_TOPK, n_sel)):
        top = jnp.max(score, axis=0, keepdims=True)
        first = jnp.min(jnp.where(score == top, blk, n_sel), axis=0, keepdims=True)
        hit = blk == first
        pen = jnp.where(hit, 0.0, pen)
        score = jnp.where(hit, -2.0, score)
    pen_ref[...] = pen.T.astype(pen_ref.dtype)


def cmp_select(z, kc, vc, bias_cmp, t):
    n_sel = t // NSA_SEL_LEN
    assert n_sel == LANE, "selection mask rides in one 128-lane tile"
    c2s = jnp.asarray(_cmp_to_sel_padded(t), BF16)
    rows = kc.shape[1]
    assert c2s.shape[0] == rows
    n_far = t // NSA_CMP_STRIDE
    qw = NSA_G * HEAD_DIM
    return pl.pallas_call(
        functools.partial(_cmp_kernel, n_far=n_far),
        out_shape=(jax.ShapeDtypeStruct((t, NSA_HEADS * HEAD_DIM), BF16),
                   jax.ShapeDtypeStruct((NSA_KV_HEADS, t, n_sel), BF16)),
        grid=(NSA_KV_HEADS, t // BLK),
        in_specs=[pl.BlockSpec((BLK, qw), lambda h, i: (i, AB_Q_N // NSA_G + h)),
                  pl.BlockSpec((None, rows, HEAD_DIM), lambda h, i: (h, 0, 0)),
                  pl.BlockSpec((None, rows, HEAD_DIM), lambda h, i: (h, 0, 0)),
                  pl.BlockSpec((rows, n_sel), lambda h, i: (0, 0)),
                  pl.BlockSpec((NSA_G, BLK, BLK), lambda h, i: (h, 0, 0))],
        out_specs=(pl.BlockSpec((BLK, qw), lambda h, i: (i, h)),
                   pl.BlockSpec((None, BLK, n_sel), lambda h, i: (h, i, 0))),
        compiler_params=_cparams(("parallel", "arbitrary")),
        name="cmp_select",
    )(z, kc, vc, c2s, bias_cmp)


SEL_CHUNK = SEL_CHUNK_BLOCKS * BLK


def _sel_kernel(q_ref, pen_ref, kaug_ref, vaug_ref, bias_ref, o_ref, qa_ref, m_ref, acc_ref):
    qb = pl.program_id(1)
    pen = pen_ref[...]
    for g in range(NSA_G):
        rows = slice(g * BLK, (g + 1) * BLK)
        qa_ref[rows, :HEAD_DIM] = (q_ref[:, g * HEAD_DIM:(g + 1) * HEAD_DIM].astype(F32) * SCALE).astype(BF16)
        qa_ref[rows, HEAD_DIM:] = pen
    m_ref[...] = jnp.full(m_ref.shape, M_INIT, F32)
    acc_ref[...] = jnp.zeros(acc_ref.shape, F32)

    def chunk(c, with_bias):
        ks = pl.multiple_of(c * SEL_CHUNK, SEL_CHUNK)
        s_all = _dot_nt(qa_ref[...], kaug_ref[pl.ds(ks, SEL_CHUNK), :])
        va = vaug_ref[pl.ds(ks, SEL_CHUNK), :]
        for g in range(NSA_G):
            rows = slice(g * BLK, (g + 1) * BLK)
            tiles = [s_all[rows, j * BLK:(j + 1) * BLK] for j in range(SEL_CHUNK_BLOCKS)]
            if with_bias:
                first = qb - c * SEL_CHUNK_BLOCKS + SEL_NEG_BLOCKS
                tiles = [tile + bias_ref[first - j, g] for j, tile in enumerate(tiles)]
            m_old = m_ref[rows, :]
            m_new = jnp.maximum(m_old, jnp.max(functools.reduce(jnp.maximum, tiles), axis=-1, keepdims=True))
            alpha = jnp.exp(m_old - m_new)
            p = jnp.concatenate([jnp.exp(tile - m_new) for tile in tiles], axis=1).astype(BF16)
            acc_ref[rows, :] = jnp.concatenate([alpha, alpha], axis=1) * acc_ref[rows, :] + _dot(p, va)
            m_ref[rows, :] = m_new

    n_far = jnp.maximum(qb - (SEL_FAR_BLOCKS - 1), 0) // SEL_CHUNK_BLOCKS

    def far_body(c, carry):
        chunk(c, False)
        return carry

    def near_body(c, carry):
        chunk(c, True)
        return carry

    lax.fori_loop(0, n_far, far_body, 0)
    lax.fori_loop(n_far, qb // SEL_CHUNK_BLOCKS + 1, near_body, 0)

    for g in range(NSA_G):
        rows = slice(g * BLK, (g + 1) * BLK)
        o_ref[:, g * HEAD_DIM:(g + 1) * HEAD_DIM] = (acc_ref[rows, :HEAD_DIM] / acc_ref[rows, HEAD_DIM:]
                                                      ).astype(o_ref.dtype)


def sel_attention(z, pen, bias_sel, t):
    n_sel = t // NSA_SEL_LEN
    assert t % SEL_CHUNK == 0
    onehot = jnp.asarray(np.arange(t)[:, None] // NSA_SEL_LEN == np.arange(n_sel)[None, :], BF16)
    kv_cols = lambda blk: z[:, blk * LANE:(blk + NSA_KV_HEADS) * LANE].reshape(t, NSA_KV_HEADS, HEAD_DIM)
    side = lambda a: jnp.broadcast_to(a[:, None, :], (t, NSA_KV_HEADS, a.shape[1]))
    kaug = jnp.concatenate([kv_cols(AB_K_S), side(onehot)], axis=2).transpose(1, 0, 2)
    vaug = jnp.concatenate([kv_cols(AB_V_S), side(jnp.ones((t, HEAD_DIM), BF16))], axis=2).transpose(1, 0, 2)
    qw = NSA_G * HEAD_DIM
    rows = NSA_G * BLK
    return pl.pallas_call(
        _sel_kernel,
        out_shape=jax.ShapeDtypeStruct((t, NSA_HEADS * HEAD_DIM), BF16),
        grid=(NSA_KV_HEADS, t // BLK),
        in_specs=[pl.BlockSpec((BLK, qw), lambda h, i: (i, AB_Q_N // NSA_G + h)),
                  pl.BlockSpec((None, BLK, n_sel), lambda h, i: (h, i, 0)),
                  pl.BlockSpec((None, t, HEAD_DIM + n_sel), lambda h, i: (h, 0, 0)),
                  pl.BlockSpec((None, t, 2 * HEAD_DIM), lambda h, i: (h, 0, 0)),
                  pl.BlockSpec((SEL_TILES, NSA_G, BLK, BLK), lambda h, i: (0, h, 0, 0))],
        out_specs=pl.BlockSpec((BLK, qw), lambda h, i: (i, h)),
        scratch_shapes=[pltpu.VMEM((rows, HEAD_DIM + n_sel), BF16), pltpu.VMEM((rows, LANE), F32),
                        pltpu.VMEM((rows, 2 * HEAD_DIM), F32)],
        compiler_params=_cparams(("parallel", "arbitrary")),
        name="sel_attention",
    )(z, pen, kaug, vaug, bias_sel)


def _banded_kernel(*refs, group, n_prev, window, has_sinks):
    if has_sinks:
        sink_ref, q_ref, k_ref, v_ref, bias_ref, o_ref = refs
    else:
        q_ref, k_ref, v_ref, bias_ref, o_ref = refs
    kv = pl.program_id(0)
    qb = pl.program_id(1)
    row = lax.broadcasted_iota(jnp.int32, (BLK, BLK), 0)
    col = lax.broadcasted_iota(jnp.int32, (BLK, BLK), 1)
    starts, valids = [], []
    for delta in range(n_prev + 1):
        kb = qb - delta
        starts.append(pl.multiple_of(jnp.maximum(kb, 0) * BLK, BLK))
        dist = delta * BLK + row - col
        limit = jnp.where(kb >= 0, window, 0)
        valids.append(jnp.logical_and(dist >= 0, dist < limit))
    for g in range(group):
        q = q_ref[:, g * HEAD_DIM:(g + 1) * HEAD_DIM]
        s = [jnp.where(valids[d], _dot_nt(q, k_ref[pl.ds(starts[d], BLK), :]) * SCALE + bias_ref[d, g], NEG_INF)
             for d in range(n_prev + 1)]
        m = functools.reduce(jnp.maximum, [jnp.max(x, axis=-1, keepdims=True) for x in s])
        if has_sinks:
            sink = sink_ref[kv * group + g]
            m = jnp.maximum(m, sink)
        e = [jnp.exp(x - m) for x in s]
        l = functools.reduce(jnp.add, [jnp.sum(x, axis=-1, keepdims=True) for x in e])
        if has_sinks:
            l = l + jnp.exp(sink - m)
        o = functools.reduce(jnp.add, [_dot(e[d].astype(BF16), v_ref[pl.ds(starts[d], BLK), :])
                                       for d in range(n_prev + 1)])
        o_ref[:, g * HEAD_DIM:(g + 1) * HEAD_DIM] = (o / l).astype(o_ref.dtype)


def banded_attention(z, bias_raw, t, *, q_block, k_block, v_block, n_kv, group, window, sinks=None):
    n_prev = -(-window // BLK)
    qw = group * HEAD_DIM
    assert q_block % group == 0
    in_specs = [pl.BlockSpec((BLK, qw), lambda h, i: (i, q_block // group + h)),
                pl.BlockSpec((t, HEAD_DIM), lambda h, i: (0, k_block + h)),
                pl.BlockSpec((t, HEAD_DIM), lambda h, i: (0, v_block + h)),
                pl.BlockSpec((n_prev + 1, group, BLK, BLK), lambda h, i: (0, h, 0, 0))]
    args = [z, z, z, bias_raw]
    if sinks is not None:
        in_specs.insert(0, pl.BlockSpec(memory_space=pltpu.SMEM))
        args.insert(0, sinks)
    return pl.pallas_call(
        functools.partial(_banded_kernel, group=group, n_prev=n_prev, window=window, has_sinks=sinks is not None),
        out_shape=jax.ShapeDtypeStruct((t, n_kv * qw), BF16),
        grid=(n_kv, t // BLK),
        in_specs=in_specs,
        out_specs=pl.BlockSpec((BLK, qw), lambda h, i: (i, h)),
        compiler_params=_cparams(("parallel", "arbitrary")),
        name="banded_attention",
    )(*args)


def _combine_kernel(g_ref, c_ref, s_ref, w_ref, o_ref):
    gates = _sigmoid(g_ref[...])
    for h in range(NSA_HEADS):
        sl = slice(h * HEAD_DIM, (h + 1) * HEAD_DIM)
        o = (gates[:, 3 * h:3 * h + 1] * c_ref[:, sl].astype(F32)
             + gates[:, 3 * h + 1:3 * h + 2] * s_ref[:, sl].astype(F32)
             + gates[:, 3 * h + 2:3 * h + 3] * w_ref[:, sl].astype(F32))
        o_ref[:, sl] = o.astype(o_ref.dtype)


def nsa_combine(gate_logits, o_cmp, o_sel, o_win):
    t, w = o_cmp.shape
    tm = 256
    row = lambda width: pl.BlockSpec((tm, width), lambda i: (i, 0))
    return pl.pallas_call(
        _combine_kernel,
        out_shape=jax.ShapeDtypeStruct((t, w), BF16),
        grid=(t // tm,),
        in_specs=[row(LANE), row(w), row(w), row(w)],
        out_specs=row(w),
        compiler_params=_cparams(("parallel",)),
        name="nsa_combine",
    )(gate_logits, o_cmp, o_sel, o_win)


def _sg_kernel(u_ref, v_ref, lng_ref, lnb_ref, w_ref, b_ref, o_ref):
    u = _gelu_tanh(u_ref[...].astype(F32))
    gv = _gelu_tanh(v_ref[...].astype(F32))
    mu = jnp.mean(gv, axis=-1, keepdims=True)
    cen = gv - mu
    var = jnp.mean(cen * cen, axis=-1, keepdims=True)
    gv = (cen * lax.rsqrt(var + RMS_EPS) * lng_ref[...] + lnb_ref[...]).astype(BF16)
    row = lax.broadcasted_iota(jnp.int32, (SG_CHUNK, SG_CHUNK), 0)
    col = lax.broadcasted_iota(jnp.int32, (SG_CHUNK, SG_CHUNK), 1)
    gw = u.shape[1] // SG_GROUPS
    for g in range(SG_GROUPS):
        sl = slice(g * gw, (g + 1) * gw)
        w = jnp.where(row >= col, w_ref[g], 0.0).astype(BF16)
        spatial = _dot(w, gv[:, sl]) + b_ref[:, g:g + 1]
        o_ref[:, sl] = (u[:, sl] * spatial).astype(o_ref.dtype)


def spatial_gating(z1, ln_g, ln_b, sg_w, sg_b, t):
    width = ln_g.shape[0]
    nblk = width // LANE
    assert width % (SG_GROUPS * LANE) == 0
    return pl.pallas_call(
        _sg_kernel,
        out_shape=jax.ShapeDtypeStruct((t, width), BF16),
        grid=(t // SG_CHUNK,),
        in_specs=[pl.BlockSpec((SG_CHUNK, width), lambda i: (i, 0)),
                  pl.BlockSpec((SG_CHUNK, width), lambda i: (i, 1)),
                  pl.BlockSpec((1, width), lambda i: (0, 0)),
                  pl.BlockSpec((1, width), lambda i: (0, 0)),
                  pl.BlockSpec((SG_GROUPS, SG_CHUNK, SG_CHUNK), lambda i: (0, 0, 0)),
                  pl.BlockSpec((SG_CHUNK, SG_GROUPS), lambda i: (0, 0))],
        out_specs=pl.BlockSpec((SG_CHUNK, width), lambda i: (i, 0)),
        compiler_params=_cparams(("parallel",)),
        name="spatial_gating",
    )(z1, z1, ln_g.reshape(1, width), ln_b.reshape(1, width), sg_w, jnp.transpose(sg_b))


MOE_TM = 512
MOE_TOP_K = 2


def moe_plan(idx, gate, t):
    n_rows = MOE_TOP_K * t + N_EXPERTS * MOE_TM
    n_blocks = n_rows // MOE_TM
    pair_e = idx[:, :MOE_TOP_K].reshape(-1)
    pair_w = gate[:, :MOE_TOP_K].reshape(-1)
    onehot = (pair_e[:, None] == jnp.arange(N_EXPERTS, dtype=jnp.int32)[None, :]).astype(jnp.int32)
    before = jnp.cumsum(onehot, axis=0) - onehot
    count = jnp.sum(onehot, axis=0)
    padded = (count + MOE_TM - 1) // MOE_TM * MOE_TM
    group_end = jnp.cumsum(padded)
    group_start = group_end - padded
    pair_row = jnp.sum(onehot * (before + group_start[None, :]), axis=1)
    row_tok = jnp.zeros((n_rows,), jnp.int32).at[pair_row].set(jnp.arange(MOE_TOP_K * t, dtype=jnp.int32) // MOE_TOP_K)
    row_w = jnp.zeros((n_rows,), F32).at[pair_row].set(pair_w)
    block_start = jnp.arange(n_blocks, dtype=jnp.int32) * MOE_TM
    block_e = jnp.minimum(jnp.sum(block_start[:, None] >= group_end[None, :], axis=1), N_EXPERTS - 1).astype(jnp.int32)
    n_used = (group_end[-1] // MOE_TM).astype(jnp.int32).reshape(1)
    return row_tok, row_w.reshape(n_rows, 1), block_e, n_used, pair_row.astype(jnp.int32)


MOE_GATHER_ROWS = 256


def _gather_rows_kernel(tok_ref, src_ref, o_ref, buf, sem):
    def row_copy(k):
        return pltpu.make_async_copy(src_ref.at[pl.ds(tok_ref[k], 1), :], buf.at[pl.ds(k, 1), :], sem)

    def start(k, carry):
        row_copy(k).start()
        return carry

    def wait(k, carry):
        row_copy(k).wait()
        return carry

    lax.fori_loop(0, MOE_GATHER_ROWS, start, 0)
    lax.fori_loop(0, MOE_GATHER_ROWS, wait, 0)
    o_ref[...] = buf[...].astype(o_ref.dtype)


def gather_rows(src, row_tok):
    n_rows = row_tok.shape[0]
    d = src.shape[1]
    return pl.pallas_call(
        _gather_rows_kernel,
        out_shape=jax.ShapeDtypeStruct((n_rows, d), BF16),
        grid=(n_rows // MOE_GATHER_ROWS,),
        in_specs=[pl.BlockSpec((MOE_GATHER_ROWS,), lambda b: (b,), memory_space=pltpu.SMEM),
                  pl.BlockSpec(memory_space=pl.ANY)],
        out_specs=pl.BlockSpec((MOE_GATHER_ROWS, d), lambda b: (b, 0)),
        scratch_shapes=[pltpu.VMEM((MOE_GATHER_ROWS, d), src.dtype), pltpu.SemaphoreType.DMA(())],
        compiler_params=_cparams(("arbitrary",)),
        name="moe_gather",
    )(row_tok, src)


def _expert_changed(b, block_e_ref):
    prev = jnp.maximum(b - 1, 0)
    return jnp.logical_or(b == 0, block_e_ref[b] != block_e_ref[prev])


def _moe_up_kernel(block_e_ref, n_used_ref, x_ref, wg_ref, wu_ref, o_ref, wg_bf, wu_bf):
    b = pl.program_id(1)
    live = b < n_used_ref[0]

    @pl.when(jnp.logical_and(live, _expert_changed(b, block_e_ref)))
    def _():
        wg_bf[...] = wg_ref[...].astype(BF16)
        wu_bf[...] = wu_ref[...].astype(BF16)

    @pl.when(live)
    def _():
        x = x_ref[...]
        gate = _dot(x, wg_bf[...])
        o_ref[...] = ((gate * _sigmoid(gate)) * _dot(x, wu_bf[...])).astype(o_ref.dtype)

    @pl.when(jnp.logical_not(live))
    def _():
        o_ref[...] = jnp.zeros(o_ref.shape, o_ref.dtype)


def _moe_down_kernel(block_e_ref, n_used_ref, h_ref, wd_ref, w_ref, o_ref, wd_bf):
    b = pl.program_id(1)
    live = b < n_used_ref[0]

    @pl.when(jnp.logical_and(live, _expert_changed(b, block_e_ref)))
    def _():
        wd_bf[...] = wd_ref[...].astype(BF16)

    @pl.when(live)
    def _():
        o_ref[...] = w_ref[...] * _dot(h_ref[...], wd_bf[...])

    @pl.when(jnp.logical_not(live))
    def _():
        o_ref[...] = jnp.zeros(o_ref.shape, o_ref.dtype)


def moe_experts(xs, row_w, block_e, n_used, w_gate, w_up, w_down):
    n_rows, d = xs.shape
    d_ff = w_gate.shape[3]
    n_blocks = n_rows // MOE_TM
    tn_up, tn_down = 512, 1024
    blk = lambda b, nu: jnp.minimum(b, nu[0] - 1)
    hid = pl.pallas_call(
        _moe_up_kernel,
        out_shape=jax.ShapeDtypeStruct((n_rows, d_ff), BF16),
        grid_spec=pltpu.PrefetchScalarGridSpec(
            num_scalar_prefetch=2,
            grid=(d_ff // tn_up, n_blocks),
            in_specs=[pl.BlockSpec((MOE_TM, d), lambda j, b, be, nu: (blk(b, nu), 0)),
                      pl.BlockSpec((None, None, d, tn_up), lambda j, b, be, nu: (0, be[blk(b, nu)], 0, j)),
                      pl.BlockSpec((None, None, d, tn_up), lambda j, b, be, nu: (0, be[blk(b, nu)], 0, j))],
            out_specs=pl.BlockSpec((MOE_TM, tn_up), lambda j, b, be, nu: (b, j)),
            scratch_shapes=[pltpu.VMEM((d, tn_up), BF16), pltpu.VMEM((d, tn_up), BF16)]),
        compiler_params=_cparams(("arbitrary", "arbitrary")),
        name="moe_up",
    )(block_e, n_used, xs, w_gate, w_up)
    return pl.pallas_call(
        _moe_down_kernel,
        out_shape=jax.ShapeDtypeStruct((n_rows, d), F32),
        grid_spec=pltpu.PrefetchScalarGridSpec(
            num_scalar_prefetch=2,
            grid=(d // tn_down, n_blocks),
            in_specs=[pl.BlockSpec((MOE_TM, d_ff), lambda j, b, be, nu: (blk(b, nu), 0)),
                      pl.BlockSpec((None, None, d_ff, tn_down), lambda j, b, be, nu: (0, be[blk(b, nu)], 0, j)),
                      pl.BlockSpec((MOE_TM, 1), lambda j, b, be, nu: (blk(b, nu), 0))],
            out_specs=pl.BlockSpec((MOE_TM, tn_down), lambda j, b, be, nu: (b, j)),
            scratch_shapes=[pltpu.VMEM((d_ff, tn_down), BF16)]),
        compiler_params=_cparams(("arbitrary", "arbitrary")),
        name="moe_down",
    )(block_e, n_used, hid, w_down, row_w)


MOE_TC = 128


def _moe_combine_kernel(rows_ref, x_ref, g_ref, y_ref, o_ref, buf, sem):
    def row_copy(k, choice):
        row = rows_ref[MOE_TOP_K * k + choice]
        return pltpu.make_async_copy(y_ref.at[pl.ds(row, 1), :], buf.at[choice, pl.ds(k, 1), :], sem)

    def start(k, carry):
        for choice in range(MOE_TOP_K):
            row_copy(k, choice).start()
        return carry

    def wait(k, carry):
        for choice in range(MOE_TOP_K):
            row_copy(k, choice).wait()
        return carry

    lax.fori_loop(0, MOE_TC, start, 0)
    lax.fori_loop(0, MOE_TC, wait, 0)
    x = x_ref[...]
    for choice in range(MOE_TOP_K):
        x = x + buf[choice]
    y = x * lax.rsqrt(jnp.mean(x * x, axis=-1, keepdims=True) + RMS_EPS)
    o_ref[...] = (y * g_ref[...]).astype(o_ref.dtype)


def moe_combine_rmsnorm(x, ys, pair_row, g):
    t, d = x.shape
    return pl.pallas_call(
        _moe_combine_kernel,
        out_shape=jax.ShapeDtypeStruct((t, d), F32),
        grid=(t // MOE_TC,),
        in_specs=[pl.BlockSpec((MOE_TOP_K * MOE_TC,), lambda i: (i,), memory_space=pltpu.SMEM),
                  pl.BlockSpec((MOE_TC, d), lambda i: (i, 0)),
                  pl.BlockSpec((1, d), lambda i: (0, 0)),
                  pl.BlockSpec(memory_space=pl.ANY)],
        out_specs=pl.BlockSpec((MOE_TC, d), lambda i: (i, 0)),
        scratch_shapes=[pltpu.VMEM((MOE_TOP_K, MOE_TC, d), F32), pltpu.SemaphoreType.DMA(())],
        compiler_params=_cparams(("arbitrary",)),
        name="moe_combine",
    )(pair_row, x, g.reshape(1, d), ys)


def kernel(x, rel_table, norm_mix, norm_ffn, norm_final, ab_w_in, ab_w_out, cmp_k_pe, cmp_k_w1, cmp_k_b1, cmp_k_w2, cmp_v_pe, cmp_v_w1, cmp_v_b1, cmp_v_w2, ffn_w_gate, ffn_w_up, ffn_w_down, cd_w_in, cd_w_out, sg_ln_g, sg_ln_b, sg_w, sg_b, swa_sinks, moe_w_router, moe_w_gate, moe_w_up, moe_w_down):
    b, t, d = x.shape
    assert b == 1
    xs = x.reshape(t, d)
    bias_raw, bias_sel, bias_cmp = bias_tiles(rel_table)

    h = rmsnorm(xs, norm_mix[0], BF16)
    ab_main = AB_MAIN_BLOCKS * LANE
    z = matmul(h, [ab_w_in], (0,), ab_main, out_dtype=BF16)
    w_gate_cols = jnp.pad(ab_w_in[0, :, ab_main:], ((0, 0), (0, LANE - (ab_w_in.shape[2] - ab_main))))
    gate_logits = matmul(h, [w_gate_cols], (), LANE, out_dtype=F32)
    o_sb = sb_attention(z, t)
    kc = compress(z, AB_K_C, cmp_k_pe[0], cmp_k_w1[0], cmp_k_b1[0], cmp_k_w2[0], t)
    vc = compress(z, AB_V_C, cmp_v_pe[0], cmp_v_w1[0], cmp_v_b1[0], cmp_v_w2[0], t)
    o_cmp, pen = cmp_select(z, kc, vc, bias_cmp, t)
    o_sel = sel_attention(z, pen, bias_sel, t)
    o_win = banded_attention(z, bias_raw, t, q_block=AB_Q_N, k_block=AB_K_W, v_block=AB_V_W,
                             n_kv=NSA_KV_HEADS, group=NSA_G, window=NSA_WINDOW)
    o_nsa = nsa_combine(gate_logits, o_cmp, o_sel, o_win)
    mixed = jnp.concatenate([o_sb, o_nsa], axis=1)
    xs = matmul(mixed, [ab_w_out], (0,), d, out_dtype=F32, res=xs)
    h = rmsnorm(xs, norm_ffn[0], BF16)
    hid = matmul(h, [ffn_w_gate, ffn_w_up], (0,), ffn_w_gate.shape[2], out_dtype=BF16, act="swiglu")
    xs = matmul(hid, [ffn_w_down], (0,), d, out_dtype=F32, res=xs)

    h = rmsnorm(xs, norm_mix[1], BF16)
    z1 = matmul(h, [cd_w_in], (0,), cd_w_in.shape[2], out_dtype=BF16)
    o_sg = spatial_gating(z1, sg_ln_g[0], sg_ln_b[0], sg_w[0], sg_b[0], t)
    o_swa = banded_attention(z1, bias_raw, t, q_block=CD_Q, k_block=CD_K, v_block=CD_V,
                             n_kv=SWA_KV_HEADS, group=SWA_HEADS // SWA_KV_HEADS, window=SWA_WINDOW,
                             sinks=swa_sinks[0])
    mixed = jnp.concatenate([o_sg, o_swa], axis=1)
    xs = matmul(mixed, [cd_w_out], (0,), d, out_dtype=F32, res=xs)
    h, gate, idx = rmsnorm_router(xs, norm_ffn[1], moe_w_router[0])
    row_tok, row_w, block_e, n_used, pair_row = moe_plan(idx, gate, t)
    rows = gather_rows(h, row_tok)
    ys = moe_experts(rows, row_w, block_e, n_used, moe_w_gate, moe_w_up, moe_w_down)
    return moe_combine_rmsnorm(xs, ys, pair_row, norm_final).reshape(b, t, d)
```

```python
import functools
import math

import numpy as np
import jax
import jax.numpy as jnp
from jax import lax
from jax.experimental import pallas as pl
from jax.experimental.pallas import tpu as pltpu

F32 = jnp.float32
BF16 = jnp.bfloat16

LANE = 128
VMEM_LIMIT = 56 * 1024 * 1024

HEAD_DIM = 128
BLK = 128
NEG_INF = -1e30
RMS_EPS = 1e-6
SCALE = HEAD_DIM ** -0.5

SB_HEADS = 16
NSA_HEADS = 16
NSA_KV_HEADS = 4
NSA_G = NSA_HEADS // NSA_KV_HEADS
NSA_CMP_LEN = 32
NSA_CMP_STRIDE = 16
NSA_SEL_LEN = 64
NSA_SEL_TOPK = 16
NSA_WINDOW = 512
NSA_FORCE_SCORE = 1e6
SG_GROUPS = 8
SG_CHUNK = 128
SWA_HEADS = 16
SWA_KV_HEADS = 2
SWA_WINDOW = 128
REL_BUCKETS = 32
REL_MAX_DIST = 1024
REL_HEADS = 16
N_EXPERTS = 8

AB_Q_SB, AB_K_SB, AB_V_SB = 0, 16, 32
AB_Q_N = 48
AB_K_C, AB_V_C, AB_K_S, AB_V_S, AB_K_W, AB_V_W = 64, 68, 72, 76, 80, 84
AB_MAIN_BLOCKS = 88
CD_Q, CD_K, CD_V = 32, 48, 50

SEL_CHUNK_BLOCKS = 4
SEL_FAR_BLOCKS = 8
SEL_NEG_BLOCKS = SEL_CHUNK_BLOCKS - 1
SEL_TILES = SEL_NEG_BLOCKS + SEL_FAR_BLOCKS + SEL_CHUNK_BLOCKS - 1
RAW_TILES = NSA_WINDOW // BLK + 1
CMP_PAD = 120
SB_SKIP_LOG = -110.0
M_INIT = -5e29


def _cparams(sem):
    return pltpu.CompilerParams(dimension_semantics=sem, vmem_limit_bytes=VMEM_LIMIT)


def _dot(a, b):
    return jnp.dot(a, b, preferred_element_type=F32)


def _dot_nt(a, b):
    return lax.dot_general(a, b, (((1,), (1,)), ((), ())), preferred_element_type=F32)


def _sigmoid(x):
    return 1.0 / (1.0 + jnp.exp(-x))


def _gelu_tanh(x):
    return 0.5 * x * (1.0 + jnp.tanh(math.sqrt(2.0 / math.pi) * (x + 0.044715 * (x * x * x))))


def _rmsnorm_kernel(x_ref, g_ref, o_ref):
    x = x_ref[...]
    y = x * lax.rsqrt(jnp.mean(x * x, axis=-1, keepdims=True) + RMS_EPS)
    o_ref[...] = (y * g_ref[...]).astype(o_ref.dtype)


def rmsnorm(x, g, out_dtype):
    t, d = x.shape
    tm = 256
    return pl.pallas_call(
        _rmsnorm_kernel,
        out_shape=jax.ShapeDtypeStruct((t, d), out_dtype),
        grid=(t // tm,),
        in_specs=[pl.BlockSpec((tm, d), lambda i: (i, 0)), pl.BlockSpec((1, d), lambda i: (0, 0))],
        out_specs=pl.BlockSpec((tm, d), lambda i: (i, 0)),
        compiler_params=_cparams(("parallel",)),
        name="rmsnorm",
    )(x, g.reshape(1, d))


def _rmsnorm_router_kernel(x_ref, g_ref, wr_ref, o_ref, gate_ref, idx_ref):
    x = x_ref[...]
    y = x * lax.rsqrt(jnp.mean(x * x, axis=-1, keepdims=True) + RMS_EPS)
    y = y * g_ref[...]
    o_ref[...] = y.astype(o_ref.dtype).reshape(o_ref.shape)
    logits = jnp.dot(y, wr_ref[...], preferred_element_type=F32, precision=lax.Precision.HIGHEST)
    lane = lax.broadcasted_iota(jnp.int32, logits.shape, 1)
    logits = jnp.where(lane < N_EXPERTS, logits, NEG_INF)
    m1 = jnp.max(logits, axis=-1, keepdims=True)
    i1 = jnp.min(jnp.where(logits == m1, lane, LANE), axis=-1, keepdims=True)
    rest = jnp.where(lane == i1, NEG_INF, logits)
    m2 = jnp.max(rest, axis=-1, keepdims=True)
    i2 = jnp.min(jnp.where(rest == m2, lane, LANE), axis=-1, keepdims=True)
    e2 = jnp.exp(m2 - m1)
    denom = 1.0 + e2
    gate_ref[...] = jnp.where(lane == 0, 1.0 / denom, jnp.where(lane == 1, e2 / denom, 0.0))
    idx_ref[...] = jnp.where(lane == 0, i1, jnp.where(lane == 1, i2, 0))


def rmsnorm_router(x, g, w_router):
    t, d = x.shape
    tm = 256
    wr = jnp.pad(w_router, ((0, 0), (0, LANE - w_router.shape[1])))
    return pl.pallas_call(
        _rmsnorm_router_kernel,
        out_shape=(jax.ShapeDtypeStruct((t, d // LANE, LANE), BF16), jax.ShapeDtypeStruct((t, LANE), F32),
                   jax.ShapeDtypeStruct((t, LANE), jnp.int32)),
        grid=(t // tm,),
        in_specs=[pl.BlockSpec((tm, d), lambda i: (i, 0)), pl.BlockSpec((1, d), lambda i: (0, 0)),
                  pl.BlockSpec((d, LANE), lambda i: (0, 0))],
        out_specs=(pl.BlockSpec((tm, d // LANE, LANE), lambda i: (i, 0, 0)), pl.BlockSpec((tm, LANE), lambda i: (i, 0)),
                   pl.BlockSpec((tm, LANE), lambda i: (i, 0))),
        compiler_params=_cparams(("parallel",)),
        name="rmsnorm_router",
    )(x, g.reshape(1, d), wr)


def _mm_epilogue(parts, res_ref, o_ref, act):
    r = parts[0]
    if act == "swiglu":
        r = (r * _sigmoid(r)) * parts[1]
    if res_ref is not None:
        r = res_ref[...] + r
    o_ref[...] = r.astype(o_ref.dtype)


def _mm_resident_kernel(*refs, n_a, n_w, has_res, act):
    a_refs = refs[:n_a]
    w_refs = refs[n_a:n_a + n_w]
    res_ref = refs[n_a + n_w] if has_res else None
    o_ref = refs[n_a + n_w + int(has_res)]
    wbf_refs = refs[n_a + n_w + int(has_res) + 1:]

    @pl.when(pl.program_id(1) == 0)
    def _():
        for w, wbf in zip(w_refs, wbf_refs):
            wbf[...] = w[...].astype(BF16)

    parts = []
    for wbf in wbf_refs:
        k0, part = 0, None
        for a_ref in a_refs:
            k1 = k0 + a_ref.shape[1]
            term = _dot(a_ref[...], wbf[k0:k1, :])
            part = term if part is None else part + term
            k0 = k1
        parts.append(part)
    _mm_epilogue(parts, res_ref, o_ref, act)


def _mm_ktiled_kernel(a_ref, w_ref, res_ref, o_ref, acc_ref, *, nk):
    k = pl.program_id(2)
    part = _dot(a_ref[...], w_ref[...].astype(BF16))

    @pl.when(k == 0)
    def _():
        acc_ref[...] = part

    @pl.when(k > 0)
    def _():
        acc_ref[...] += part

    @pl.when(k == nk - 1)
    def _():
        _mm_epilogue([acc_ref[...]], res_ref, o_ref, None)


MM_RESIDENT_MAX_K = 4096


def matmul(a, ws, lead, n_cols, *, out_dtype, res=None, act=None):
    a_parts = list(a) if isinstance(a, (list, tuple)) else [a]
    m = a_parts[0].shape[0]
    kdim = sum(part.shape[1] for part in a_parts)
    lead = tuple(lead)
    squeeze = (None,) * len(lead)
    if kdim <= MM_RESIDENT_MAX_K:
        tm = 1024
        tn = min(n_cols, 512 // len(ws))
        assert m % tm == 0 and n_cols % tn == 0
        in_specs = [pl.BlockSpec((tm, part.shape[1]), lambda j, i: (i, 0)) for part in a_parts]
        in_specs += [pl.BlockSpec(squeeze + (kdim, tn), lambda j, i: lead + (0, j)) for _ in ws]
        args = [*a_parts, *ws]
        if res is not None:
            in_specs.append(pl.BlockSpec((tm, tn), lambda j, i: (i, j)))
            args.append(res)
        return pl.pallas_call(
            functools.partial(_mm_resident_kernel, n_a=len(a_parts), n_w=len(ws), has_res=res is not None, act=act),
            out_shape=jax.ShapeDtypeStruct((m, n_cols), out_dtype),
            grid=(n_cols // tn, m // tm),
            in_specs=in_specs,
            out_specs=pl.BlockSpec((tm, tn), lambda j, i: (i, j)),
            scratch_shapes=[pltpu.VMEM((kdim, tn), BF16) for _ in ws],
            compiler_params=_cparams(("parallel", "arbitrary")),
            name="matmul_resident",
        )(*args)
    assert len(ws) == 1 and act is None and res is not None and len(a_parts) == 1
    a = a_parts[0]
    tm, tn, tk = 2048, 512, 2048
    assert m % tm == 0 and n_cols % tn == 0 and kdim % tk == 0
    nk = kdim // tk
    return pl.pallas_call(
        functools.partial(_mm_ktiled_kernel, nk=nk),
        out_shape=jax.ShapeDtypeStruct((m, n_cols), out_dtype),
        grid=(n_cols // tn, m // tm, nk),
        in_specs=[pl.BlockSpec((tm, tk), lambda j, i, k: (i, k)),
                  pl.BlockSpec(squeeze + (tk, tn), lambda j, i, k: lead + (k, j)),
                  pl.BlockSpec((tm, tn), lambda j, i, k: (i, j))],
        out_specs=pl.BlockSpec((tm, tn), lambda j, i, k: (i, j)),
        scratch_shapes=[pltpu.VMEM((tm, tn), F32)],
        compiler_params=_cparams(("parallel", "parallel", "arbitrary")),
        name="matmul_ktiled",
    )(a, ws[0], res)


def _bucket_thresholds():
    d = np.arange(0, 4 * REL_MAX_DIST)
    max_exact = REL_BUCKETS // 2
    log_ratio = np.log(np.maximum(d, 1).astype(np.float64) / max_exact) / math.log(REL_MAX_DIST / max_exact)
    far = np.minimum(max_exact + (log_ratio * (REL_BUCKETS - max_exact)).astype(np.int32), REL_BUCKETS - 1)
    bucket = np.where(d < max_exact, d, far)
    assert np.all(np.diff(bucket) >= 0) and bucket[-1] == REL_BUCKETS - 1
    return [int(np.argmax(bucket >= b)) for b in range(REL_BUCKETS)]


_BUCKET_THR = _bucket_thresholds()
REL_FAR_DIST = _BUCKET_THR[REL_BUCKETS - 1]
assert REL_FAR_DIST <= SEL_FAR_BLOCKS * BLK - (BLK - 1)
assert REL_FAR_DIST <= NSA_CMP_STRIDE * (CMP_PAD + 1) - (NSA_CMP_LEN - 1)


def _bias_kernel(tab_ref, raw_ref, sel_ref, cmp_ref):
    h = pl.program_id(0)
    row = lax.broadcasted_iota(jnp.int32, (BLK, BLK), 0)
    col = lax.broadcasted_iota(jnp.int32, (BLK, BLK), 1)

    def lookup(dist):
        dd = jnp.maximum(dist, 0)
        b = jnp.full(dist.shape, tab_ref[0, h], F32)
        for bk in range(1, REL_BUCKETS):
            b = jnp.where(dd >= _BUCKET_THR[bk], tab_ref[bk, h], b)
        return b

    far = tab_ref[REL_BUCKETS - 1, h]
    for d in range(RAW_TILES):
        raw_ref[d] = lookup(d * BLK + row - col)
    for step in range(SEL_TILES):
        dist = (step - SEL_NEG_BLOCKS) * BLK + row - col
        sel_ref[step] = jnp.where(dist < 0, NEG_INF, lookup(dist) - far)
    dist_c = row - NSA_CMP_STRIDE * (col - CMP_PAD) - (NSA_CMP_LEN - 1)
    cmp_ref[...] = lookup(dist_c) - far


def bias_tiles(rel_table):
    tiles = lambda n: pl.BlockSpec((n, None, BLK, BLK), lambda h: (0, h, 0, 0))
    return pl.pallas_call(
        _bias_kernel,
        out_shape=(jax.ShapeDtypeStruct((RAW_TILES, REL_HEADS, BLK, BLK), F32),
                   jax.ShapeDtypeStruct((SEL_TILES, REL_HEADS, BLK, BLK), F32),
                   jax.ShapeDtypeStruct((REL_HEADS, BLK, BLK), F32)),
        grid=(REL_HEADS,),
        in_specs=[pl.BlockSpec(memory_space=pltpu.SMEM)],
        out_specs=(tiles(RAW_TILES), tiles(SEL_TILES), pl.BlockSpec((None, BLK, BLK), lambda h: (h, 0, 0))),
        compiler_params=_cparams(("arbitrary",)),
        name="bias_tiles",
    )(rel_table)


SB_BQ = 256
SB_GROUP = 4


def _sb_kernel(q_ref, k_ref, v_ref, o_ref, c_ref, acc_ref):
    qb = pl.program_id(1)
    row = lax.broadcasted_iota(jnp.int32, (SB_BQ, SB_BQ), 0)
    col = lax.broadcasted_iota(jnp.int32, (SB_BQ, SB_BQ), 1)
    upper = (row > col).astype(BF16)

    def block(h, kb, past):
        cols = slice(h * HEAD_DIM, (h + 1) * HEAD_DIM)
        ks = pl.multiple_of(kb * SB_BQ, SB_BQ)
        z = _dot_nt(q_ref[:, cols], k_ref[pl.ds(ks, SB_BQ), cols]) * SCALE
        sp = jnp.maximum(z, 0.0) + jnp.log(1.0 + jnp.exp(-jnp.abs(z)))
        log_stay = -sp
        if past is not None:
            log_stay = jnp.where(past, log_stay, 0.0)
        hi = log_stay.astype(BF16)
        lo = (log_stay - hi.astype(F32)).astype(BF16)
        later = _dot(hi, upper) + _dot(lo, upper)
        tot = jnp.broadcast_to(jnp.sum(log_stay, axis=-1, keepdims=True), (SB_BQ, HEAD_DIM))
        if past is not None:
            w = jnp.where(past, jnp.exp((z - sp) + later), 0.0)
            c_new = tot
        else:
            carry = c_ref[:, cols]
            w = jnp.exp((z - sp) + later + jnp.concatenate([carry] * (SB_BQ // HEAD_DIM), axis=1))
            c_new = carry + tot
        contrib = _dot(w.astype(BF16), v_ref[pl.ds(ks, SB_BQ), cols])
        if past is not None:
            acc_ref[:, cols] = contrib
        else:
            acc_ref[:, cols] += contrib
        c_ref[:, cols] = c_new
        return jnp.max(c_new)

    def sweep(kb, past):
        return functools.reduce(jnp.maximum, [block(h, kb, past) for h in range(SB_GROUP)])

    def cond(state):
        kb, c_max = state
        return jnp.logical_and(kb >= 0, c_max > SB_SKIP_LOG)

    def body(state):
        kb, _ = state
        return kb - 1, sweep(kb, None)

    lax.while_loop(cond, body, (qb - 1, sweep(qb, col < row)))
    o_ref[...] = acc_ref[...].astype(o_ref.dtype)


def sb_attention(z, t):
    width = SB_GROUP * HEAD_DIM
    assert SB_HEADS % SB_GROUP == 0 and AB_K_SB % SB_GROUP == 0 and AB_V_SB % SB_GROUP == 0
    return pl.pallas_call(
        _sb_kernel,
        out_shape=jax.ShapeDtypeStruct((t, SB_HEADS * HEAD_DIM), BF16),
        grid=(SB_HEADS // SB_GROUP, t // SB_BQ),
        in_specs=[pl.BlockSpec((SB_BQ, width), lambda h, i: (i, AB_Q_SB // SB_GROUP + h)),
                  pl.BlockSpec((t, width), lambda h, i: (0, AB_K_SB // SB_GROUP + h)),
                  pl.BlockSpec((t, width), lambda h, i: (0, AB_V_SB // SB_GROUP + h))],
        out_specs=pl.BlockSpec((SB_BQ, width), lambda h, i: (i, h)),
        scratch_shapes=[pltpu.VMEM((SB_BQ, width), F32), pltpu.VMEM((SB_BQ, width), F32)],
        compiler_params=_cparams(("parallel", "arbitrary")),
        name="sb_attention",
    )(z, z, z)


def _compress_kernel(blk_ref, pe_ref, w1_ref, b1_ref, w2_ref, o_ref, *, n_rows):
    x = (blk_ref[...].astype(F32) + pe_ref[...]).astype(BF16)
    hid = _gelu_tanh(_dot(x, w1_ref[...].astype(BF16)) + b1_ref[...])
    out = _dot(hid.astype(BF16), w2_ref[...].astype(BF16))
    o_ref[0:CMP_PAD, :] = jnp.zeros((CMP_PAD, HEAD_DIM), o_ref.dtype)
    o_ref[CMP_PAD:CMP_PAD + n_rows, :] = out.astype(o_ref.dtype)
    o_ref[CMP_PAD + n_rows:, :] = jnp.zeros((o_ref.shape[0] - CMP_PAD - n_rows, HEAD_DIM), o_ref.dtype)


def compress(z, col_block, pe, w1, b1, w2, t):
    n_chunk = t // NSA_CMP_STRIDE
    a = z[:, col_block * LANE:(col_block + NSA_KV_HEADS) * LANE]
    a = a.reshape(n_chunk, NSA_CMP_STRIDE, NSA_KV_HEADS, HEAD_DIM).transpose(2, 0, 1, 3)
    a = a.reshape(NSA_KV_HEADS, n_chunk, NSA_CMP_STRIDE * HEAD_DIM)
    nxt = jnp.concatenate([a[:, 1:], jnp.zeros_like(a[:, :1])], axis=1)
    blk = jnp.concatenate([a, nxt], axis=-1)
    width = NSA_CMP_LEN * HEAD_DIM
    hidden = w1.shape[-1]
    rows = CMP_PAD + n_chunk + 8
    return pl.pallas_call(
        functools.partial(_compress_kernel, n_rows=n_chunk),
        out_shape=jax.ShapeDtypeStruct((NSA_KV_HEADS, rows, HEAD_DIM), BF16),
        grid=(NSA_KV_HEADS,),
        in_specs=[pl.BlockSpec((None, n_chunk, width), lambda h: (h, 0, 0)),
                  pl.BlockSpec((1, width), lambda h: (0, 0)),
                  pl.BlockSpec((width, hidden), lambda h: (0, 0)),
                  pl.BlockSpec((1, hidden), lambda h: (0, 0)),
                  pl.BlockSpec((hidden, HEAD_DIM), lambda h: (0, 0))],
        out_specs=pl.BlockSpec((None, rows, HEAD_DIM), lambda h: (h, 0, 0)),
        compiler_params=_cparams(("parallel",)),
        name="compress",
    )(blk, pe.reshape(1, width), w1, b1.reshape(1, hidden), w2)


def _cmp_to_sel_padded(t):
    n_cmp = (t - NSA_CMP_LEN) // NSA_CMP_STRIDE + 1
    n_sel = t // NSA_SEL_LEN
    cmp_end = np.arange(n_cmp) * NSA_CMP_STRIDE + NSA_CMP_LEN - 1
    cmp_start = cmp_end - (NSA_CMP_LEN - 1)
    sel_start = np.arange(n_sel) * NSA_SEL_LEN
    overlap = (np.minimum(cmp_end[:, None], sel_start[None, :] + NSA_SEL_LEN - 1)
               - np.maximum(cmp_start[:, None], sel_start[None, :]) + 1)
    c2s = np.clip(overlap, 0, None).astype(np.float32) / NSA_CMP_LEN
    out = np.zeros((CMP_PAD + n_cmp + 9, n_sel), np.float32)
    out[CMP_PAD:CMP_PAD + n_cmp] = c2s
    return out


def _cmp_kernel(q_ref, kc_ref, vc_ref, c2s_ref, bias_ref, o_ref, pen_ref, *, n_far):
    qb = pl.program_id(1)
    r0 = pl.multiple_of(qb * (BLK // NSA_CMP_STRIDE), 8)
    kc_far = kc_ref[0:n_far, :]
    kc_near = kc_ref[pl.ds(r0, BLK), :]
    vs_far = jnp.concatenate([vc_ref[0:n_far, :], c2s_ref[0:n_far, :]], axis=1)
    vs_near = jnp.concatenate([vc_ref[pl.ds(r0, BLK), :], c2s_ref[pl.ds(r0, BLK), :]], axis=1)
    r_far = lax.broadcasted_iota(jnp.int32, (BLK, n_far), 1)
    far_ok = jnp.logical_and(r_far >= CMP_PAD, r_far < r0)
    row = lax.broadcasted_iota(jnp.int32, (BLK, BLK), 0)
    col = lax.broadcasted_iota(jnp.int32, (BLK, BLK), 1)
    dist_c = row - NSA_CMP_STRIDE * (col - CMP_PAD) - (NSA_CMP_LEN - 1)
    near_ok = jnp.logical_and(dist_c >= 0, col + r0 >= CMP_PAD)

    n_sel = c2s_ref.shape[1]
    imp = jnp.zeros((BLK, n_sel), F32)
    for g in range(NSA_G):
        q = q_ref[:, g * HEAD_DIM:(g + 1) * HEAD_DIM]
        s_far = jnp.where(far_ok, _dot_nt(q, kc_far) * SCALE, NEG_INF)
        s_near = jnp.where(near_ok, _dot_nt(q, kc_near) * SCALE + bias_ref[g], NEG_INF)
        m = jnp.maximum(jnp.max(s_far, axis=-1, keepdims=True), jnp.max(s_near, axis=-1, keepdims=True))
        e_far = jnp.where(far_ok, jnp.exp(s_far - m), 0.0)
        e_near = jnp.where(near_ok, jnp.exp(s_near - m), 0.0)
        l = jnp.sum(e_far, axis=-1, keepdims=True) + jnp.sum(e_near, axis=-1, keepdims=True)
        inv = 1.0 / jnp.where(l > 0.0, l, 1.0)
        res = (_dot(e_far.astype(BF16), vs_far) + _dot(e_near.astype(BF16), vs_near)) * inv
        o_ref[:, g * HEAD_DIM:(g + 1) * HEAD_DIM] = res[:, :HEAD_DIM].astype(o_ref.dtype)
        imp = imp + res[:, HEAD_DIM:]

    blk = lax.broadcasted_iota(jnp.int32, (n_sel, BLK), 0)
    t_pos = qb * BLK + lax.broadcasted_iota(jnp.int32, (n_sel, BLK), 1)
    cur = t_pos // NSA_SEL_LEN
    forced = jnp.logical_or(blk == 0, jnp.logical_or(blk == cur, blk == cur - 1))
    score = jnp.where(forced, NSA_FORCE_SCORE, jnp.where(blk <= cur, imp.T, -1.0))
    pen = jnp.full((n_sel, BLK), NEG_INF, F32)
    for _ in range(min(NSA_SEL_TOPK, n_sel)):
        top = jnp.max(score, axis=0, keepdims=True)
        first = jnp.min(jnp.where(score == top, blk, n_sel), axis=0, keepdims=True)
        hit = blk == first
        pen = jnp.where(hit, 0.0, pen)
        score = jnp.where(hit, -2.0, score)
    pen_ref[...] = pen.T.astype(pen_ref.dtype)


def cmp_select(z, kc, vc, bias_cmp, t):
    n_sel = t // NSA_SEL_LEN
    assert n_sel == LANE, "selection mask rides in one 128-lane tile"
    c2s = jnp.asarray(_cmp_to_sel_padded(t), BF16)
    rows = kc.shape[1]
    assert c2s.shape[0] == rows
    n_far = t // NSA_CMP_STRIDE
    qw = NSA_G * HEAD_DIM
    return pl.pallas_call(
        functools.partial(_cmp_kernel, n_far=n_far),
        out_shape=(jax.ShapeDtypeStruct((t, NSA_HEADS * HEAD_DIM), BF16),
                   jax.ShapeDtypeStruct((NSA_KV_HEADS, t, n_sel), BF16)),
        grid=(NSA_KV_HEADS, t // BLK),
        in_specs=[pl.BlockSpec((BLK, qw), lambda h, i: (i, AB_Q_N // NSA_G + h)),
                  pl.BlockSpec((None, rows, HEAD_DIM), lambda h, i: (h, 0, 0)),
                  pl.BlockSpec((None, rows, HEAD_DIM), lambda h, i: (h, 0, 0)),
                  pl.BlockSpec((rows, n_sel), lambda h, i: (0, 0)),
                  pl.BlockSpec((NSA_G, BLK, BLK), lambda h, i: (h, 0, 0))],
        out_specs=(pl.BlockSpec((BLK, qw), lambda h, i: (i, h)),
                   pl.BlockSpec((None, BLK, n_sel), lambda h, i: (h, i, 0))),
        compiler_params=_cparams(("parallel", "arbitrary")),
        name="cmp_select",
    )(z, kc, vc, c2s, bias_cmp)


SEL_CHUNK = SEL_CHUNK_BLOCKS * BLK


def _sel_kernel(q_ref, pen_ref, kaug_ref, vaug_ref, bias_ref, o_ref, qa_ref, s_ref, cmax_ref, m_ref, acc_ref):
    qb = pl.program_id(1)
    pen = pen_ref[...]
    for g in range(NSA_G):
        rows = slice(g * BLK, (g + 1) * BLK)
        qa_ref[rows, :HEAD_DIM] = (q_ref[:, g * HEAD_DIM:(g + 1) * HEAD_DIM].astype(F32) * SCALE).astype(BF16)
        qa_ref[rows, HEAD_DIM:] = pen
    m_ref[...] = jnp.full(m_ref.shape, M_INIT, F32)
    acc_ref[...] = jnp.zeros(acc_ref.shape, F32)
    n_chunks = qb // SEL_CHUNK_BLOCKS + 1
    last_chunk = kaug_ref.shape[0] // SEL_CHUNK - 1

    def chunk_start(c):
        return pl.multiple_of(jnp.minimum(c, last_chunk) * SEL_CHUNK, SEL_CHUNK)

    def scores(c, slot):
        s_all = _dot_nt(qa_ref[...], kaug_ref[pl.ds(chunk_start(c), SEL_CHUNK), :])
        first = qb - c * SEL_CHUNK_BLOCKS + SEL_NEG_BLOCKS
        for g in range(NSA_G):
            rows = slice(g * BLK, (g + 1) * BLK)
            tiles = [s_all[rows, j * BLK:(j + 1) * BLK] + bias_ref[jnp.clip(first - j, 0, SEL_TILES - 1), g]
                     for j in range(SEL_CHUNK_BLOCKS)]
            for j, tile in enumerate(tiles):
                s_ref[slot, rows, j * BLK:(j + 1) * BLK] = tile
            cmax_ref[slot, rows, :] = jnp.broadcast_to(
                jnp.max(functools.reduce(jnp.maximum, tiles), axis=-1, keepdims=True), (BLK, LANE))

    def absorb(c, slot):
        va = vaug_ref[pl.ds(chunk_start(c), SEL_CHUNK), :]
        for g in range(NSA_G):
            rows = slice(g * BLK, (g + 1) * BLK)
            m_old = m_ref[rows, :]
            m_new = jnp.maximum(m_old, cmax_ref[slot, rows, :])
            alpha = jnp.exp(m_old - m_new)
            p = jnp.concatenate([jnp.exp(s_ref[slot, rows, j * BLK:(j + 1) * BLK] - m_new)
                                 for j in range(SEL_CHUNK_BLOCKS)], axis=1).astype(BF16)
            acc_ref[rows, :] = jnp.concatenate([alpha, alpha], axis=1) * acc_ref[rows, :] + _dot(p, va)
            m_ref[rows, :] = m_new

    def pair(k, carry):
        scores(2 * k + 1, 1)
        absorb(2 * k, 0)
        scores(2 * k + 2, 0)
        absorb(2 * k + 1, 1)
        return carry

    scores(0, 0)
    lax.fori_loop(0, (n_chunks + 1) // 2, pair, 0)

    for g in range(NSA_G):
        rows = slice(g * BLK, (g + 1) * BLK)
        o_ref[:, g * HEAD_DIM:(g + 1) * HEAD_DIM] = (acc_ref[rows, :HEAD_DIM] / acc_ref[rows, HEAD_DIM:]
                                                      ).astype(o_ref.dtype)


def sel_attention(z, pen, bias_sel, t):
    n_sel = t // NSA_SEL_LEN
    assert t % SEL_CHUNK == 0
    onehot = jnp.asarray(np.arange(t)[:, None] // NSA_SEL_LEN == np.arange(n_sel)[None, :], BF16)
    kv_cols = lambda blk: z[:, blk * LANE:(blk + NSA_KV_HEADS) * LANE].reshape(t, NSA_KV_HEADS, HEAD_DIM)
    side = lambda a: jnp.broadcast_to(a[:, None, :], (t, NSA_KV_HEADS, a.shape[1]))
    kaug = jnp.concatenate([kv_cols(AB_K_S), side(onehot)], axis=2).transpose(1, 0, 2)
    vaug = jnp.concatenate([kv_cols(AB_V_S), side(jnp.ones((t, HEAD_DIM), BF16))], axis=2).transpose(1, 0, 2)
    qw = NSA_G * HEAD_DIM
    rows = NSA_G * BLK
    return pl.pallas_call(
        _sel_kernel,
        out_shape=jax.ShapeDtypeStruct((t, NSA_HEADS * HEAD_DIM), BF16),
        grid=(NSA_KV_HEADS, t // BLK),
        in_specs=[pl.BlockSpec((BLK, qw), lambda h, i: (i, AB_Q_N // NSA_G + h)),
                  pl.BlockSpec((None, BLK, n_sel), lambda h, i: (h, i, 0)),
                  pl.BlockSpec((None, t, HEAD_DIM + n_sel), lambda h, i: (h, 0, 0)),
                  pl.BlockSpec((None, t, 2 * HEAD_DIM), lambda h, i: (h, 0, 0)),
                  pl.BlockSpec((SEL_TILES, NSA_G, BLK, BLK), lambda h, i: (0, h, 0, 0))],
        out_specs=pl.BlockSpec((BLK, qw), lambda h, i: (i, h)),
        scratch_shapes=[pltpu.VMEM((rows, HEAD_DIM + n_sel), BF16), pltpu.VMEM((2, rows, SEL_CHUNK), F32),
                        pltpu.VMEM((2, rows, LANE), F32), pltpu.VMEM((rows, LANE), F32),
                        pltpu.VMEM((rows, 2 * HEAD_DIM), F32)],
        compiler_params=_cparams(("parallel", "arbitrary")),
        name="sel_attention",
    )(z, pen, kaug, vaug, bias_sel)


def _banded_kernel(*refs, group, n_prev, window, has_sinks):
    if has_sinks:
        sink_ref, q_ref, k_ref, v_ref, bias_ref, o_ref, qs_ref = refs
    else:
        q_ref, k_ref, v_ref, bias_ref, o_ref, qs_ref = refs
    kv = pl.program_id(0)
    qb = pl.program_id(1)
    for g in range(group):
        qs_ref[g * BLK:(g + 1) * BLK, :] = (q_ref[:, g * HEAD_DIM:(g + 1) * HEAD_DIM].astype(F32) * SCALE).astype(BF16)
    qs = qs_ref[...]
    row = lax.broadcasted_iota(jnp.int32, (BLK, BLK), 0)
    col = lax.broadcasted_iota(jnp.int32, (BLK, BLK), 1)
    ones = jnp.ones((BLK, HEAD_DIM), BF16)
    tiles, vaugs = [], []
    for delta in range(n_prev + 1):
        kb = qb - delta
        start = pl.multiple_of(jnp.maximum(kb, 0) * BLK, BLK)
        dist = delta * BLK + row - col
        limit = jnp.where(kb >= 0, window, 0)
        valid = jnp.logical_and(dist >= 0, dist < limit)
        bias = jnp.concatenate([jnp.where(valid, bias_ref[delta, g], NEG_INF) for g in range(group)], axis=0)
        tiles.append(_dot_nt(qs, k_ref[pl.ds(start, BLK), :]) + bias)
        vaugs.append(jnp.concatenate([v_ref[pl.ds(start, BLK), :], ones], axis=1))
    m = jnp.max(functools.reduce(jnp.maximum, tiles), axis=-1, keepdims=True)
    if has_sinks:
        sink = jnp.concatenate([jnp.full((BLK, 1), sink_ref[kv * group + g], F32) for g in range(group)], axis=0)
        m = jnp.maximum(m, sink)
    acc = functools.reduce(jnp.add, [_dot(jnp.exp(tile - m).astype(BF16), vaug) for tile, vaug in zip(tiles, vaugs)])
    denom = acc[:, HEAD_DIM:]
    if has_sinks:
        denom = denom + jnp.exp(sink - m)
    out = acc[:, :HEAD_DIM] / denom
    for g in range(group):
        o_ref[:, g * HEAD_DIM:(g + 1) * HEAD_DIM] = out[g * BLK:(g + 1) * BLK, :].astype(o_ref.dtype)


def banded_attention(z, bias_raw, t, *, q_block, k_block, v_block, n_kv, group, window, sinks=None):
    n_prev = -(-window // BLK)
    qw = group * HEAD_DIM
    assert q_block % group == 0
    in_specs = [pl.BlockSpec((BLK, qw), lambda h, i: (i, q_block // group + h)),
                pl.BlockSpec((t, HEAD_DIM), lambda h, i: (0, k_block + h)),
                pl.BlockSpec((t, HEAD_DIM), lambda h, i: (0, v_block + h)),
                pl.BlockSpec((n_prev + 1, group, BLK, BLK), lambda h, i: (0, h, 0, 0))]
    args = [z, z, z, bias_raw]
    if sinks is not None:
        in_specs.insert(0, pl.BlockSpec(memory_space=pltpu.SMEM))
        args.insert(0, sinks)
    return pl.pallas_call(
        functools.partial(_banded_kernel, group=group, n_prev=n_prev, window=window, has_sinks=sinks is not None),
        out_shape=jax.ShapeDtypeStruct((t, n_kv * qw), BF16),
        grid=(n_kv, t // BLK),
        in_specs=in_specs,
        out_specs=pl.BlockSpec((BLK, qw), lambda h, i: (i, h)),
        scratch_shapes=[pltpu.VMEM((group * BLK, HEAD_DIM), BF16)],
        compiler_params=_cparams(("parallel", "arbitrary")),
        name="banded_attention",
    )(*args)


def _combine_kernel(g_ref, c_ref, s_ref, w_ref, o_ref):
    gates = _sigmoid(g_ref[...])
    for h in range(NSA_HEADS):
        sl = slice(h * HEAD_DIM, (h + 1) * HEAD_DIM)
        o = (gates[:, 3 * h:3 * h + 1] * c_ref[:, sl].astype(F32)
             + gates[:, 3 * h + 1:3 * h + 2] * s_ref[:, sl].astype(F32)
             + gates[:, 3 * h + 2:3 * h + 3] * w_ref[:, sl].astype(F32))
        o_ref[:, sl] = o.astype(o_ref.dtype)


def nsa_combine(gate_logits, o_cmp, o_sel, o_win):
    t, w = o_cmp.shape
    tm = 256
    row = lambda width: pl.BlockSpec((tm, width), lambda i: (i, 0))
    return pl.pallas_call(
        _combine_kernel,
        out_shape=jax.ShapeDtypeStruct((t, w), BF16),
        grid=(t // tm,),
        in_specs=[row(LANE), row(w), row(w), row(w)],
        out_specs=row(w),
        compiler_params=_cparams(("parallel",)),
        name="nsa_combine",
    )(gate_logits, o_cmp, o_sel, o_win)


def _sg_kernel(u_ref, v_ref, lng_ref, lnb_ref, w_ref, b_ref, o_ref):
    u = _gelu_tanh(u_ref[...].astype(F32))
    gv = _gelu_tanh(v_ref[...].astype(F32))
    mu = jnp.mean(gv, axis=-1, keepdims=True)
    cen = gv - mu
    var = jnp.mean(cen * cen, axis=-1, keepdims=True)
    gv = (cen * lax.rsqrt(var + RMS_EPS) * lng_ref[...] + lnb_ref[...]).astype(BF16)
    row = lax.broadcasted_iota(jnp.int32, (SG_CHUNK, SG_CHUNK), 0)
    col = lax.broadcasted_iota(jnp.int32, (SG_CHUNK, SG_CHUNK), 1)
    gw = u.shape[1] // SG_GROUPS
    for g in range(SG_GROUPS):
        sl = slice(g * gw, (g + 1) * gw)
        w = jnp.where(row >= col, w_ref[g], 0.0).astype(BF16)
        spatial = _dot(w, gv[:, sl]) + b_ref[:, g:g + 1]
        o_ref[:, sl] = (u[:, sl] * spatial).astype(o_ref.dtype)


def spatial_gating(z1, ln_g, ln_b, sg_w, sg_b, t):
    width = ln_g.shape[0]
    assert width % (SG_GROUPS * LANE) == 0
    return pl.pallas_call(
        _sg_kernel,
        out_shape=jax.ShapeDtypeStruct((t, width), BF16),
        grid=(t // SG_CHUNK,),
        in_specs=[pl.BlockSpec((SG_CHUNK, width), lambda i: (i, 0)),
                  pl.BlockSpec((SG_CHUNK, width), lambda i: (i, 1)),
                  pl.BlockSpec((1, width), lambda i: (0, 0)),
                  pl.BlockSpec((1, width), lambda i: (0, 0)),
                  pl.BlockSpec((SG_GROUPS, SG_CHUNK, SG_CHUNK), lambda i: (0, 0, 0)),
                  pl.BlockSpec((SG_CHUNK, SG_GROUPS), lambda i: (0, 0))],
        out_specs=pl.BlockSpec((SG_CHUNK, width), lambda i: (i, 0)),
        compiler_params=_cparams(("parallel",)),
        name="spatial_gating",
    )(z1, z1, ln_g.reshape(1, width), ln_b.reshape(1, width), sg_w, jnp.transpose(sg_b))


MOE_TM = 512
MOE_TOP_K = 2


def moe_plan(idx, gate, t):
    n_rows = MOE_TOP_K * t + N_EXPERTS * MOE_TM
    n_blocks = n_rows // MOE_TM
    pair_e = idx[:, :MOE_TOP_K].reshape(-1)
    pair_w = gate[:, :MOE_TOP_K].reshape(-1)
    onehot = (pair_e[:, None] == jnp.arange(N_EXPERTS, dtype=jnp.int32)[None, :]).astype(jnp.int32)
    before = jnp.cumsum(onehot, axis=0) - onehot
    count = jnp.sum(onehot, axis=0)
    padded = (count + MOE_TM - 1) // MOE_TM * MOE_TM
    group_end = jnp.cumsum(padded)
    group_start = group_end - padded
    pair_row = jnp.sum(onehot * (before + group_start[None, :]), axis=1)
    row_tok = jnp.zeros((n_rows,), jnp.int32).at[pair_row].set(jnp.arange(MOE_TOP_K * t, dtype=jnp.int32) // MOE_TOP_K)
    row_w = jnp.zeros((n_rows,), F32).at[pair_row].set(pair_w)
    block_start = jnp.arange(n_blocks, dtype=jnp.int32) * MOE_TM
    block_e = jnp.minimum(jnp.sum(block_start[:, None] >= group_end[None, :], axis=1), N_EXPERTS - 1).astype(jnp.int32)
    n_used = (group_end[-1] // MOE_TM).astype(jnp.int32).reshape(1)
    return row_tok, row_w.reshape(n_rows, 1), block_e, n_used, pair_row.astype(jnp.int32)


MOE_GATHER_ROWS = 256


def _gather_rows_kernel(tok_ref, tok_next_ref, src_ref, o_ref, buf, sem):
    b = pl.program_id(0)
    slot = b % 2

    def row_copy(idx_ref, k, s):
        return pltpu.make_async_copy(src_ref.at[idx_ref[k]], buf.at[s, k], sem.at[s])

    def start_block(idx_ref, s):
        def start(k, carry):
            row_copy(idx_ref, k, s).start()
            return carry
        lax.fori_loop(0, MOE_GATHER_ROWS, start, 0, unroll=8)

    @pl.when(b == 0)
    def _():
        start_block(tok_ref, 0)

    @pl.when(b + 1 < pl.num_programs(0))
    def _():
        start_block(tok_next_ref, 1 - slot)

    def wait(k, carry):
        row_copy(tok_ref, k, slot).wait()
        return carry

    lax.fori_loop(0, MOE_GATHER_ROWS, wait, 0, unroll=8)
    o_ref[...] = buf[slot].reshape(o_ref.shape)


def gather_rows(src, row_tok):
    n_rows = row_tok.shape[0]
    d = src.shape[1] * src.shape[2]
    n_steps = n_rows // MOE_GATHER_ROWS
    return pl.pallas_call(
        _gather_rows_kernel,
        out_shape=jax.ShapeDtypeStruct((n_rows, d), src.dtype),
        grid=(n_steps,),
        in_specs=[pl.BlockSpec((MOE_GATHER_ROWS,), lambda b: (b,), memory_space=pltpu.SMEM),
                  pl.BlockSpec((MOE_GATHER_ROWS,), lambda b: (jnp.minimum(b + 1, n_steps - 1),),
                               memory_space=pltpu.SMEM),
                  pl.BlockSpec(memory_space=pl.ANY)],
        out_specs=pl.BlockSpec((MOE_GATHER_ROWS, d), lambda b: (b, 0)),
        scratch_shapes=[pltpu.VMEM((2, MOE_GATHER_ROWS) + src.shape[1:], src.dtype), pltpu.SemaphoreType.DMA((2,))],
        compiler_params=_cparams(("arbitrary",)),
        name="moe_gather",
    )(row_tok, row_tok, src)


def _expert_changed(b, block_e_ref):
    prev = jnp.maximum(b - 1, 0)
    return jnp.logical_or(b == 0, block_e_ref[b] != block_e_ref[prev])


def _moe_up_kernel(block_e_ref, n_used_ref, x_ref, wg_ref, wu_ref, o_ref, wg_bf, wu_bf):
    b = pl.program_id(1)
    live = b < n_used_ref[0]

    @pl.when(jnp.logical_and(live, _expert_changed(b, block_e_ref)))
    def _():
        wg_bf[...] = wg_ref[...].astype(BF16)
        wu_bf[...] = wu_ref[...].astype(BF16)

    @pl.when(live)
    def _():
        x = x_ref[...]
        gate = _dot(x, wg_bf[...])
        o_ref[...] = ((gate * _sigmoid(gate)) * _dot(x, wu_bf[...])).astype(o_ref.dtype)

    @pl.when(jnp.logical_not(live))
    def _():
        o_ref[...] = jnp.zeros(o_ref.shape, o_ref.dtype)


def _moe_down_kernel(block_e_ref, n_used_ref, h_ref, wd_ref, w_ref, o_ref, wd_bf):
    b = pl.program_id(1)
    live = b < n_used_ref[0]

    @pl.when(jnp.logical_and(live, _expert_changed(b, block_e_ref)))
    def _():
        wd_bf[...] = wd_ref[...].astype(BF16)

    @pl.when(live)
    def _():
        o_ref[...] = w_ref[...] * _dot(h_ref[...], wd_bf[...])

    @pl.when(jnp.logical_not(live))
    def _():
        o_ref[...] = jnp.zeros(o_ref.shape, o_ref.dtype)


def moe_experts(xs, row_w, block_e, n_used, w_gate, w_up, w_down):
    n_rows, d = xs.shape
    d_ff = w_gate.shape[3]
    n_blocks = n_rows // MOE_TM
    tn_up, tn_down = 512, 1024
    blk = lambda b, nu: jnp.minimum(b, nu[0] - 1)
    hid = pl.pallas_call(
        _moe_up_kernel,
        out_shape=jax.ShapeDtypeStruct((n_rows, d_ff), BF16),
        grid_spec=pltpu.PrefetchScalarGridSpec(
            num_scalar_prefetch=2,
            grid=(d_ff // tn_up, n_blocks),
            in_specs=[pl.BlockSpec((MOE_TM, d), lambda j, b, be, nu: (blk(b, nu), 0)),
                      pl.BlockSpec((None, None, d, tn_up), lambda j, b, be, nu: (0, be[blk(b, nu)], 0, j)),
                      pl.BlockSpec((None, None, d, tn_up), lambda j, b, be, nu: (0, be[blk(b, nu)], 0, j))],
            out_specs=pl.BlockSpec((MOE_TM, tn_up), lambda j, b, be, nu: (b, j)),
            scratch_shapes=[pltpu.VMEM((d, tn_up), BF16), pltpu.VMEM((d, tn_up), BF16)]),
        compiler_params=_cparams(("arbitrary", "arbitrary")),
        name="moe_up",
    )(block_e, n_used, xs, w_gate, w_up)
    return pl.pallas_call(
        _moe_down_kernel,
        out_shape=jax.ShapeDtypeStruct((n_rows, d), F32),
        grid_spec=pltpu.PrefetchScalarGridSpec(
            num_scalar_prefetch=2,
            grid=(d // tn_down, n_blocks),
            in_specs=[pl.BlockSpec((MOE_TM, d_ff), lambda j, b, be, nu: (blk(b, nu), 0)),
                      pl.BlockSpec((None, None, d_ff, tn_down), lambda j, b, be, nu: (0, be[blk(b, nu)], 0, j)),
                      pl.BlockSpec((MOE_TM, 1), lambda j, b, be, nu: (blk(b, nu), 0))],
            out_specs=pl.BlockSpec((MOE_TM, tn_down), lambda j, b, be, nu: (b, j)),
            scratch_shapes=[pltpu.VMEM((d_ff, tn_down), BF16)]),
        compiler_params=_cparams(("arbitrary", "arbitrary")),
        name="moe_down",
    )(block_e, n_used, hid, w_down, row_w)


MOE_TC = 128


def _moe_combine_kernel(rows_ref, rows_next_ref, x_ref, g_ref, y_ref, o_ref, buf, sem):
    b = pl.program_id(0)
    slot = b % 2

    def row_copy(idx_ref, k, choice, s):
        row = idx_ref[MOE_TOP_K * k + choice]
        return pltpu.make_async_copy(y_ref.at[pl.ds(row, 1), :], buf.at[s, choice, pl.ds(k, 1), :], sem.at[s])

    def start_block(idx_ref, s):
        def start(k, carry):
            for choice in range(MOE_TOP_K):
                row_copy(idx_ref, k, choice, s).start()
            return carry
        lax.fori_loop(0, MOE_TC, start, 0, unroll=4)

    @pl.when(b == 0)
    def _():
        start_block(rows_ref, 0)

    @pl.when(b + 1 < pl.num_programs(0))
    def _():
        start_block(rows_next_ref, 1 - slot)

    def wait(k, carry):
        for choice in range(MOE_TOP_K):
            row_copy(rows_ref, k, choice, slot).wait()
        return carry

    lax.fori_loop(0, MOE_TC, wait, 0, unroll=4)
    x = x_ref[...]
    for choice in range(MOE_TOP_K):
        x = x + buf[slot, choice]
    y = x * lax.rsqrt(jnp.mean(x * x, axis=-1, keepdims=True) + RMS_EPS)
    o_ref[...] = (y * g_ref[...]).astype(o_ref.dtype)


def moe_combine_rmsnorm(x, ys, pair_row, g):
    t, d = x.shape
    n_steps = t // MOE_TC
    idx_block = lambda index_map: pl.BlockSpec((MOE_TOP_K * MOE_TC,), index_map, memory_space=pltpu.SMEM)
    return pl.pallas_call(
        _moe_combine_kernel,
        out_shape=jax.ShapeDtypeStruct((t, d), F32),
        grid=(n_steps,),
        in_specs=[idx_block(lambda i: (i,)),
                  idx_block(lambda i: (jnp.minimum(i + 1, n_steps - 1),)),
                  pl.BlockSpec((MOE_TC, d), lambda i: (i, 0)),
                  pl.BlockSpec((1, d), lambda i: (0, 0)),
                  pl.BlockSpec(memory_space=pl.ANY)],
        out_specs=pl.BlockSpec((MOE_TC, d), lambda i: (i, 0)),
        scratch_shapes=[pltpu.VMEM((2, MOE_TOP_K, MOE_TC, d), F32), pltpu.SemaphoreType.DMA((2,))],
        compiler_params=_cparams(("arbitrary",)),
        name="moe_combine",
    )(pair_row, pair_row, x, g.reshape(1, d), ys)


def kernel(x, rel_table, norm_mix, norm_ffn, norm_final, ab_w_in, ab_w_out, cmp_k_pe, cmp_k_w1, cmp_k_b1, cmp_k_w2, cmp_v_pe, cmp_v_w1, cmp_v_b1, cmp_v_w2, ffn_w_gate, ffn_w_up, ffn_w_down, cd_w_in, cd_w_out, sg_ln_g, sg_ln_b, sg_w, sg_b, swa_sinks, moe_w_router, moe_w_gate, moe_w_up, moe_w_down):
    b, t, d = x.shape
    assert b == 1
    xs = x.reshape(t, d)
    bias_raw, bias_sel, bias_cmp = bias_tiles(rel_table)

    h = rmsnorm(xs, norm_mix[0], BF16)
    ab_main = AB_MAIN_BLOCKS * LANE
    z = matmul(h, [ab_w_in], (0,), ab_main, out_dtype=BF16)
    w_gate_cols = jnp.pad(ab_w_in[0, :, ab_main:], ((0, 0), (0, LANE - (ab_w_in.shape[2] - ab_main))))
    gate_logits = matmul(h, [w_gate_cols], (), LANE, out_dtype=F32)
    o_sb = sb_attention(z, t)
    kc = compress(z, AB_K_C, cmp_k_pe[0], cmp_k_w1[0], cmp_k_b1[0], cmp_k_w2[0], t)
    vc = compress(z, AB_V_C, cmp_v_pe[0], cmp_v_w1[0], cmp_v_b1[0], cmp_v_w2[0], t)
    o_cmp, pen = cmp_select(z, kc, vc, bias_cmp, t)
    o_sel = sel_attention(z, pen, bias_sel, t)
    o_win = banded_attention(z, bias_raw, t, q_block=AB_Q_N, k_block=AB_K_W, v_block=AB_V_W,
                             n_kv=NSA_KV_HEADS, group=NSA_G, window=NSA_WINDOW)
    o_nsa = nsa_combine(gate_logits, o_cmp, o_sel, o_win)
    xs = matmul([o_sb, o_nsa], [ab_w_out], (0,), d, out_dtype=F32, res=xs)
    h = rmsnorm(xs, norm_ffn[0], BF16)
    hid = matmul(h, [ffn_w_gate, ffn_w_up], (0,), ffn_w_gate.shape[2], out_dtype=BF16, act="swiglu")
    xs = matmul(hid, [ffn_w_down], (0,), d, out_dtype=F32, res=xs)

    h = rmsnorm(xs, norm_mix[1], BF16)
    z1 = matmul(h, [cd_w_in], (0,), cd_w_in.shape[2], out_dtype=BF16)
    o_sg = spatial_gating(z1, sg_ln_g[0], sg_ln_b[0], sg_w[0], sg_b[0], t)
    o_swa = banded_attention(z1, bias_raw, t, q_block=CD_Q, k_block=CD_K, v_block=CD_V,
                             n_kv=SWA_KV_HEADS, group=SWA_HEADS // SWA_KV_HEADS, window=SWA_WINDOW,
                             sinks=swa_sinks[0])
    xs = matmul([o_sg, o_swa], [cd_w_out], (0,), d, out_dtype=F32, res=xs)
    h, gate, idx = rmsnorm_router(xs, norm_ffn[1], moe_w_router[0])
    row_tok, row_w, block_e, n_used, pair_row = moe_plan(idx, gate, t)
    rows = gather_rows(h, row_tok)
    ys = moe_experts(rows, row_w, block_e, n_used, moe_w_gate, moe_w_up, moe_w_down)
    return moe_combine_rmsnorm(xs, ys, pair_row, norm_final).reshape(b, t, d)
```

```python
import functools
import math

import numpy as np
import jax
import jax.numpy as jnp
from jax import lax
from jax.experimental import pallas as pl
from jax.experimental.pallas import tpu as pltpu

F32 = jnp.float32
BF16 = jnp.bfloat16

LANE = 128
VMEM_LIMIT = 56 * 1024 * 1024

HEAD_DIM = 128
BLK = 128
NEG_INF = -1e30
RMS_EPS = 1e-6
SCALE = HEAD_DIM ** -0.5

SB_HEADS = 16
NSA_HEADS = 16
NSA_KV_HEADS = 4
NSA_G = NSA_HEADS // NSA_KV_HEADS
NSA_CMP_LEN = 32
NSA_CMP_STRIDE = 16
NSA_SEL_LEN = 64
NSA_SEL_TOPK = 16
NSA_WINDOW = 512
NSA_FORCE_SCORE = 1e6
SG_GROUPS = 8
SG_CHUNK = 128
SWA_HEADS = 16
SWA_KV_HEADS = 2
SWA_WINDOW = 128
REL_BUCKETS = 32
REL_MAX_DIST = 1024
REL_HEADS = 16
N_EXPERTS = 8

AB_Q_SB, AB_K_SB, AB_V_SB = 0, 16, 32
AB_Q_N = 48
AB_K_C, AB_V_C, AB_K_S, AB_V_S, AB_K_W, AB_V_W = 64, 68, 72, 76, 80, 84
AB_MAIN_BLOCKS = 88
CD_Q, CD_K, CD_V = 32, 48, 50

SEL_CHUNK_BLOCKS = 4
SEL_FAR_BLOCKS = 8
SEL_NEG_BLOCKS = SEL_CHUNK_BLOCKS - 1
SEL_TILES = SEL_NEG_BLOCKS + SEL_FAR_BLOCKS + SEL_CHUNK_BLOCKS - 1
RAW_TILES = NSA_WINDOW // BLK + 1
CMP_PAD = 120
SB_SKIP_LOG = -110.0
M_INIT = -5e29


def _cparams(sem):
    return pltpu.CompilerParams(dimension_semantics=sem, vmem_limit_bytes=VMEM_LIMIT)


def _dot(a, b):
    return jnp.dot(a, b, preferred_element_type=F32)


def _dot_nt(a, b):
    return lax.dot_general(a, b, (((1,), (1,)), ((), ())), preferred_element_type=F32)


def _sigmoid(x):
    return 1.0 / (1.0 + jnp.exp(-x))


def _gelu_tanh(x):
    return 0.5 * x * (1.0 + jnp.tanh(math.sqrt(2.0 / math.pi) * (x + 0.044715 * (x * x * x))))


def _rmsnorm_kernel(x_ref, g_ref, o_ref):
    x = x_ref[...]
    y = x * lax.rsqrt(jnp.mean(x * x, axis=-1, keepdims=True) + RMS_EPS)
    o_ref[...] = (y * g_ref[...]).astype(o_ref.dtype)


def rmsnorm(x, g, out_dtype):
    t, d = x.shape
    tm = 256
    return pl.pallas_call(
        _rmsnorm_kernel,
        out_shape=jax.ShapeDtypeStruct((t, d), out_dtype),
        grid=(t // tm,),
        in_specs=[pl.BlockSpec((tm, d), lambda i: (i, 0)), pl.BlockSpec((1, d), lambda i: (0, 0))],
        out_specs=pl.BlockSpec((tm, d), lambda i: (i, 0)),
        compiler_params=_cparams(("parallel",)),
        name="rmsnorm",
    )(x, g.reshape(1, d))


def _rmsnorm_router_kernel(x_ref, g_ref, wr_ref, o_ref, gate_ref, idx_ref):
    x = x_ref[...]
    y = x * lax.rsqrt(jnp.mean(x * x, axis=-1, keepdims=True) + RMS_EPS)
    y = y * g_ref[...]
    o_ref[...] = y.astype(o_ref.dtype).reshape(o_ref.shape)
    logits = jnp.dot(y, wr_ref[...], preferred_element_type=F32, precision=lax.Precision.HIGHEST)
    lane = lax.broadcasted_iota(jnp.int32, logits.shape, 1)
    logits = jnp.where(lane < N_EXPERTS, logits, NEG_INF)
    m1 = jnp.max(logits, axis=-1, keepdims=True)
    i1 = jnp.min(jnp.where(logits == m1, lane, LANE), axis=-1, keepdims=True)
    rest = jnp.where(lane == i1, NEG_INF, logits)
    m2 = jnp.max(rest, axis=-1, keepdims=True)
    i2 = jnp.min(jnp.where(rest == m2, lane, LANE), axis=-1, keepdims=True)
    e2 = jnp.exp(m2 - m1)
    denom = 1.0 + e2
    gate_ref[...] = jnp.where(lane == 0, 1.0 / denom, jnp.where(lane == 1, e2 / denom, 0.0))
    idx_ref[...] = jnp.where(lane == 0, i1, jnp.where(lane == 1, i2, 0))


def rmsnorm_router(x, g, w_router):
    t, d = x.shape
    tm = 256
    wr = jnp.pad(w_router, ((0, 0), (0, LANE - w_router.shape[1])))
    return pl.pallas_call(
        _rmsnorm_router_kernel,
        out_shape=(jax.ShapeDtypeStruct((t, d // LANE, LANE), BF16), jax.ShapeDtypeStruct((t, LANE), F32),
                   jax.ShapeDtypeStruct((t, LANE), jnp.int32)),
        grid=(t // tm,),
        in_specs=[pl.BlockSpec((tm, d), lambda i: (i, 0)), pl.BlockSpec((1, d), lambda i: (0, 0)),
                  pl.BlockSpec((d, LANE), lambda i: (0, 0))],
        out_specs=(pl.BlockSpec((tm, d // LANE, LANE), lambda i: (i, 0, 0)), pl.BlockSpec((tm, LANE), lambda i: (i, 0)),
                   pl.BlockSpec((tm, LANE), lambda i: (i, 0))),
        compiler_params=_cparams(("parallel",)),
        name="rmsnorm_router",
    )(x, g.reshape(1, d), wr)


def _mm_epilogue(parts, res_ref, o_ref, act):
    r = parts[0]
    if act == "swiglu":
        r = (r * _sigmoid(r)) * parts[1]
    if res_ref is not None:
        r = res_ref[...] + r
    o_ref[...] = r.astype(o_ref.dtype)


def _mm_resident_kernel(*refs, n_a, n_w, has_res, act, w_transposed):
    a_refs = refs[:n_a]
    w_refs = refs[n_a:n_a + n_w]
    res_ref = refs[n_a + n_w] if has_res else None
    o_ref = refs[n_a + n_w + int(has_res)]
    wbf_refs = refs[n_a + n_w + int(has_res) + 1:]

    @pl.when(pl.program_id(1) == 0)
    def _():
        for w, wbf in zip(w_refs, wbf_refs):
            wbf[...] = w[...].astype(BF16)

    parts = []
    for wbf in wbf_refs:
        k0, part = 0, None
        for a_ref in a_refs:
            k1 = k0 + a_ref.shape[1]
            term = _dot_nt(a_ref[...], wbf[:, k0:k1]) if w_transposed else _dot(a_ref[...], wbf[k0:k1, :])
            part = term if part is None else part + term
            k0 = k1
        parts.append(part)
    _mm_epilogue(parts, res_ref, o_ref, act)


def _mm_ktiled_kernel(a_ref, w_ref, res_ref, o_ref, acc_ref, *, nk):
    k = pl.program_id(2)
    part = _dot(a_ref[...], w_ref[...].astype(BF16))

    @pl.when(k == 0)
    def _():
        acc_ref[...] = part

    @pl.when(k > 0)
    def _():
        acc_ref[...] += part

    @pl.when(k == nk - 1)
    def _():
        _mm_epilogue([acc_ref[...]], res_ref, o_ref, None)


MM_RESIDENT_MAX_K = 4096


def matmul(a, ws, lead, n_cols, *, out_dtype, res=None, act=None, w_transposed=False):
    a_parts = list(a) if isinstance(a, (list, tuple)) else [a]
    m = a_parts[0].shape[0]
    kdim = sum(part.shape[1] for part in a_parts)
    lead = tuple(lead)
    squeeze = (None,) * len(lead)
    if kdim <= MM_RESIDENT_MAX_K:
        tm = 1024
        tn = min(n_cols, 512 // len(ws))
        assert m % tm == 0 and n_cols % tn == 0
        in_specs = [pl.BlockSpec((tm, part.shape[1]), lambda j, i: (i, 0)) for part in a_parts]
        panel, panel_index = ((tn, kdim), lambda j: (j, 0)) if w_transposed else ((kdim, tn), lambda j: (0, j))
        in_specs += [pl.BlockSpec(squeeze + panel, lambda j, i: lead + panel_index(j)) for _ in ws]
        args = [*a_parts, *ws]
        if res is not None:
            in_specs.append(pl.BlockSpec((tm, tn), lambda j, i: (i, j)))
            args.append(res)
        return pl.pallas_call(
            functools.partial(_mm_resident_kernel, n_a=len(a_parts), n_w=len(ws), has_res=res is not None, act=act,
                              w_transposed=w_transposed),
            out_shape=jax.ShapeDtypeStruct((m, n_cols), out_dtype),
            grid=(n_cols // tn, m // tm),
            in_specs=in_specs,
            out_specs=pl.BlockSpec((tm, tn), lambda j, i: (i, j)),
            scratch_shapes=[pltpu.VMEM(panel, BF16) for _ in ws],
            compiler_params=_cparams(("parallel", "arbitrary")),
            name="matmul_resident",
        )(*args)
    assert len(ws) == 1 and act is None and res is not None and len(a_parts) == 1 and not w_transposed
    a = a_parts[0]
    tm, tn, tk = 2048, 512, 2048
    assert m % tm == 0 and n_cols % tn == 0 and kdim % tk == 0
    nk = kdim // tk
    return pl.pallas_call(
        functools.partial(_mm_ktiled_kernel, nk=nk),
        out_shape=jax.ShapeDtypeStruct((m, n_cols), out_dtype),
        grid=(n_cols // tn, m // tm, nk),
        in_specs=[pl.BlockSpec((tm, tk), lambda j, i, k: (i, k)),
                  pl.BlockSpec(squeeze + (tk, tn), lambda j, i, k: lead + (k, j)),
                  pl.BlockSpec((tm, tn), lambda j, i, k: (i, j))],
        out_specs=pl.BlockSpec((tm, tn), lambda j, i, k: (i, j)),
        scratch_shapes=[pltpu.VMEM((tm, tn), F32)],
        compiler_params=_cparams(("parallel", "parallel", "arbitrary")),
        name="matmul_ktiled",
    )(a, ws[0], res)


def _bucket_thresholds():
    d = np.arange(0, 4 * REL_MAX_DIST)
    max_exact = REL_BUCKETS // 2
    log_ratio = np.log(np.maximum(d, 1).astype(np.float64) / max_exact) / math.log(REL_MAX_DIST / max_exact)
    far = np.minimum(max_exact + (log_ratio * (REL_BUCKETS - max_exact)).astype(np.int32), REL_BUCKETS - 1)
    bucket = np.where(d < max_exact, d, far)
    assert np.all(np.diff(bucket) >= 0) and bucket[-1] == REL_BUCKETS - 1
    return [int(np.argmax(bucket >= b)) for b in range(REL_BUCKETS)]


_BUCKET_THR = _bucket_thresholds()
REL_FAR_DIST = _BUCKET_THR[REL_BUCKETS - 1]
assert REL_FAR_DIST <= SEL_FAR_BLOCKS * BLK - (BLK - 1)
assert REL_FAR_DIST <= NSA_CMP_STRIDE * (CMP_PAD + 1) - (NSA_CMP_LEN - 1)


def _bias_kernel(tab_ref, raw_ref, sel_ref, cmp_ref):
    h = pl.program_id(0)
    row = lax.broadcasted_iota(jnp.int32, (BLK, BLK), 0)
    col = lax.broadcasted_iota(jnp.int32, (BLK, BLK), 1)

    def lookup(dist):
        dd = jnp.maximum(dist, 0)
        b = jnp.full(dist.shape, tab_ref[0, h], F32)
        for bk in range(1, REL_BUCKETS):
            b = jnp.where(dd >= _BUCKET_THR[bk], tab_ref[bk, h], b)
        return b

    far = tab_ref[REL_BUCKETS - 1, h]
    for d in range(RAW_TILES):
        raw_ref[d] = lookup(d * BLK + row - col)
    for step in range(SEL_TILES):
        dist = (step - SEL_NEG_BLOCKS) * BLK + row - col
        sel_ref[step] = jnp.where(dist < 0, NEG_INF, lookup(dist) - far)
    dist_c = row - NSA_CMP_STRIDE * (col - CMP_PAD) - (NSA_CMP_LEN - 1)
    cmp_ref[...] = lookup(dist_c) - far


def bias_tiles(rel_table):
    tiles = lambda n: pl.BlockSpec((n, None, BLK, BLK), lambda h: (0, h, 0, 0))
    return pl.pallas_call(
        _bias_kernel,
        out_shape=(jax.ShapeDtypeStruct((RAW_TILES, REL_HEADS, BLK, BLK), F32),
                   jax.ShapeDtypeStruct((SEL_TILES, REL_HEADS, BLK, BLK), F32),
                   jax.ShapeDtypeStruct((REL_HEADS, BLK, BLK), F32)),
        grid=(REL_HEADS,),
        in_specs=[pl.BlockSpec(memory_space=pltpu.SMEM)],
        out_specs=(tiles(RAW_TILES), tiles(SEL_TILES), pl.BlockSpec((None, BLK, BLK), lambda h: (h, 0, 0))),
        compiler_params=_cparams(("arbitrary",)),
        name="bias_tiles",
    )(rel_table)


SB_BQ = 256
SB_GROUP = 4


def _sb_kernel(q_ref, k_ref, v_ref, o_ref, c_ref, acc_ref):
    qb = pl.program_id(1)
    row = lax.broadcasted_iota(jnp.int32, (SB_BQ, SB_BQ), 0)
    col = lax.broadcasted_iota(jnp.int32, (SB_BQ, SB_BQ), 1)
    upper = (row > col).astype(BF16)

    def block(h, kb, past):
        cols = slice(h * HEAD_DIM, (h + 1) * HEAD_DIM)
        ks = pl.multiple_of(kb * SB_BQ, SB_BQ)
        z = _dot_nt(q_ref[:, cols], k_ref[pl.ds(ks, SB_BQ), cols]) * SCALE
        sp = jnp.maximum(z, 0.0) + jnp.log(1.0 + jnp.exp(-jnp.abs(z)))
        log_stay = -sp
        if past is not None:
            log_stay = jnp.where(past, log_stay, 0.0)
        later = _dot(log_stay.astype(BF16), upper)
        tot = jnp.broadcast_to(jnp.sum(log_stay, axis=-1, keepdims=True), (SB_BQ, HEAD_DIM))
        if past is not None:
            w = jnp.where(past, jnp.exp((z - sp) + later), 0.0)
            c_new = tot
        else:
            carry = c_ref[:, cols]
            w = jnp.exp((z - sp) + later + jnp.concatenate([carry] * (SB_BQ // HEAD_DIM), axis=1))
            c_new = carry + tot
        contrib = _dot(w.astype(BF16), v_ref[pl.ds(ks, SB_BQ), cols])
        if past is not None:
            acc_ref[:, cols] = contrib
        else:
            acc_ref[:, cols] += contrib
        c_ref[:, cols] = c_new
        return jnp.max(c_new)

    def sweep(kb, past):
        return functools.reduce(jnp.maximum, [block(h, kb, past) for h in range(SB_GROUP)])

    def cond(state):
        kb, c_max = state
        return jnp.logical_and(kb >= 0, c_max > SB_SKIP_LOG)

    def body(state):
        kb, _ = state
        return kb - 1, sweep(kb, None)

    lax.while_loop(cond, body, (qb - 1, sweep(qb, col < row)))
    o_ref[...] = acc_ref[...].astype(o_ref.dtype)


def sb_attention(z, t):
    width = SB_GROUP * HEAD_DIM
    assert SB_HEADS % SB_GROUP == 0 and AB_K_SB % SB_GROUP == 0 and AB_V_SB % SB_GROUP == 0
    return pl.pallas_call(
        _sb_kernel,
        out_shape=jax.ShapeDtypeStruct((t, SB_HEADS * HEAD_DIM), BF16),
        grid=(SB_HEADS // SB_GROUP, t // SB_BQ),
        in_specs=[pl.BlockSpec((SB_BQ, width), lambda h, i: (i, AB_Q_SB // SB_GROUP + h)),
                  pl.BlockSpec((t, width), lambda h, i: (0, AB_K_SB // SB_GROUP + h)),
                  pl.BlockSpec((t, width), lambda h, i: (0, AB_V_SB // SB_GROUP + h))],
        out_specs=pl.BlockSpec((SB_BQ, width), lambda h, i: (i, h)),
        scratch_shapes=[pltpu.VMEM((SB_BQ, width), F32), pltpu.VMEM((SB_BQ, width), F32)],
        compiler_params=_cparams(("parallel", "arbitrary")),
        name="sb_attention",
    )(z, z, z)


def _compress_kernel(a_ref, pe_ref, w1_ref, b1_ref, w2_ref, o_ref):
    n_chunk = a_ref.shape[0]
    half = NSA_CMP_STRIDE * HEAD_DIM
    chunks = a_ref[...].reshape(n_chunk, half).astype(F32)
    first = _dot((chunks + pe_ref[:, :half]).astype(BF16), w1_ref[:half, :].astype(BF16))
    second = _dot((chunks + pe_ref[:, half:]).astype(BF16), w1_ref[half:, :].astype(BF16))
    hid = _gelu_tanh(first + pltpu.roll(second, n_chunk - 1, 0) + b1_ref[...])
    out = _dot(hid.astype(BF16), w2_ref[...].astype(BF16))
    o_ref[0:CMP_PAD, :] = jnp.zeros((CMP_PAD, HEAD_DIM), o_ref.dtype)
    o_ref[CMP_PAD:CMP_PAD + n_chunk, :] = out.astype(o_ref.dtype)
    o_ref[CMP_PAD + n_chunk:, :] = jnp.zeros((o_ref.shape[0] - CMP_PAD - n_chunk, HEAD_DIM), o_ref.dtype)


def compress(z, col_block, pe, w1, b1, w2, t):
    assert NSA_CMP_LEN == 2 * NSA_CMP_STRIDE
    n_chunk = t // NSA_CMP_STRIDE
    width = NSA_CMP_LEN * HEAD_DIM
    hidden = w1.shape[-1]
    rows = CMP_PAD + n_chunk + 8
    return pl.pallas_call(
        _compress_kernel,
        out_shape=jax.ShapeDtypeStruct((NSA_KV_HEADS, rows, HEAD_DIM), BF16),
        grid=(NSA_KV_HEADS,),
        in_specs=[pl.BlockSpec((n_chunk, NSA_CMP_STRIDE, HEAD_DIM), lambda h: (0, 0, col_block + h)),
                  pl.BlockSpec((1, width), lambda h: (0, 0)),
                  pl.BlockSpec((width, hidden), lambda h: (0, 0)),
                  pl.BlockSpec((1, hidden), lambda h: (0, 0)),
                  pl.BlockSpec((hidden, HEAD_DIM), lambda h: (0, 0))],
        out_specs=pl.BlockSpec((None, rows, HEAD_DIM), lambda h: (h, 0, 0)),
        compiler_params=_cparams(("parallel",)),
        name="compress",
    )(z.reshape(n_chunk, NSA_CMP_STRIDE, z.shape[1]), pe.reshape(1, width), w1, b1.reshape(1, hidden), w2)


def _cmp_to_sel_padded(t):
    n_cmp = (t - NSA_CMP_LEN) // NSA_CMP_STRIDE + 1
    n_sel = t // NSA_SEL_LEN
    cmp_end = np.arange(n_cmp) * NSA_CMP_STRIDE + NSA_CMP_LEN - 1
    cmp_start = cmp_end - (NSA_CMP_LEN - 1)
    sel_start = np.arange(n_sel) * NSA_SEL_LEN
    overlap = (np.minimum(cmp_end[:, None], sel_start[None, :] + NSA_SEL_LEN - 1)
               - np.maximum(cmp_start[:, None], sel_start[None, :]) + 1)
    c2s = np.clip(overlap, 0, None).astype(np.float32) / NSA_CMP_LEN
    out = np.zeros((CMP_PAD + n_cmp + 9, n_sel), np.float32)
    out[CMP_PAD:CMP_PAD + n_cmp] = c2s
    return out


def _cmp_win_kernel(q_ref, kc_ref, vc_ref, c2s_ref, bias_ref, kw_ref, vw_ref, bias_win_ref, o_ref, pen_ref, ow_ref,
                    qs_ref, *, n_far):
    qb = pl.program_id(1)
    _stack_scaled_queries(q_ref, qs_ref, NSA_G)
    r0 = pl.multiple_of(qb * (BLK // NSA_CMP_STRIDE), 8)
    kc_far = kc_ref[0:n_far, :]
    kc_near = kc_ref[pl.ds(r0, BLK), :]
    vs_far = jnp.concatenate([vc_ref[0:n_far, :], c2s_ref[0:n_far, :]], axis=1)
    vs_near = jnp.concatenate([vc_ref[pl.ds(r0, BLK), :], c2s_ref[pl.ds(r0, BLK), :]], axis=1)
    r_far = lax.broadcasted_iota(jnp.int32, (BLK, n_far), 1)
    far_ok = jnp.logical_and(r_far >= CMP_PAD, r_far < r0)
    row = lax.broadcasted_iota(jnp.int32, (BLK, BLK), 0)
    col = lax.broadcasted_iota(jnp.int32, (BLK, BLK), 1)
    dist_c = row - NSA_CMP_STRIDE * (col - CMP_PAD) - (NSA_CMP_LEN - 1)
    near_ok = jnp.logical_and(dist_c >= 0, col + r0 >= CMP_PAD)

    far_mask = jnp.where(far_ok, 0.0, NEG_INF)

    n_sel = c2s_ref.shape[1]
    imp = jnp.zeros((BLK, n_sel), F32)
    for g in range(NSA_G):
        q = qs_ref[g * BLK:(g + 1) * BLK, :]
        s_far = _dot_nt(q, kc_far) + far_mask
        s_near = _dot_nt(q, kc_near) + jnp.where(near_ok, bias_ref[g], NEG_INF)
        m = jnp.maximum(jnp.maximum(jnp.max(s_far, axis=-1, keepdims=True), jnp.max(s_near, axis=-1, keepdims=True)),
                        M_INIT)
        e_far = jnp.exp(s_far - m)
        e_near = jnp.exp(s_near - m)
        l = jnp.sum(e_far, axis=-1, keepdims=True) + jnp.sum(e_near, axis=-1, keepdims=True)
        inv = 1.0 / jnp.where(l > 0.0, l, 1.0)
        res = (_dot(e_far.astype(BF16), vs_far) + _dot(e_near.astype(BF16), vs_near)) * inv
        o_ref[:, g * HEAD_DIM:(g + 1) * HEAD_DIM] = res[:, :HEAD_DIM].astype(o_ref.dtype)
        imp = imp + res[:, HEAD_DIM:]

    blk = lax.broadcasted_iota(jnp.int32, (n_sel, BLK), 0)
    t_pos = qb * BLK + lax.broadcasted_iota(jnp.int32, (n_sel, BLK), 1)
    cur = t_pos // NSA_SEL_LEN
    forced = jnp.logical_or(blk == 0, jnp.logical_or(blk == cur, blk == cur - 1))
    score = jnp.where(forced, NSA_FORCE_SCORE, jnp.where(blk <= cur, imp.T, -1.0))
    pen = jnp.full((n_sel, BLK), NEG_INF, F32)
    for _ in range(min(NSA_SEL_TOPK, n_sel)):
        top = jnp.max(score, axis=0, keepdims=True)
        first = jnp.min(jnp.where(score == top, blk, n_sel), axis=0, keepdims=True)
        hit = blk == first
        pen = jnp.where(hit, 0.0, pen)
        score = jnp.where(hit, -2.0, score)
    pen_ref[...] = pen.T.astype(pen_ref.dtype)

    _banded_body(q_ref, kw_ref, vw_ref, bias_win_ref, ow_ref, qs_ref, None, group=NSA_G,
                 n_prev=NSA_WINDOW // BLK, window=NSA_WINDOW, scale_q=False)


def cmp_select_window(z, kc, vc, bias_cmp, bias_raw, t):
    assert NSA_WINDOW % BLK == 0
    n_sel = t // NSA_SEL_LEN
    assert n_sel == LANE, "selection mask rides in one 128-lane tile"
    c2s = jnp.asarray(_cmp_to_sel_padded(t), BF16)
    rows = kc.shape[1]
    assert c2s.shape[0] == rows
    n_far = t // NSA_CMP_STRIDE
    qw = NSA_G * HEAD_DIM
    return pl.pallas_call(
        functools.partial(_cmp_win_kernel, n_far=n_far),
        out_shape=(jax.ShapeDtypeStruct((t, NSA_HEADS * HEAD_DIM), BF16),
                   jax.ShapeDtypeStruct((NSA_KV_HEADS, t, n_sel), BF16),
                   jax.ShapeDtypeStruct((t, NSA_HEADS * HEAD_DIM), BF16)),
        grid=(NSA_KV_HEADS, t // BLK),
        in_specs=[pl.BlockSpec((BLK, qw), lambda h, i: (i, AB_Q_N // NSA_G + h)),
                  pl.BlockSpec((None, rows, HEAD_DIM), lambda h, i: (h, 0, 0)),
                  pl.BlockSpec((None, rows, HEAD_DIM), lambda h, i: (h, 0, 0)),
                  pl.BlockSpec((rows, n_sel), lambda h, i: (0, 0)),
                  pl.BlockSpec((NSA_G, BLK, BLK), lambda h, i: (h, 0, 0)),
                  pl.BlockSpec((t, HEAD_DIM), lambda h, i: (0, AB_K_W + h)),
                  pl.BlockSpec((t, HEAD_DIM), lambda h, i: (0, AB_V_W + h)),
                  pl.BlockSpec((RAW_TILES, NSA_G, BLK, BLK), lambda h, i: (0, h, 0, 0))],
        out_specs=(pl.BlockSpec((BLK, qw), lambda h, i: (i, h)),
                   pl.BlockSpec((None, BLK, n_sel), lambda h, i: (h, i, 0)),
                   pl.BlockSpec((BLK, qw), lambda h, i: (i, h))),
        scratch_shapes=[pltpu.VMEM((NSA_G * BLK, HEAD_DIM), BF16)],
        compiler_params=_cparams(("parallel", "arbitrary")),
        name="cmp_select_window",
    )(z, kc, vc, c2s, bias_cmp, z, z, bias_raw)


SEL_CHUNK = SEL_CHUNK_BLOCKS * BLK


def _sel_kernel(q_ref, pen_ref, kaug_ref, vaug_ref, bias_ref, gate_ref, ocmp_ref, owin_ref, o_ref,
                qa_ref, s_ref, cmax_ref, m_ref, acc_ref):
    qb = pl.program_id(1)
    pen = pen_ref[...]
    for g in range(NSA_G):
        rows = slice(g * BLK, (g + 1) * BLK)
        qa_ref[rows, :HEAD_DIM] = (q_ref[:, g * HEAD_DIM:(g + 1) * HEAD_DIM].astype(F32) * SCALE).astype(BF16)
        qa_ref[rows, HEAD_DIM:] = pen
    m_ref[...] = jnp.full(m_ref.shape, M_INIT, F32)
    acc_ref[...] = jnp.zeros(acc_ref.shape, F32)
    n_chunks = qb // SEL_CHUNK_BLOCKS + 1
    last_chunk = kaug_ref.shape[0] // SEL_CHUNK - 1

    def chunk_start(c):
        return pl.multiple_of(jnp.minimum(c, last_chunk) * SEL_CHUNK, SEL_CHUNK)

    def scores(c, slot):
        s_all = _dot_nt(qa_ref[...], kaug_ref[pl.ds(chunk_start(c), SEL_CHUNK), :])
        first = qb - c * SEL_CHUNK_BLOCKS + SEL_NEG_BLOCKS
        for g in range(NSA_G):
            rows = slice(g * BLK, (g + 1) * BLK)
            tiles = [s_all[rows, j * BLK:(j + 1) * BLK] + bias_ref[jnp.clip(first - j, 0, SEL_TILES - 1), g]
                     for j in range(SEL_CHUNK_BLOCKS)]
            for j, tile in enumerate(tiles):
                s_ref[slot, rows, j * BLK:(j + 1) * BLK] = tile
            cmax_ref[slot, rows, :] = jnp.broadcast_to(
                jnp.max(functools.reduce(jnp.maximum, tiles), axis=-1, keepdims=True), (BLK, LANE))

    def absorb(c, slot):
        va = vaug_ref[pl.ds(chunk_start(c), SEL_CHUNK), :]
        for g in range(NSA_G):
            rows = slice(g * BLK, (g + 1) * BLK)
            m_old = m_ref[rows, :]
            m_new = jnp.maximum(m_old, cmax_ref[slot, rows, :])
            alpha = jnp.exp(m_old - m_new)
            p = jnp.concatenate([jnp.exp(s_ref[slot, rows, j * BLK:(j + 1) * BLK] - m_new)
                                 for j in range(SEL_CHUNK_BLOCKS)], axis=1).astype(BF16)
            acc_ref[rows, :] = jnp.concatenate([alpha, alpha], axis=1) * acc_ref[rows, :] + _dot(p, va)
            m_ref[rows, :] = m_new

    def pair(k, carry):
        scores(2 * k + 1, 1)
        absorb(2 * k, 0)
        scores(2 * k + 2, 0)
        absorb(2 * k + 1, 1)
        return carry

    scores(0, 0)
    lax.fori_loop(0, (n_chunks + 1) // 2, pair, 0)

    gates = _sigmoid(gate_ref[...])
    for g in range(NSA_G):
        rows = slice(g * BLK, (g + 1) * BLK)
        cols = slice(g * HEAD_DIM, (g + 1) * HEAD_DIM)
        o_sel = acc_ref[rows, :HEAD_DIM] / acc_ref[rows, HEAD_DIM:]
        mixed = (gates[:, 3 * g:3 * g + 1] * ocmp_ref[:, cols].astype(F32)
                 + gates[:, 3 * g + 1:3 * g + 2] * o_sel
                 + gates[:, 3 * g + 2:3 * g + 3] * owin_ref[:, cols].astype(F32))
        o_ref[:, cols] = mixed.astype(o_ref.dtype)


def sel_attention_mix(z, pen, bias_sel, gate_logits, o_cmp, o_win, t):
    n_sel = t // NSA_SEL_LEN
    n_gate = 3 * NSA_G
    gates_kv = gate_logits[:, :NSA_KV_HEADS * n_gate].reshape(t, NSA_KV_HEADS, n_gate).transpose(1, 0, 2)
    gates_kv = jnp.pad(gates_kv, ((0, 0), (0, 0), (0, LANE - n_gate)))
    assert t % SEL_CHUNK == 0
    onehot = jnp.asarray(np.arange(t)[:, None] // NSA_SEL_LEN == np.arange(n_sel)[None, :], BF16)
    kv_cols = lambda blk: z[:, blk * LANE:(blk + NSA_KV_HEADS) * LANE].reshape(t, NSA_KV_HEADS, HEAD_DIM)
    side = lambda a: jnp.broadcast_to(a[:, None, :], (t, NSA_KV_HEADS, a.shape[1]))
    kaug = jnp.concatenate([kv_cols(AB_K_S), side(onehot)], axis=2).transpose(1, 0, 2)
    vaug = jnp.concatenate([kv_cols(AB_V_S), side(jnp.ones((t, HEAD_DIM), BF16))], axis=2).transpose(1, 0, 2)
    qw = NSA_G * HEAD_DIM
    rows = NSA_G * BLK
    return pl.pallas_call(
        _sel_kernel,
        out_shape=jax.ShapeDtypeStruct((t, NSA_HEADS * HEAD_DIM), BF16),
        grid=(NSA_KV_HEADS, t // BLK),
        in_specs=[pl.BlockSpec((BLK, qw), lambda h, i: (i, AB_Q_N // NSA_G + h)),
                  pl.BlockSpec((None, BLK, n_sel), lambda h, i: (h, i, 0)),
                  pl.BlockSpec((None, t, HEAD_DIM + n_sel), lambda h, i: (h, 0, 0)),
                  pl.BlockSpec((None, t, 2 * HEAD_DIM), lambda h, i: (h, 0, 0)),
                  pl.BlockSpec((SEL_TILES, NSA_G, BLK, BLK), lambda h, i: (0, h, 0, 0)),
                  pl.BlockSpec((None, BLK, LANE), lambda h, i: (h, i, 0)),
                  pl.BlockSpec((BLK, qw), lambda h, i: (i, h)),
                  pl.BlockSpec((BLK, qw), lambda h, i: (i, h))],
        out_specs=pl.BlockSpec((BLK, qw), lambda h, i: (i, h)),
        scratch_shapes=[pltpu.VMEM((rows, HEAD_DIM + n_sel), BF16), pltpu.VMEM((2, rows, SEL_CHUNK), F32),
                        pltpu.VMEM((2, rows, LANE), F32), pltpu.VMEM((rows, LANE), F32),
                        pltpu.VMEM((rows, 2 * HEAD_DIM), F32)],
        compiler_params=_cparams(("parallel", "arbitrary")),
        name="sel_attention",
    )(z, pen, kaug, vaug, bias_sel, gates_kv, o_cmp, o_win)


def _banded_body(q_ref, k_ref, v_ref, bias_ref, o_ref, qs_ref, sink_ref, *, group, n_prev, window, scale_q):
    kv = pl.program_id(0)
    qb = pl.program_id(1)
    if scale_q:
        _stack_scaled_queries(q_ref, qs_ref, group)
    qs = qs_ref[...]
    row = lax.broadcasted_iota(jnp.int32, (BLK, BLK), 0)
    col = lax.broadcasted_iota(jnp.int32, (BLK, BLK), 1)
    ones = jnp.ones((BLK, HEAD_DIM), BF16)
    tiles, vaugs = [], []
    for delta in range(n_prev + 1):
        kb = qb - delta
        start = pl.multiple_of(jnp.maximum(kb, 0) * BLK, BLK)
        dist = delta * BLK + row - col
        limit = jnp.where(kb >= 0, window, 0)
        valid = jnp.logical_and(dist >= 0, dist < limit)
        bias = jnp.concatenate([jnp.where(valid, bias_ref[delta, g], NEG_INF) for g in range(group)], axis=0)
        tiles.append(_dot_nt(qs, k_ref[pl.ds(start, BLK), :]) + bias)
        vaugs.append(jnp.concatenate([v_ref[pl.ds(start, BLK), :], ones], axis=1))
    m = jnp.max(functools.reduce(jnp.maximum, tiles), axis=-1, keepdims=True)
    if sink_ref is not None:
        sink = jnp.concatenate([jnp.full((BLK, 1), sink_ref[kv * group + g], F32) for g in range(group)], axis=0)
        m = jnp.maximum(m, sink)
    acc = functools.reduce(jnp.add, [_dot(jnp.exp(tile - m).astype(BF16), vaug) for tile, vaug in zip(tiles, vaugs)])
    denom = acc[:, HEAD_DIM:]
    if sink_ref is not None:
        denom = denom + jnp.exp(sink - m)
    out = acc[:, :HEAD_DIM] / denom
    for g in range(group):
        o_ref[:, g * HEAD_DIM:(g + 1) * HEAD_DIM] = out[g * BLK:(g + 1) * BLK, :].astype(o_ref.dtype)


def _stack_scaled_queries(q_ref, qs_ref, group):
    for g in range(group):
        qs_ref[g * BLK:(g + 1) * BLK, :] = (q_ref[:, g * HEAD_DIM:(g + 1) * HEAD_DIM].astype(F32) * SCALE).astype(BF16)


def _banded_kernel(sink_ref, q_ref, k_ref, v_ref, bias_ref, o_ref, qs_ref, *, group, n_prev, window):
    _banded_body(q_ref, k_ref, v_ref, bias_ref, o_ref, qs_ref, sink_ref, group=group, n_prev=n_prev, window=window,
                 scale_q=True)


def banded_attention(z, bias_raw, t, *, q_block, k_block, v_block, n_kv, group, window, sinks):
    n_prev = -(-window // BLK)
    qw = group * HEAD_DIM
    assert q_block % group == 0
    return pl.pallas_call(
        functools.partial(_banded_kernel, group=group, n_prev=n_prev, window=window),
        out_shape=jax.ShapeDtypeStruct((t, n_kv * qw), BF16),
        grid=(n_kv, t // BLK),
        in_specs=[pl.BlockSpec(memory_space=pltpu.SMEM),
                  pl.BlockSpec((BLK, qw), lambda h, i: (i, q_block // group + h)),
                  pl.BlockSpec((t, HEAD_DIM), lambda h, i: (0, k_block + h)),
                  pl.BlockSpec((t, HEAD_DIM), lambda h, i: (0, v_block + h)),
                  pl.BlockSpec((n_prev + 1, group, BLK, BLK), lambda h, i: (0, h, 0, 0))],
        out_specs=pl.BlockSpec((BLK, qw), lambda h, i: (i, h)),
        scratch_shapes=[pltpu.VMEM((group * BLK, HEAD_DIM), BF16)],
        compiler_params=_cparams(("parallel", "arbitrary")),
        name="banded_attention",
    )(sinks, z, z, z, bias_raw)


def _sg_kernel(u_ref, v_ref, lng_ref, lnb_ref, w_ref, b_ref, o_ref):
    u = _gelu_tanh(u_ref[...].astype(F32))
    gv = _gelu_tanh(v_ref[...].astype(F32))
    mu = jnp.mean(gv, axis=-1, keepdims=True)
    cen = gv - mu
    var = jnp.mean(cen * cen, axis=-1, keepdims=True)
    gv = (cen * lax.rsqrt(var + RMS_EPS) * lng_ref[...] + lnb_ref[...]).astype(BF16)
    row = lax.broadcasted_iota(jnp.int32, (SG_CHUNK, SG_CHUNK), 0)
    col = lax.broadcasted_iota(jnp.int32, (SG_CHUNK, SG_CHUNK), 1)
    gw = u.shape[1] // SG_GROUPS
    for g in range(SG_GROUPS):
        sl = slice(g * gw, (g + 1) * gw)
        w = jnp.where(row >= col, w_ref[g], 0.0).astype(BF16)
        spatial = _dot(w, gv[:, sl]) + b_ref[:, g:g + 1]
        o_ref[:, sl] = (u[:, sl] * spatial).astype(o_ref.dtype)


def spatial_gating(z1, ln_g, ln_b, sg_w, sg_b, t):
    width = ln_g.shape[0]
    assert width % (SG_GROUPS * LANE) == 0
    return pl.pallas_call(
        _sg_kernel,
        out_shape=jax.ShapeDtypeStruct((t, width), BF16),
        grid=(t // SG_CHUNK,),
        in_specs=[pl.BlockSpec((SG_CHUNK, width), lambda i: (i, 0)),
                  pl.BlockSpec((SG_CHUNK, width), lambda i: (i, 1)),
                  pl.BlockSpec((1, width), lambda i: (0, 0)),
                  pl.BlockSpec((1, width), lambda i: (0, 0)),
                  pl.BlockSpec((SG_GROUPS, SG_CHUNK, SG_CHUNK), lambda i: (0, 0, 0)),
                  pl.BlockSpec((SG_CHUNK, SG_GROUPS), lambda i: (0, 0))],
        out_specs=pl.BlockSpec((SG_CHUNK, width), lambda i: (i, 0)),
        compiler_params=_cparams(("parallel",)),
        name="spatial_gating",
    )(z1, z1, ln_g.reshape(1, width), ln_b.reshape(1, width), sg_w, jnp.transpose(sg_b))


MOE_TM = 512
MOE_TOP_K = 2


def moe_plan(idx, gate, t):
    n_rows = MOE_TOP_K * t + N_EXPERTS * MOE_TM
    n_blocks = n_rows // MOE_TM
    pair_e = idx[:, :MOE_TOP_K].reshape(-1)
    pair_w = gate[:, :MOE_TOP_K].reshape(-1)
    onehot = (pair_e[:, None] == jnp.arange(N_EXPERTS, dtype=jnp.int32)[None, :]).astype(jnp.int32)
    before = jnp.cumsum(onehot, axis=0) - onehot
    count = jnp.sum(onehot, axis=0)
    padded = (count + MOE_TM - 1) // MOE_TM * MOE_TM
    group_end = jnp.cumsum(padded)
    group_start = group_end - padded
    pair_row = jnp.sum(onehot * (before + group_start[None, :]), axis=1)
    row_pair = jnp.full((n_rows,), -1, jnp.int32).at[pair_row].set(jnp.arange(MOE_TOP_K * t, dtype=jnp.int32))
    row_tok = jnp.maximum(row_pair, 0) // MOE_TOP_K
    row_w = jnp.where(row_pair >= 0, pair_w[jnp.maximum(row_pair, 0)], 0.0)
    block_start = jnp.arange(n_blocks, dtype=jnp.int32) * MOE_TM
    block_e = jnp.minimum(jnp.sum(block_start[:, None] >= group_end[None, :], axis=1), N_EXPERTS - 1).astype(jnp.int32)
    n_used = (group_end[-1] // MOE_TM).astype(jnp.int32).reshape(1)
    return row_tok, row_w.reshape(n_rows, 1), block_e, n_used, pair_row.astype(jnp.int32)


MOE_GATHER_ROWS = 256


def _gather_rows_kernel(tok_ref, tok_next_ref, src_ref, o_ref, buf, sem):
    b = pl.program_id(0)
    slot = b % 2

    def row_copy(idx_ref, k, s):
        return pltpu.make_async_copy(src_ref.at[idx_ref[k]], buf.at[s, k], sem.at[s])

    def start_block(idx_ref, s):
        def start(k, carry):
            row_copy(idx_ref, k, s).start()
            return carry
        lax.fori_loop(0, MOE_GATHER_ROWS, start, 0, unroll=8)

    @pl.when(b == 0)
    def _():
        start_block(tok_ref, 0)

    @pl.when(b + 1 < pl.num_programs(0))
    def _():
        start_block(tok_next_ref, 1 - slot)

    def wait(k, carry):
        row_copy(tok_ref, k, slot).wait()
        return carry

    lax.fori_loop(0, MOE_GATHER_ROWS, wait, 0, unroll=8)
    o_ref[...] = buf[slot].reshape(o_ref.shape)


def gather_rows(src, row_tok):
    n_rows = row_tok.shape[0]
    d = src.shape[1] * src.shape[2]
    n_steps = n_rows // MOE_GATHER_ROWS
    return pl.pallas_call(
        _gather_rows_kernel,
        out_shape=jax.ShapeDtypeStruct((n_rows, d), src.dtype),
        grid=(n_steps,),
        in_specs=[pl.BlockSpec((MOE_GATHER_ROWS,), lambda b: (b,), memory_space=pltpu.SMEM),
                  pl.BlockSpec((MOE_GATHER_ROWS,), lambda b: (jnp.minimum(b + 1, n_steps - 1),),
                               memory_space=pltpu.SMEM),
                  pl.BlockSpec(memory_space=pl.ANY)],
        out_specs=pl.BlockSpec((MOE_GATHER_ROWS, d), lambda b: (b, 0)),
        scratch_shapes=[pltpu.VMEM((2, MOE_GATHER_ROWS) + src.shape[1:], src.dtype), pltpu.SemaphoreType.DMA((2,))],
        compiler_params=_cparams(("arbitrary",)),
        name="moe_gather",
    )(row_tok, row_tok, src)


def _expert_changed(b, block_e_ref):
    prev = jnp.maximum(b - 1, 0)
    return jnp.logical_or(b == 0, block_e_ref[b] != block_e_ref[prev])


def _moe_up_kernel(block_e_ref, n_used_ref, x_ref, wg_ref, wu_ref, o_ref, wg_bf, wu_bf):
    b = pl.program_id(1)
    live = b < n_used_ref[0]

    @pl.when(jnp.logical_and(live, _expert_changed(b, block_e_ref)))
    def _():
        wg_bf[...] = wg_ref[...].astype(BF16)
        wu_bf[...] = wu_ref[...].astype(BF16)

    @pl.when(live)
    def _():
        x = x_ref[...]
        gate = _dot(x, wg_bf[...])
        o_ref[...] = ((gate * _sigmoid(gate)) * _dot(x, wu_bf[...])).astype(o_ref.dtype)

    @pl.when(jnp.logical_not(live))
    def _():
        o_ref[...] = jnp.zeros(o_ref.shape, o_ref.dtype)


def _moe_down_kernel(block_e_ref, n_used_ref, h_ref, wd_ref, w_ref, o_ref, wd_bf):
    b = pl.program_id(1)
    live = b < n_used_ref[0]

    @pl.when(jnp.logical_and(live, _expert_changed(b, block_e_ref)))
    def _():
        wd_bf[...] = wd_ref[...].astype(BF16)

    @pl.when(live)
    def _():
        o_ref[...] = w_ref[...] * _dot(h_ref[...], wd_bf[...])

    @pl.when(jnp.logical_not(live))
    def _():
        o_ref[...] = jnp.zeros(o_ref.shape, o_ref.dtype)


def moe_experts(xs, row_w, block_e, n_used, w_gate, w_up, w_down):
    n_rows, d = xs.shape
    d_ff = w_gate.shape[3]
    n_blocks = n_rows // MOE_TM
    tn_up, tn_down = 512, 1024
    blk = lambda b, nu: jnp.minimum(b, nu[0] - 1)
    hid = pl.pallas_call(
        _moe_up_kernel,
        out_shape=jax.ShapeDtypeStruct((n_rows, d_ff), BF16),
        grid_spec=pltpu.PrefetchScalarGridSpec(
            num_scalar_prefetch=2,
            grid=(d_ff // tn_up, n_blocks),
            in_specs=[pl.BlockSpec((MOE_TM, d), lambda j, b, be, nu: (blk(b, nu), 0)),
                      pl.BlockSpec((None, None, d, tn_up), lambda j, b, be, nu: (0, be[blk(b, nu)], 0, j)),
                      pl.BlockSpec((None, None, d, tn_up), lambda j, b, be, nu: (0, be[blk(b, nu)], 0, j))],
            out_specs=pl.BlockSpec((MOE_TM, tn_up), lambda j, b, be, nu: (b, j)),
            scratch_shapes=[pltpu.VMEM((d, tn_up), BF16), pltpu.VMEM((d, tn_up), BF16)]),
        compiler_params=_cparams(("arbitrary", "arbitrary")),
        name="moe_up",
    )(block_e, n_used, xs, w_gate, w_up)
    return pl.pallas_call(
        _moe_down_kernel,
        out_shape=jax.ShapeDtypeStruct((n_rows, d), F32),
        grid_spec=pltpu.PrefetchScalarGridSpec(
            num_scalar_prefetch=2,
            grid=(d // tn_down, n_blocks),
            in_specs=[pl.BlockSpec((MOE_TM, d_ff), lambda j, b, be, nu: (blk(b, nu), 0)),
                      pl.BlockSpec((None, None, d_ff, tn_down), lambda j, b, be, nu: (0, be[blk(b, nu)], 0, j)),
                      pl.BlockSpec((MOE_TM, 1), lambda j, b, be, nu: (blk(b, nu), 0))],
            out_specs=pl.BlockSpec((MOE_TM, tn_down), lambda j, b, be, nu: (b, j)),
            scratch_shapes=[pltpu.VMEM((d_ff, tn_down), BF16)]),
        compiler_params=_cparams(("arbitrary", "arbitrary")),
        name="moe_down",
    )(block_e, n_used, hid, w_down, row_w)


MOE_TC = 128


def _moe_combine_kernel(rows_ref, rows_next_ref, x_ref, g_ref, y_ref, o_ref, buf, sem):
    b = pl.program_id(0)
    slot = b % 2

    def row_copy(idx_ref, k, choice, s):
        row = idx_ref[MOE_TOP_K * k + choice]
        return pltpu.make_async_copy(y_ref.at[pl.ds(row, 1), :], buf.at[s, choice, pl.ds(k, 1), :], sem.at[s])

    def start_block(idx_ref, s):
        def start(k, carry):
            for choice in range(MOE_TOP_K):
                row_copy(idx_ref, k, choice, s).start()
            return carry
        lax.fori_loop(0, MOE_TC, start, 0, unroll=4)

    @pl.when(b == 0)
    def _():
        start_block(rows_ref, 0)

    @pl.when(b + 1 < pl.num_programs(0))
    def _():
        start_block(rows_next_ref, 1 - slot)

    def wait(k, carry):
        for choice in range(MOE_TOP_K):
            row_copy(rows_ref, k, choice, slot).wait()
        return carry

    lax.fori_loop(0, MOE_TC, wait, 0, unroll=4)
    x = x_ref[...]
    for choice in range(MOE_TOP_K):
        x = x + buf[slot, choice]
    y = x * lax.rsqrt(jnp.mean(x * x, axis=-1, keepdims=True) + RMS_EPS)
    o_ref[...] = (y * g_ref[...]).astype(o_ref.dtype)


def moe_combine_rmsnorm(x, ys, pair_row, g):
    t, d = x.shape
    n_steps = t // MOE_TC
    idx_block = lambda index_map: pl.BlockSpec((MOE_TOP_K * MOE_TC,), index_map, memory_space=pltpu.SMEM)
    return pl.pallas_call(
        _moe_combine_kernel,
        out_shape=jax.ShapeDtypeStruct((t, d), F32),
        grid=(n_steps,),
        in_specs=[idx_block(lambda i: (i,)),
                  idx_block(lambda i: (jnp.minimum(i + 1, n_steps - 1),)),
                  pl.BlockSpec((MOE_TC, d), lambda i: (i, 0)),
                  pl.BlockSpec((1, d), lambda i: (0, 0)),
                  pl.BlockSpec(memory_space=pl.ANY)],
        out_specs=pl.BlockSpec((MOE_TC, d), lambda i: (i, 0)),
        scratch_shapes=[pltpu.VMEM((2, MOE_TOP_K, MOE_TC, d), F32), pltpu.SemaphoreType.DMA((2,))],
        compiler_params=_cparams(("arbitrary",)),
        name="moe_combine",
    )(pair_row, pair_row, x, g.reshape(1, d), ys)


def kernel(x, rel_table, norm_mix, norm_ffn, norm_final, ab_w_in, ab_w_out, cmp_k_pe, cmp_k_w1, cmp_k_b1, cmp_k_w2, cmp_v_pe, cmp_v_w1, cmp_v_b1, cmp_v_w2, ffn_w_gate, ffn_w_up, ffn_w_down, cd_w_in, cd_w_out, sg_ln_g, sg_ln_b, sg_w, sg_b, swa_sinks, moe_w_router, moe_w_gate, moe_w_up, moe_w_down):
    b, t, d = x.shape
    assert b == 1
    xs = x.reshape(t, d)
    bias_raw, bias_sel, bias_cmp = bias_tiles(rel_table)

    h = rmsnorm(xs, norm_mix[0], BF16)
    ab_main = AB_MAIN_BLOCKS * LANE
    w_in_t = jnp.swapaxes(ab_w_in, 1, 2)
    z = matmul(h, [w_in_t], (0,), ab_main, out_dtype=BF16, w_transposed=True)
    w_gate_rows = jnp.pad(w_in_t[0, ab_main:], ((0, LANE - (w_in_t.shape[1] - ab_main)), (0, 0)))
    gate_logits = matmul(h, [w_gate_rows], (), LANE, out_dtype=F32, w_transposed=True)
    o_sb = sb_attention(z, t)
    kc = compress(z, AB_K_C, cmp_k_pe[0], cmp_k_w1[0], cmp_k_b1[0], cmp_k_w2[0], t)
    vc = compress(z, AB_V_C, cmp_v_pe[0], cmp_v_w1[0], cmp_v_b1[0], cmp_v_w2[0], t)
    o_cmp, pen, o_win = cmp_select_window(z, kc, vc, bias_cmp, bias_raw, t)
    o_nsa = sel_attention_mix(z, pen, bias_sel, gate_logits, o_cmp, o_win, t)
    xs = matmul([o_sb, o_nsa], [ab_w_out], (0,), d, out_dtype=F32, res=xs)
    h = rmsnorm(xs, norm_ffn[0], BF16)
    hid = matmul(h, [ffn_w_gate, ffn_w_up], (0,), ffn_w_gate.shape[2], out_dtype=BF16, act="swiglu")
    xs = matmul(hid, [ffn_w_down], (0,), d, out_dtype=F32, res=xs)

    h = rmsnorm(xs, norm_mix[1], BF16)
    z1 = matmul(h, [cd_w_in], (0,), cd_w_in.shape[2], out_dtype=BF16)
    o_sg = spatial_gating(z1, sg_ln_g[0], sg_ln_b[0], sg_w[0], sg_b[0], t)
    o_swa = banded_attention(z1, bias_raw, t, q_block=CD_Q, k_block=CD_K, v_block=CD_V,
                             n_kv=SWA_KV_HEADS, group=SWA_HEADS // SWA_KV_HEADS, window=SWA_WINDOW,
                             sinks=swa_sinks[0])
    xs = matmul([o_sg, o_swa], [cd_w_out], (0,), d, out_dtype=F32, res=xs)
    h, gate, idx = rmsnorm_router(xs, norm_ffn[1], moe_w_router[0])
    row_tok, row_w, block_e, n_used, pair_row = moe_plan(idx, gate, t)
    rows = gather_rows(h, row_tok)
    ys = moe_experts(rows, row_w, block_e, n_used, moe_w_gate, moe_w_up, moe_w_down)
    return moe_combine_rmsnorm(xs, ys, pair_row, norm_final).reshape(b, t, d)
```

```python
import functools
import math

import numpy as np
import jax
import jax.numpy as jnp
from jax import lax
from jax.experimental import pallas as pl
from jax.experimental.pallas import tpu as pltpu

F32 = jnp.float32
BF16 = jnp.bfloat16

LANE = 128
VMEM_LIMIT = 56 * 1024 * 1024

HEAD_DIM = 128
BLK = 128
NEG_INF = -1e30
RMS_EPS = 1e-6
SCALE = HEAD_DIM ** -0.5

SB_HEADS = 16
NSA_HEADS = 16
NSA_KV_HEADS = 4
NSA_G = NSA_HEADS // NSA_KV_HEADS
NSA_CMP_LEN = 32
NSA_CMP_STRIDE = 16
NSA_SEL_LEN = 64
NSA_SEL_TOPK = 16
NSA_WINDOW = 512
NSA_FORCE_SCORE = 1e6
SG_GROUPS = 8
SG_CHUNK = 128
SWA_HEADS = 16
SWA_KV_HEADS = 2
SWA_WINDOW = 128
REL_BUCKETS = 32
REL_MAX_DIST = 1024
REL_HEADS = 16
N_EXPERTS = 8

AB_Q_SB, AB_K_SB, AB_V_SB = 0, 16, 32
AB_Q_N = 48
AB_K_C, AB_V_C, AB_K_S, AB_V_S, AB_K_W, AB_V_W = 64, 68, 72, 76, 80, 84
AB_MAIN_BLOCKS = 88
CD_Q, CD_K, CD_V = 32, 48, 50

SEL_CHUNK_BLOCKS = 4
SEL_FAR_BLOCKS = 8
SEL_NEG_BLOCKS = SEL_CHUNK_BLOCKS - 1
SEL_TILES = SEL_NEG_BLOCKS + SEL_FAR_BLOCKS + SEL_CHUNK_BLOCKS - 1
RAW_TILES = NSA_WINDOW // BLK + 1
CMP_PAD = 120
SB_SKIP_LOG = -110.0
M_INIT = -5e29


def _cparams(sem):
    return pltpu.CompilerParams(dimension_semantics=sem, vmem_limit_bytes=VMEM_LIMIT)


def _dot(a, b):
    return jnp.dot(a, b, preferred_element_type=F32)


def _dot_nt(a, b):
    return lax.dot_general(a, b, (((1,), (1,)), ((), ())), preferred_element_type=F32)


def _dot_split(a, b):
    a_hi = a.astype(BF16)
    b_hi = b.astype(BF16)
    a_lo = (a - a_hi.astype(F32)).astype(BF16)
    b_lo = (b - b_hi.astype(F32)).astype(BF16)
    return _dot(a_hi, b_hi) + (_dot(a_hi, b_lo) + _dot(a_lo, b_hi))


def _sigmoid(x):
    return 1.0 / (1.0 + jnp.exp(-x))


def _gelu_tanh(x):
    return 0.5 * x * (1.0 + jnp.tanh(math.sqrt(2.0 / math.pi) * (x + 0.044715 * (x * x * x))))


def _rmsnorm_kernel(x_ref, g_ref, o_ref):
    x = x_ref[...]
    y = x * lax.rsqrt(jnp.mean(x * x, axis=-1, keepdims=True) + RMS_EPS)
    o_ref[...] = (y * g_ref[...]).astype(o_ref.dtype)


def rmsnorm(x, g, out_dtype):
    t, d = x.shape
    tm = 256
    return pl.pallas_call(
        _rmsnorm_kernel,
        out_shape=jax.ShapeDtypeStruct((t, d), out_dtype),
        grid=(t // tm,),
        in_specs=[pl.BlockSpec((tm, d), lambda i: (i, 0)), pl.BlockSpec((1, d), lambda i: (0, 0))],
        out_specs=pl.BlockSpec((tm, d), lambda i: (i, 0)),
        compiler_params=_cparams(("parallel",)),
        name="rmsnorm",
    )(x, g.reshape(1, d))


def _rmsnorm_router_kernel(x_ref, g_ref, wr_ref, o_ref, gate_ref, idx_ref):
    x = x_ref[...]
    y = x * lax.rsqrt(jnp.mean(x * x, axis=-1, keepdims=True) + RMS_EPS)
    y = y * g_ref[...]
    o_ref[...] = y.astype(o_ref.dtype).reshape(o_ref.shape)
    logits = _dot_split(y, wr_ref[...])
    lane = lax.broadcasted_iota(jnp.int32, logits.shape, 1)
    logits = jnp.where(lane < N_EXPERTS, logits, NEG_INF)
    m1 = jnp.max(logits, axis=-1, keepdims=True)
    i1 = jnp.min(jnp.where(logits == m1, lane, LANE), axis=-1, keepdims=True)
    rest = jnp.where(lane == i1, NEG_INF, logits)
    m2 = jnp.max(rest, axis=-1, keepdims=True)
    i2 = jnp.min(jnp.where(rest == m2, lane, LANE), axis=-1, keepdims=True)
    e2 = jnp.exp(m2 - m1)
    denom = 1.0 + e2
    gate_ref[...] = jnp.where(lane == 0, 1.0 / denom, jnp.where(lane == 1, e2 / denom, 0.0))
    idx_ref[...] = jnp.where(lane == 0, i1, jnp.where(lane == 1, i2, 0))


def rmsnorm_router(x, g, w_router):
    t, d = x.shape
    tm = 256
    wr = jnp.pad(w_router, ((0, 0), (0, LANE - w_router.shape[1])))
    return pl.pallas_call(
        _rmsnorm_router_kernel,
        out_shape=(jax.ShapeDtypeStruct((t, d // LANE, LANE), BF16), jax.ShapeDtypeStruct((t, LANE), F32),
                   jax.ShapeDtypeStruct((t, LANE), jnp.int32)),
        grid=(t // tm,),
        in_specs=[pl.BlockSpec((tm, d), lambda i: (i, 0)), pl.BlockSpec((1, d), lambda i: (0, 0)),
                  pl.BlockSpec((d, LANE), lambda i: (0, 0))],
        out_specs=(pl.BlockSpec((tm, d // LANE, LANE), lambda i: (i, 0, 0)), pl.BlockSpec((tm, LANE), lambda i: (i, 0)),
                   pl.BlockSpec((tm, LANE), lambda i: (i, 0))),
        compiler_params=_cparams(("parallel",)),
        name="rmsnorm_router",
    )(x, g.reshape(1, d), wr)


def _mm_epilogue(parts, res_ref, o_ref, act):
    r = parts[0]
    if act == "swiglu":
        r = (r * _sigmoid(r)) * parts[1]
    if res_ref is not None:
        r = res_ref[...] + r
    o_ref[...] = r.astype(o_ref.dtype)


def _mm_resident_kernel(*refs, n_a, n_w, has_res, act, w_transposed):
    a_refs = refs[:n_a]
    w_refs = refs[n_a:n_a + n_w]
    res_ref = refs[n_a + n_w] if has_res else None
    o_ref = refs[n_a + n_w + int(has_res)]
    wbf_refs = refs[n_a + n_w + int(has_res) + 1:]

    @pl.when(pl.program_id(1) == 0)
    def _():
        for w, wbf in zip(w_refs, wbf_refs):
            wbf[...] = w[...].astype(BF16)

    parts = []
    for wbf in wbf_refs:
        k0, part = 0, None
        for a_ref in a_refs:
            k1 = k0 + a_ref.shape[1]
            term = _dot_nt(a_ref[...], wbf[:, k0:k1]) if w_transposed else _dot(a_ref[...], wbf[k0:k1, :])
            part = term if part is None else part + term
            k0 = k1
        parts.append(part)
    _mm_epilogue(parts, res_ref, o_ref, act)


def _mm_ktiled_kernel(a_ref, w_ref, res_ref, o_ref, acc_ref, *, nk):
    k = pl.program_id(2)
    part = _dot(a_ref[...], w_ref[...].astype(BF16))

    @pl.when(k == 0)
    def _():
        acc_ref[...] = part

    @pl.when(k > 0)
    def _():
        acc_ref[...] += part

    @pl.when(k == nk - 1)
    def _():
        _mm_epilogue([acc_ref[...]], res_ref, o_ref, None)


MM_RESIDENT_MAX_K = 4096
MM_VMEM_BUDGET = 50 * 1024 * 1024


def _resident_tiles(m, kdim, n_cols, n_w, out_bytes, has_res):
    for tn in (1024, 512, 256, LANE):
        for tm in (1024, 512):
            if n_cols % tn or m % tm:
                continue
            a_buf = 2 * tm * kdim * 2
            w_buf = n_w * (2 * kdim * tn * 4 + kdim * tn * 2)
            o_buf = 2 * tm * tn * out_bytes + (2 * tm * tn * 4 if has_res else 0)
            if a_buf + w_buf + o_buf <= MM_VMEM_BUDGET:
                return tm, tn
    raise ValueError("no resident-panel tiling fits VMEM")


def matmul(a, ws, lead, n_cols, *, out_dtype, res=None, act=None, w_transposed=False):
    a_parts = list(a) if isinstance(a, (list, tuple)) else [a]
    m = a_parts[0].shape[0]
    kdim = sum(part.shape[1] for part in a_parts)
    lead = tuple(lead)
    squeeze = (None,) * len(lead)
    if kdim <= MM_RESIDENT_MAX_K:
        tm, tn = _resident_tiles(m, kdim, n_cols, len(ws), jnp.dtype(out_dtype).itemsize, res is not None)
        in_specs = [pl.BlockSpec((tm, part.shape[1]), lambda j, i: (i, 0)) for part in a_parts]
        panel, panel_index = ((tn, kdim), lambda j: (j, 0)) if w_transposed else ((kdim, tn), lambda j: (0, j))
        in_specs += [pl.BlockSpec(squeeze + panel, lambda j, i: lead + panel_index(j)) for _ in ws]
        args = [*a_parts, *ws]
        if res is not None:
            in_specs.append(pl.BlockSpec((tm, tn), lambda j, i: (i, j)))
            args.append(res)
        return pl.pallas_call(
            functools.partial(_mm_resident_kernel, n_a=len(a_parts), n_w=len(ws), has_res=res is not None, act=act,
                              w_transposed=w_transposed),
            out_shape=jax.ShapeDtypeStruct((m, n_cols), out_dtype),
            grid=(n_cols // tn, m // tm),
            in_specs=in_specs,
            out_specs=pl.BlockSpec((tm, tn), lambda j, i: (i, j)),
            scratch_shapes=[pltpu.VMEM(panel, BF16) for _ in ws],
            compiler_params=_cparams(("parallel", "arbitrary")),
            name="matmul_resident",
        )(*args)
    assert len(ws) == 1 and act is None and res is not None and len(a_parts) == 1 and not w_transposed
    a = a_parts[0]
    tm, tn, tk = 2048, 512, 2048
    assert m % tm == 0 and n_cols % tn == 0 and kdim % tk == 0
    nk = kdim // tk
    return pl.pallas_call(
        functools.partial(_mm_ktiled_kernel, nk=nk),
        out_shape=jax.ShapeDtypeStruct((m, n_cols), out_dtype),
        grid=(n_cols // tn, m // tm, nk),
        in_specs=[pl.BlockSpec((tm, tk), lambda j, i, k: (i, k)),
                  pl.BlockSpec(squeeze + (tk, tn), lambda j, i, k: lead + (k, j)),
                  pl.BlockSpec((tm, tn), lambda j, i, k: (i, j))],
        out_specs=pl.BlockSpec((tm, tn), lambda j, i, k: (i, j)),
        scratch_shapes=[pltpu.VMEM((tm, tn), F32)],
        compiler_params=_cparams(("parallel", "parallel", "arbitrary")),
        name="matmul_ktiled",
    )(a, ws[0], res)


def _bucket_thresholds():
    d = np.arange(0, 4 * REL_MAX_DIST)
    max_exact = REL_BUCKETS // 2
    log_ratio = np.log(np.maximum(d, 1).astype(np.float64) / max_exact) / math.log(REL_MAX_DIST / max_exact)
    far = np.minimum(max_exact + (log_ratio * (REL_BUCKETS - max_exact)).astype(np.int32), REL_BUCKETS - 1)
    bucket = np.where(d < max_exact, d, far)
    assert np.all(np.diff(bucket) >= 0) and bucket[-1] == REL_BUCKETS - 1
    return [int(np.argmax(bucket >= b)) for b in range(REL_BUCKETS)]


_BUCKET_THR = _bucket_thresholds()
REL_FAR_DIST = _BUCKET_THR[REL_BUCKETS - 1]
assert REL_FAR_DIST <= SEL_FAR_BLOCKS * BLK - (BLK - 1)
assert REL_FAR_DIST <= NSA_CMP_STRIDE * (CMP_PAD + 1) - (NSA_CMP_LEN - 1)


def _bias_kernel(tab_ref, raw_ref, sel_ref, cmp_ref):
    h = pl.program_id(0)
    row = lax.broadcasted_iota(jnp.int32, (BLK, BLK), 0)
    col = lax.broadcasted_iota(jnp.int32, (BLK, BLK), 1)

    def lookup(dist):
        dd = jnp.maximum(dist, 0)
        b = jnp.full(dist.shape, tab_ref[0, h], F32)
        for bk in range(1, REL_BUCKETS):
            b = jnp.where(dd >= _BUCKET_THR[bk], tab_ref[bk, h], b)
        return b

    far = tab_ref[REL_BUCKETS - 1, h]
    for d in range(RAW_TILES):
        raw_ref[d] = lookup(d * BLK + row - col)
    for step in range(SEL_TILES):
        dist = (step - SEL_NEG_BLOCKS) * BLK + row - col
        sel_ref[step] = jnp.where(dist < 0, NEG_INF, lookup(dist) - far)
    dist_c = row - NSA_CMP_STRIDE * (col - CMP_PAD) - (NSA_CMP_LEN - 1)
    cmp_ref[...] = lookup(dist_c) - far


def bias_tiles(rel_table):
    tiles = lambda n: pl.BlockSpec((n, None, BLK, BLK), lambda h: (0, h, 0, 0))
    return pl.pallas_call(
        _bias_kernel,
        out_shape=(jax.ShapeDtypeStruct((RAW_TILES, REL_HEADS, BLK, BLK), F32),
                   jax.ShapeDtypeStruct((SEL_TILES, REL_HEADS, BLK, BLK), F32),
                   jax.ShapeDtypeStruct((REL_HEADS, BLK, BLK), F32)),
        grid=(REL_HEADS,),
        in_specs=[pl.BlockSpec(memory_space=pltpu.SMEM)],
        out_specs=(tiles(RAW_TILES), tiles(SEL_TILES), pl.BlockSpec((None, BLK, BLK), lambda h: (h, 0, 0))),
        compiler_params=_cparams(("arbitrary",)),
        name="bias_tiles",
    )(rel_table)


SB_BQ = 256
SB_GROUP = 4


def _sb_kernel(q_ref, k_ref, v_ref, o_ref, c_ref, acc_ref):
    qb = pl.program_id(1)
    row = lax.broadcasted_iota(jnp.int32, (SB_BQ, SB_BQ), 0)
    col = lax.broadcasted_iota(jnp.int32, (SB_BQ, SB_BQ), 1)
    upper = (row > col).astype(BF16)

    def block(h, kb, past):
        cols = slice(h * HEAD_DIM, (h + 1) * HEAD_DIM)
        ks = pl.multiple_of(kb * SB_BQ, SB_BQ)
        z = _dot_nt(q_ref[:, cols], k_ref[pl.ds(ks, SB_BQ), cols]) * SCALE
        sp = jnp.maximum(z, 0.0) + jnp.log(1.0 + jnp.exp(-jnp.abs(z)))
        log_stay = -sp
        if past is not None:
            log_stay = jnp.where(past, log_stay, 0.0)
        later = _dot(log_stay.astype(BF16), upper)
        tot = jnp.broadcast_to(jnp.sum(log_stay, axis=-1, keepdims=True), (SB_BQ, HEAD_DIM))
        if past is not None:
            w = jnp.where(past, jnp.exp((z - sp) + later), 0.0)
            c_new = tot
        else:
            carry = c_ref[:, cols]
            w = jnp.exp((z - sp) + later + jnp.concatenate([carry] * (SB_BQ // HEAD_DIM), axis=1))
            c_new = carry + tot
        contrib = _dot(w.astype(BF16), v_ref[pl.ds(ks, SB_BQ), cols])
        if past is not None:
            acc_ref[:, cols] = contrib
        else:
            acc_ref[:, cols] += contrib
        c_ref[:, cols] = c_new
        return jnp.max(c_new)

    def sweep(kb, past):
        return functools.reduce(jnp.maximum, [block(h, kb, past) for h in range(SB_GROUP)])

    def cond(state):
        kb, c_max = state
        return jnp.logical_and(kb >= 0, c_max > SB_SKIP_LOG)

    def body(state):
        kb, _ = state
        return kb - 1, sweep(kb, None)

    lax.while_loop(cond, body, (qb - 1, sweep(qb, col < row)))
    o_ref[...] = acc_ref[...].astype(o_ref.dtype)


def sb_attention(z, t):
    width = SB_GROUP * HEAD_DIM
    assert SB_HEADS % SB_GROUP == 0 and AB_K_SB % SB_GROUP == 0 and AB_V_SB % SB_GROUP == 0
    return pl.pallas_call(
        _sb_kernel,
        out_shape=jax.ShapeDtypeStruct((t, SB_HEADS * HEAD_DIM), BF16),
        grid=(SB_HEADS // SB_GROUP, t // SB_BQ),
        in_specs=[pl.BlockSpec((SB_BQ, width), lambda h, i: (i, AB_Q_SB // SB_GROUP + h)),
                  pl.BlockSpec((t, width), lambda h, i: (0, AB_K_SB // SB_GROUP + h)),
                  pl.BlockSpec((t, width), lambda h, i: (0, AB_V_SB // SB_GROUP + h))],
        out_specs=pl.BlockSpec((SB_BQ, width), lambda h, i: (i, h)),
        scratch_shapes=[pltpu.VMEM((SB_BQ, width), F32), pltpu.VMEM((SB_BQ, width), F32)],
        compiler_params=_cparams(("parallel", "arbitrary")),
        name="sb_attention",
    )(z, z, z)


def _compress_kernel(a_ref, pe_ref, w1_ref, b1_ref, w2_ref, o_ref):
    n_chunk = a_ref.shape[0]
    half = NSA_CMP_STRIDE * HEAD_DIM
    chunks = a_ref[...].reshape(n_chunk, half).astype(F32)
    first = _dot((chunks + pe_ref[:, :half]).astype(BF16), w1_ref[:half, :].astype(BF16))
    second = _dot((chunks + pe_ref[:, half:]).astype(BF16), w1_ref[half:, :].astype(BF16))
    hid = _gelu_tanh(first + pltpu.roll(second, n_chunk - 1, 0) + b1_ref[...])
    out = _dot(hid.astype(BF16), w2_ref[...].astype(BF16))
    o_ref[0:CMP_PAD, :] = jnp.zeros((CMP_PAD, HEAD_DIM), o_ref.dtype)
    o_ref[CMP_PAD:CMP_PAD + n_chunk, :] = out.astype(o_ref.dtype)
    o_ref[CMP_PAD + n_chunk:, :] = jnp.zeros((o_ref.shape[0] - CMP_PAD - n_chunk, HEAD_DIM), o_ref.dtype)


def compress(z, col_block, pe, w1, b1, w2, t):
    assert NSA_CMP_LEN == 2 * NSA_CMP_STRIDE
    n_chunk = t // NSA_CMP_STRIDE
    width = NSA_CMP_LEN * HEAD_DIM
    hidden = w1.shape[-1]
    rows = CMP_PAD + n_chunk + 8
    return pl.pallas_call(
        _compress_kernel,
        out_shape=jax.ShapeDtypeStruct((NSA_KV_HEADS, rows, HEAD_DIM), BF16),
        grid=(NSA_KV_HEADS,),
        in_specs=[pl.BlockSpec((n_chunk, NSA_CMP_STRIDE, HEAD_DIM), lambda h: (0, 0, col_block + h)),
                  pl.BlockSpec((1, width), lambda h: (0, 0)),
                  pl.BlockSpec((width, hidden), lambda h: (0, 0)),
                  pl.BlockSpec((1, hidden), lambda h: (0, 0)),
                  pl.BlockSpec((hidden, HEAD_DIM), lambda h: (0, 0))],
        out_specs=pl.BlockSpec((None, rows, HEAD_DIM), lambda h: (h, 0, 0)),
        compiler_params=_cparams(("parallel",)),
        name="compress",
    )(z.reshape(n_chunk, NSA_CMP_STRIDE, z.shape[1]), pe.reshape(1, width), w1, b1.reshape(1, hidden), w2)


def _cmp_to_sel_padded(t):
    n_cmp = (t - NSA_CMP_LEN) // NSA_CMP_STRIDE + 1
    n_sel = t // NSA_SEL_LEN
    cmp_end = np.arange(n_cmp) * NSA_CMP_STRIDE + NSA_CMP_LEN - 1
    cmp_start = cmp_end - (NSA_CMP_LEN - 1)
    sel_start = np.arange(n_sel) * NSA_SEL_LEN
    overlap = (np.minimum(cmp_end[:, None], sel_start[None, :] + NSA_SEL_LEN - 1)
               - np.maximum(cmp_start[:, None], sel_start[None, :]) + 1)
    c2s = np.clip(overlap, 0, None).astype(np.float32) / NSA_CMP_LEN
    out = np.zeros((CMP_PAD + n_cmp + 9, n_sel), np.float32)
    out[CMP_PAD:CMP_PAD + n_cmp] = c2s
    return out


def _cmp_win_kernel(q_ref, kc_ref, vc_ref, c2s_ref, bias_ref, kw_ref, vw_ref, bias_win_ref, o_ref, pen_ref, ow_ref,
                    qs_ref, *, n_far):
    qb = pl.program_id(1)
    _stack_scaled_queries(q_ref, qs_ref, NSA_G)
    r0 = pl.multiple_of(qb * (BLK // NSA_CMP_STRIDE), 8)
    kc_far = kc_ref[0:n_far, :]
    kc_near = kc_ref[pl.ds(r0, BLK), :]
    vs_far = jnp.concatenate([vc_ref[0:n_far, :], c2s_ref[0:n_far, :]], axis=1)
    vs_near = jnp.concatenate([vc_ref[pl.ds(r0, BLK), :], c2s_ref[pl.ds(r0, BLK), :]], axis=1)
    r_far = lax.broadcasted_iota(jnp.int32, (BLK, n_far), 1)
    far_ok = jnp.logical_and(r_far >= CMP_PAD, r_far < r0)
    row = lax.broadcasted_iota(jnp.int32, (BLK, BLK), 0)
    col = lax.broadcasted_iota(jnp.int32, (BLK, BLK), 1)
    dist_c = row - NSA_CMP_STRIDE * (col - CMP_PAD) - (NSA_CMP_LEN - 1)
    near_ok = jnp.logical_and(dist_c >= 0, col + r0 >= CMP_PAD)

    far_mask = jnp.where(far_ok, 0.0, NEG_INF)

    n_sel = c2s_ref.shape[1]
    imp = jnp.zeros((BLK, n_sel), F32)
    for g in range(NSA_G):
        q = qs_ref[g * BLK:(g + 1) * BLK, :]
        s_far = _dot_nt(q, kc_far) + far_mask
        s_near = _dot_nt(q, kc_near) + jnp.where(near_ok, bias_ref[g], NEG_INF)
        m = jnp.maximum(jnp.maximum(jnp.max(s_far, axis=-1, keepdims=True), jnp.max(s_near, axis=-1, keepdims=True)),
                        M_INIT)
        e_far = jnp.exp(s_far - m)
        e_near = jnp.exp(s_near - m)
        l = jnp.sum(e_far, axis=-1, keepdims=True) + jnp.sum(e_near, axis=-1, keepdims=True)
        inv = 1.0 / jnp.where(l > 0.0, l, 1.0)
        res = (_dot(e_far.astype(BF16), vs_far) + _dot(e_near.astype(BF16), vs_near)) * inv
        o_ref[:, g * HEAD_DIM:(g + 1) * HEAD_DIM] = res[:, :HEAD_DIM].astype(o_ref.dtype)
        imp = imp + res[:, HEAD_DIM:]

    blk = lax.broadcasted_iota(jnp.int32, (n_sel, BLK), 0)
    t_pos = qb * BLK + lax.broadcasted_iota(jnp.int32, (n_sel, BLK), 1)
    cur = t_pos // NSA_SEL_LEN
    forced = jnp.logical_or(blk == 0, jnp.logical_or(blk == cur, blk == cur - 1))
    score = jnp.where(forced, NSA_FORCE_SCORE, jnp.where(blk <= cur, imp.T, -1.0))
    pen = jnp.full((n_sel, BLK), NEG_INF, F32)
    for _ in range(min(NSA_SEL_TOPK, n_sel)):
        top = jnp.max(score, axis=0, keepdims=True)
        first = jnp.min(jnp.where(score == top, blk, n_sel), axis=0, keepdims=True)
        hit = blk == first
        pen = jnp.where(hit, 0.0, pen)
        score = jnp.where(hit, -2.0, score)
    pen_ref[...] = pen.T.astype(pen_ref.dtype)

    _banded_body(q_ref, kw_ref, vw_ref, bias_win_ref, ow_ref, qs_ref, None, group=NSA_G,
                 n_prev=NSA_WINDOW // BLK, window=NSA_WINDOW, scale_q=False)


def cmp_select_window(z, kc, vc, bias_cmp, bias_raw, t):
    assert NSA_WINDOW % BLK == 0
    n_sel = t // NSA_SEL_LEN
    assert n_sel == LANE, "selection mask rides in one 128-lane tile"
    c2s = jnp.asarray(_cmp_to_sel_padded(t), BF16)
    rows = kc.shape[1]
    assert c2s.shape[0] == rows
    n_far = t // NSA_CMP_STRIDE
    qw = NSA_G * HEAD_DIM
    return pl.pallas_call(
        functools.partial(_cmp_win_kernel, n_far=n_far),
        out_shape=(jax.ShapeDtypeStruct((t, NSA_HEADS * HEAD_DIM), BF16),
                   jax.ShapeDtypeStruct((NSA_KV_HEADS, t, n_sel), BF16),
                   jax.ShapeDtypeStruct((t, NSA_HEADS * HEAD_DIM), BF16)),
        grid=(NSA_KV_HEADS, t // BLK),
        in_specs=[pl.BlockSpec((BLK, qw), lambda h, i: (i, AB_Q_N // NSA_G + h)),
                  pl.BlockSpec((None, rows, HEAD_DIM), lambda h, i: (h, 0, 0)),
                  pl.BlockSpec((None, rows, HEAD_DIM), lambda h, i: (h, 0, 0)),
                  pl.BlockSpec((rows, n_sel), lambda h, i: (0, 0)),
                  pl.BlockSpec((NSA_G, BLK, BLK), lambda h, i: (h, 0, 0)),
                  pl.BlockSpec((t, HEAD_DIM), lambda h, i: (0, AB_K_W + h)),
                  pl.BlockSpec((t, HEAD_DIM), lambda h, i: (0, AB_V_W + h)),
                  pl.BlockSpec((RAW_TILES, NSA_G, BLK, BLK), lambda h, i: (0, h, 0, 0))],
        out_specs=(pl.BlockSpec((BLK, qw), lambda h, i: (i, h)),
                   pl.BlockSpec((None, BLK, n_sel), lambda h, i: (h, i, 0)),
                   pl.BlockSpec((BLK, qw), lambda h, i: (i, h))),
        scratch_shapes=[pltpu.VMEM((NSA_G * BLK, HEAD_DIM), BF16)],
        compiler_params=_cparams(("parallel", "arbitrary")),
        name="cmp_select_window",
    )(z, kc, vc, c2s, bias_cmp, z, z, bias_raw)


SEL_CHUNK = SEL_CHUNK_BLOCKS * BLK


def _sel_kernel(q_ref, pen_ref, kaug_ref, vaug_ref, bias_ref, gate_ref, ocmp_ref, owin_ref, o_ref,
                qa_ref, s_ref, cmax_ref, m_ref, acc_ref):
    qb = pl.program_id(1)
    pen = pen_ref[...]
    for g in range(NSA_G):
        rows = slice(g * BLK, (g + 1) * BLK)
        qa_ref[rows, :HEAD_DIM] = (q_ref[:, g * HEAD_DIM:(g + 1) * HEAD_DIM].astype(F32) * SCALE).astype(BF16)
        qa_ref[rows, HEAD_DIM:] = pen
    m_ref[...] = jnp.full(m_ref.shape, M_INIT, F32)
    acc_ref[...] = jnp.zeros(acc_ref.shape, F32)
    n_chunks = qb // SEL_CHUNK_BLOCKS + 1
    last_chunk = kaug_ref.shape[0] // SEL_CHUNK - 1

    def chunk_start(c):
        return pl.multiple_of(jnp.minimum(c, last_chunk) * SEL_CHUNK, SEL_CHUNK)

    def scores(c, slot):
        s_all = _dot_nt(qa_ref[...], kaug_ref[pl.ds(chunk_start(c), SEL_CHUNK), :])
        first = qb - c * SEL_CHUNK_BLOCKS + SEL_NEG_BLOCKS
        for g in range(NSA_G):
            rows = slice(g * BLK, (g + 1) * BLK)
            tiles = [s_all[rows, j * BLK:(j + 1) * BLK] + bias_ref[jnp.clip(first - j, 0, SEL_TILES - 1), g]
                     for j in range(SEL_CHUNK_BLOCKS)]
            for j, tile in enumerate(tiles):
                s_ref[slot, rows, j * BLK:(j + 1) * BLK] = tile
            cmax_ref[slot, rows, :] = jnp.broadcast_to(
                jnp.max(functools.reduce(jnp.maximum, tiles), axis=-1, keepdims=True), (BLK, LANE))

    def absorb(c, slot):
        va = vaug_ref[pl.ds(chunk_start(c), SEL_CHUNK), :]
        for g in range(NSA_G):
            rows = slice(g * BLK, (g + 1) * BLK)
            m_old = m_ref[rows, :]
            m_new = jnp.maximum(m_old, cmax_ref[slot, rows, :])
            alpha = jnp.exp(m_old - m_new)
            p = jnp.concatenate([jnp.exp(s_ref[slot, rows, j * BLK:(j + 1) * BLK] - m_new)
                                 for j in range(SEL_CHUNK_BLOCKS)], axis=1).astype(BF16)
            acc_ref[rows, :] = jnp.concatenate([alpha, alpha], axis=1) * acc_ref[rows, :] + _dot(p, va)
            m_ref[rows, :] = m_new

    def pair(k, carry):
        scores(2 * k + 1, 1)
        absorb(2 * k, 0)
        scores(2 * k + 2, 0)
        absorb(2 * k + 1, 1)
        return carry

    scores(0, 0)
    lax.fori_loop(0, n_chunks // 2, pair, 0)

    @pl.when(n_chunks % 2 == 1)
    def _():
        absorb(n_chunks - 1, 0)

    gates = _sigmoid(gate_ref[...])
    for g in range(NSA_G):
        rows = slice(g * BLK, (g + 1) * BLK)
        cols = slice(g * HEAD_DIM, (g + 1) * HEAD_DIM)
        o_sel = acc_ref[rows, :HEAD_DIM] / acc_ref[rows, HEAD_DIM:]
        mixed = (gates[:, 3 * g:3 * g + 1] * ocmp_ref[:, cols].astype(F32)
                 + gates[:, 3 * g + 1:3 * g + 2] * o_sel
                 + gates[:, 3 * g + 2:3 * g + 3] * owin_ref[:, cols].astype(F32))
        o_ref[:, cols] = mixed.astype(o_ref.dtype)


def sel_attention_mix(z, pen, bias_sel, gate_logits, o_cmp, o_win, t):
    n_sel = t // NSA_SEL_LEN
    n_gate = 3 * NSA_G
    gates_kv = gate_logits[:, :NSA_KV_HEADS * n_gate].reshape(t, NSA_KV_HEADS, n_gate).transpose(1, 0, 2)
    gates_kv = jnp.pad(gates_kv, ((0, 0), (0, 0), (0, LANE - n_gate)))
    assert t % SEL_CHUNK == 0
    onehot = jnp.asarray(np.arange(t)[:, None] // NSA_SEL_LEN == np.arange(n_sel)[None, :], BF16)
    kv_cols = lambda blk: z[:, blk * LANE:(blk + NSA_KV_HEADS) * LANE].reshape(t, NSA_KV_HEADS, HEAD_DIM)
    side = lambda a: jnp.broadcast_to(a[:, None, :], (t, NSA_KV_HEADS, a.shape[1]))
    kaug = jnp.concatenate([kv_cols(AB_K_S), side(onehot)], axis=2).transpose(1, 0, 2)
    vaug = jnp.concatenate([kv_cols(AB_V_S), side(jnp.ones((t, HEAD_DIM), BF16))], axis=2).transpose(1, 0, 2)
    qw = NSA_G * HEAD_DIM
    rows = NSA_G * BLK
    return pl.pallas_call(
        _sel_kernel,
        out_shape=jax.ShapeDtypeStruct((t, NSA_HEADS * HEAD_DIM), BF16),
        grid=(NSA_KV_HEADS, t // BLK),
        in_specs=[pl.BlockSpec((BLK, qw), lambda h, i: (i, AB_Q_N // NSA_G + h)),
                  pl.BlockSpec((None, BLK, n_sel), lambda h, i: (h, i, 0)),
                  pl.BlockSpec((None, t, HEAD_DIM + n_sel), lambda h, i: (h, 0, 0)),
                  pl.BlockSpec((None, t, 2 * HEAD_DIM), lambda h, i: (h, 0, 0)),
                  pl.BlockSpec((SEL_TILES, NSA_G, BLK, BLK), lambda h, i: (0, h, 0, 0)),
                  pl.BlockSpec((None, BLK, LANE), lambda h, i: (h, i, 0)),
                  pl.BlockSpec((BLK, qw), lambda h, i: (i, h)),
                  pl.BlockSpec((BLK, qw), lambda h, i: (i, h))],
        out_specs=pl.BlockSpec((BLK, qw), lambda h, i: (i, h)),
        scratch_shapes=[pltpu.VMEM((rows, HEAD_DIM + n_sel), BF16), pltpu.VMEM((2, rows, SEL_CHUNK), F32),
                        pltpu.VMEM((2, rows, LANE), F32), pltpu.VMEM((rows, LANE), F32),
                        pltpu.VMEM((rows, 2 * HEAD_DIM), F32)],
        compiler_params=_cparams(("parallel", "arbitrary")),
        name="sel_attention",
    )(z, pen, kaug, vaug, bias_sel, gates_kv, o_cmp, o_win)


def _banded_body(q_ref, k_ref, v_ref, bias_ref, o_ref, qs_ref, sink_ref, *, group, n_prev, window, scale_q):
    kv = pl.program_id(0)
    qb = pl.program_id(1)
    if scale_q:
        _stack_scaled_queries(q_ref, qs_ref, group)
    qs = qs_ref[...]
    row = lax.broadcasted_iota(jnp.int32, (BLK, BLK), 0)
    col = lax.broadcasted_iota(jnp.int32, (BLK, BLK), 1)
    ones = jnp.ones((BLK, HEAD_DIM), BF16)
    tiles, vaugs = [], []
    for delta in range(n_prev + 1):
        kb = qb - delta
        start = pl.multiple_of(jnp.maximum(kb, 0) * BLK, BLK)
        dist = delta * BLK + row - col
        limit = jnp.where(kb >= 0, window, 0)
        valid = jnp.logical_and(dist >= 0, dist < limit)
        bias = jnp.concatenate([jnp.where(valid, bias_ref[delta, g], NEG_INF) for g in range(group)], axis=0)
        tiles.append(_dot_nt(qs, k_ref[pl.ds(start, BLK), :]) + bias)
        vaugs.append(jnp.concatenate([v_ref[pl.ds(start, BLK), :], ones], axis=1))
    m = jnp.max(functools.reduce(jnp.maximum, tiles), axis=-1, keepdims=True)
    if sink_ref is not None:
        sink = jnp.concatenate([jnp.full((BLK, 1), sink_ref[kv * group + g], F32) for g in range(group)], axis=0)
        m = jnp.maximum(m, sink)
    acc = functools.reduce(jnp.add, [_dot(jnp.exp(tile - m).astype(BF16), vaug) for tile, vaug in zip(tiles, vaugs)])
    denom = acc[:, HEAD_DIM:]
    if sink_ref is not None:
        denom = denom + jnp.exp(sink - m)
    out = acc[:, :HEAD_DIM] / denom
    for g in range(group):
        o_ref[:, g * HEAD_DIM:(g + 1) * HEAD_DIM] = out[g * BLK:(g + 1) * BLK, :].astype(o_ref.dtype)


def _stack_scaled_queries(q_ref, qs_ref, group):
    for g in range(group):
        qs_ref[g * BLK:(g + 1) * BLK, :] = (q_ref[:, g * HEAD_DIM:(g + 1) * HEAD_DIM].astype(F32) * SCALE).astype(BF16)


def _banded_kernel(sink_ref, q_ref, k_ref, v_ref, bias_ref, o_ref, qs_ref, *, group, n_prev, window):
    _banded_body(q_ref, k_ref, v_ref, bias_ref, o_ref, qs_ref, sink_ref, group=group, n_prev=n_prev, window=window,
                 scale_q=True)


def banded_attention(z, bias_raw, t, *, q_block, k_block, v_block, n_kv, group, window, sinks):
    n_prev = -(-window // BLK)
    qw = group * HEAD_DIM
    assert q_block % group == 0
    return pl.pallas_call(
        functools.partial(_banded_kernel, group=group, n_prev=n_prev, window=window),
        out_shape=jax.ShapeDtypeStruct((t, n_kv * qw), BF16),
        grid=(n_kv, t // BLK),
        in_specs=[pl.BlockSpec(memory_space=pltpu.SMEM),
                  pl.BlockSpec((BLK, qw), lambda h, i: (i, q_block // group + h)),
                  pl.BlockSpec((t, HEAD_DIM), lambda h, i: (0, k_block + h)),
                  pl.BlockSpec((t, HEAD_DIM), lambda h, i: (0, v_block + h)),
                  pl.BlockSpec((n_prev + 1, group, BLK, BLK), lambda h, i: (0, h, 0, 0))],
        out_specs=pl.BlockSpec((BLK, qw), lambda h, i: (i, h)),
        scratch_shapes=[pltpu.VMEM((group * BLK, HEAD_DIM), BF16)],
        compiler_params=_cparams(("parallel", "arbitrary")),
        name="banded_attention",
    )(sinks, z, z, z, bias_raw)


def _sg_kernel(u_ref, v_ref, lng_ref, lnb_ref, w_ref, b_ref, o_ref):
    u = _gelu_tanh(u_ref[...].astype(F32))
    gv = _gelu_tanh(v_ref[...].astype(F32))
    mu = jnp.mean(gv, axis=-1, keepdims=True)
    cen = gv - mu
    var = jnp.mean(cen * cen, axis=-1, keepdims=True)
    gv = (cen * lax.rsqrt(var + RMS_EPS) * lng_ref[...] + lnb_ref[...]).astype(BF16)
    row = lax.broadcasted_iota(jnp.int32, (SG_CHUNK, SG_CHUNK), 0)
    col = lax.broadcasted_iota(jnp.int32, (SG_CHUNK, SG_CHUNK), 1)
    gw = u.shape[1] // SG_GROUPS
    for g in range(SG_GROUPS):
        sl = slice(g * gw, (g + 1) * gw)
        w = jnp.where(row >= col, w_ref[g], 0.0).astype(BF16)
        spatial = _dot(w, gv[:, sl]) + b_ref[:, g:g + 1]
        o_ref[:, sl] = (u[:, sl] * spatial).astype(o_ref.dtype)


def spatial_gating(z1, ln_g, ln_b, sg_w, sg_b, t):
    width = ln_g.shape[0]
    assert width % (SG_GROUPS * LANE) == 0
    return pl.pallas_call(
        _sg_kernel,
        out_shape=jax.ShapeDtypeStruct((t, width), BF16),
        grid=(t // SG_CHUNK,),
        in_specs=[pl.BlockSpec((SG_CHUNK, width), lambda i: (i, 0)),
                  pl.BlockSpec((SG_CHUNK, width), lambda i: (i, 1)),
                  pl.BlockSpec((1, width), lambda i: (0, 0)),
                  pl.BlockSpec((1, width), lambda i: (0, 0)),
                  pl.BlockSpec((SG_GROUPS, SG_CHUNK, SG_CHUNK), lambda i: (0, 0, 0)),
                  pl.BlockSpec((SG_CHUNK, SG_GROUPS), lambda i: (0, 0))],
        out_specs=pl.BlockSpec((SG_CHUNK, width), lambda i: (i, 0)),
        compiler_params=_cparams(("parallel",)),
        name="spatial_gating",
    )(z1, z1, ln_g.reshape(1, width), ln_b.reshape(1, width), sg_w, jnp.transpose(sg_b))


MOE_TM = 512
MOE_TOP_K = 2


def moe_plan(idx, gate, t):
    n_rows = MOE_TOP_K * t + N_EXPERTS * MOE_TM
    n_blocks = n_rows // MOE_TM
    pair_e = idx[:, :MOE_TOP_K].reshape(-1)
    pair_w = gate[:, :MOE_TOP_K].reshape(-1)
    onehot = (pair_e[:, None] == jnp.arange(N_EXPERTS, dtype=jnp.int32)[None, :]).astype(jnp.int32)
    before = jnp.cumsum(onehot, axis=0) - onehot
    count = jnp.sum(onehot, axis=0)
    padded = (count + MOE_TM - 1) // MOE_TM * MOE_TM
    group_end = jnp.cumsum(padded)
    group_start = group_end - padded
    pair_row = jnp.sum(onehot * (before + group_start[None, :]), axis=1)
    row_pair = jnp.full((n_rows,), -1, jnp.int32).at[pair_row].set(jnp.arange(MOE_TOP_K * t, dtype=jnp.int32))
    row_tok = jnp.maximum(row_pair, 0) // MOE_TOP_K
    row_w = jnp.where(row_pair >= 0, pair_w[jnp.maximum(row_pair, 0)], 0.0)
    block_start = jnp.arange(n_blocks, dtype=jnp.int32) * MOE_TM
    block_e = jnp.minimum(jnp.sum(block_start[:, None] >= group_end[None, :], axis=1), N_EXPERTS - 1).astype(jnp.int32)
    block_rows = (group_start + count)[block_e] - block_start
    block_half = (block_rows <= MOE_TM // 2).astype(jnp.int32)
    n_used = (group_end[-1] // MOE_TM).astype(jnp.int32).reshape(1)
    return row_tok, row_w.reshape(n_rows, 1), block_e, block_half, n_used, pair_row.astype(jnp.int32)


MOE_GATHER_ROWS = 256


def _gather_rows_kernel(tok_ref, tok_next_ref, src_ref, o_ref, buf, sem):
    b = pl.program_id(0)
    slot = b % 2

    def row_copy(idx_ref, k, s):
        return pltpu.make_async_copy(src_ref.at[idx_ref[k]], buf.at[s, k], sem.at[s])

    def start_block(idx_ref, s):
        def start(k, carry):
            row_copy(idx_ref, k, s).start()
            return carry
        lax.fori_loop(0, MOE_GATHER_ROWS, start, 0, unroll=8)

    @pl.when(b == 0)
    def _():
        start_block(tok_ref, 0)

    @pl.when(b + 1 < pl.num_programs(0))
    def _():
        start_block(tok_next_ref, 1 - slot)

    def wait(k, carry):
        row_copy(tok_ref, k, slot).wait()
        return carry

    lax.fori_loop(0, MOE_GATHER_ROWS, wait, 0, unroll=8)
    o_ref[...] = buf[slot].reshape(o_ref.shape)


def gather_rows(src, row_tok):
    n_rows = row_tok.shape[0]
    d = src.shape[1] * src.shape[2]
    n_steps = n_rows // MOE_GATHER_ROWS
    return pl.pallas_call(
        _gather_rows_kernel,
        out_shape=jax.ShapeDtypeStruct((n_rows, d), src.dtype),
        grid=(n_steps,),
        in_specs=[pl.BlockSpec((MOE_GATHER_ROWS,), lambda b: (b,), memory_space=pltpu.SMEM),
                  pl.BlockSpec((MOE_GATHER_ROWS,), lambda b: (jnp.minimum(b + 1, n_steps - 1),),
                               memory_space=pltpu.SMEM),
                  pl.BlockSpec(memory_space=pl.ANY)],
        out_specs=pl.BlockSpec((MOE_GATHER_ROWS, d), lambda b: (b, 0)),
        scratch_shapes=[pltpu.VMEM((2, MOE_GATHER_ROWS) + src.shape[1:], src.dtype), pltpu.SemaphoreType.DMA((2,))],
        compiler_params=_cparams(("arbitrary",)),
        name="moe_gather",
    )(row_tok, row_tok, src)


def _expert_changed(b, block_e_ref):
    prev = jnp.maximum(b - 1, 0)
    return jnp.logical_or(b == 0, block_e_ref[b] != block_e_ref[prev])


def _moe_rows(b, half_ref, n_used_ref, compute, o_ref):
    live = b < n_used_ref[0]
    half = half_ref[jnp.minimum(b, n_used_ref[0] - 1)] == 1
    lower, upper = slice(0, MOE_TM // 2), slice(MOE_TM // 2, MOE_TM)

    @pl.when(jnp.logical_and(live, jnp.logical_not(half)))
    def _():
        compute(slice(0, MOE_TM))

    @pl.when(jnp.logical_and(live, half))
    def _():
        compute(lower)
        o_ref[upper, :] = jnp.zeros((MOE_TM // 2, o_ref.shape[1]), o_ref.dtype)

    @pl.when(jnp.logical_not(live))
    def _():
        o_ref[...] = jnp.zeros(o_ref.shape, o_ref.dtype)


def _moe_up_kernel(block_e_ref, half_ref, n_used_ref, x_ref, wg_ref, wu_ref, o_ref, wg_bf, wu_bf):
    b = pl.program_id(1)

    @pl.when(jnp.logical_and(b < n_used_ref[0], _expert_changed(b, block_e_ref)))
    def _():
        wg_bf[...] = wg_ref[...].astype(BF16)
        wu_bf[...] = wu_ref[...].astype(BF16)

    def compute(rows):
        x = x_ref[rows, :]
        gate = _dot(x, wg_bf[...])
        o_ref[rows, :] = ((gate * _sigmoid(gate)) * _dot(x, wu_bf[...])).astype(o_ref.dtype)

    _moe_rows(b, half_ref, n_used_ref, compute, o_ref)


def _moe_down_kernel(block_e_ref, half_ref, n_used_ref, h_ref, wd_ref, w_ref, o_ref, wd_bf):
    b = pl.program_id(1)

    @pl.when(jnp.logical_and(b < n_used_ref[0], _expert_changed(b, block_e_ref)))
    def _():
        wd_bf[...] = wd_ref[...].astype(BF16)

    def compute(rows):
        o_ref[rows, :] = w_ref[rows, :] * _dot(h_ref[rows, :], wd_bf[...])

    _moe_rows(b, half_ref, n_used_ref, compute, o_ref)


def moe_experts(xs, row_w, block_e, block_half, n_used, w_gate, w_up, w_down):
    n_rows, d = xs.shape
    d_ff = w_gate.shape[3]
    n_blocks = n_rows // MOE_TM
    tn_up, tn_down = 512, 1024
    blk = lambda b, nu: jnp.minimum(b, nu[0] - 1)
    row_block = lambda j, b, be, bh, nu: (blk(b, nu), 0)
    panel = lambda j, b, be, bh, nu: (0, be[blk(b, nu)], 0, j)
    out_block = lambda j, b, be, bh, nu: (b, j)
    hid = pl.pallas_call(
        _moe_up_kernel,
        out_shape=jax.ShapeDtypeStruct((n_rows, d_ff), BF16),
        grid_spec=pltpu.PrefetchScalarGridSpec(
            num_scalar_prefetch=3,
            grid=(d_ff // tn_up, n_blocks),
            in_specs=[pl.BlockSpec((MOE_TM, d), row_block),
                      pl.BlockSpec((None, None, d, tn_up), panel),
                      pl.BlockSpec((None, None, d, tn_up), panel)],
            out_specs=pl.BlockSpec((MOE_TM, tn_up), out_block),
            scratch_shapes=[pltpu.VMEM((d, tn_up), BF16), pltpu.VMEM((d, tn_up), BF16)]),
        compiler_params=_cparams(("arbitrary", "arbitrary")),
        name="moe_up",
    )(block_e, block_half, n_used, xs, w_gate, w_up)
    return pl.pallas_call(
        _moe_down_kernel,
        out_shape=jax.ShapeDtypeStruct((n_rows, d), F32),
        grid_spec=pltpu.PrefetchScalarGridSpec(
            num_scalar_prefetch=3,
            grid=(d // tn_down, n_blocks),
            in_specs=[pl.BlockSpec((MOE_TM, d_ff), row_block),
                      pl.BlockSpec((None, None, d_ff, tn_down), panel),
                      pl.BlockSpec((MOE_TM, 1), row_block)],
            out_specs=pl.BlockSpec((MOE_TM, tn_down), out_block),
            scratch_shapes=[pltpu.VMEM((d_ff, tn_down), BF16)]),
        compiler_params=_cparams(("arbitrary", "arbitrary")),
        name="moe_down",
    )(block_e, block_half, n_used, hid, w_down, row_w)


MOE_TC = 128


def _moe_combine_kernel(rows_ref, rows_next_ref, x_ref, g_ref, y_ref, o_ref, buf, sem):
    b = pl.program_id(0)
    slot = b % 2

    def row_copy(idx_ref, k, choice, s):
        row = idx_ref[MOE_TOP_K * k + choice]
        return pltpu.make_async_copy(y_ref.at[pl.ds(row, 1), :], buf.at[s, choice, pl.ds(k, 1), :], sem.at[s])

    def start_block(idx_ref, s):
        def start(k, carry):
            for choice in range(MOE_TOP_K):
                row_copy(idx_ref, k, choice, s).start()
            return carry
        lax.fori_loop(0, MOE_TC, start, 0, unroll=4)

    @pl.when(b == 0)
    def _():
        start_block(rows_ref, 0)

    @pl.when(b + 1 < pl.num_programs(0))
    def _():
        start_block(rows_next_ref, 1 - slot)

    def wait(k, carry):
        for choice in range(MOE_TOP_K):
            row_copy(rows_ref, k, choice, slot).wait()
        return carry

    lax.fori_loop(0, MOE_TC, wait, 0, unroll=4)
    x = x_ref[...]
    for choice in range(MOE_TOP_K):
        x = x + buf[slot, choice]
    y = x * lax.rsqrt(jnp.mean(x * x, axis=-1, keepdims=True) + RMS_EPS)
    o_ref[...] = (y * g_ref[...]).astype(o_ref.dtype)


def moe_combine_rmsnorm(x, ys, pair_row, g):
    t, d = x.shape
    n_steps = t // MOE_TC
    idx_block = lambda index_map: pl.BlockSpec((MOE_TOP_K * MOE_TC,), index_map, memory_space=pltpu.SMEM)
    return pl.pallas_call(
        _moe_combine_kernel,
        out_shape=jax.ShapeDtypeStruct((t, d), F32),
        grid=(n_steps,),
        in_specs=[idx_block(lambda i: (i,)),
                  idx_block(lambda i: (jnp.minimum(i + 1, n_steps - 1),)),
                  pl.BlockSpec((MOE_TC, d), lambda i: (i, 0)),
                  pl.BlockSpec((1, d), lambda i: (0, 0)),
                  pl.BlockSpec(memory_space=pl.ANY)],
        out_specs=pl.BlockSpec((MOE_TC, d), lambda i: (i, 0)),
        scratch_shapes=[pltpu.VMEM((2, MOE_TOP_K, MOE_TC, d), F32), pltpu.SemaphoreType.DMA((2,))],
        compiler_params=_cparams(("arbitrary",)),
        name="moe_combine",
    )(pair_row, pair_row, x, g.reshape(1, d), ys)


def kernel(x, rel_table, norm_mix, norm_ffn, norm_final, ab_w_in, ab_w_out, cmp_k_pe, cmp_k_w1, cmp_k_b1, cmp_k_w2, cmp_v_pe, cmp_v_w1, cmp_v_b1, cmp_v_w2, ffn_w_gate, ffn_w_up, ffn_w_down, cd_w_in, cd_w_out, sg_ln_g, sg_ln_b, sg_w, sg_b, swa_sinks, moe_w_router, moe_w_gate, moe_w_up, moe_w_down):
    b, t, d = x.shape
    assert b == 1
    xs = x.reshape(t, d)
    bias_raw, bias_sel, bias_cmp = bias_tiles(rel_table)

    h = rmsnorm(xs, norm_mix[0], BF16)
    ab_main = AB_MAIN_BLOCKS * LANE
    w_in_t = jnp.swapaxes(ab_w_in, 1, 2)
    z = matmul(h, [w_in_t], (0,), ab_main, out_dtype=BF16, w_transposed=True)
    w_gate_rows = jnp.pad(w_in_t[0, ab_main:], ((0, LANE - (w_in_t.shape[1] - ab_main)), (0, 0)))
    gate_logits = matmul(h, [w_gate_rows], (), LANE, out_dtype=F32, w_transposed=True)
    o_sb = sb_attention(z, t)
    kc = compress(z, AB_K_C, cmp_k_pe[0], cmp_k_w1[0], cmp_k_b1[0], cmp_k_w2[0], t)
    vc = compress(z, AB_V_C, cmp_v_pe[0], cmp_v_w1[0], cmp_v_b1[0], cmp_v_w2[0], t)
    o_cmp, pen, o_win = cmp_select_window(z, kc, vc, bias_cmp, bias_raw, t)
    o_nsa = sel_attention_mix(z, pen, bias_sel, gate_logits, o_cmp, o_win, t)
    xs = matmul([o_sb, o_nsa], [ab_w_out], (0,), d, out_dtype=F32, res=xs)
    h = rmsnorm(xs, norm_ffn[0], BF16)
    hid = matmul(h, [ffn_w_gate, ffn_w_up], (0,), ffn_w_gate.shape[2], out_dtype=BF16, act="swiglu")
    xs = matmul(hid, [ffn_w_down], (0,), d, out_dtype=F32, res=xs)

    h = rmsnorm(xs, norm_mix[1], BF16)
    z1 = matmul(h, [cd_w_in], (0,), cd_w_in.shape[2], out_dtype=BF16)
    o_sg = spatial_gating(z1, sg_ln_g[0], sg_ln_b[0], sg_w[0], sg_b[0], t)
    o_swa = banded_attention(z1, bias_raw, t, q_block=CD_Q, k_block=CD_K, v_block=CD_V,
                             n_kv=SWA_KV_HEADS, group=SWA_HEADS // SWA_KV_HEADS, window=SWA_WINDOW,
                             sinks=swa_sinks[0])
    xs = matmul([o_sg, o_swa], [cd_w_out], (0,), d, out_dtype=F32, res=xs)
    h, gate, idx = rmsnorm_router(xs, norm_ffn[1], moe_w_router[0])
    row_tok, row_w, block_e, block_half, n_used, pair_row = moe_plan(idx, gate, t)
    rows = gather_rows(h, row_tok)
    ys = moe_experts(rows, row_w, block_e, block_half, n_used, moe_w_gate, moe_w_up, moe_w_down)
    return moe_combine_rmsnorm(xs, ys, pair_row, norm_final).reshape(b, t, d)
```

```python
import functools
import math

import numpy as np
import jax
import jax.numpy as jnp
from jax import lax
from jax.experimental import pallas as pl
from jax.experimental.pallas import tpu as pltpu

F32 = jnp.float32
BF16 = jnp.bfloat16

LANE = 128
VMEM_LIMIT = 56 * 1024 * 1024

HEAD_DIM = 128
BLK = 128
NEG_INF = -1e30
RMS_EPS = 1e-6
SCALE = HEAD_DIM ** -0.5

SB_HEADS = 16
NSA_HEADS = 16
NSA_KV_HEADS = 4
NSA_G = NSA_HEADS // NSA_KV_HEADS
NSA_CMP_LEN = 32
NSA_CMP_STRIDE = 16
NSA_SEL_LEN = 64
NSA_SEL_TOPK = 16
NSA_WINDOW = 512
NSA_FORCE_SCORE = 1e6
SG_GROUPS = 8
SG_CHUNK = 128
SWA_HEADS = 16
SWA_KV_HEADS = 2
SWA_WINDOW = 128
REL_BUCKETS = 32
REL_MAX_DIST = 1024
REL_HEADS = 16
N_EXPERTS = 8

AB_Q_SB, AB_K_SB, AB_V_SB = 0, 16, 32
AB_Q_N = 48
AB_K_C, AB_V_C, AB_K_S, AB_V_S, AB_K_W, AB_V_W = 64, 68, 72, 76, 80, 84
AB_MAIN_BLOCKS = 88
CD_Q, CD_K, CD_V = 32, 48, 50

SEL_CHUNK_BLOCKS = 4
SEL_FAR_BLOCKS = 8
SEL_NEG_BLOCKS = SEL_CHUNK_BLOCKS - 1
SEL_TILES = SEL_NEG_BLOCKS + SEL_FAR_BLOCKS + SEL_CHUNK_BLOCKS - 1
RAW_TILES = NSA_WINDOW // BLK + 1
CMP_PAD = 120
SB_SKIP_LOG = -110.0
M_INIT = -5e29


def _cparams(sem):
    return pltpu.CompilerParams(dimension_semantics=sem, vmem_limit_bytes=VMEM_LIMIT)


def _dot(a, b):
    return jnp.dot(a, b, preferred_element_type=F32)


def _dot_nt(a, b):
    return lax.dot_general(a, b, (((1,), (1,)), ((), ())), preferred_element_type=F32)


def _dot_split(a, b):
    a_hi = a.astype(BF16)
    b_hi = b.astype(BF16)
    a_lo = (a - a_hi.astype(F32)).astype(BF16)
    b_lo = (b - b_hi.astype(F32)).astype(BF16)
    return _dot(a_hi, b_hi) + (_dot(a_hi, b_lo) + _dot(a_lo, b_hi))


def _sigmoid(x):
    return 1.0 / (1.0 + jnp.exp(-x))


def _gelu_tanh(x):
    return 0.5 * x * (1.0 + jnp.tanh(math.sqrt(2.0 / math.pi) * (x + 0.044715 * (x * x * x))))


def _rmsnorm_kernel(x_ref, g_ref, o_ref):
    x = x_ref[...]
    y = x * lax.rsqrt(jnp.mean(x * x, axis=-1, keepdims=True) + RMS_EPS)
    o_ref[...] = (y * g_ref[...]).astype(o_ref.dtype)


def rmsnorm(x, g, out_dtype):
    t, d = x.shape
    tm = 256
    return pl.pallas_call(
        _rmsnorm_kernel,
        out_shape=jax.ShapeDtypeStruct((t, d), out_dtype),
        grid=(t // tm,),
        in_specs=[pl.BlockSpec((tm, d), lambda i: (i, 0)), pl.BlockSpec((1, d), lambda i: (0, 0))],
        out_specs=pl.BlockSpec((tm, d), lambda i: (i, 0)),
        compiler_params=_cparams(("parallel",)),
        name="rmsnorm",
    )(x, g.reshape(1, d))


def _rmsnorm_router_kernel(x_ref, g_ref, wr_ref, o_ref, gate_ref, idx_ref):
    x = x_ref[...]
    y = x * lax.rsqrt(jnp.mean(x * x, axis=-1, keepdims=True) + RMS_EPS)
    y = y * g_ref[...]
    o_ref[...] = y.astype(o_ref.dtype).reshape(o_ref.shape)
    logits = _dot_split(y, wr_ref[...])
    lane = lax.broadcasted_iota(jnp.int32, logits.shape, 1)
    logits = jnp.where(lane < N_EXPERTS, logits, NEG_INF)
    m1 = jnp.max(logits, axis=-1, keepdims=True)
    i1 = jnp.min(jnp.where(logits == m1, lane, LANE), axis=-1, keepdims=True)
    rest = jnp.where(lane == i1, NEG_INF, logits)
    m2 = jnp.max(rest, axis=-1, keepdims=True)
    i2 = jnp.min(jnp.where(rest == m2, lane, LANE), axis=-1, keepdims=True)
    e2 = jnp.exp(m2 - m1)
    denom = 1.0 + e2
    gate_ref[...] = jnp.where(lane == 0, 1.0 / denom, jnp.where(lane == 1, e2 / denom, 0.0))
    idx_ref[...] = jnp.where(lane == 0, i1, jnp.where(lane == 1, i2, 0))


def rmsnorm_router(x, g, w_router):
    t, d = x.shape
    tm = 256
    wr = jnp.pad(w_router, ((0, 0), (0, LANE - w_router.shape[1])))
    return pl.pallas_call(
        _rmsnorm_router_kernel,
        out_shape=(jax.ShapeDtypeStruct((t, d // LANE, LANE), BF16), jax.ShapeDtypeStruct((t, LANE), F32),
                   jax.ShapeDtypeStruct((t, LANE), jnp.int32)),
        grid=(t // tm,),
        in_specs=[pl.BlockSpec((tm, d), lambda i: (i, 0)), pl.BlockSpec((1, d), lambda i: (0, 0)),
                  pl.BlockSpec((d, LANE), lambda i: (0, 0))],
        out_specs=(pl.BlockSpec((tm, d // LANE, LANE), lambda i: (i, 0, 0)), pl.BlockSpec((tm, LANE), lambda i: (i, 0)),
                   pl.BlockSpec((tm, LANE), lambda i: (i, 0))),
        compiler_params=_cparams(("parallel",)),
        name="rmsnorm_router",
    )(x, g.reshape(1, d), wr)


def _mm_epilogue(parts, res_ref, o_ref, act):
    r = parts[0]
    if act == "swiglu":
        r = (r * _sigmoid(r)) * parts[1]
    if res_ref is not None:
        r = res_ref[...] + r
    o_ref[...] = r.astype(o_ref.dtype)


def _panel_copy(w_ref, lead, col0, tn, transposed, stage_ref, sem_ref, widx):
    window = (pl.ds(col0, tn), slice(None)) if transposed else (slice(None), pl.ds(col0, tn))
    return pltpu.make_async_copy(w_ref.at[tuple(lead) + window], stage_ref, sem_ref.at[widx])


def _mm_resident_kernel(*refs, n_a, n_w, has_res, act, w_transposed, lead, tn):
    a_refs = refs[:n_a]
    w_refs = refs[n_a:n_a + n_w]
    res_ref = refs[n_a + n_w] if has_res else None
    o_ref = refs[n_a + n_w + int(has_res)]
    scratch = refs[n_a + n_w + int(has_res) + 1:]
    wbf_refs, stage_refs, sem = scratch[:n_w], scratch[n_w:2 * n_w], scratch[2 * n_w]
    j = pl.program_id(0)

    def panel(widx, jj):
        return _panel_copy(w_refs[widx], lead, jj * tn, tn, w_transposed, stage_refs[widx], sem, widx)

    @pl.when(pl.program_id(1) == 0)
    def _():
        @pl.when(j == 0)
        def _():
            for widx in range(n_w):
                panel(widx, 0).start()

        for widx in range(n_w):
            panel(widx, j).wait()
            wbf_refs[widx][...] = stage_refs[widx][...].astype(BF16)

        @pl.when(j + 1 < pl.num_programs(0))
        def _():
            for widx in range(n_w):
                panel(widx, j + 1).start()

    parts = []
    for wbf in wbf_refs:
        k0, part = 0, None
        for a_ref in a_refs:
            k1 = k0 + a_ref.shape[1]
            term = _dot_nt(a_ref[...], wbf[:, k0:k1]) if w_transposed else _dot(a_ref[...], wbf[k0:k1, :])
            part = term if part is None else part + term
            k0 = k1
        parts.append(part)
    _mm_epilogue(parts, res_ref, o_ref, act)


def _mm_ktiled_kernel(a_ref, w_ref, res_ref, o_ref, acc_ref, *, nk):
    k = pl.program_id(2)
    part = _dot(a_ref[...], w_ref[...].astype(BF16))

    @pl.when(k == 0)
    def _():
        acc_ref[...] = part

    @pl.when(k > 0)
    def _():
        acc_ref[...] += part

    @pl.when(k == nk - 1)
    def _():
        _mm_epilogue([acc_ref[...]], res_ref, o_ref, None)


MM_RESIDENT_MAX_K = 4096
MM_VMEM_BUDGET = 50 * 1024 * 1024


def _resident_tiles(m, kdim, n_cols, n_w, out_bytes, has_res):
    for tn in (1024, 512, 256, LANE):
        for tm in (1024, 512):
            if n_cols % tn or m % tm:
                continue
            a_buf = 2 * tm * kdim * 2
            w_buf = n_w * (kdim * tn * 4 + kdim * tn * 2)
            o_buf = 2 * tm * tn * out_bytes + (2 * tm * tn * 4 if has_res else 0)
            if a_buf + w_buf + o_buf <= MM_VMEM_BUDGET:
                return tm, tn
    raise ValueError("no resident-panel tiling fits VMEM")


def matmul(a, ws, lead, n_cols, *, out_dtype, res=None, act=None, w_transposed=False):
    a_parts = list(a) if isinstance(a, (list, tuple)) else [a]
    m = a_parts[0].shape[0]
    kdim = sum(part.shape[1] for part in a_parts)
    lead = tuple(lead)
    squeeze = (None,) * len(lead)
    if kdim <= MM_RESIDENT_MAX_K:
        tm, tn = _resident_tiles(m, kdim, n_cols, len(ws), jnp.dtype(out_dtype).itemsize, res is not None)
        in_specs = [pl.BlockSpec((tm, part.shape[1]), lambda j, i: (i, 0)) for part in a_parts]
        panel = (tn, kdim) if w_transposed else (kdim, tn)
        in_specs += [pl.BlockSpec(memory_space=pl.ANY) for _ in ws]
        args = [*a_parts, *ws]
        if res is not None:
            in_specs.append(pl.BlockSpec((tm, tn), lambda j, i: (i, j)))
            args.append(res)
        return pl.pallas_call(
            functools.partial(_mm_resident_kernel, n_a=len(a_parts), n_w=len(ws), has_res=res is not None, act=act,
                              w_transposed=w_transposed, lead=lead, tn=tn),
            out_shape=jax.ShapeDtypeStruct((m, n_cols), out_dtype),
            grid=(n_cols // tn, m // tm),
            in_specs=in_specs,
            out_specs=pl.BlockSpec((tm, tn), lambda j, i: (i, j)),
            scratch_shapes=([pltpu.VMEM(panel, BF16) for _ in ws] + [pltpu.VMEM(panel, F32) for _ in ws]
                            + [pltpu.SemaphoreType.DMA((len(ws),))]),
            compiler_params=_cparams(("arbitrary", "arbitrary")),
            name="matmul_resident",
        )(*args)
    assert len(ws) == 1 and act is None and res is not None and len(a_parts) == 1 and not w_transposed
    a = a_parts[0]
    tm, tn, tk = 2048, 512, 2048
    assert m % tm == 0 and n_cols % tn == 0 and kdim % tk == 0
    nk = kdim // tk
    return pl.pallas_call(
        functools.partial(_mm_ktiled_kernel, nk=nk),
        out_shape=jax.ShapeDtypeStruct((m, n_cols), out_dtype),
        grid=(n_cols // tn, m // tm, nk),
        in_specs=[pl.BlockSpec((tm, tk), lambda j, i, k: (i, k)),
                  pl.BlockSpec(squeeze + (tk, tn), lambda j, i, k: lead + (k, j)),
                  pl.BlockSpec((tm, tn), lambda j, i, k: (i, j))],
        out_specs=pl.BlockSpec((tm, tn), lambda j, i, k: (i, j)),
        scratch_shapes=[pltpu.VMEM((tm, tn), F32)],
        compiler_params=_cparams(("parallel", "parallel", "arbitrary")),
        name="matmul_ktiled",
    )(a, ws[0], res)


def _bucket_thresholds():
    d = np.arange(0, 4 * REL_MAX_DIST)
    max_exact = REL_BUCKETS // 2
    log_ratio = np.log(np.maximum(d, 1).astype(np.float64) / max_exact) / math.log(REL_MAX_DIST / max_exact)
    far = np.minimum(max_exact + (log_ratio * (REL_BUCKETS - max_exact)).astype(np.int32), REL_BUCKETS - 1)
    bucket = np.where(d < max_exact, d, far)
    assert np.all(np.diff(bucket) >= 0) and bucket[-1] == REL_BUCKETS - 1
    return [int(np.argmax(bucket >= b)) for b in range(REL_BUCKETS)]


_BUCKET_THR = _bucket_thresholds()
REL_FAR_DIST = _BUCKET_THR[REL_BUCKETS - 1]
assert REL_FAR_DIST <= SEL_FAR_BLOCKS * BLK - (BLK - 1)
assert REL_FAR_DIST <= NSA_CMP_STRIDE * (CMP_PAD + 1) - (NSA_CMP_LEN - 1)


def _bias_kernel(tab_ref, raw_ref, sel_ref, cmp_ref):
    h = pl.program_id(0)
    row = lax.broadcasted_iota(jnp.int32, (BLK, BLK), 0)
    col = lax.broadcasted_iota(jnp.int32, (BLK, BLK), 1)

    def lookup(dist):
        dd = jnp.maximum(dist, 0)
        b = jnp.full(dist.shape, tab_ref[0, h], F32)
        for bk in range(1, REL_BUCKETS):
            b = jnp.where(dd >= _BUCKET_THR[bk], tab_ref[bk, h], b)
        return b

    far = tab_ref[REL_BUCKETS - 1, h]
    for d in range(RAW_TILES):
        raw_ref[d] = lookup(d * BLK + row - col)
    for step in range(SEL_TILES):
        dist = (step - SEL_NEG_BLOCKS) * BLK + row - col
        sel_ref[step] = jnp.where(dist < 0, NEG_INF, lookup(dist) - far)
    dist_c = row - NSA_CMP_STRIDE * (col - CMP_PAD) - (NSA_CMP_LEN - 1)
    cmp_ref[...] = lookup(dist_c) - far


def bias_tiles(rel_table):
    tiles = lambda n: pl.BlockSpec((n, None, BLK, BLK), lambda h: (0, h, 0, 0))
    return pl.pallas_call(
        _bias_kernel,
        out_shape=(jax.ShapeDtypeStruct((RAW_TILES, REL_HEADS, BLK, BLK), F32),
                   jax.ShapeDtypeStruct((SEL_TILES, REL_HEADS, BLK, BLK), F32),
                   jax.ShapeDtypeStruct((REL_HEADS, BLK, BLK), F32)),
        grid=(REL_HEADS,),
        in_specs=[pl.BlockSpec(memory_space=pltpu.SMEM)],
        out_specs=(tiles(RAW_TILES), tiles(SEL_TILES), pl.BlockSpec((None, BLK, BLK), lambda h: (h, 0, 0))),
        compiler_params=_cparams(("arbitrary",)),
        name="bias_tiles",
    )(rel_table)


SB_BQ = 256
SB_GROUP = 4


def _sb_kernel(q_ref, k_ref, v_ref, o_ref, c_ref, acc_ref):
    qb = pl.program_id(1)
    row = lax.broadcasted_iota(jnp.int32, (SB_BQ, SB_BQ), 0)
    col = lax.broadcasted_iota(jnp.int32, (SB_BQ, SB_BQ), 1)
    upper = (row > col).astype(BF16)

    def block(h, kb, past):
        cols = slice(h * HEAD_DIM, (h + 1) * HEAD_DIM)
        ks = pl.multiple_of(kb * SB_BQ, SB_BQ)
        z = _dot_nt(q_ref[:, cols], k_ref[pl.ds(ks, SB_BQ), cols]) * SCALE
        sp = jnp.maximum(z, 0.0) + jnp.log(1.0 + jnp.exp(-jnp.abs(z)))
        log_stay = -sp
        if past is not None:
            log_stay = jnp.where(past, log_stay, 0.0)
        later = _dot(log_stay.astype(BF16), upper)
        tot = jnp.broadcast_to(jnp.sum(log_stay, axis=-1, keepdims=True), (SB_BQ, HEAD_DIM))
        if past is not None:
            w = jnp.where(past, jnp.exp((z - sp) + later), 0.0)
            c_new = tot
        else:
            carry = c_ref[:, cols]
            w = jnp.exp((z - sp) + later + jnp.concatenate([carry] * (SB_BQ // HEAD_DIM), axis=1))
            c_new = carry + tot
        contrib = _dot(w.astype(BF16), v_ref[pl.ds(ks, SB_BQ), cols])
        if past is not None:
            acc_ref[:, cols] = contrib
        else:
            acc_ref[:, cols] += contrib
        c_ref[:, cols] = c_new
        return jnp.max(c_new)

    def sweep(kb, past):
        return functools.reduce(jnp.maximum, [block(h, kb, past) for h in range(SB_GROUP)])

    def cond(state):
        kb, c_max = state
        return jnp.logical_and(kb >= 0, c_max > SB_SKIP_LOG)

    def body(state):
        kb, _ = state
        return kb - 1, sweep(kb, None)

    lax.while_loop(cond, body, (qb - 1, sweep(qb, col < row)))
    o_ref[...] = acc_ref[...].astype(o_ref.dtype)


def sb_attention(z, t):
    width = SB_GROUP * HEAD_DIM
    assert SB_HEADS % SB_GROUP == 0 and AB_K_SB % SB_GROUP == 0 and AB_V_SB % SB_GROUP == 0
    return pl.pallas_call(
        _sb_kernel,
        out_shape=jax.ShapeDtypeStruct((t, SB_HEADS * HEAD_DIM), BF16),
        grid=(SB_HEADS // SB_GROUP, t // SB_BQ),
        in_specs=[pl.BlockSpec((SB_BQ, width), lambda h, i: (i, AB_Q_SB // SB_GROUP + h)),
                  pl.BlockSpec((t, width), lambda h, i: (0, AB_K_SB // SB_GROUP + h)),
                  pl.BlockSpec((t, width), lambda h, i: (0, AB_V_SB // SB_GROUP + h))],
        out_specs=pl.BlockSpec((SB_BQ, width), lambda h, i: (i, h)),
        scratch_shapes=[pltpu.VMEM((SB_BQ, width), F32), pltpu.VMEM((SB_BQ, width), F32)],
        compiler_params=_cparams(("parallel", "arbitrary")),
        name="sb_attention",
    )(z, z, z)


def _compress_kernel(a_ref, pe_ref, w1_ref, b1_ref, w2_ref, o_ref):
    n_chunk = a_ref.shape[0]
    half = NSA_CMP_STRIDE * HEAD_DIM
    chunks = a_ref[...].reshape(n_chunk, half).astype(F32)
    first = _dot((chunks + pe_ref[:, :half]).astype(BF16), w1_ref[:half, :].astype(BF16))
    second = _dot((chunks + pe_ref[:, half:]).astype(BF16), w1_ref[half:, :].astype(BF16))
    hid = _gelu_tanh(first + pltpu.roll(second, n_chunk - 1, 0) + b1_ref[...])
    out = _dot(hid.astype(BF16), w2_ref[...].astype(BF16))
    o_ref[0:CMP_PAD, :] = jnp.zeros((CMP_PAD, HEAD_DIM), o_ref.dtype)
    o_ref[CMP_PAD:CMP_PAD + n_chunk, :] = out.astype(o_ref.dtype)
    o_ref[CMP_PAD + n_chunk:, :] = jnp.zeros((o_ref.shape[0] - CMP_PAD - n_chunk, HEAD_DIM), o_ref.dtype)


def compress(z, col_block, pe, w1, b1, w2, t):
    assert NSA_CMP_LEN == 2 * NSA_CMP_STRIDE
    n_chunk = t // NSA_CMP_STRIDE
    width = NSA_CMP_LEN * HEAD_DIM
    hidden = w1.shape[-1]
    rows = CMP_PAD + n_chunk + 8
    return pl.pallas_call(
        _compress_kernel,
        out_shape=jax.ShapeDtypeStruct((NSA_KV_HEADS, rows, HEAD_DIM), BF16),
        grid=(NSA_KV_HEADS,),
        in_specs=[pl.BlockSpec((n_chunk, NSA_CMP_STRIDE, HEAD_DIM), lambda h: (0, 0, col_block + h)),
                  pl.BlockSpec((1, width), lambda h: (0, 0)),
                  pl.BlockSpec((width, hidden), lambda h: (0, 0)),
                  pl.BlockSpec((1, hidden), lambda h: (0, 0)),
                  pl.BlockSpec((hidden, HEAD_DIM), lambda h: (0, 0))],
        out_specs=pl.BlockSpec((None, rows, HEAD_DIM), lambda h: (h, 0, 0)),
        compiler_params=_cparams(("parallel",)),
        name="compress",
    )(z.reshape(n_chunk, NSA_CMP_STRIDE, z.shape[1]), pe.reshape(1, width), w1, b1.reshape(1, hidden), w2)


def _cmp_to_sel_padded(t):
    n_cmp = (t - NSA_CMP_LEN) // NSA_CMP_STRIDE + 1
    n_sel = t // NSA_SEL_LEN
    cmp_end = np.arange(n_cmp) * NSA_CMP_STRIDE + NSA_CMP_LEN - 1
    cmp_start = cmp_end - (NSA_CMP_LEN - 1)
    sel_start = np.arange(n_sel) * NSA_SEL_LEN
    overlap = (np.minimum(cmp_end[:, None], sel_start[None, :] + NSA_SEL_LEN - 1)
               - np.maximum(cmp_start[:, None], sel_start[None, :]) + 1)
    c2s = np.clip(overlap, 0, None).astype(np.float32) / NSA_CMP_LEN
    out = np.zeros((CMP_PAD + n_cmp + 9, n_sel), np.float32)
    out[CMP_PAD:CMP_PAD + n_cmp] = c2s
    return out


def _cmp_win_kernel(q_ref, kc_ref, vc_ref, c2s_ref, bias_ref, kw_ref, vw_ref, bias_win_ref, o_ref, pen_ref, ow_ref,
                    qs_ref, *, n_far):
    qb = pl.program_id(1)
    _stack_scaled_queries(q_ref, qs_ref, NSA_G)
    r0 = pl.multiple_of(qb * (BLK // NSA_CMP_STRIDE), 8)
    kc_far = kc_ref[0:n_far, :]
    kc_near = kc_ref[pl.ds(r0, BLK), :]
    vs_far = jnp.concatenate([vc_ref[0:n_far, :], c2s_ref[0:n_far, :]], axis=1)
    vs_near = jnp.concatenate([vc_ref[pl.ds(r0, BLK), :], c2s_ref[pl.ds(r0, BLK), :]], axis=1)
    r_far = lax.broadcasted_iota(jnp.int32, (BLK, n_far), 1)
    far_ok = jnp.logical_and(r_far >= CMP_PAD, r_far < r0)
    row = lax.broadcasted_iota(jnp.int32, (BLK, BLK), 0)
    col = lax.broadcasted_iota(jnp.int32, (BLK, BLK), 1)
    dist_c = row - NSA_CMP_STRIDE * (col - CMP_PAD) - (NSA_CMP_LEN - 1)
    near_ok = jnp.logical_and(dist_c >= 0, col + r0 >= CMP_PAD)

    far_mask = jnp.where(far_ok, 0.0, NEG_INF)

    n_sel = c2s_ref.shape[1]
    imp = jnp.zeros((BLK, n_sel), F32)
    for g in range(NSA_G):
        q = qs_ref[g * BLK:(g + 1) * BLK, :]
        s_far = _dot_nt(q, kc_far) + far_mask
        s_near = _dot_nt(q, kc_near) + jnp.where(near_ok, bias_ref[g], NEG_INF)
        m = jnp.maximum(jnp.maximum(jnp.max(s_far, axis=-1, keepdims=True), jnp.max(s_near, axis=-1, keepdims=True)),
                        M_INIT)
        e_far = jnp.exp(s_far - m)
        e_near = jnp.exp(s_near - m)
        l = jnp.sum(e_far, axis=-1, keepdims=True) + jnp.sum(e_near, axis=-1, keepdims=True)
        inv = 1.0 / jnp.where(l > 0.0, l, 1.0)
        res = (_dot(e_far.astype(BF16), vs_far) + _dot(e_near.astype(BF16), vs_near)) * inv
        o_ref[:, g * HEAD_DIM:(g + 1) * HEAD_DIM] = res[:, :HEAD_DIM].astype(o_ref.dtype)
        imp = imp + res[:, HEAD_DIM:]

    blk = lax.broadcasted_iota(jnp.int32, (n_sel, BLK), 0)
    t_pos = qb * BLK + lax.broadcasted_iota(jnp.int32, (n_sel, BLK), 1)
    cur = t_pos // NSA_SEL_LEN
    forced = jnp.logical_or(blk == 0, jnp.logical_or(blk == cur, blk == cur - 1))
    score = jnp.where(forced, NSA_FORCE_SCORE, jnp.where(blk <= cur, imp.T, -1.0))
    pen = jnp.full((n_sel, BLK), NEG_INF, F32)
    for _ in range(min(NSA_SEL_TOPK, n_sel)):
        top = jnp.max(score, axis=0, keepdims=True)
        first = jnp.min(jnp.where(score == top, blk, n_sel), axis=0, keepdims=True)
        hit = blk == first
        pen = jnp.where(hit, 0.0, pen)
        score = jnp.where(hit, -2.0, score)
    pen_ref[...] = pen.T.astype(pen_ref.dtype)

    _banded_body(q_ref, kw_ref, vw_ref, bias_win_ref, ow_ref, qs_ref, None, group=NSA_G,
                 n_prev=NSA_WINDOW // BLK, window=NSA_WINDOW, scale_q=False)


def cmp_select_window(z, kc, vc, bias_cmp, bias_raw, t):
    assert NSA_WINDOW % BLK == 0
    n_sel = t // NSA_SEL_LEN
    assert n_sel == LANE, "selection mask rides in one 128-lane tile"
    c2s = jnp.asarray(_cmp_to_sel_padded(t), BF16)
    rows = kc.shape[1]
    assert c2s.shape[0] == rows
    n_far = t // NSA_CMP_STRIDE
    qw = NSA_G * HEAD_DIM
    return pl.pallas_call(
        functools.partial(_cmp_win_kernel, n_far=n_far),
        out_shape=(jax.ShapeDtypeStruct((t, NSA_HEADS * HEAD_DIM), BF16),
                   jax.ShapeDtypeStruct((NSA_KV_HEADS, t, n_sel), BF16),
                   jax.ShapeDtypeStruct((t, NSA_HEADS * HEAD_DIM), BF16)),
        grid=(NSA_KV_HEADS, t // BLK),
        in_specs=[pl.BlockSpec((BLK, qw), lambda h, i: (i, AB_Q_N // NSA_G + h)),
                  pl.BlockSpec((None, rows, HEAD_DIM), lambda h, i: (h, 0, 0)),
                  pl.BlockSpec((None, rows, HEAD_DIM), lambda h, i: (h, 0, 0)),
                  pl.BlockSpec((rows, n_sel), lambda h, i: (0, 0)),
                  pl.BlockSpec((NSA_G, BLK, BLK), lambda h, i: (h, 0, 0)),
                  pl.BlockSpec((t, HEAD_DIM), lambda h, i: (0, AB_K_W + h)),
                  pl.BlockSpec((t, HEAD_DIM), lambda h, i: (0, AB_V_W + h)),
                  pl.BlockSpec((RAW_TILES, NSA_G, BLK, BLK), lambda h, i: (0, h, 0, 0))],
        out_specs=(pl.BlockSpec((BLK, qw), lambda h, i: (i, h)),
                   pl.BlockSpec((None, BLK, n_sel), lambda h, i: (h, i, 0)),
                   pl.BlockSpec((BLK, qw), lambda h, i: (i, h))),
        scratch_shapes=[pltpu.VMEM((NSA_G * BLK, HEAD_DIM), BF16)],
        compiler_params=_cparams(("parallel", "arbitrary")),
        name="cmp_select_window",
    )(z, kc, vc, c2s, bias_cmp, z, z, bias_raw)


SEL_CHUNK = SEL_CHUNK_BLOCKS * BLK


def _sel_kernel(q_ref, pen_ref, kaug_ref, vaug_ref, bias_ref, gate_ref, ocmp_ref, owin_ref, o_ref,
                qa_ref, s_ref, cmax_ref, m_ref, acc_ref):
    qb = pl.program_id(1)
    pen = pen_ref[...]
    for g in range(NSA_G):
        rows = slice(g * BLK, (g + 1) * BLK)
        qa_ref[rows, :HEAD_DIM] = (q_ref[:, g * HEAD_DIM:(g + 1) * HEAD_DIM].astype(F32) * SCALE).astype(BF16)
        qa_ref[rows, HEAD_DIM:] = pen
    m_ref[...] = jnp.full(m_ref.shape, M_INIT, F32)
    acc_ref[...] = jnp.zeros(acc_ref.shape, F32)
    n_chunks = qb // SEL_CHUNK_BLOCKS + 1
    last_chunk = kaug_ref.shape[0] // SEL_CHUNK - 1

    def chunk_start(c):
        return pl.multiple_of(jnp.minimum(c, last_chunk) * SEL_CHUNK, SEL_CHUNK)

    def scores(c, slot):
        s_all = _dot_nt(qa_ref[...], kaug_ref[pl.ds(chunk_start(c), SEL_CHUNK), :])
        first = qb - c * SEL_CHUNK_BLOCKS + SEL_NEG_BLOCKS
        for g in range(NSA_G):
            rows = slice(g * BLK, (g + 1) * BLK)
            tiles = [s_all[rows, j * BLK:(j + 1) * BLK] + bias_ref[jnp.clip(first - j, 0, SEL_TILES - 1), g]
                     for j in range(SEL_CHUNK_BLOCKS)]
            for j, tile in enumerate(tiles):
                s_ref[slot, rows, j * BLK:(j + 1) * BLK] = tile
            cmax_ref[slot, rows, :] = jnp.broadcast_to(
                jnp.max(functools.reduce(jnp.maximum, tiles), axis=-1, keepdims=True), (BLK, LANE))

    def absorb(c, slot):
        va = vaug_ref[pl.ds(chunk_start(c), SEL_CHUNK), :]
        for g in range(NSA_G):
            rows = slice(g * BLK, (g + 1) * BLK)
            m_old = m_ref[rows, :]
            m_new = jnp.maximum(m_old, cmax_ref[slot, rows, :])
            alpha = jnp.exp(m_old - m_new)
            p = jnp.concatenate([jnp.exp(s_ref[slot, rows, j * BLK:(j + 1) * BLK] - m_new)
                                 for j in range(SEL_CHUNK_BLOCKS)], axis=1).astype(BF16)
            acc_ref[rows, :] = jnp.concatenate([alpha, alpha], axis=1) * acc_ref[rows, :] + _dot(p, va)
            m_ref[rows, :] = m_new

    def pair(k, carry):
        scores(2 * k + 1, 1)
        absorb(2 * k, 0)
        scores(2 * k + 2, 0)
        absorb(2 * k + 1, 1)
        return carry

    scores(0, 0)
    lax.fori_loop(0, n_chunks // 2, pair, 0)

    @pl.when(n_chunks % 2 == 1)
    def _():
        absorb(n_chunks - 1, 0)

    gates = _sigmoid(gate_ref[...])
    for g in range(NSA_G):
        rows = slice(g * BLK, (g + 1) * BLK)
        cols = slice(g * HEAD_DIM, (g + 1) * HEAD_DIM)
        o_sel = acc_ref[rows, :HEAD_DIM] / acc_ref[rows, HEAD_DIM:]
        mixed = (gates[:, 3 * g:3 * g + 1] * ocmp_ref[:, cols].astype(F32)
                 + gates[:, 3 * g + 1:3 * g + 2] * o_sel
                 + gates[:, 3 * g + 2:3 * g + 3] * owin_ref[:, cols].astype(F32))
        o_ref[:, cols] = mixed.astype(o_ref.dtype)


def sel_attention_mix(z, pen, bias_sel, gate_logits, o_cmp, o_win, t):
    n_sel = t // NSA_SEL_LEN
    n_gate = 3 * NSA_G
    gates_kv = gate_logits[:, :NSA_KV_HEADS * n_gate].reshape(t, NSA_KV_HEADS, n_gate).transpose(1, 0, 2)
    gates_kv = jnp.pad(gates_kv, ((0, 0), (0, 0), (0, LANE - n_gate)))
    assert t % SEL_CHUNK == 0
    onehot = jnp.asarray(np.arange(t)[:, None] // NSA_SEL_LEN == np.arange(n_sel)[None, :], BF16)
    kv_cols = lambda blk: z[:, blk * LANE:(blk + NSA_KV_HEADS) * LANE].reshape(t, NSA_KV_HEADS, HEAD_DIM)
    side = lambda a: jnp.broadcast_to(a[:, None, :], (t, NSA_KV_HEADS, a.shape[1]))
    kaug = jnp.concatenate([kv_cols(AB_K_S), side(onehot)], axis=2).transpose(1, 0, 2)
    vaug = jnp.concatenate([kv_cols(AB_V_S), side(jnp.ones((t, HEAD_DIM), BF16))], axis=2).transpose(1, 0, 2)
    qw = NSA_G * HEAD_DIM
    rows = NSA_G * BLK
    return pl.pallas_call(
        _sel_kernel,
        out_shape=jax.ShapeDtypeStruct((t, NSA_HEADS * HEAD_DIM), BF16),
        grid=(NSA_KV_HEADS, t // BLK),
        in_specs=[pl.BlockSpec((BLK, qw), lambda h, i: (i, AB_Q_N // NSA_G + h)),
                  pl.BlockSpec((None, BLK, n_sel), lambda h, i: (h, i, 0)),
                  pl.BlockSpec((None, t, HEAD_DIM + n_sel), lambda h, i: (h, 0, 0)),
                  pl.BlockSpec((None, t, 2 * HEAD_DIM), lambda h, i: (h, 0, 0)),
                  pl.BlockSpec((SEL_TILES, NSA_G, BLK, BLK), lambda h, i: (0, h, 0, 0)),
                  pl.BlockSpec((None, BLK, LANE), lambda h, i: (h, i, 0)),
                  pl.BlockSpec((BLK, qw), lambda h, i: (i, h)),
                  pl.BlockSpec((BLK, qw), lambda h, i: (i, h))],
        out_specs=pl.BlockSpec((BLK, qw), lambda h, i: (i, h)),
        scratch_shapes=[pltpu.VMEM((rows, HEAD_DIM + n_sel), BF16), pltpu.VMEM((2, rows, SEL_CHUNK), F32),
                        pltpu.VMEM((2, rows, LANE), F32), pltpu.VMEM((rows, LANE), F32),
                        pltpu.VMEM((rows, 2 * HEAD_DIM), F32)],
        compiler_params=_cparams(("parallel", "arbitrary")),
        name="sel_attention",
    )(z, pen, kaug, vaug, bias_sel, gates_kv, o_cmp, o_win)


def _banded_body(q_ref, k_ref, v_ref, bias_ref, o_ref, qs_ref, sink_ref, *, group, n_prev, window, scale_q):
    kv = pl.program_id(0)
    qb = pl.program_id(1)
    if scale_q:
        _stack_scaled_queries(q_ref, qs_ref, group)
    qs = qs_ref[...]
    row = lax.broadcasted_iota(jnp.int32, (BLK, BLK), 0)
    col = lax.broadcasted_iota(jnp.int32, (BLK, BLK), 1)
    ones = jnp.ones((BLK, HEAD_DIM), BF16)
    tiles, vaugs = [], []
    for delta in range(n_prev + 1):
        kb = qb - delta
        start = pl.multiple_of(jnp.maximum(kb, 0) * BLK, BLK)
        dist = delta * BLK + row - col
        limit = jnp.where(kb >= 0, window, 0)
        valid = jnp.logical_and(dist >= 0, dist < limit)
        bias = jnp.concatenate([jnp.where(valid, bias_ref[delta, g], NEG_INF) for g in range(group)], axis=0)
        tiles.append(_dot_nt(qs, k_ref[pl.ds(start, BLK), :]) + bias)
        vaugs.append(jnp.concatenate([v_ref[pl.ds(start, BLK), :], ones], axis=1))
    m = jnp.max(functools.reduce(jnp.maximum, tiles), axis=-1, keepdims=True)
    if sink_ref is not None:
        sink = jnp.concatenate([jnp.full((BLK, 1), sink_ref[kv * group + g], F32) for g in range(group)], axis=0)
        m = jnp.maximum(m, sink)
    acc = functools.reduce(jnp.add, [_dot(jnp.exp(tile - m).astype(BF16), vaug) for tile, vaug in zip(tiles, vaugs)])
    denom = acc[:, HEAD_DIM:]
    if sink_ref is not None:
        denom = denom + jnp.exp(sink - m)
    out = acc[:, :HEAD_DIM] / denom
    for g in range(group):
        o_ref[:, g * HEAD_DIM:(g + 1) * HEAD_DIM] = out[g * BLK:(g + 1) * BLK, :].astype(o_ref.dtype)


def _stack_scaled_queries(q_ref, qs_ref, group):
    for g in range(group):
        qs_ref[g * BLK:(g + 1) * BLK, :] = (q_ref[:, g * HEAD_DIM:(g + 1) * HEAD_DIM].astype(F32) * SCALE).astype(BF16)


def _banded_kernel(sink_ref, q_ref, k_ref, v_ref, bias_ref, o_ref, qs_ref, *, group, n_prev, window):
    _banded_body(q_ref, k_ref, v_ref, bias_ref, o_ref, qs_ref, sink_ref, group=group, n_prev=n_prev, window=window,
                 scale_q=True)


def banded_attention(z, bias_raw, t, *, q_block, k_block, v_block, n_kv, group, window, sinks):
    n_prev = -(-window // BLK)
    qw = group * HEAD_DIM
    assert q_block % group == 0
    return pl.pallas_call(
        functools.partial(_banded_kernel, group=group, n_prev=n_prev, window=window),
        out_shape=jax.ShapeDtypeStruct((t, n_kv * qw), BF16),
        grid=(n_kv, t // BLK),
        in_specs=[pl.BlockSpec(memory_space=pltpu.SMEM),
                  pl.BlockSpec((BLK, qw), lambda h, i: (i, q_block // group + h)),
                  pl.BlockSpec((t, HEAD_DIM), lambda h, i: (0, k_block + h)),
                  pl.BlockSpec((t, HEAD_DIM), lambda h, i: (0, v_block + h)),
                  pl.BlockSpec((n_prev + 1, group, BLK, BLK), lambda h, i: (0, h, 0, 0))],
        out_specs=pl.BlockSpec((BLK, qw), lambda h, i: (i, h)),
        scratch_shapes=[pltpu.VMEM((group * BLK, HEAD_DIM), BF16)],
        compiler_params=_cparams(("parallel", "arbitrary")),
        name="banded_attention",
    )(sinks, z, z, z, bias_raw)


def _sg_kernel(u_ref, v_ref, lng_ref, lnb_ref, w_ref, b_ref, o_ref):
    u = _gelu_tanh(u_ref[...].astype(F32))
    gv = _gelu_tanh(v_ref[...].astype(F32))
    mu = jnp.mean(gv, axis=-1, keepdims=True)
    cen = gv - mu
    var = jnp.mean(cen * cen, axis=-1, keepdims=True)
    gv = (cen * lax.rsqrt(var + RMS_EPS) * lng_ref[...] + lnb_ref[...]).astype(BF16)
    row = lax.broadcasted_iota(jnp.int32, (SG_CHUNK, SG_CHUNK), 0)
    col = lax.broadcasted_iota(jnp.int32, (SG_CHUNK, SG_CHUNK), 1)
    gw = u.shape[1] // SG_GROUPS
    for g in range(SG_GROUPS):
        sl = slice(g * gw, (g + 1) * gw)
        w = jnp.where(row >= col, w_ref[g], 0.0).astype(BF16)
        spatial = _dot(w, gv[:, sl]) + b_ref[:, g:g + 1]
        o_ref[:, sl] = (u[:, sl] * spatial).astype(o_ref.dtype)


def spatial_gating(z1, ln_g, ln_b, sg_w, sg_b, t):
    width = ln_g.shape[0]
    assert width % (SG_GROUPS * LANE) == 0
    return pl.pallas_call(
        _sg_kernel,
        out_shape=jax.ShapeDtypeStruct((t, width), BF16),
        grid=(t // SG_CHUNK,),
        in_specs=[pl.BlockSpec((SG_CHUNK, width), lambda i: (i, 0)),
                  pl.BlockSpec((SG_CHUNK, width), lambda i: (i, 1)),
                  pl.BlockSpec((1, width), lambda i: (0, 0)),
                  pl.BlockSpec((1, width), lambda i: (0, 0)),
                  pl.BlockSpec((SG_GROUPS, SG_CHUNK, SG_CHUNK), lambda i: (0, 0, 0)),
                  pl.BlockSpec((SG_CHUNK, SG_GROUPS), lambda i: (0, 0))],
        out_specs=pl.BlockSpec((SG_CHUNK, width), lambda i: (i, 0)),
        compiler_params=_cparams(("parallel",)),
        name="spatial_gating",
    )(z1, z1, ln_g.reshape(1, width), ln_b.reshape(1, width), sg_w, jnp.transpose(sg_b))


MOE_TM = 512
MOE_TOP_K = 2


def moe_plan(idx, gate, t):
    n_rows = MOE_TOP_K * t + N_EXPERTS * MOE_TM
    n_blocks = n_rows // MOE_TM
    pair_e = idx[:, :MOE_TOP_K].reshape(-1)
    pair_w = gate[:, :MOE_TOP_K].reshape(-1)
    onehot = (pair_e[:, None] == jnp.arange(N_EXPERTS, dtype=jnp.int32)[None, :]).astype(jnp.int32)
    before = jnp.cumsum(onehot, axis=0) - onehot
    count = jnp.sum(onehot, axis=0)
    padded = (count + MOE_TM - 1) // MOE_TM * MOE_TM
    group_end = jnp.cumsum(padded)
    group_start = group_end - padded
    pair_row = jnp.sum(onehot * (before + group_start[None, :]), axis=1)
    row_pair = jnp.full((n_rows,), -1, jnp.int32).at[pair_row].set(jnp.arange(MOE_TOP_K * t, dtype=jnp.int32))
    row_tok = jnp.maximum(row_pair, 0) // MOE_TOP_K
    row_w = jnp.where(row_pair >= 0, pair_w[jnp.maximum(row_pair, 0)], 0.0)
    block_start = jnp.arange(n_blocks, dtype=jnp.int32) * MOE_TM
    block_e = jnp.minimum(jnp.sum(block_start[:, None] >= group_end[None, :], axis=1), N_EXPERTS - 1).astype(jnp.int32)
    block_rows = (group_start + count)[block_e] - block_start
    block_half = (block_rows <= MOE_TM // 2).astype(jnp.int32)
    n_used = (group_end[-1] // MOE_TM).astype(jnp.int32).reshape(1)
    nonempty = count > 0
    group_e = jnp.argsort(jnp.logical_not(nonempty), stable=True).astype(jnp.int32)
    block_group = (jnp.cumsum(nonempty.astype(jnp.int32)) - 1)[block_e].astype(jnp.int32)
    n_groups = jnp.sum(nonempty.astype(jnp.int32)).reshape(1)
    blocks = (block_e, block_half, n_used, block_group, group_e, n_groups)
    return row_tok, row_w.reshape(n_rows, 1), blocks, pair_row.astype(jnp.int32)


MOE_GATHER_ROWS = 256


def _gather_rows_kernel(tok_ref, tok_next_ref, src_ref, o_ref, buf, sem):
    b = pl.program_id(0)
    slot = b % 2

    def row_copy(idx_ref, k, s):
        return pltpu.make_async_copy(src_ref.at[idx_ref[k]], buf.at[s, k], sem.at[s])

    def start_block(idx_ref, s):
        def start(k, carry):
            row_copy(idx_ref, k, s).start()
            return carry
        lax.fori_loop(0, MOE_GATHER_ROWS, start, 0, unroll=8)

    @pl.when(b == 0)
    def _():
        start_block(tok_ref, 0)

    @pl.when(b + 1 < pl.num_programs(0))
    def _():
        start_block(tok_next_ref, 1 - slot)

    def wait(k, carry):
        row_copy(tok_ref, k, slot).wait()
        return carry

    lax.fori_loop(0, MOE_GATHER_ROWS, wait, 0, unroll=8)
    o_ref[...] = buf[slot].reshape(o_ref.shape)


def gather_rows(src, row_tok):
    n_rows = row_tok.shape[0]
    d = src.shape[1] * src.shape[2]
    n_steps = n_rows // MOE_GATHER_ROWS
    return pl.pallas_call(
        _gather_rows_kernel,
        out_shape=jax.ShapeDtypeStruct((n_rows, d), src.dtype),
        grid=(n_steps,),
        in_specs=[pl.BlockSpec((MOE_GATHER_ROWS,), lambda b: (b,), memory_space=pltpu.SMEM),
                  pl.BlockSpec((MOE_GATHER_ROWS,), lambda b: (jnp.minimum(b + 1, n_steps - 1),),
                               memory_space=pltpu.SMEM),
                  pl.BlockSpec(memory_space=pl.ANY)],
        out_specs=pl.BlockSpec((MOE_GATHER_ROWS, d), lambda b: (b, 0)),
        scratch_shapes=[pltpu.VMEM((2, MOE_GATHER_ROWS) + src.shape[1:], src.dtype), pltpu.SemaphoreType.DMA((2,))],
        compiler_params=_cparams(("arbitrary",)),
        name="moe_gather",
    )(row_tok, row_tok, src)


def _moe_rows(b, half_ref, n_used_ref, compute, o_ref):
    live = b < n_used_ref[0]
    half = half_ref[jnp.minimum(b, n_used_ref[0] - 1)] == 1
    lower, upper = slice(0, MOE_TM // 2), slice(MOE_TM // 2, MOE_TM)

    @pl.when(jnp.logical_and(live, jnp.logical_not(half)))
    def _():
        compute(slice(0, MOE_TM))

    @pl.when(jnp.logical_and(live, half))
    def _():
        compute(lower)
        o_ref[upper, :] = jnp.zeros((MOE_TM // 2, o_ref.shape[1]), o_ref.dtype)

    @pl.when(jnp.logical_not(live))
    def _():
        o_ref[...] = jnp.zeros(o_ref.shape, o_ref.dtype)


def _moe_panels(tables, w_refs, wbf_refs, stage_refs, sem, tn):
    block_e_ref, _, n_used_ref, block_group_ref, group_e_ref, n_groups_ref = tables
    j, b = pl.program_id(0), pl.program_id(1)
    n_groups = n_groups_ref[0]
    group = block_group_ref[b]
    first_block = jnp.logical_or(b == 0, group != block_group_ref[jnp.maximum(b - 1, 0)])

    def panel(widx, expert, jj):
        return _panel_copy(w_refs[widx], (0, expert), jj * tn, tn, False, stage_refs[widx], sem, widx)

    @pl.when(jnp.logical_and(b < n_used_ref[0], first_block))
    def _():
        number = j * n_groups + group

        @pl.when(number == 0)
        def _():
            for widx in range(len(w_refs)):
                panel(widx, block_e_ref[b], 0).start()

        for widx in range(len(w_refs)):
            panel(widx, block_e_ref[b], j).wait()
            wbf_refs[widx][...] = stage_refs[widx][...].astype(BF16)

        @pl.when(number + 1 < pl.num_programs(0) * n_groups)
        def _():
            wraps = group + 1 == n_groups
            next_e = group_e_ref[jnp.where(wraps, 0, group + 1)]
            next_j = jnp.where(wraps, j + 1, j)
            for widx in range(len(w_refs)):
                panel(widx, next_e, next_j).start()


def _moe_up_kernel(*refs, tn):
    tables, (x_ref, wg_ref, wu_ref, o_ref, wg_bf, wu_bf, wg_st, wu_st, sem) = refs[:6], refs[6:]
    _moe_panels(tables, (wg_ref, wu_ref), (wg_bf, wu_bf), (wg_st, wu_st), sem, tn)

    def compute(rows):
        x = x_ref[rows, :]
        gate = _dot(x, wg_bf[...])
        o_ref[rows, :] = ((gate * _sigmoid(gate)) * _dot(x, wu_bf[...])).astype(o_ref.dtype)

    _moe_rows(pl.program_id(1), tables[1], tables[2], compute, o_ref)


def _moe_down_kernel(*refs, tn):
    tables, (h_ref, wd_ref, w_ref, o_ref, wd_bf, wd_st, sem) = refs[:6], refs[6:]
    _moe_panels(tables, (wd_ref,), (wd_bf,), (wd_st,), sem, tn)

    def compute(rows):
        o_ref[rows, :] = w_ref[rows, :] * _dot(h_ref[rows, :], wd_bf[...])

    _moe_rows(pl.program_id(1), tables[1], tables[2], compute, o_ref)


def moe_experts(xs, row_w, blocks, w_gate, w_up, w_down):
    n_rows, d = xs.shape
    d_ff = w_gate.shape[3]
    n_blocks = n_rows // MOE_TM
    tn_up, tn_down = 512, 1024
    row_block = lambda j, b, be, bh, nu, bg, ge, ng: (jnp.minimum(b, nu[0] - 1), 0)
    out_block = lambda j, b, be, bh, nu, bg, ge, ng: (b, j)
    in_hbm = pl.BlockSpec(memory_space=pl.ANY)
    hid = pl.pallas_call(
        functools.partial(_moe_up_kernel, tn=tn_up),
        out_shape=jax.ShapeDtypeStruct((n_rows, d_ff), BF16),
        grid_spec=pltpu.PrefetchScalarGridSpec(
            num_scalar_prefetch=len(blocks),
            grid=(d_ff // tn_up, n_blocks),
            in_specs=[pl.BlockSpec((MOE_TM, d), row_block), in_hbm, in_hbm],
            out_specs=pl.BlockSpec((MOE_TM, tn_up), out_block),
            scratch_shapes=[pltpu.VMEM((d, tn_up), BF16), pltpu.VMEM((d, tn_up), BF16),
                            pltpu.VMEM((d, tn_up), F32), pltpu.VMEM((d, tn_up), F32),
                            pltpu.SemaphoreType.DMA((2,))]),
        compiler_params=_cparams(("arbitrary", "arbitrary")),
        name="moe_up",
    )(*blocks, xs, w_gate, w_up)
    return pl.pallas_call(
        functools.partial(_moe_down_kernel, tn=tn_down),
        out_shape=jax.ShapeDtypeStruct((n_rows, d), F32),
        grid_spec=pltpu.PrefetchScalarGridSpec(
            num_scalar_prefetch=len(blocks),
            grid=(d // tn_down, n_blocks),
            in_specs=[pl.BlockSpec((MOE_TM, d_ff), row_block), in_hbm, pl.BlockSpec((MOE_TM, 1), row_block)],
            out_specs=pl.BlockSpec((MOE_TM, tn_down), out_block),
            scratch_shapes=[pltpu.VMEM((d_ff, tn_down), BF16), pltpu.VMEM((d_ff, tn_down), F32),
                            pltpu.SemaphoreType.DMA((1,))]),
        compiler_params=_cparams(("arbitrary", "arbitrary")),
        name="moe_down",
    )(*blocks, hid, w_down, row_w)


MOE_TC = 128


def _moe_combine_kernel(rows_ref, rows_next_ref, x_ref, g_ref, y_ref, o_ref, buf, sem):
    b = pl.program_id(0)
    slot = b % 2

    def row_copy(idx_ref, k, choice, s):
        row = idx_ref[MOE_TOP_K * k + choice]
        return pltpu.make_async_copy(y_ref.at[pl.ds(row, 1), :], buf.at[s, choice, pl.ds(k, 1), :], sem.at[s])

    def start_block(idx_ref, s):
        def start(k, carry):
            for choice in range(MOE_TOP_K):
                row_copy(idx_ref, k, choice, s).start()
            return carry
        lax.fori_loop(0, MOE_TC, start, 0, unroll=4)

    @pl.when(b == 0)
    def _():
        start_block(rows_ref, 0)

    @pl.when(b + 1 < pl.num_programs(0))
    def _():
        start_block(rows_next_ref, 1 - slot)

    def wait(k, carry):
        for choice in range(MOE_TOP_K):
            row_copy(rows_ref, k, choice, slot).wait()
        return carry

    lax.fori_loop(0, MOE_TC, wait, 0, unroll=4)
    x = x_ref[...]
    for choice in range(MOE_TOP_K):
        x = x + buf[slot, choice]
    y = x * lax.rsqrt(jnp.mean(x * x, axis=-1, keepdims=True) + RMS_EPS)
    o_ref[...] = (y * g_ref[...]).astype(o_ref.dtype)


def moe_combine_rmsnorm(x, ys, pair_row, g):
    t, d = x.shape
    n_steps = t // MOE_TC
    idx_block = lambda index_map: pl.BlockSpec((MOE_TOP_K * MOE_TC,), index_map, memory_space=pltpu.SMEM)
    return pl.pallas_call(
        _moe_combine_kernel,
        out_shape=jax.ShapeDtypeStruct((t, d), F32),
        grid=(n_steps,),
        in_specs=[idx_block(lambda i: (i,)),
                  idx_block(lambda i: (jnp.minimum(i + 1, n_steps - 1),)),
                  pl.BlockSpec((MOE_TC, d), lambda i: (i, 0)),
                  pl.BlockSpec((1, d), lambda i: (0, 0)),
                  pl.BlockSpec(memory_space=pl.ANY)],
        out_specs=pl.BlockSpec((MOE_TC, d), lambda i: (i, 0)),
        scratch_shapes=[pltpu.VMEM((2, MOE_TOP_K, MOE_TC, d), F32), pltpu.SemaphoreType.DMA((2,))],
        compiler_params=_cparams(("arbitrary",)),
        name="moe_combine",
    )(pair_row, pair_row, x, g.reshape(1, d), ys)


def kernel(x, rel_table, norm_mix, norm_ffn, norm_final, ab_w_in, ab_w_out, cmp_k_pe, cmp_k_w1, cmp_k_b1, cmp_k_w2, cmp_v_pe, cmp_v_w1, cmp_v_b1, cmp_v_w2, ffn_w_gate, ffn_w_up, ffn_w_down, cd_w_in, cd_w_out, sg_ln_g, sg_ln_b, sg_w, sg_b, swa_sinks, moe_w_router, moe_w_gate, moe_w_up, moe_w_down):
    b, t, d = x.shape
    assert b == 1
    xs = x.reshape(t, d)
    bias_raw, bias_sel, bias_cmp = bias_tiles(rel_table)

    h = rmsnorm(xs, norm_mix[0], BF16)
    ab_main = AB_MAIN_BLOCKS * LANE
    w_in_t = jnp.swapaxes(ab_w_in, 1, 2)
    z = matmul(h, [w_in_t], (0,), ab_main, out_dtype=BF16, w_transposed=True)
    w_gate_rows = jnp.pad(w_in_t[0, ab_main:], ((0, LANE - (w_in_t.shape[1] - ab_main)), (0, 0)))
    gate_logits = matmul(h, [w_gate_rows], (), LANE, out_dtype=F32, w_transposed=True)
    o_sb = sb_attention(z, t)
    kc = compress(z, AB_K_C, cmp_k_pe[0], cmp_k_w1[0], cmp_k_b1[0], cmp_k_w2[0], t)
    vc = compress(z, AB_V_C, cmp_v_pe[0], cmp_v_w1[0], cmp_v_b1[0], cmp_v_w2[0], t)
    o_cmp, pen, o_win = cmp_select_window(z, kc, vc, bias_cmp, bias_raw, t)
    o_nsa = sel_attention_mix(z, pen, bias_sel, gate_logits, o_cmp, o_win, t)
    xs = matmul([o_sb, o_nsa], [ab_w_out], (0,), d, out_dtype=F32, res=xs)
    h = rmsnorm(xs, norm_ffn[0], BF16)
    hid = matmul(h, [ffn_w_gate, ffn_w_up], (0,), ffn_w_gate.shape[2], out_dtype=BF16, act="swiglu")
    xs = matmul(hid, [ffn_w_down], (0,), d, out_dtype=F32, res=xs)

    h = rmsnorm(xs, norm_mix[1], BF16)
    z1 = matmul(h, [cd_w_in], (0,), cd_w_in.shape[2], out_dtype=BF16)
    o_sg = spatial_gating(z1, sg_ln_g[0], sg_ln_b[0], sg_w[0], sg_b[0], t)
    o_swa = banded_attention(z1, bias_raw, t, q_block=CD_Q, k_block=CD_K, v_block=CD_V,
                             n_kv=SWA_KV_HEADS, group=SWA_HEADS // SWA_KV_HEADS, window=SWA_WINDOW,
                             sinks=swa_sinks[0])
    xs = matmul([o_sg, o_swa], [cd_w_out], (0,), d, out_dtype=F32, res=xs)
    h, gate, idx = rmsnorm_router(xs, norm_ffn[1], moe_w_router[0])
    row_tok, row_w, blocks, pair_row = moe_plan(idx, gate, t)
    rows = gather_rows(h, row_tok)
    ys = moe_experts(rows, row_w, blocks, moe_w_gate, moe_w_up, moe_w_down)
    return moe_combine_rmsnorm(xs, ys, pair_row, norm_final).reshape(b, t, d)
```

```python
import functools
import math

import numpy as np
import jax
import jax.numpy as jnp
from jax import lax
from jax.experimental import pallas as pl
from jax.experimental.pallas import tpu as pltpu

F32 = jnp.float32
BF16 = jnp.bfloat16

LANE = 128
VMEM_LIMIT = 56 * 1024 * 1024

HEAD_DIM = 128
BLK = 128
NEG_INF = -1e30
RMS_EPS = 1e-6
SCALE = HEAD_DIM ** -0.5

SB_HEADS = 16
NSA_HEADS = 16
NSA_KV_HEADS = 4
NSA_G = NSA_HEADS // NSA_KV_HEADS
NSA_CMP_LEN = 32
NSA_CMP_STRIDE = 16
NSA_SEL_LEN = 64
NSA_SEL_TOPK = 16
NSA_WINDOW = 512
NSA_FORCE_SCORE = 1e6
SG_GROUPS = 8
SG_CHUNK = 128
SWA_HEADS = 16
SWA_KV_HEADS = 2
SWA_WINDOW = 128
REL_BUCKETS = 32
REL_MAX_DIST = 1024
REL_HEADS = 16
N_EXPERTS = 8

AB_Q_SB, AB_K_SB, AB_V_SB = 0, 16, 32
AB_Q_N = 48
AB_K_C, AB_V_C, AB_K_S, AB_V_S, AB_K_W, AB_V_W = 64, 68, 72, 76, 80, 84
AB_MAIN_BLOCKS = 88
CD_Q, CD_K, CD_V = 32, 48, 50

SEL_CHUNK_BLOCKS = 4
SEL_FAR_BLOCKS = 8
SEL_NEG_BLOCKS = SEL_CHUNK_BLOCKS - 1
SEL_TILES = SEL_NEG_BLOCKS + SEL_FAR_BLOCKS + SEL_CHUNK_BLOCKS - 1
RAW_TILES = NSA_WINDOW // BLK + 1
CMP_PAD = 120
SB_SKIP_LOG = -110.0
M_INIT = -5e29


def _cparams(sem):
    return pltpu.CompilerParams(dimension_semantics=sem, vmem_limit_bytes=VMEM_LIMIT)


def _dot(a, b):
    return jnp.dot(a, b, preferred_element_type=F32)


def _dot_nt(a, b):
    return lax.dot_general(a, b, (((1,), (1,)), ((), ())), preferred_element_type=F32)


def _dot_split(a, b):
    a_hi = a.astype(BF16)
    b_hi = b.astype(BF16)
    a_lo = (a - a_hi.astype(F32)).astype(BF16)
    b_lo = (b - b_hi.astype(F32)).astype(BF16)
    return _dot(a_hi, b_hi) + (_dot(a_hi, b_lo) + _dot(a_lo, b_hi))


def _sigmoid(x):
    return 1.0 / (1.0 + jnp.exp(-x))


def _gelu_tanh(x):
    return 0.5 * x * (1.0 + jnp.tanh(math.sqrt(2.0 / math.pi) * (x + 0.044715 * (x * x * x))))


def _rmsnorm_kernel(x_ref, g_ref, o_ref):
    x = x_ref[...]
    y = x * lax.rsqrt(jnp.mean(x * x, axis=-1, keepdims=True) + RMS_EPS)
    o_ref[...] = (y * g_ref[...]).astype(o_ref.dtype)


def rmsnorm(x, g, out_dtype):
    t, d = x.shape
    tm = 256
    return pl.pallas_call(
        _rmsnorm_kernel,
        out_shape=jax.ShapeDtypeStruct((t, d), out_dtype),
        grid=(t // tm,),
        in_specs=[pl.BlockSpec((tm, d), lambda i: (i, 0)), pl.BlockSpec((1, d), lambda i: (0, 0))],
        out_specs=pl.BlockSpec((tm, d), lambda i: (i, 0)),
        compiler_params=_cparams(("parallel",)),
        name="rmsnorm",
    )(x, g.reshape(1, d))


def _rmsnorm_router_kernel(x_ref, g_ref, wr_ref, o_ref, gate_ref, idx_ref):
    x = x_ref[...]
    y = x * lax.rsqrt(jnp.mean(x * x, axis=-1, keepdims=True) + RMS_EPS)
    y = y * g_ref[...]
    o_ref[...] = y.astype(o_ref.dtype).reshape(o_ref.shape)
    logits = _dot_split(y, wr_ref[...])
    lane = lax.broadcasted_iota(jnp.int32, logits.shape, 1)
    logits = jnp.where(lane < N_EXPERTS, logits, NEG_INF)
    m1 = jnp.max(logits, axis=-1, keepdims=True)
    i1 = jnp.min(jnp.where(logits == m1, lane, LANE), axis=-1, keepdims=True)
    rest = jnp.where(lane == i1, NEG_INF, logits)
    m2 = jnp.max(rest, axis=-1, keepdims=True)
    i2 = jnp.min(jnp.where(rest == m2, lane, LANE), axis=-1, keepdims=True)
    e2 = jnp.exp(m2 - m1)
    denom = 1.0 + e2
    gate_ref[...] = jnp.where(lane == 0, 1.0 / denom, jnp.where(lane == 1, e2 / denom, 0.0))
    idx_ref[...] = jnp.where(lane == 0, i1, jnp.where(lane == 1, i2, 0))


def rmsnorm_router(x, g, w_router):
    t, d = x.shape
    tm = 256
    wr = jnp.pad(w_router, ((0, 0), (0, LANE - w_router.shape[1])))
    return pl.pallas_call(
        _rmsnorm_router_kernel,
        out_shape=(jax.ShapeDtypeStruct((t, d // LANE, LANE), BF16), jax.ShapeDtypeStruct((t, LANE), F32),
                   jax.ShapeDtypeStruct((t, LANE), jnp.int32)),
        grid=(t // tm,),
        in_specs=[pl.BlockSpec((tm, d), lambda i: (i, 0)), pl.BlockSpec((1, d), lambda i: (0, 0)),
                  pl.BlockSpec((d, LANE), lambda i: (0, 0))],
        out_specs=(pl.BlockSpec((tm, d // LANE, LANE), lambda i: (i, 0, 0)), pl.BlockSpec((tm, LANE), lambda i: (i, 0)),
                   pl.BlockSpec((tm, LANE), lambda i: (i, 0))),
        compiler_params=_cparams(("parallel",)),
        name="rmsnorm_router",
    )(x, g.reshape(1, d), wr)


def _mm_epilogue(parts, res_ref, o_ref, act):
    r = parts[0]
    if act == "swiglu":
        r = (r * _sigmoid(r)) * parts[1]
    if res_ref is not None:
        r = res_ref[...] + r
    o_ref[...] = r.astype(o_ref.dtype)


def _panel_copy(w_ref, lead, col0, tn, transposed, stage_ref, sem_ref, widx):
    window = (pl.ds(col0, tn), slice(None)) if transposed else (slice(None), pl.ds(col0, tn))
    return pltpu.make_async_copy(w_ref.at[tuple(lead) + window], stage_ref, sem_ref.at[widx])


def _mm_resident_kernel(*refs, n_a, n_w, has_res, act, w_transposed, lead, tn):
    a_refs = refs[:n_a]
    w_refs = refs[n_a:n_a + n_w]
    res_ref = refs[n_a + n_w] if has_res else None
    o_ref = refs[n_a + n_w + int(has_res)]
    scratch = refs[n_a + n_w + int(has_res) + 1:]
    wbf_refs, stage_refs, sem = scratch[:n_w], scratch[n_w:2 * n_w], scratch[2 * n_w]
    j = pl.program_id(0)

    def panel(widx, jj):
        return _panel_copy(w_refs[widx], lead, jj * tn, tn, w_transposed, stage_refs[widx], sem, widx)

    @pl.when(pl.program_id(1) == 0)
    def _():
        @pl.when(j == 0)
        def _():
            for widx in range(n_w):
                panel(widx, 0).start()

        for widx in range(n_w):
            panel(widx, j).wait()
            wbf_refs[widx][...] = stage_refs[widx][...].astype(BF16)

        @pl.when(j + 1 < pl.num_programs(0))
        def _():
            for widx in range(n_w):
                panel(widx, j + 1).start()

    parts = []
    for wbf in wbf_refs:
        k0, part = 0, None
        for a_ref in a_refs:
            k1 = k0 + a_ref.shape[1]
            term = _dot_nt(a_ref[...], wbf[:, k0:k1]) if w_transposed else _dot(a_ref[...], wbf[k0:k1, :])
            part = term if part is None else part + term
            k0 = k1
        parts.append(part)
    _mm_epilogue(parts, res_ref, o_ref, act)


def _mm_ktiled_kernel(a_ref, w_ref, res_ref, o_ref, acc_ref, *, nk):
    k = pl.program_id(2)
    part = _dot(a_ref[...], w_ref[...].astype(BF16))

    @pl.when(k == 0)
    def _():
        acc_ref[...] = part

    @pl.when(k > 0)
    def _():
        acc_ref[...] += part

    @pl.when(k == nk - 1)
    def _():
        _mm_epilogue([acc_ref[...]], res_ref, o_ref, None)


MM_RESIDENT_MAX_K = 4096
MM_VMEM_BUDGET = 50 * 1024 * 1024


def _resident_tiles(m, kdim, n_cols, n_w, out_bytes, has_res):
    for tn in (1024, 512, 256, LANE):
        for tm in (1024, 512):
            if n_cols % tn or m % tm:
                continue
            a_buf = 2 * tm * kdim * 2
            w_buf = n_w * (kdim * tn * 4 + kdim * tn * 2)
            o_buf = 2 * tm * tn * out_bytes + (2 * tm * tn * 4 if has_res else 0)
            if a_buf + w_buf + o_buf <= MM_VMEM_BUDGET:
                return tm, tn
    raise ValueError("no resident-panel tiling fits VMEM")


def matmul(a, ws, lead, n_cols, *, out_dtype, res=None, act=None, w_transposed=False):
    a_parts = list(a) if isinstance(a, (list, tuple)) else [a]
    m = a_parts[0].shape[0]
    kdim = sum(part.shape[1] for part in a_parts)
    lead = tuple(lead)
    squeeze = (None,) * len(lead)
    if kdim <= MM_RESIDENT_MAX_K:
        tm, tn = _resident_tiles(m, kdim, n_cols, len(ws), jnp.dtype(out_dtype).itemsize, res is not None)
        in_specs = [pl.BlockSpec((tm, part.shape[1]), lambda j, i: (i, 0)) for part in a_parts]
        panel = (tn, kdim) if w_transposed else (kdim, tn)
        in_specs += [pl.BlockSpec(memory_space=pl.ANY) for _ in ws]
        args = [*a_parts, *ws]
        if res is not None:
            in_specs.append(pl.BlockSpec((tm, tn), lambda j, i: (i, j)))
            args.append(res)
        return pl.pallas_call(
            functools.partial(_mm_resident_kernel, n_a=len(a_parts), n_w=len(ws), has_res=res is not None, act=act,
                              w_transposed=w_transposed, lead=lead, tn=tn),
            out_shape=jax.ShapeDtypeStruct((m, n_cols), out_dtype),
            grid=(n_cols // tn, m // tm),
            in_specs=in_specs,
            out_specs=pl.BlockSpec((tm, tn), lambda j, i: (i, j)),
            scratch_shapes=([pltpu.VMEM(panel, BF16) for _ in ws] + [pltpu.VMEM(panel, F32) for _ in ws]
                            + [pltpu.SemaphoreType.DMA((len(ws),))]),
            compiler_params=_cparams(("arbitrary", "arbitrary")),
            name="matmul_resident",
        )(*args)
    assert len(ws) == 1 and act is None and res is not None and len(a_parts) == 1 and not w_transposed
    a = a_parts[0]
    tm, tn, tk = 2048, 512, 2048
    assert m % tm == 0 and n_cols % tn == 0 and kdim % tk == 0
    nk = kdim // tk
    return pl.pallas_call(
        functools.partial(_mm_ktiled_kernel, nk=nk),
        out_shape=jax.ShapeDtypeStruct((m, n_cols), out_dtype),
        grid=(n_cols // tn, m // tm, nk),
        in_specs=[pl.BlockSpec((tm, tk), lambda j, i, k: (i, k)),
                  pl.BlockSpec(squeeze + (tk, tn), lambda j, i, k: lead + (k, j)),
                  pl.BlockSpec((tm, tn), lambda j, i, k: (i, j))],
        out_specs=pl.BlockSpec((tm, tn), lambda j, i, k: (i, j)),
        scratch_shapes=[pltpu.VMEM((tm, tn), F32)],
        compiler_params=_cparams(("parallel", "parallel", "arbitrary")),
        name="matmul_ktiled",
    )(a, ws[0], res)


def _bucket_thresholds():
    d = np.arange(0, 4 * REL_MAX_DIST)
    max_exact = REL_BUCKETS // 2
    log_ratio = np.log(np.maximum(d, 1).astype(np.float64) / max_exact) / math.log(REL_MAX_DIST / max_exact)
    far = np.minimum(max_exact + (log_ratio * (REL_BUCKETS - max_exact)).astype(np.int32), REL_BUCKETS - 1)
    bucket = np.where(d < max_exact, d, far)
    assert np.all(np.diff(bucket) >= 0) and bucket[-1] == REL_BUCKETS - 1
    return [int(np.argmax(bucket >= b)) for b in range(REL_BUCKETS)]


_BUCKET_THR = _bucket_thresholds()
REL_FAR_DIST = _BUCKET_THR[REL_BUCKETS - 1]
assert REL_FAR_DIST <= SEL_FAR_BLOCKS * BLK - (BLK - 1)
assert REL_FAR_DIST <= NSA_CMP_STRIDE * (CMP_PAD + 1) - (NSA_CMP_LEN - 1)


def _bias_kernel(tab_ref, raw_ref, sel_ref, cmp_ref):
    h = pl.program_id(0)
    row = lax.broadcasted_iota(jnp.int32, (BLK, BLK), 0)
    col = lax.broadcasted_iota(jnp.int32, (BLK, BLK), 1)

    def lookup(dist):
        dd = jnp.maximum(dist, 0)
        b = jnp.full(dist.shape, tab_ref[0, h], F32)
        for bk in range(1, REL_BUCKETS):
            b = jnp.where(dd >= _BUCKET_THR[bk], tab_ref[bk, h], b)
        return b

    far = tab_ref[REL_BUCKETS - 1, h]
    for d in range(RAW_TILES):
        raw_ref[d] = lookup(d * BLK + row - col)
    for step in range(SEL_TILES):
        dist = (step - SEL_NEG_BLOCKS) * BLK + row - col
        sel_ref[step] = jnp.where(dist < 0, NEG_INF, lookup(dist) - far)
    dist_c = row - NSA_CMP_STRIDE * (col - CMP_PAD) - (NSA_CMP_LEN - 1)
    cmp_ref[...] = lookup(dist_c) - far


def bias_tiles(rel_table):
    tiles = lambda n: pl.BlockSpec((n, None, BLK, BLK), lambda h: (0, h, 0, 0))
    return pl.pallas_call(
        _bias_kernel,
        out_shape=(jax.ShapeDtypeStruct((RAW_TILES, REL_HEADS, BLK, BLK), F32),
                   jax.ShapeDtypeStruct((SEL_TILES, REL_HEADS, BLK, BLK), F32),
                   jax.ShapeDtypeStruct((REL_HEADS, BLK, BLK), F32)),
        grid=(REL_HEADS,),
        in_specs=[pl.BlockSpec(memory_space=pltpu.SMEM)],
        out_specs=(tiles(RAW_TILES), tiles(SEL_TILES), pl.BlockSpec((None, BLK, BLK), lambda h: (h, 0, 0))),
        compiler_params=_cparams(("arbitrary",)),
        name="bias_tiles",
    )(rel_table)


SB_BQ = 256
SB_GROUP = 4


def _sb_kernel(q_ref, k_ref, v_ref, o_ref, c_ref, acc_ref):
    qb = pl.program_id(1)
    row = lax.broadcasted_iota(jnp.int32, (SB_BQ, SB_BQ), 0)
    col = lax.broadcasted_iota(jnp.int32, (SB_BQ, SB_BQ), 1)
    upper = (row > col).astype(BF16)

    def block(h, kb, past):
        cols = slice(h * HEAD_DIM, (h + 1) * HEAD_DIM)
        ks = pl.multiple_of(kb * SB_BQ, SB_BQ)
        z = _dot_nt(q_ref[:, cols], k_ref[pl.ds(ks, SB_BQ), cols]) * SCALE
        sp = jnp.maximum(z, 0.0) + jnp.log(1.0 + jnp.exp(-jnp.abs(z)))
        log_stay = -sp
        if past is not None:
            log_stay = jnp.where(past, log_stay, 0.0)
        later = _dot(log_stay.astype(BF16), upper)
        tot = jnp.broadcast_to(jnp.sum(log_stay, axis=-1, keepdims=True), (SB_BQ, HEAD_DIM))
        if past is not None:
            w = jnp.where(past, jnp.exp((z - sp) + later), 0.0)
            c_new = tot
        else:
            carry = c_ref[:, cols]
            w = jnp.exp((z - sp) + later + jnp.concatenate([carry] * (SB_BQ // HEAD_DIM), axis=1))
            c_new = carry + tot
        contrib = _dot(w.astype(BF16), v_ref[pl.ds(ks, SB_BQ), cols])
        if past is not None:
            acc_ref[:, cols] = contrib
        else:
            acc_ref[:, cols] += contrib
        c_ref[:, cols] = c_new
        return jnp.max(c_new)

    def sweep(kb, past):
        return functools.reduce(jnp.maximum, [block(h, kb, past) for h in range(SB_GROUP)])

    def cond(state):
        kb, c_max = state
        return jnp.logical_and(kb >= 0, c_max > SB_SKIP_LOG)

    def body(state):
        kb, _ = state
        return kb - 1, sweep(kb, None)

    lax.while_loop(cond, body, (qb - 1, sweep(qb, col < row)))
    o_ref[...] = acc_ref[...].astype(o_ref.dtype)


def sb_attention(z, t):
    width = SB_GROUP * HEAD_DIM
    assert SB_HEADS % SB_GROUP == 0 and AB_K_SB % SB_GROUP == 0 and AB_V_SB % SB_GROUP == 0
    return pl.pallas_call(
        _sb_kernel,
        out_shape=jax.ShapeDtypeStruct((t, SB_HEADS * HEAD_DIM), BF16),
        grid=(SB_HEADS // SB_GROUP, t // SB_BQ),
        in_specs=[pl.BlockSpec((SB_BQ, width), lambda h, i: (i, AB_Q_SB // SB_GROUP + h)),
                  pl.BlockSpec((t, width), lambda h, i: (0, AB_K_SB // SB_GROUP + h)),
                  pl.BlockSpec((t, width), lambda h, i: (0, AB_V_SB // SB_GROUP + h))],
        out_specs=pl.BlockSpec((SB_BQ, width), lambda h, i: (i, h)),
        scratch_shapes=[pltpu.VMEM((SB_BQ, width), F32), pltpu.VMEM((SB_BQ, width), F32)],
        compiler_params=_cparams(("parallel", "arbitrary")),
        name="sb_attention",
    )(z, z, z)


def _compress_kernel(a_ref, pe_ref, w1_ref, b1_ref, w2_ref, o_ref):
    n_chunk = a_ref.shape[0]
    half = NSA_CMP_STRIDE * HEAD_DIM
    chunks = a_ref[...].reshape(n_chunk, half).astype(F32)
    first = _dot((chunks + pe_ref[:, :half]).astype(BF16), w1_ref[:half, :].astype(BF16))
    second = _dot((chunks + pe_ref[:, half:]).astype(BF16), w1_ref[half:, :].astype(BF16))
    hid = _gelu_tanh(first + pltpu.roll(second, n_chunk - 1, 0) + b1_ref[...])
    out = _dot(hid.astype(BF16), w2_ref[...].astype(BF16))
    o_ref[0:CMP_PAD, :] = jnp.zeros((CMP_PAD, HEAD_DIM), o_ref.dtype)
    o_ref[CMP_PAD:CMP_PAD + n_chunk, :] = out.astype(o_ref.dtype)
    o_ref[CMP_PAD + n_chunk:, :] = jnp.zeros((o_ref.shape[0] - CMP_PAD - n_chunk, HEAD_DIM), o_ref.dtype)


def compress(z, col_block, pe, w1, b1, w2, t):
    assert NSA_CMP_LEN == 2 * NSA_CMP_STRIDE
    n_chunk = t // NSA_CMP_STRIDE
    width = NSA_CMP_LEN * HEAD_DIM
    hidden = w1.shape[-1]
    rows = CMP_PAD + n_chunk + 8
    return pl.pallas_call(
        _compress_kernel,
        out_shape=jax.ShapeDtypeStruct((NSA_KV_HEADS, rows, HEAD_DIM), BF16),
        grid=(NSA_KV_HEADS,),
        in_specs=[pl.BlockSpec((n_chunk, NSA_CMP_STRIDE, HEAD_DIM), lambda h: (0, 0, col_block + h)),
                  pl.BlockSpec((1, width), lambda h: (0, 0)),
                  pl.BlockSpec((width, hidden), lambda h: (0, 0)),
                  pl.BlockSpec((1, hidden), lambda h: (0, 0)),
                  pl.BlockSpec((hidden, HEAD_DIM), lambda h: (0, 0))],
        out_specs=pl.BlockSpec((None, rows, HEAD_DIM), lambda h: (h, 0, 0)),
        compiler_params=_cparams(("parallel",)),
        name="compress",
    )(z.reshape(n_chunk, NSA_CMP_STRIDE, z.shape[1]), pe.reshape(1, width), w1, b1.reshape(1, hidden), w2)


def _cmp_to_sel_padded(t):
    n_cmp = (t - NSA_CMP_LEN) // NSA_CMP_STRIDE + 1
    n_sel = t // NSA_SEL_LEN
    cmp_end = np.arange(n_cmp) * NSA_CMP_STRIDE + NSA_CMP_LEN - 1
    cmp_start = cmp_end - (NSA_CMP_LEN - 1)
    sel_start = np.arange(n_sel) * NSA_SEL_LEN
    overlap = (np.minimum(cmp_end[:, None], sel_start[None, :] + NSA_SEL_LEN - 1)
               - np.maximum(cmp_start[:, None], sel_start[None, :]) + 1)
    c2s = np.clip(overlap, 0, None).astype(np.float32) / NSA_CMP_LEN
    out = np.zeros((CMP_PAD + n_cmp + 9, n_sel), np.float32)
    out[CMP_PAD:CMP_PAD + n_cmp] = c2s
    return out


def _cmp_win_kernel(q_ref, kc_ref, vc_ref, c2s_ref, bias_ref, kw_ref, vw_ref, bias_win_ref, o_ref, pen_ref, ow_ref,
                    qs_ref, *, n_far):
    qb = pl.program_id(1)
    _stack_scaled_queries(q_ref, qs_ref, NSA_G)
    r0 = pl.multiple_of(qb * (BLK // NSA_CMP_STRIDE), 8)
    kc_far = kc_ref[0:n_far, :]
    kc_near = kc_ref[pl.ds(r0, BLK), :]
    vs_far = jnp.concatenate([vc_ref[0:n_far, :], c2s_ref[0:n_far, :]], axis=1)
    vs_near = jnp.concatenate([vc_ref[pl.ds(r0, BLK), :], c2s_ref[pl.ds(r0, BLK), :]], axis=1)
    r_far = lax.broadcasted_iota(jnp.int32, (BLK, n_far), 1)
    far_ok = jnp.logical_and(r_far >= CMP_PAD, r_far < r0)
    row = lax.broadcasted_iota(jnp.int32, (BLK, BLK), 0)
    col = lax.broadcasted_iota(jnp.int32, (BLK, BLK), 1)
    dist_c = row - NSA_CMP_STRIDE * (col - CMP_PAD) - (NSA_CMP_LEN - 1)
    near_ok = jnp.logical_and(dist_c >= 0, col + r0 >= CMP_PAD)

    far_mask = jnp.where(far_ok, 0.0, NEG_INF)

    n_sel = c2s_ref.shape[1]
    imp = jnp.zeros((BLK, n_sel), F32)
    for g in range(NSA_G):
        q = qs_ref[g * BLK:(g + 1) * BLK, :]
        s_far = _dot_nt(q, kc_far) + far_mask
        s_near = _dot_nt(q, kc_near) + jnp.where(near_ok, bias_ref[g], NEG_INF)
        m = jnp.maximum(jnp.maximum(jnp.max(s_far, axis=-1, keepdims=True), jnp.max(s_near, axis=-1, keepdims=True)),
                        M_INIT)
        e_far = jnp.exp(s_far - m)
        e_near = jnp.exp(s_near - m)
        l = jnp.sum(e_far, axis=-1, keepdims=True) + jnp.sum(e_near, axis=-1, keepdims=True)
        inv = 1.0 / jnp.where(l > 0.0, l, 1.0)
        res = (_dot(e_far.astype(BF16), vs_far) + _dot(e_near.astype(BF16), vs_near)) * inv
        o_ref[:, g * HEAD_DIM:(g + 1) * HEAD_DIM] = res[:, :HEAD_DIM].astype(o_ref.dtype)
        imp = imp + res[:, HEAD_DIM:]

    blk = lax.broadcasted_iota(jnp.int32, (n_sel, BLK), 0)
    t_pos = qb * BLK + lax.broadcasted_iota(jnp.int32, (n_sel, BLK), 1)
    cur = t_pos // NSA_SEL_LEN
    forced = jnp.logical_or(blk == 0, jnp.logical_or(blk == cur, blk == cur - 1))
    score = jnp.where(forced, NSA_FORCE_SCORE, jnp.where(blk <= cur, imp.T, -1.0))
    pen = jnp.full((n_sel, BLK), NEG_INF, F32)
    for _ in range(min(NSA_SEL_TOPK, n_sel)):
        top = jnp.max(score, axis=0, keepdims=True)
        first = jnp.min(jnp.where(score == top, blk, n_sel), axis=0, keepdims=True)
        hit = blk == first
        pen = jnp.where(hit, 0.0, pen)
        score = jnp.where(hit, -2.0, score)
    pen_ref[...] = pen.T.astype(pen_ref.dtype)

    _banded_body(q_ref, kw_ref, vw_ref, bias_win_ref, ow_ref, qs_ref, None, qb, group=NSA_G,
                 n_prev=NSA_WINDOW // BLK, window=NSA_WINDOW, scale_q=False)


def cmp_select_window(z, kc, vc, bias_cmp, bias_raw, t):
    assert NSA_WINDOW % BLK == 0
    n_sel = t // NSA_SEL_LEN
    assert n_sel == LANE, "selection mask rides in one 128-lane tile"
    c2s = jnp.asarray(_cmp_to_sel_padded(t), BF16)
    rows = kc.shape[1]
    assert c2s.shape[0] == rows
    n_far = t // NSA_CMP_STRIDE
    qw = NSA_G * HEAD_DIM
    return pl.pallas_call(
        functools.partial(_cmp_win_kernel, n_far=n_far),
        out_shape=(jax.ShapeDtypeStruct((t, NSA_HEADS * HEAD_DIM), BF16),
                   jax.ShapeDtypeStruct((NSA_KV_HEADS, t, n_sel), BF16),
                   jax.ShapeDtypeStruct((t, NSA_HEADS * HEAD_DIM), BF16)),
        grid=(NSA_KV_HEADS, t // BLK),
        in_specs=[pl.BlockSpec((BLK, qw), lambda h, i: (i, AB_Q_N // NSA_G + h)),
                  pl.BlockSpec((None, rows, HEAD_DIM), lambda h, i: (h, 0, 0)),
                  pl.BlockSpec((None, rows, HEAD_DIM), lambda h, i: (h, 0, 0)),
                  pl.BlockSpec((rows, n_sel), lambda h, i: (0, 0)),
                  pl.BlockSpec((NSA_G, BLK, BLK), lambda h, i: (h, 0, 0)),
                  pl.BlockSpec((t, HEAD_DIM), lambda h, i: (0, AB_K_W + h)),
                  pl.BlockSpec((t, HEAD_DIM), lambda h, i: (0, AB_V_W + h)),
                  pl.BlockSpec((RAW_TILES, NSA_G, BLK, BLK), lambda h, i: (0, h, 0, 0))],
        out_specs=(pl.BlockSpec((BLK, qw), lambda h, i: (i, h)),
                   pl.BlockSpec((None, BLK, n_sel), lambda h, i: (h, i, 0)),
                   pl.BlockSpec((BLK, qw), lambda h, i: (i, h))),
        scratch_shapes=[pltpu.VMEM((NSA_G * BLK, HEAD_DIM), BF16)],
        compiler_params=_cparams(("parallel", "arbitrary")),
        name="cmp_select_window",
    )(z, kc, vc, c2s, bias_cmp, z, z, bias_raw)


SEL_CHUNK = SEL_CHUNK_BLOCKS * BLK


def _sel_kernel(q_ref, pen_ref, kaug_ref, vaug_ref, bias_ref, gate_ref, ocmp_ref, owin_ref, o_ref,
                qa_ref, s_ref, cmax_ref, m_ref, acc_ref):
    qb = pl.program_id(1)
    pen = pen_ref[...]
    for g in range(NSA_G):
        rows = slice(g * BLK, (g + 1) * BLK)
        qa_ref[rows, :HEAD_DIM] = (q_ref[:, g * HEAD_DIM:(g + 1) * HEAD_DIM].astype(F32) * SCALE).astype(BF16)
        qa_ref[rows, HEAD_DIM:] = pen
    m_ref[...] = jnp.full(m_ref.shape, M_INIT, F32)
    acc_ref[...] = jnp.zeros(acc_ref.shape, F32)
    n_chunks = qb // SEL_CHUNK_BLOCKS + 1
    last_chunk = kaug_ref.shape[0] // SEL_CHUNK - 1

    def chunk_start(c):
        return pl.multiple_of(jnp.minimum(c, last_chunk) * SEL_CHUNK, SEL_CHUNK)

    def scores(c, slot):
        s_all = _dot_nt(qa_ref[...], kaug_ref[pl.ds(chunk_start(c), SEL_CHUNK), :])
        first = qb - c * SEL_CHUNK_BLOCKS + SEL_NEG_BLOCKS
        n_rows = NSA_G * BLK
        tiles = [s_all[:, j * BLK:(j + 1) * BLK] + bias_ref[jnp.clip(first - j, 0, SEL_TILES - 1)].reshape(n_rows, BLK)
                 for j in range(SEL_CHUNK_BLOCKS)]
        for j, tile in enumerate(tiles):
            s_ref[slot, :, j * BLK:(j + 1) * BLK] = tile
        cmax_ref[slot] = jnp.broadcast_to(
            jnp.max(functools.reduce(jnp.maximum, tiles), axis=-1, keepdims=True), (n_rows, LANE))

    def absorb(c, slot):
        va = vaug_ref[pl.ds(chunk_start(c), SEL_CHUNK), :]
        m_old = m_ref[...]
        m_new = jnp.maximum(m_old, cmax_ref[slot])
        alpha = jnp.exp(m_old - m_new)
        p = jnp.concatenate([jnp.exp(s_ref[slot, :, j * BLK:(j + 1) * BLK] - m_new)
                             for j in range(SEL_CHUNK_BLOCKS)], axis=1).astype(BF16)
        acc_ref[...] = jnp.concatenate([alpha, alpha], axis=1) * acc_ref[...] + _dot(p, va)
        m_ref[...] = m_new

    def pair(k, carry):
        scores(2 * k + 1, 1)
        absorb(2 * k, 0)
        scores(2 * k + 2, 0)
        absorb(2 * k + 1, 1)
        return carry

    scores(0, 0)
    lax.fori_loop(0, n_chunks // 2, pair, 0)

    @pl.when(n_chunks % 2 == 1)
    def _():
        absorb(n_chunks - 1, 0)

    gates = _sigmoid(gate_ref[...])
    for g in range(NSA_G):
        rows = slice(g * BLK, (g + 1) * BLK)
        cols = slice(g * HEAD_DIM, (g + 1) * HEAD_DIM)
        o_sel = acc_ref[rows, :HEAD_DIM] / acc_ref[rows, HEAD_DIM:]
        mixed = (gates[:, 3 * g:3 * g + 1] * ocmp_ref[:, cols].astype(F32)
                 + gates[:, 3 * g + 1:3 * g + 2] * o_sel
                 + gates[:, 3 * g + 2:3 * g + 3] * owin_ref[:, cols].astype(F32))
        o_ref[:, cols] = mixed.astype(o_ref.dtype)


def sel_attention_mix(z, pen, bias_sel, gate_logits, o_cmp, o_win, t):
    n_sel = t // NSA_SEL_LEN
    n_gate = 3 * NSA_G
    gates_kv = gate_logits[:, :NSA_KV_HEADS * n_gate].reshape(t, NSA_KV_HEADS, n_gate).transpose(1, 0, 2)
    gates_kv = jnp.pad(gates_kv, ((0, 0), (0, 0), (0, LANE - n_gate)))
    assert t % SEL_CHUNK == 0
    onehot = jnp.asarray(np.arange(t)[:, None] // NSA_SEL_LEN == np.arange(n_sel)[None, :], BF16)
    kv_cols = lambda blk: z[:, blk * LANE:(blk + NSA_KV_HEADS) * LANE].reshape(t, NSA_KV_HEADS, HEAD_DIM)
    side = lambda a: jnp.broadcast_to(a[:, None, :], (t, NSA_KV_HEADS, a.shape[1]))
    kaug = jnp.concatenate([kv_cols(AB_K_S), side(onehot)], axis=2).transpose(1, 0, 2)
    vaug = jnp.concatenate([kv_cols(AB_V_S), side(jnp.ones((t, HEAD_DIM), BF16))], axis=2).transpose(1, 0, 2)
    qw = NSA_G * HEAD_DIM
    rows = NSA_G * BLK
    return pl.pallas_call(
        _sel_kernel,
        out_shape=jax.ShapeDtypeStruct((t, NSA_HEADS * HEAD_DIM), BF16),
        grid=(NSA_KV_HEADS, t // BLK),
        in_specs=[pl.BlockSpec((BLK, qw), lambda h, i: (i, AB_Q_N // NSA_G + h)),
                  pl.BlockSpec((None, BLK, n_sel), lambda h, i: (h, i, 0)),
                  pl.BlockSpec((None, t, HEAD_DIM + n_sel), lambda h, i: (h, 0, 0)),
                  pl.BlockSpec((None, t, 2 * HEAD_DIM), lambda h, i: (h, 0, 0)),
                  pl.BlockSpec((SEL_TILES, NSA_G, BLK, BLK), lambda h, i: (0, h, 0, 0)),
                  pl.BlockSpec((None, BLK, LANE), lambda h, i: (h, i, 0)),
                  pl.BlockSpec((BLK, qw), lambda h, i: (i, h)),
                  pl.BlockSpec((BLK, qw), lambda h, i: (i, h))],
        out_specs=pl.BlockSpec((BLK, qw), lambda h, i: (i, h)),
        scratch_shapes=[pltpu.VMEM((rows, HEAD_DIM + n_sel), BF16), pltpu.VMEM((2, rows, SEL_CHUNK), F32),
                        pltpu.VMEM((2, rows, LANE), F32), pltpu.VMEM((rows, LANE), F32),
                        pltpu.VMEM((rows, 2 * HEAD_DIM), F32)],
        compiler_params=_cparams(("parallel", "arbitrary")),
        name="sel_attention",
    )(z, pen, kaug, vaug, bias_sel, gates_kv, o_cmp, o_win)


def _banded_body(q_ref, k_ref, v_ref, bias_ref, o_ref, qs_ref, sink_of, qb, *, group, n_prev, window, scale_q):
    if scale_q:
        _stack_scaled_queries(q_ref, qs_ref, group)
    qs = qs_ref[...]
    row = lax.broadcasted_iota(jnp.int32, (BLK, BLK), 0)
    col = lax.broadcasted_iota(jnp.int32, (BLK, BLK), 1)
    ones = jnp.ones((BLK, HEAD_DIM), BF16)
    tiles, vaugs = [], []
    for delta in range(n_prev + 1):
        kb = qb - delta
        start = pl.multiple_of(jnp.maximum(kb, 0) * BLK, BLK)
        dist = delta * BLK + row - col
        limit = jnp.where(kb >= 0, window, 0)
        valid = jnp.logical_and(dist >= 0, dist < limit)
        bias = jnp.concatenate([jnp.where(valid, bias_ref[delta, g], NEG_INF) for g in range(group)], axis=0)
        tiles.append(_dot_nt(qs, k_ref[pl.ds(start, BLK), :]) + bias)
        vaugs.append(jnp.concatenate([v_ref[pl.ds(start, BLK), :], ones], axis=1))
    m = jnp.max(functools.reduce(jnp.maximum, tiles), axis=-1, keepdims=True)
    if sink_of is not None:
        sink = jnp.concatenate([jnp.full((BLK, 1), sink_of(g), F32) for g in range(group)], axis=0)
        m = jnp.maximum(m, sink)
    acc = functools.reduce(jnp.add, [_dot(jnp.exp(tile - m).astype(BF16), vaug) for tile, vaug in zip(tiles, vaugs)])
    denom = acc[:, HEAD_DIM:]
    if sink_of is not None:
        denom = denom + jnp.exp(sink - m)
    out = acc[:, :HEAD_DIM] / denom
    for g in range(group):
        o_ref[:, g * HEAD_DIM:(g + 1) * HEAD_DIM] = out[g * BLK:(g + 1) * BLK, :].astype(o_ref.dtype)


def _stack_scaled_queries(q_ref, qs_ref, group):
    for g in range(group):
        qs_ref[g * BLK:(g + 1) * BLK, :] = (q_ref[:, g * HEAD_DIM:(g + 1) * HEAD_DIM].astype(F32) * SCALE).astype(BF16)


SWA_G = SWA_HEADS // SWA_KV_HEADS


def _sg_swa_kernel(sink_ref, u_ref, v_ref, lng_ref, lnb_ref, w_ref, b_ref, q_ref, k_ref, vv_ref, bias_ref,
                   o_sg_ref, o_swa_ref, qs_ref):
    u = _gelu_tanh(u_ref[...].astype(F32))
    gv = _gelu_tanh(v_ref[...].astype(F32))
    mu = jnp.mean(gv, axis=-1, keepdims=True)
    cen = gv - mu
    var = jnp.mean(cen * cen, axis=-1, keepdims=True)
    gv = (cen * lax.rsqrt(var + RMS_EPS) * lng_ref[...] + lnb_ref[...]).astype(BF16)
    row = lax.broadcasted_iota(jnp.int32, (SG_CHUNK, SG_CHUNK), 0)
    col = lax.broadcasted_iota(jnp.int32, (SG_CHUNK, SG_CHUNK), 1)
    gw = u.shape[1] // SG_GROUPS
    for g in range(SG_GROUPS):
        sl = slice(g * gw, (g + 1) * gw)
        w = jnp.where(row >= col, w_ref[g], 0.0).astype(BF16)
        spatial = _dot(w, gv[:, sl]) + b_ref[:, g:g + 1]
        o_sg_ref[:, sl] = (u[:, sl] * spatial).astype(o_sg_ref.dtype)

    qw = SWA_G * HEAD_DIM
    for kv in range(SWA_KV_HEADS):
        heads = pl.ds(kv * qw, qw)
        _banded_body(q_ref.at[:, heads], k_ref.at[:, pl.ds(kv * HEAD_DIM, HEAD_DIM)],
                     vv_ref.at[:, pl.ds(kv * HEAD_DIM, HEAD_DIM)], bias_ref.at[:, pl.ds(kv * SWA_G, SWA_G)],
                     o_swa_ref.at[:, heads], qs_ref, lambda g, kv=kv: sink_ref[kv * SWA_G + g], pl.program_id(0),
                     group=SWA_G, n_prev=-(-SWA_WINDOW // BLK), window=SWA_WINDOW, scale_q=True)


def spatial_gating_swa(z1, ln_g, ln_b, sg_w, sg_b, bias_raw, sinks, t):
    width = ln_g.shape[0]
    assert width % (SG_GROUPS * LANE) == 0 and SG_CHUNK == BLK and CD_Q * LANE == 2 * width
    n_tiles = -(-SWA_WINDOW // BLK) + 1
    qw = SWA_HEADS * HEAD_DIM
    kvw = SWA_KV_HEADS * HEAD_DIM
    row = lambda w, blk: pl.BlockSpec((BLK, w), lambda i: (i, blk))
    full = lambda shape: pl.BlockSpec(shape, lambda i: (0,) * len(shape))
    return pl.pallas_call(
        _sg_swa_kernel,
        out_shape=(jax.ShapeDtypeStruct((t, width), BF16), jax.ShapeDtypeStruct((t, qw), BF16)),
        grid=(t // BLK,),
        in_specs=[pl.BlockSpec(memory_space=pltpu.SMEM),
                  row(width, 0), row(width, 1), full((1, width)), full((1, width)),
                  full((SG_GROUPS, SG_CHUNK, SG_CHUNK)), full((SG_CHUNK, SG_GROUPS)),
                  row(qw, CD_Q * LANE // qw),
                  pl.BlockSpec((t, kvw), lambda i: (0, CD_K * LANE // kvw)),
                  pl.BlockSpec((t, kvw), lambda i: (0, CD_V * LANE // kvw)),
                  full((n_tiles, REL_HEADS, BLK, BLK))],
        out_specs=(row(width, 0), row(qw, 0)),
        scratch_shapes=[pltpu.VMEM((SWA_G * BLK, HEAD_DIM), BF16)],
        compiler_params=_cparams(("arbitrary",)),
        name="spatial_gating_swa",
    )(sinks, z1, z1, ln_g.reshape(1, width), ln_b.reshape(1, width), sg_w, jnp.transpose(sg_b), z1, z1, z1, bias_raw)


MOE_TM = 512
MOE_TOP_K = 2


def moe_plan(idx, gate, t):
    n_rows = MOE_TOP_K * t + N_EXPERTS * MOE_TM
    n_blocks = n_rows // MOE_TM
    pair_e = idx[:, :MOE_TOP_K].reshape(-1)
    pair_w = gate[:, :MOE_TOP_K].reshape(-1)
    onehot = (pair_e[:, None] == jnp.arange(N_EXPERTS, dtype=jnp.int32)[None, :]).astype(jnp.int32)
    before = jnp.cumsum(onehot, axis=0) - onehot
    count = jnp.sum(onehot, axis=0)
    padded = (count + MOE_TM - 1) // MOE_TM * MOE_TM
    group_end = jnp.cumsum(padded)
    group_start = group_end - padded
    pair_row = jnp.sum(onehot * (before + group_start[None, :]), axis=1)
    row_pair = jnp.full((n_rows,), -1, jnp.int32).at[pair_row].set(jnp.arange(MOE_TOP_K * t, dtype=jnp.int32))
    row_tok = jnp.maximum(row_pair, 0) // MOE_TOP_K
    row_w = jnp.where(row_pair >= 0, pair_w[jnp.maximum(row_pair, 0)], 0.0)
    block_start = jnp.arange(n_blocks, dtype=jnp.int32) * MOE_TM
    block_e = jnp.minimum(jnp.sum(block_start[:, None] >= group_end[None, :], axis=1), N_EXPERTS - 1).astype(jnp.int32)
    block_rows = (group_start + count)[block_e] - block_start
    block_half = (block_rows <= MOE_TM // 2).astype(jnp.int32)
    n_used = (group_end[-1] // MOE_TM).astype(jnp.int32).reshape(1)
    nonempty = count > 0
    group_e = jnp.argsort(jnp.logical_not(nonempty), stable=True).astype(jnp.int32)
    block_group = (jnp.cumsum(nonempty.astype(jnp.int32)) - 1)[block_e].astype(jnp.int32)
    n_groups = jnp.sum(nonempty.astype(jnp.int32)).reshape(1)
    blocks = (block_e, block_half, n_used, block_group, group_e, n_groups)
    return row_tok, row_w.reshape(n_rows, 1), blocks, pair_row.astype(jnp.int32)


MOE_GATHER_ROWS = 256
DMA_BURST = 8


def _gather_rows_kernel(tok_ref, tok_next_ref, src_ref, o_ref, buf, sem):
    b = pl.program_id(0)
    slot = b % 2

    def row_copy(idx_ref, k, s):
        return pltpu.make_async_copy(src_ref.at[idx_ref[k]], buf.at[s, k], sem.at[s])

    def start_block(idx_ref, s):
        def start(k8, carry):
            for u in range(DMA_BURST):
                row_copy(idx_ref, k8 * DMA_BURST + u, s).start(priority=u % 2)
            return carry
        lax.fori_loop(0, MOE_GATHER_ROWS // DMA_BURST, start, 0)

    @pl.when(b == 0)
    def _():
        start_block(tok_ref, 0)

    @pl.when(b + 1 < pl.num_programs(0))
    def _():
        start_block(tok_next_ref, 1 - slot)

    def wait(k, carry):
        row_copy(tok_ref, k, slot).wait()
        return carry

    lax.fori_loop(0, MOE_GATHER_ROWS, wait, 0, unroll=8)
    o_ref[...] = buf[slot].reshape(o_ref.shape)


def gather_rows(src, row_tok):
    n_rows = row_tok.shape[0]
    d = src.shape[1] * src.shape[2]
    n_steps = n_rows // MOE_GATHER_ROWS
    return pl.pallas_call(
        _gather_rows_kernel,
        out_shape=jax.ShapeDtypeStruct((n_rows, d), src.dtype),
        grid=(n_steps,),
        in_specs=[pl.BlockSpec((MOE_GATHER_ROWS,), lambda b: (b,), memory_space=pltpu.SMEM),
                  pl.BlockSpec((MOE_GATHER_ROWS,), lambda b: (jnp.minimum(b + 1, n_steps - 1),),
                               memory_space=pltpu.SMEM),
                  pl.BlockSpec(memory_space=pl.ANY)],
        out_specs=pl.BlockSpec((MOE_GATHER_ROWS, d), lambda b: (b, 0)),
        scratch_shapes=[pltpu.VMEM((2, MOE_GATHER_ROWS) + src.shape[1:], src.dtype), pltpu.SemaphoreType.DMA((2,))],
        compiler_params=_cparams(("arbitrary",)),
        name="moe_gather",
    )(row_tok, row_tok, src)


def _moe_rows(b, half_ref, n_used_ref, compute, o_ref):
    live = b < n_used_ref[0]
    half = half_ref[jnp.minimum(b, n_used_ref[0] - 1)] == 1
    lower, upper = slice(0, MOE_TM // 2), slice(MOE_TM // 2, MOE_TM)

    @pl.when(jnp.logical_and(live, jnp.logical_not(half)))
    def _():
        compute(slice(0, MOE_TM))

    @pl.when(jnp.logical_and(live, half))
    def _():
        compute(lower)
        o_ref[upper, :] = jnp.zeros((MOE_TM // 2, o_ref.shape[1]), o_ref.dtype)

    @pl.when(jnp.logical_not(live))
    def _():
        o_ref[...] = jnp.zeros(o_ref.shape, o_ref.dtype)


def _moe_panels(tables, w_refs, wbf_refs, stage_refs, sem, tn):
    block_e_ref, _, n_used_ref, block_group_ref, group_e_ref, n_groups_ref = tables
    j, b = pl.program_id(0), pl.program_id(1)
    n_groups = n_groups_ref[0]
    group = block_group_ref[b]
    first_block = jnp.logical_or(b == 0, group != block_group_ref[jnp.maximum(b - 1, 0)])

    def panel(widx, expert, jj):
        return _panel_copy(w_refs[widx], (0, expert), jj * tn, tn, False, stage_refs[widx], sem, widx)

    @pl.when(jnp.logical_and(b < n_used_ref[0], first_block))
    def _():
        number = j * n_groups + group

        @pl.when(number == 0)
        def _():
            for widx in range(len(w_refs)):
                panel(widx, block_e_ref[b], 0).start()

        for widx in range(len(w_refs)):
            panel(widx, block_e_ref[b], j).wait()
            wbf_refs[widx][...] = stage_refs[widx][...].astype(BF16)

        @pl.when(number + 1 < pl.num_programs(0) * n_groups)
        def _():
            wraps = group + 1 == n_groups
            next_e = group_e_ref[jnp.where(wraps, 0, group + 1)]
            next_j = jnp.where(wraps, j + 1, j)
            for widx in range(len(w_refs)):
                panel(widx, next_e, next_j).start()


def _moe_up_kernel(*refs, tn):
    tables, (x_ref, wg_ref, wu_ref, o_ref, wg_bf, wu_bf, wg_st, wu_st, sem) = refs[:6], refs[6:]
    _moe_panels(tables, (wg_ref, wu_ref), (wg_bf, wu_bf), (wg_st, wu_st), sem, tn)

    def compute(rows):
        x = x_ref[rows, :]
        gate = _dot(x, wg_bf[...])
        o_ref[rows, :] = ((gate * _sigmoid(gate)) * _dot(x, wu_bf[...])).astype(o_ref.dtype)

    _moe_rows(pl.program_id(1), tables[1], tables[2], compute, o_ref)


def _moe_down_kernel(*refs, tn):
    tables, (h_ref, wd_ref, w_ref, o_ref, wd_bf, wd_st, sem) = refs[:6], refs[6:]
    _moe_panels(tables, (wd_ref,), (wd_bf,), (wd_st,), sem, tn)

    def compute(rows):
        o_ref[rows, :] = w_ref[rows, :] * _dot(h_ref[rows, :], wd_bf[...])

    _moe_rows(pl.program_id(1), tables[1], tables[2], compute, o_ref)


def moe_experts(xs, row_w, blocks, w_gate, w_up, w_down):
    n_rows, d = xs.shape
    d_ff = w_gate.shape[3]
    n_blocks = n_rows // MOE_TM
    tn_up, tn_down = 512, 1024
    row_block = lambda j, b, be, bh, nu, bg, ge, ng: (jnp.minimum(b, nu[0] - 1), 0)
    out_block = lambda j, b, be, bh, nu, bg, ge, ng: (b, j)
    in_hbm = pl.BlockSpec(memory_space=pl.ANY)
    hid = pl.pallas_call(
        functools.partial(_moe_up_kernel, tn=tn_up),
        out_shape=jax.ShapeDtypeStruct((n_rows, d_ff), BF16),
        grid_spec=pltpu.PrefetchScalarGridSpec(
            num_scalar_prefetch=len(blocks),
            grid=(d_ff // tn_up, n_blocks),
            in_specs=[pl.BlockSpec((MOE_TM, d), row_block), in_hbm, in_hbm],
            out_specs=pl.BlockSpec((MOE_TM, tn_up), out_block),
            scratch_shapes=[pltpu.VMEM((d, tn_up), BF16), pltpu.VMEM((d, tn_up), BF16),
                            pltpu.VMEM((d, tn_up), F32), pltpu.VMEM((d, tn_up), F32),
                            pltpu.SemaphoreType.DMA((2,))]),
        compiler_params=_cparams(("arbitrary", "arbitrary")),
        name="moe_up",
    )(*blocks, xs, w_gate, w_up)
    return pl.pallas_call(
        functools.partial(_moe_down_kernel, tn=tn_down),
        out_shape=jax.ShapeDtypeStruct((n_rows, d), F32),
        grid_spec=pltpu.PrefetchScalarGridSpec(
            num_scalar_prefetch=len(blocks),
            grid=(d // tn_down, n_blocks),
            in_specs=[pl.BlockSpec((MOE_TM, d_ff), row_block), in_hbm, pl.BlockSpec((MOE_TM, 1), row_block)],
            out_specs=pl.BlockSpec((MOE_TM, tn_down), out_block),
            scratch_shapes=[pltpu.VMEM((d_ff, tn_down), BF16), pltpu.VMEM((d_ff, tn_down), F32),
                            pltpu.SemaphoreType.DMA((1,))]),
        compiler_params=_cparams(("arbitrary", "arbitrary")),
        name="moe_down",
    )(*blocks, hid, w_down, row_w)


MOE_TC = 128


def _moe_combine_kernel(rows_ref, rows_next_ref, x_ref, g_ref, y_ref, o_ref, buf, sem):
    b = pl.program_id(0)
    slot = b % 2

    def row_copy(idx_ref, k, choice, s):
        row = idx_ref[MOE_TOP_K * k + choice]
        return pltpu.make_async_copy(y_ref.at[pl.ds(row, 1), :], buf.at[s, choice, pl.ds(k, 1), :], sem.at[s])

    def start_block(idx_ref, s):
        def start(k4, carry):
            for u in range(DMA_BURST // MOE_TOP_K):
                for choice in range(MOE_TOP_K):
                    row_copy(idx_ref, k4 * (DMA_BURST // MOE_TOP_K) + u, choice, s).start(priority=choice % 2)
            return carry
        lax.fori_loop(0, MOE_TC * MOE_TOP_K // DMA_BURST, start, 0)

    @pl.when(b == 0)
    def _():
        start_block(rows_ref, 0)

    @pl.when(b + 1 < pl.num_programs(0))
    def _():
        start_block(rows_next_ref, 1 - slot)

    def wait(k, carry):
        for choice in range(MOE_TOP_K):
            row_copy(rows_ref, k, choice, slot).wait()
        return carry

    lax.fori_loop(0, MOE_TC, wait, 0, unroll=4)
    x = x_ref[...]
    for choice in range(MOE_TOP_K):
        x = x + buf[slot, choice]
    y = x * lax.rsqrt(jnp.mean(x * x, axis=-1, keepdims=True) + RMS_EPS)
    o_ref[...] = (y * g_ref[...]).astype(o_ref.dtype)


def moe_combine_rmsnorm(x, ys, pair_row, g):
    t, d = x.shape
    n_steps = t // MOE_TC
    idx_block = lambda index_map: pl.BlockSpec((MOE_TOP_K * MOE_TC,), index_map, memory_space=pltpu.SMEM)
    return pl.pallas_call(
        _moe_combine_kernel,
        out_shape=jax.ShapeDtypeStruct((t, d), F32),
        grid=(n_steps,),
        in_specs=[idx_block(lambda i: (i,)),
                  idx_block(lambda i: (jnp.minimum(i + 1, n_steps - 1),)),
                  pl.BlockSpec((MOE_TC, d), lambda i: (i, 0)),
                  pl.BlockSpec((1, d), lambda i: (0, 0)),
                  pl.BlockSpec(memory_space=pl.ANY)],
        out_specs=pl.BlockSpec((MOE_TC, d), lambda i: (i, 0)),
        scratch_shapes=[pltpu.VMEM((2, MOE_TOP_K, MOE_TC, d), F32), pltpu.SemaphoreType.DMA((2,))],
        compiler_params=_cparams(("arbitrary",)),
        name="moe_combine",
    )(pair_row, pair_row, x, g.reshape(1, d), ys)


def kernel(x, rel_table, norm_mix, norm_ffn, norm_final, ab_w_in, ab_w_out, cmp_k_pe, cmp_k_w1, cmp_k_b1, cmp_k_w2, cmp_v_pe, cmp_v_w1, cmp_v_b1, cmp_v_w2, ffn_w_gate, ffn_w_up, ffn_w_down, cd_w_in, cd_w_out, sg_ln_g, sg_ln_b, sg_w, sg_b, swa_sinks, moe_w_router, moe_w_gate, moe_w_up, moe_w_down):
    b, t, d = x.shape
    assert b == 1
    xs = x.reshape(t, d)
    bias_raw, bias_sel, bias_cmp = bias_tiles(rel_table)

    h = rmsnorm(xs, norm_mix[0], BF16)
    ab_main = AB_MAIN_BLOCKS * LANE
    w_in_t = jnp.swapaxes(ab_w_in, 1, 2)
    z = matmul(h, [w_in_t], (0,), ab_main, out_dtype=BF16, w_transposed=True)
    w_gate_rows = jnp.pad(w_in_t[0, ab_main:], ((0, LANE - (w_in_t.shape[1] - ab_main)), (0, 0)))
    gate_logits = matmul(h, [w_gate_rows], (), LANE, out_dtype=F32, w_transposed=True)
    o_sb = sb_attention(z, t)
    kc = compress(z, AB_K_C, cmp_k_pe[0], cmp_k_w1[0], cmp_k_b1[0], cmp_k_w2[0], t)
    vc = compress(z, AB_V_C, cmp_v_pe[0], cmp_v_w1[0], cmp_v_b1[0], cmp_v_w2[0], t)
    o_cmp, pen, o_win = cmp_select_window(z, kc, vc, bias_cmp, bias_raw, t)
    o_nsa = sel_attention_mix(z, pen, bias_sel, gate_logits, o_cmp, o_win, t)
    xs = matmul([o_sb, o_nsa], [ab_w_out], (0,), d, out_dtype=F32, res=xs)
    h = rmsnorm(xs, norm_ffn[0], BF16)
    hid = matmul(h, [ffn_w_gate, ffn_w_up], (0,), ffn_w_gate.shape[2], out_dtype=BF16, act="swiglu")
    xs = matmul(hid, [ffn_w_down], (0,), d, out_dtype=F32, res=xs)

    h = rmsnorm(xs, norm_mix[1], BF16)
    z1 = matmul(h, [cd_w_in], (0,), cd_w_in.shape[2], out_dtype=BF16)
    o_sg, o_swa = spatial_gating_swa(z1, sg_ln_g[0], sg_ln_b[0], sg_w[0], sg_b[0], bias_raw, swa_sinks[0], t)
    xs = matmul([o_sg, o_swa], [cd_w_out], (0,), d, out_dtype=F32, res=xs)
    h, gate, idx = rmsnorm_router(xs, norm_ffn[1], moe_w_router[0])
    row_tok, row_w, blocks, pair_row = moe_plan(idx, gate, t)
    rows = gather_rows(h, row_tok)
    ys = moe_experts(rows, row_w, blocks, moe_w_gate, moe_w_up, moe_w_down)
    return moe_combine_rmsnorm(xs, ys, pair_row, norm_final).reshape(b, t, d)
```

```python
import functools
import math

import numpy as np
import jax
import jax.numpy as jnp
from jax import lax
from jax.experimental import pallas as pl
from jax.experimental.pallas import tpu as pltpu

F32 = jnp.float32
BF16 = jnp.bfloat16

LANE = 128
VMEM_LIMIT = 56 * 1024 * 1024

HEAD_DIM = 128
BLK = 128
NEG_INF = -1e30
RMS_EPS = 1e-6
SCALE = HEAD_DIM ** -0.5

SB_HEADS = 16
NSA_HEADS = 16
NSA_KV_HEADS = 4
NSA_G = NSA_HEADS // NSA_KV_HEADS
NSA_CMP_LEN = 32
NSA_CMP_STRIDE = 16
NSA_SEL_LEN = 64
NSA_SEL_TOPK = 16
NSA_WINDOW = 512
NSA_FORCE_SCORE = 1e6
SG_GROUPS = 8
SG_CHUNK = 128
SWA_HEADS = 16
SWA_KV_HEADS = 2
SWA_WINDOW = 128
REL_BUCKETS = 32
REL_MAX_DIST = 1024
REL_HEADS = 16
N_EXPERTS = 8

AB_Q_SB, AB_K_SB, AB_V_SB = 0, 16, 32
AB_Q_N = 48
AB_K_C, AB_V_C, AB_K_S, AB_V_S, AB_K_W, AB_V_W = 64, 68, 72, 76, 80, 84
AB_MAIN_BLOCKS = 88
CD_Q, CD_K, CD_V = 32, 48, 50

SEL_CHUNK_BLOCKS = 4
SEL_FAR_BLOCKS = 8
SEL_NEG_BLOCKS = SEL_CHUNK_BLOCKS - 1
SEL_TILES = SEL_NEG_BLOCKS + SEL_FAR_BLOCKS + SEL_CHUNK_BLOCKS - 1
RAW_TILES = NSA_WINDOW // BLK + 1
CMP_PAD = 120
SB_SKIP_LOG = -110.0
M_INIT = -5e29


def _cparams(sem):
    return pltpu.CompilerParams(dimension_semantics=sem, vmem_limit_bytes=VMEM_LIMIT)


def _dot(a, b):
    return jnp.dot(a, b, preferred_element_type=F32)


def _dot_nt(a, b):
    return lax.dot_general(a, b, (((1,), (1,)), ((), ())), preferred_element_type=F32)


def _dot_split(a, b):
    a_hi = a.astype(BF16)
    b_hi = b.astype(BF16)
    a_lo = (a - a_hi.astype(F32)).astype(BF16)
    b_lo = (b - b_hi.astype(F32)).astype(BF16)
    return _dot(a_hi, b_hi) + (_dot(a_hi, b_lo) + _dot(a_lo, b_hi))


def _sigmoid(x):
    return 1.0 / (1.0 + jnp.exp(-x))


def _gelu_tanh(x):
    return 0.5 * x * (1.0 + jnp.tanh(math.sqrt(2.0 / math.pi) * (x + 0.044715 * (x * x * x))))


def _rmsnorm_kernel(x_ref, g_ref, o_ref):
    x = x_ref[...]
    y = x * lax.rsqrt(jnp.mean(x * x, axis=-1, keepdims=True) + RMS_EPS)
    o_ref[...] = (y * g_ref[...]).astype(o_ref.dtype)


def rmsnorm(x, g, out_dtype):
    t, d = x.shape
    tm = 256
    return pl.pallas_call(
        _rmsnorm_kernel,
        out_shape=jax.ShapeDtypeStruct((t, d), out_dtype),
        grid=(t // tm,),
        in_specs=[pl.BlockSpec((tm, d), lambda i: (i, 0)), pl.BlockSpec((1, d), lambda i: (0, 0))],
        out_specs=pl.BlockSpec((tm, d), lambda i: (i, 0)),
        compiler_params=_cparams(("parallel",)),
        name="rmsnorm",
    )(x, g.reshape(1, d))


def _rmsnorm_router_kernel(x_ref, g_ref, wr_ref, o_ref, gate_ref, idx_ref):
    x = x_ref[...]
    y = x * lax.rsqrt(jnp.mean(x * x, axis=-1, keepdims=True) + RMS_EPS)
    y = y * g_ref[...]
    o_ref[...] = y.astype(o_ref.dtype).reshape(o_ref.shape)
    logits = _dot_split(y, wr_ref[...])
    lane = lax.broadcasted_iota(jnp.int32, logits.shape, 1)
    logits = jnp.where(lane < N_EXPERTS, logits, NEG_INF)
    m1 = jnp.max(logits, axis=-1, keepdims=True)
    i1 = jnp.min(jnp.where(logits == m1, lane, LANE), axis=-1, keepdims=True)
    rest = jnp.where(lane == i1, NEG_INF, logits)
    m2 = jnp.max(rest, axis=-1, keepdims=True)
    i2 = jnp.min(jnp.where(rest == m2, lane, LANE), axis=-1, keepdims=True)
    e2 = jnp.exp(m2 - m1)
    denom = 1.0 + e2
    gate_ref[...] = jnp.where(lane == 0, 1.0 / denom, jnp.where(lane == 1, e2 / denom, 0.0))
    idx_ref[...] = jnp.where(lane == 0, i1, jnp.where(lane == 1, i2, 0))


def rmsnorm_router(x, g, w_router):
    t, d = x.shape
    tm = 256
    wr = jnp.pad(w_router, ((0, 0), (0, LANE - w_router.shape[1])))
    return pl.pallas_call(
        _rmsnorm_router_kernel,
        out_shape=(jax.ShapeDtypeStruct((t, d // LANE, LANE), BF16), jax.ShapeDtypeStruct((t, LANE), F32),
                   jax.ShapeDtypeStruct((t, LANE), jnp.int32)),
        grid=(t // tm,),
        in_specs=[pl.BlockSpec((tm, d), lambda i: (i, 0)), pl.BlockSpec((1, d), lambda i: (0, 0)),
                  pl.BlockSpec((d, LANE), lambda i: (0, 0))],
        out_specs=(pl.BlockSpec((tm, d // LANE, LANE), lambda i: (i, 0, 0)), pl.BlockSpec((tm, LANE), lambda i: (i, 0)),
                   pl.BlockSpec((tm, LANE), lambda i: (i, 0))),
        compiler_params=_cparams(("parallel",)),
        name="rmsnorm_router",
    )(x, g.reshape(1, d), wr)


def _mm_epilogue(parts, res_ref, o_ref, act):
    r = parts[0]
    if act == "swiglu":
        r = (r * _sigmoid(r)) * parts[1]
    if res_ref is not None:
        r = res_ref[...] + r
    o_ref[...] = r.astype(o_ref.dtype)


def _panel_copy(w_ref, lead, col0, tn, transposed, stage_ref, sem_ref, widx):
    window = (pl.ds(col0, tn), slice(None)) if transposed else (slice(None), pl.ds(col0, tn))
    return pltpu.make_async_copy(w_ref.at[tuple(lead) + window], stage_ref, sem_ref.at[widx])


def _mm_resident_kernel(*refs, n_a, n_w, has_res, act, w_transposed, lead, tn):
    a_refs = refs[:n_a]
    w_refs = refs[n_a:n_a + n_w]
    res_ref = refs[n_a + n_w] if has_res else None
    o_ref = refs[n_a + n_w + int(has_res)]
    scratch = refs[n_a + n_w + int(has_res) + 1:]
    wbf_refs, stage_refs, sem = scratch[:n_w], scratch[n_w:2 * n_w], scratch[2 * n_w]
    j = pl.program_id(0)

    def panel(widx, jj):
        return _panel_copy(w_refs[widx], lead, jj * tn, tn, w_transposed, stage_refs[widx], sem, widx)

    @pl.when(pl.program_id(1) == 0)
    def _():
        @pl.when(j == 0)
        def _():
            for widx in range(n_w):
                panel(widx, 0).start()

        for widx in range(n_w):
            panel(widx, j).wait()
            wbf_refs[widx][...] = stage_refs[widx][...].astype(BF16)

        @pl.when(j + 1 < pl.num_programs(0))
        def _():
            for widx in range(n_w):
                panel(widx, j + 1).start()

    parts = []
    for wbf in wbf_refs:
        k0, part = 0, None
        for a_ref in a_refs:
            k1 = k0 + a_ref.shape[1]
            term = _dot_nt(a_ref[...], wbf[:, k0:k1]) if w_transposed else _dot(a_ref[...], wbf[k0:k1, :])
            part = term if part is None else part + term
            k0 = k1
        parts.append(part)
    _mm_epilogue(parts, res_ref, o_ref, act)


def _mm_ktiled_kernel(a_ref, w_ref, res_ref, o_ref, acc_ref, *, nk):
    k = pl.program_id(2)
    part = _dot(a_ref[...], w_ref[...].astype(BF16))

    @pl.when(k == 0)
    def _():
        acc_ref[...] = part

    @pl.when(k > 0)
    def _():
        acc_ref[...] += part

    @pl.when(k == nk - 1)
    def _():
        _mm_epilogue([acc_ref[...]], res_ref, o_ref, None)


MM_RESIDENT_MAX_K = 4096
MM_VMEM_BUDGET = 50 * 1024 * 1024


def _resident_tiles(m, kdim, n_cols, n_w, out_bytes, has_res):
    for tn in (1024, 512, 256, LANE):
        for tm in (1024, 512):
            if n_cols % tn or m % tm:
                continue
            a_buf = 2 * tm * kdim * 2
            w_buf = n_w * (kdim * tn * 4 + kdim * tn * 2)
            o_buf = 2 * tm * tn * out_bytes + (2 * tm * tn * 4 if has_res else 0)
            if a_buf + w_buf + o_buf <= MM_VMEM_BUDGET:
                return tm, tn
    raise ValueError("no resident-panel tiling fits VMEM")


def matmul(a, ws, lead, n_cols, *, out_dtype, res=None, act=None, w_transposed=False):
    a_parts = list(a) if isinstance(a, (list, tuple)) else [a]
    m = a_parts[0].shape[0]
    kdim = sum(part.shape[1] for part in a_parts)
    lead = tuple(lead)
    squeeze = (None,) * len(lead)
    if kdim <= MM_RESIDENT_MAX_K:
        tm, tn = _resident_tiles(m, kdim, n_cols, len(ws), jnp.dtype(out_dtype).itemsize, res is not None)
        in_specs = [pl.BlockSpec((tm, part.shape[1]), lambda j, i: (i, 0)) for part in a_parts]
        panel = (tn, kdim) if w_transposed else (kdim, tn)
        in_specs += [pl.BlockSpec(memory_space=pl.ANY) for _ in ws]
        args = [*a_parts, *ws]
        if res is not None:
            in_specs.append(pl.BlockSpec((tm, tn), lambda j, i: (i, j)))
            args.append(res)
        return pl.pallas_call(
            functools.partial(_mm_resident_kernel, n_a=len(a_parts), n_w=len(ws), has_res=res is not None, act=act,
                              w_transposed=w_transposed, lead=lead, tn=tn),
            out_shape=jax.ShapeDtypeStruct((m, n_cols), out_dtype),
            grid=(n_cols // tn, m // tm),
            in_specs=in_specs,
            out_specs=pl.BlockSpec((tm, tn), lambda j, i: (i, j)),
            scratch_shapes=([pltpu.VMEM(panel, BF16) for _ in ws] + [pltpu.VMEM(panel, F32) for _ in ws]
                            + [pltpu.SemaphoreType.DMA((len(ws),))]),
            compiler_params=_cparams(("arbitrary", "arbitrary")),
            name="matmul_resident",
        )(*args)
    assert len(ws) == 1 and act is None and res is not None and len(a_parts) == 1 and not w_transposed
    a = a_parts[0]
    tm, tn, tk = 2048, 512, 2048
    assert m % tm == 0 and n_cols % tn == 0 and kdim % tk == 0
    nk = kdim // tk
    return pl.pallas_call(
        functools.partial(_mm_ktiled_kernel, nk=nk),
        out_shape=jax.ShapeDtypeStruct((m, n_cols), out_dtype),
        grid=(n_cols // tn, m // tm, nk),
        in_specs=[pl.BlockSpec((tm, tk), lambda j, i, k: (i, k)),
                  pl.BlockSpec(squeeze + (tk, tn), lambda j, i, k: lead + (k, j)),
                  pl.BlockSpec((tm, tn), lambda j, i, k: (i, j))],
        out_specs=pl.BlockSpec((tm, tn), lambda j, i, k: (i, j)),
        scratch_shapes=[pltpu.VMEM((tm, tn), F32)],
        compiler_params=_cparams(("parallel", "parallel", "arbitrary")),
        name="matmul_ktiled",
    )(a, ws[0], res)


def _bucket_thresholds():
    d = np.arange(0, 4 * REL_MAX_DIST)
    max_exact = REL_BUCKETS // 2
    log_ratio = np.log(np.maximum(d, 1).astype(np.float64) / max_exact) / math.log(REL_MAX_DIST / max_exact)
    far = np.minimum(max_exact + (log_ratio * (REL_BUCKETS - max_exact)).astype(np.int32), REL_BUCKETS - 1)
    bucket = np.where(d < max_exact, d, far)
    assert np.all(np.diff(bucket) >= 0) and bucket[-1] == REL_BUCKETS - 1
    return [int(np.argmax(bucket >= b)) for b in range(REL_BUCKETS)]


_BUCKET_THR = _bucket_thresholds()
REL_FAR_DIST = _BUCKET_THR[REL_BUCKETS - 1]
assert REL_FAR_DIST <= SEL_FAR_BLOCKS * BLK - (BLK - 1)
assert REL_FAR_DIST <= NSA_CMP_STRIDE * (CMP_PAD + 1) - (NSA_CMP_LEN - 1)


def _bias_kernel(tab_ref, raw_ref, sel_ref, cmp_ref):
    h = pl.program_id(0)
    row = lax.broadcasted_iota(jnp.int32, (BLK, BLK), 0)
    col = lax.broadcasted_iota(jnp.int32, (BLK, BLK), 1)

    def lookup(dist):
        dd = jnp.maximum(dist, 0)
        b = jnp.full(dist.shape, tab_ref[0, h], F32)
        for bk in range(1, REL_BUCKETS):
            b = jnp.where(dd >= _BUCKET_THR[bk], tab_ref[bk, h], b)
        return b

    far = tab_ref[REL_BUCKETS - 1, h]
    for d in range(RAW_TILES):
        raw_ref[d] = lookup(d * BLK + row - col)
    for step in range(SEL_TILES):
        dist = (step - SEL_NEG_BLOCKS) * BLK + row - col
        sel_ref[step] = jnp.where(dist < 0, NEG_INF, lookup(dist) - far)
    dist_c = row - NSA_CMP_STRIDE * (col - CMP_PAD) - (NSA_CMP_LEN - 1)
    cmp_ref[...] = lookup(dist_c) - far


def bias_tiles(rel_table):
    tiles = lambda n: pl.BlockSpec((n, None, BLK, BLK), lambda h: (0, h, 0, 0))
    return pl.pallas_call(
        _bias_kernel,
        out_shape=(jax.ShapeDtypeStruct((RAW_TILES, REL_HEADS, BLK, BLK), F32),
                   jax.ShapeDtypeStruct((SEL_TILES, REL_HEADS, BLK, BLK), F32),
                   jax.ShapeDtypeStruct((REL_HEADS, BLK, BLK), F32)),
        grid=(REL_HEADS,),
        in_specs=[pl.BlockSpec(memory_space=pltpu.SMEM)],
        out_specs=(tiles(RAW_TILES), tiles(SEL_TILES), pl.BlockSpec((None, BLK, BLK), lambda h: (h, 0, 0))),
        compiler_params=_cparams(("arbitrary",)),
        name="bias_tiles",
    )(rel_table)


SB_BQ = 256
SB_GROUP = 4


def _sb_kernel(q_ref, k_ref, v_ref, o_ref, c_ref, acc_ref):
    qb = pl.program_id(1)
    row = lax.broadcasted_iota(jnp.int32, (SB_BQ, SB_BQ), 0)
    col = lax.broadcasted_iota(jnp.int32, (SB_BQ, SB_BQ), 1)
    upper = (row > col).astype(BF16)

    def block(h, kb, past, live=None):
        cols = slice(h * HEAD_DIM, (h + 1) * HEAD_DIM)
        ks = pl.multiple_of(kb * SB_BQ, SB_BQ)
        z = _dot_nt(q_ref[:, cols], k_ref[pl.ds(ks, SB_BQ), cols]) * SCALE
        sp = jnp.maximum(z, 0.0) + jnp.log(1.0 + jnp.exp(-jnp.abs(z)))
        log_stay = -sp
        if past is not None:
            log_stay = jnp.where(past, log_stay, 0.0)
        if live is not None:
            log_stay = jnp.where(live, log_stay, 0.0)
        later = _dot(log_stay.astype(BF16), upper)
        tot = jnp.broadcast_to(jnp.sum(log_stay, axis=-1, keepdims=True), (SB_BQ, HEAD_DIM))
        if past is not None:
            w = jnp.where(past, jnp.exp((z - sp) + later), 0.0)
            c_new = tot
        else:
            carry = c_ref[:, cols]
            w = jnp.exp((z - sp) + later + jnp.concatenate([carry] * (SB_BQ // HEAD_DIM), axis=1))
            if live is not None:
                w = jnp.where(live, w, 0.0)
            c_new = carry + tot
        contrib = _dot(w.astype(BF16), v_ref[pl.ds(ks, SB_BQ), cols])
        if past is not None:
            acc_ref[:, cols] = contrib
        else:
            acc_ref[:, cols] += contrib
        c_ref[:, cols] = c_new
        return jnp.max(c_new)

    def sweep(kb, past, live=None):
        return functools.reduce(jnp.maximum, [block(h, kb, past, live) for h in range(SB_GROUP)])

    def cond(state):
        kb, c_max = state
        return jnp.logical_and(kb >= 0, c_max > SB_SKIP_LOG)

    def body(state):
        kb, _ = state
        return kb - 1, sweep(kb, None)

    sweep(qb, col < row)
    c_max = sweep(jnp.maximum(qb - 1, 0), None, qb > 0)
    lax.while_loop(cond, body, (qb - 2, c_max))
    o_ref[...] = acc_ref[...].astype(o_ref.dtype)


def sb_attention(z, t):
    width = SB_GROUP * HEAD_DIM
    assert SB_HEADS % SB_GROUP == 0 and AB_K_SB % SB_GROUP == 0 and AB_V_SB % SB_GROUP == 0
    return pl.pallas_call(
        _sb_kernel,
        out_shape=jax.ShapeDtypeStruct((t, SB_HEADS * HEAD_DIM), BF16),
        grid=(SB_HEADS // SB_GROUP, t // SB_BQ),
        in_specs=[pl.BlockSpec((SB_BQ, width), lambda h, i: (i, AB_Q_SB // SB_GROUP + h)),
                  pl.BlockSpec((t, width), lambda h, i: (0, AB_K_SB // SB_GROUP + h)),
                  pl.BlockSpec((t, width), lambda h, i: (0, AB_V_SB // SB_GROUP + h))],
        out_specs=pl.BlockSpec((SB_BQ, width), lambda h, i: (i, h)),
        scratch_shapes=[pltpu.VMEM((SB_BQ, width), F32), pltpu.VMEM((SB_BQ, width), F32)],
        compiler_params=_cparams(("parallel", "arbitrary")),
        name="sb_attention",
    )(z, z, z)


def _compress_kernel(a_ref, pe_ref, w1_ref, b1_ref, w2_ref, o_ref):
    n_chunk = a_ref.shape[0]
    half = NSA_CMP_STRIDE * HEAD_DIM
    chunks = a_ref[...].reshape(n_chunk, half).astype(F32)
    first = _dot((chunks + pe_ref[:, :half]).astype(BF16), w1_ref[:half, :].astype(BF16))
    second = _dot((chunks + pe_ref[:, half:]).astype(BF16), w1_ref[half:, :].astype(BF16))
    hid = _gelu_tanh(first + pltpu.roll(second, n_chunk - 1, 0) + b1_ref[...])
    out = _dot(hid.astype(BF16), w2_ref[...].astype(BF16))
    o_ref[0:CMP_PAD, :] = jnp.zeros((CMP_PAD, HEAD_DIM), o_ref.dtype)
    o_ref[CMP_PAD:CMP_PAD + n_chunk, :] = out.astype(o_ref.dtype)
    o_ref[CMP_PAD + n_chunk:, :] = jnp.zeros((o_ref.shape[0] - CMP_PAD - n_chunk, HEAD_DIM), o_ref.dtype)


def compress(z, col_block, pe, w1, b1, w2, t):
    assert NSA_CMP_LEN == 2 * NSA_CMP_STRIDE
    n_chunk = t // NSA_CMP_STRIDE
    width = NSA_CMP_LEN * HEAD_DIM
    hidden = w1.shape[-1]
    rows = CMP_PAD + n_chunk + 8
    return pl.pallas_call(
        _compress_kernel,
        out_shape=jax.ShapeDtypeStruct((NSA_KV_HEADS, rows, HEAD_DIM), BF16),
        grid=(NSA_KV_HEADS,),
        in_specs=[pl.BlockSpec((n_chunk, NSA_CMP_STRIDE, HEAD_DIM), lambda h: (0, 0, col_block + h)),
                  pl.BlockSpec((1, width), lambda h: (0, 0)),
                  pl.BlockSpec((width, hidden), lambda h: (0, 0)),
                  pl.BlockSpec((1, hidden), lambda h: (0, 0)),
                  pl.BlockSpec((hidden, HEAD_DIM), lambda h: (0, 0))],
        out_specs=pl.BlockSpec((None, rows, HEAD_DIM), lambda h: (h, 0, 0)),
        compiler_params=_cparams(("parallel",)),
        name="compress",
    )(z.reshape(n_chunk, NSA_CMP_STRIDE, z.shape[1]), pe.reshape(1, width), w1, b1.reshape(1, hidden), w2)


def _cmp_to_sel_padded(t):
    n_cmp = (t - NSA_CMP_LEN) // NSA_CMP_STRIDE + 1
    n_sel = t // NSA_SEL_LEN
    cmp_end = np.arange(n_cmp) * NSA_CMP_STRIDE + NSA_CMP_LEN - 1
    cmp_start = cmp_end - (NSA_CMP_LEN - 1)
    sel_start = np.arange(n_sel) * NSA_SEL_LEN
    overlap = (np.minimum(cmp_end[:, None], sel_start[None, :] + NSA_SEL_LEN - 1)
               - np.maximum(cmp_start[:, None], sel_start[None, :]) + 1)
    c2s = np.clip(overlap, 0, None).astype(np.float32) / NSA_CMP_LEN
    out = np.zeros((CMP_PAD + n_cmp + 9, n_sel), np.float32)
    out[CMP_PAD:CMP_PAD + n_cmp] = c2s
    return out


def _cmp_win_kernel(q_ref, kc_ref, vc_ref, c2s_ref, bias_ref, kw_ref, vw_ref, bias_win_ref, o_ref, pen_ref, ow_ref,
                    qs_ref, *, n_far):
    qb = pl.program_id(1)
    _stack_scaled_queries(q_ref, qs_ref, NSA_G)
    r0 = pl.multiple_of(qb * (BLK // NSA_CMP_STRIDE), 8)
    kc_far = kc_ref[0:n_far, :]
    kc_near = kc_ref[pl.ds(r0, BLK), :]
    vs_far = jnp.concatenate([vc_ref[0:n_far, :], c2s_ref[0:n_far, :]], axis=1)
    vs_near = jnp.concatenate([vc_ref[pl.ds(r0, BLK), :], c2s_ref[pl.ds(r0, BLK), :]], axis=1)
    r_far = lax.broadcasted_iota(jnp.int32, (BLK, n_far), 1)
    far_ok = jnp.logical_and(r_far >= CMP_PAD, r_far < r0)
    row = lax.broadcasted_iota(jnp.int32, (BLK, BLK), 0)
    col = lax.broadcasted_iota(jnp.int32, (BLK, BLK), 1)
    dist_c = row - NSA_CMP_STRIDE * (col - CMP_PAD) - (NSA_CMP_LEN - 1)
    near_ok = jnp.logical_and(dist_c >= 0, col + r0 >= CMP_PAD)

    far_mask = jnp.where(far_ok, 0.0, NEG_INF)

    n_sel = c2s_ref.shape[1]
    imp = jnp.zeros((BLK, n_sel), F32)
    for g in range(NSA_G):
        q = qs_ref[g * BLK:(g + 1) * BLK, :]
        s_far = _dot_nt(q, kc_far) + far_mask
        s_near = _dot_nt(q, kc_near) + jnp.where(near_ok, bias_ref[g], NEG_INF)
        m = jnp.maximum(jnp.maximum(jnp.max(s_far, axis=-1, keepdims=True), jnp.max(s_near, axis=-1, keepdims=True)),
                        M_INIT)
        e_far = jnp.exp(s_far - m)
        e_near = jnp.exp(s_near - m)
        l = jnp.sum(e_far, axis=-1, keepdims=True) + jnp.sum(e_near, axis=-1, keepdims=True)
        inv = 1.0 / jnp.where(l > 0.0, l, 1.0)
        res = (_dot(e_far.astype(BF16), vs_far) + _dot(e_near.astype(BF16), vs_near)) * inv
        o_ref[:, g * HEAD_DIM:(g + 1) * HEAD_DIM] = res[:, :HEAD_DIM].astype(o_ref.dtype)
        imp = imp + res[:, HEAD_DIM:]

    blk = lax.broadcasted_iota(jnp.int32, (n_sel, BLK), 0)
    t_pos = qb * BLK + lax.broadcasted_iota(jnp.int32, (n_sel, BLK), 1)
    cur = t_pos // NSA_SEL_LEN
    forced = jnp.logical_or(blk == 0, jnp.logical_or(blk == cur, blk == cur - 1))
    score = jnp.where(forced, NSA_FORCE_SCORE, jnp.where(blk <= cur, imp.T, -1.0))
    pen = jnp.full((n_sel, BLK), NEG_INF, F32)
    for _ in range(min(NSA_SEL_TOPK, n_sel)):
        top = jnp.max(score, axis=0, keepdims=True)
        first = jnp.min(jnp.where(score == top, blk, n_sel), axis=0, keepdims=True)
        hit = blk == first
        pen = jnp.where(hit, 0.0, pen)
        score = jnp.where(hit, -2.0, score)
    pen_ref[...] = pen.T.astype(pen_ref.dtype)

    _banded_body(q_ref, kw_ref, vw_ref, bias_win_ref, ow_ref, qs_ref, None, qb, group=NSA_G,
                 n_prev=NSA_WINDOW // BLK, window=NSA_WINDOW, scale_q=False)


def cmp_select_window(z, kc, vc, bias_cmp, bias_raw, t):
    assert NSA_WINDOW % BLK == 0
    n_sel = t // NSA_SEL_LEN
    assert n_sel == LANE, "selection mask rides in one 128-lane tile"
    c2s = jnp.asarray(_cmp_to_sel_padded(t), BF16)
    rows = kc.shape[1]
    assert c2s.shape[0] == rows
    n_far = t // NSA_CMP_STRIDE
    qw = NSA_G * HEAD_DIM
    return pl.pallas_call(
        functools.partial(_cmp_win_kernel, n_far=n_far),
        out_shape=(jax.ShapeDtypeStruct((t, NSA_HEADS * HEAD_DIM), BF16),
                   jax.ShapeDtypeStruct((NSA_KV_HEADS, t, n_sel), BF16),
                   jax.ShapeDtypeStruct((t, NSA_HEADS * HEAD_DIM), BF16)),
        grid=(NSA_KV_HEADS, t // BLK),
        in_specs=[pl.BlockSpec((BLK, qw), lambda h, i: (i, AB_Q_N // NSA_G + h)),
                  pl.BlockSpec((None, rows, HEAD_DIM), lambda h, i: (h, 0, 0)),
                  pl.BlockSpec((None, rows, HEAD_DIM), lambda h, i: (h, 0, 0)),
                  pl.BlockSpec((rows, n_sel), lambda h, i: (0, 0)),
                  pl.BlockSpec((NSA_G, BLK, BLK), lambda h, i: (h, 0, 0)),
                  pl.BlockSpec((t, HEAD_DIM), lambda h, i: (0, AB_K_W + h)),
                  pl.BlockSpec((t, HEAD_DIM), lambda h, i: (0, AB_V_W + h)),
                  pl.BlockSpec((RAW_TILES, NSA_G, BLK, BLK), lambda h, i: (0, h, 0, 0))],
        out_specs=(pl.BlockSpec((BLK, qw), lambda h, i: (i, h)),
                   pl.BlockSpec((None, BLK, n_sel), lambda h, i: (h, i, 0)),
                   pl.BlockSpec((BLK, qw), lambda h, i: (i, h))),
        scratch_shapes=[pltpu.VMEM((NSA_G * BLK, HEAD_DIM), BF16)],
        compiler_params=_cparams(("parallel", "arbitrary")),
        name="cmp_select_window",
    )(z, kc, vc, c2s, bias_cmp, z, z, bias_raw)


SEL_CHUNK = SEL_CHUNK_BLOCKS * BLK


def _sel_kernel(q_ref, pen_ref, kaug_ref, vaug_ref, bias_ref, gate_ref, ocmp_ref, owin_ref, o_ref,
                qa_ref, s_ref, cmax_ref, m_ref, acc_ref):
    qb = pl.program_id(1)
    pen = pen_ref[...]
    for g in range(NSA_G):
        rows = slice(g * BLK, (g + 1) * BLK)
        qa_ref[rows, :HEAD_DIM] = (q_ref[:, g * HEAD_DIM:(g + 1) * HEAD_DIM].astype(F32) * SCALE).astype(BF16)
        qa_ref[rows, HEAD_DIM:] = pen
    m_ref[...] = jnp.full(m_ref.shape, M_INIT, F32)
    acc_ref[...] = jnp.zeros(acc_ref.shape, F32)
    n_chunks = qb // SEL_CHUNK_BLOCKS + 1
    last_chunk = kaug_ref.shape[0] // SEL_CHUNK - 1

    def chunk_start(c):
        return pl.multiple_of(jnp.minimum(c, last_chunk) * SEL_CHUNK, SEL_CHUNK)

    def scores(c, slot):
        s_all = _dot_nt(qa_ref[...], kaug_ref[pl.ds(chunk_start(c), SEL_CHUNK), :])
        first = qb - c * SEL_CHUNK_BLOCKS + SEL_NEG_BLOCKS
        n_rows = NSA_G * BLK
        tiles = [s_all[:, j * BLK:(j + 1) * BLK] + bias_ref[jnp.clip(first - j, 0, SEL_TILES - 1)].reshape(n_rows, BLK)
                 for j in range(SEL_CHUNK_BLOCKS)]
        for j, tile in enumerate(tiles):
            s_ref[slot, :, j * BLK:(j + 1) * BLK] = tile
        cmax_ref[slot] = jnp.broadcast_to(
            jnp.max(functools.reduce(jnp.maximum, tiles), axis=-1, keepdims=True), (n_rows, LANE))

    def absorb(c, slot):
        va = vaug_ref[pl.ds(chunk_start(c), SEL_CHUNK), :]
        m_old = m_ref[...]
        m_new = jnp.maximum(m_old, cmax_ref[slot])
        alpha = jnp.exp(m_old - m_new)
        p = jnp.concatenate([jnp.exp(s_ref[slot, :, j * BLK:(j + 1) * BLK] - m_new)
                             for j in range(SEL_CHUNK_BLOCKS)], axis=1).astype(BF16)
        acc_ref[...] = jnp.concatenate([alpha, alpha], axis=1) * acc_ref[...] + _dot(p, va)
        m_ref[...] = m_new

    def pair(k, carry):
        scores(2 * k + 1, 1)
        absorb(2 * k, 0)
        scores(2 * k + 2, 0)
        absorb(2 * k + 1, 1)
        return carry

    scores(0, 0)
    lax.fori_loop(0, n_chunks // 2, pair, 0)

    @pl.when(n_chunks % 2 == 1)
    def _():
        absorb(n_chunks - 1, 0)

    gates = _sigmoid(gate_ref[...])
    for g in range(NSA_G):
        rows = slice(g * BLK, (g + 1) * BLK)
        cols = slice(g * HEAD_DIM, (g + 1) * HEAD_DIM)
        o_sel = acc_ref[rows, :HEAD_DIM] / acc_ref[rows, HEAD_DIM:]
        mixed = (gates[:, 3 * g:3 * g + 1] * ocmp_ref[:, cols].astype(F32)
                 + gates[:, 3 * g + 1:3 * g + 2] * o_sel
                 + gates[:, 3 * g + 2:3 * g + 3] * owin_ref[:, cols].astype(F32))
        o_ref[:, cols] = mixed.astype(o_ref.dtype)


def sel_attention_mix(z, pen, bias_sel, gate_logits, o_cmp, o_win, t):
    n_sel = t // NSA_SEL_LEN
    n_gate = 3 * NSA_G
    gates_kv = gate_logits[:, :NSA_KV_HEADS * n_gate].reshape(t, NSA_KV_HEADS, n_gate).transpose(1, 0, 2)
    gates_kv = jnp.pad(gates_kv, ((0, 0), (0, 0), (0, LANE - n_gate)))
    assert t % SEL_CHUNK == 0
    onehot = jnp.asarray(np.arange(t)[:, None] // NSA_SEL_LEN == np.arange(n_sel)[None, :], BF16)
    kv_cols = lambda blk: z[:, blk * LANE:(blk + NSA_KV_HEADS) * LANE].reshape(t, NSA_KV_HEADS, HEAD_DIM)
    side = lambda a: jnp.broadcast_to(a[:, None, :], (t, NSA_KV_HEADS, a.shape[1]))
    kaug = jnp.concatenate([kv_cols(AB_K_S), side(onehot)], axis=2).transpose(1, 0, 2)
    vaug = jnp.concatenate([kv_cols(AB_V_S), side(jnp.ones((t, HEAD_DIM), BF16))], axis=2).transpose(1, 0, 2)
    qw = NSA_G * HEAD_DIM
    rows = NSA_G * BLK
    return pl.pallas_call(
        _sel_kernel,
        out_shape=jax.ShapeDtypeStruct((t, NSA_HEADS * HEAD_DIM), BF16),
        grid=(NSA_KV_HEADS, t // BLK),
        in_specs=[pl.BlockSpec((BLK, qw), lambda h, i: (i, AB_Q_N // NSA_G + h)),
                  pl.BlockSpec((None, BLK, n_sel), lambda h, i: (h, i, 0)),
                  pl.BlockSpec((None, t, HEAD_DIM + n_sel), lambda h, i: (h, 0, 0)),
                  pl.BlockSpec((None, t, 2 * HEAD_DIM), lambda h, i: (h, 0, 0)),
                  pl.BlockSpec((SEL_TILES, NSA_G, BLK, BLK), lambda h, i: (0, h, 0, 0)),
                  pl.BlockSpec((None, BLK, LANE), lambda h, i: (h, i, 0)),
                  pl.BlockSpec((BLK, qw), lambda h, i: (i, h)),
                  pl.BlockSpec((BLK, qw), lambda h, i: (i, h))],
        out_specs=pl.BlockSpec((BLK, qw), lambda h, i: (i, h)),
        scratch_shapes=[pltpu.VMEM((rows, HEAD_DIM + n_sel), BF16), pltpu.VMEM((2, rows, SEL_CHUNK), F32),
                        pltpu.VMEM((2, rows, LANE), F32), pltpu.VMEM((rows, LANE), F32),
                        pltpu.VMEM((rows, 2 * HEAD_DIM), F32)],
        compiler_params=_cparams(("parallel", "arbitrary")),
        name="sel_attention",
    )(z, pen, kaug, vaug, bias_sel, gates_kv, o_cmp, o_win)


def _banded_body(q_ref, k_ref, v_ref, bias_ref, o_ref, qs_ref, sink_of, qb, *, group, n_prev, window, scale_q):
    if scale_q:
        _stack_scaled_queries(q_ref, qs_ref, group)
    qs = qs_ref[...]
    row = lax.broadcasted_iota(jnp.int32, (BLK, BLK), 0)
    col = lax.broadcasted_iota(jnp.int32, (BLK, BLK), 1)
    ones = jnp.ones((BLK, HEAD_DIM), BF16)
    tiles, vaugs = [], []
    for delta in range(n_prev + 1):
        kb = qb - delta
        start = pl.multiple_of(jnp.maximum(kb, 0) * BLK, BLK)
        dist = delta * BLK + row - col
        limit = jnp.where(kb >= 0, window, 0)
        valid = jnp.logical_and(dist >= 0, dist < limit)
        bias = jnp.concatenate([jnp.where(valid, bias_ref[delta, g], NEG_INF) for g in range(group)], axis=0)
        tiles.append(_dot_nt(qs, k_ref[pl.ds(start, BLK), :]) + bias)
        vaugs.append(jnp.concatenate([v_ref[pl.ds(start, BLK), :], ones], axis=1))
    m = jnp.max(functools.reduce(jnp.maximum, tiles), axis=-1, keepdims=True)
    if sink_of is not None:
        sink = jnp.concatenate([jnp.full((BLK, 1), sink_of(g), F32) for g in range(group)], axis=0)
        m = jnp.maximum(m, sink)
    acc = functools.reduce(jnp.add, [_dot(jnp.exp(tile - m).astype(BF16), vaug) for tile, vaug in zip(tiles, vaugs)])
    denom = acc[:, HEAD_DIM:]
    if sink_of is not None:
        denom = denom + jnp.exp(sink - m)
    out = acc[:, :HEAD_DIM] / denom
    for g in range(group):
        o_ref[:, g * HEAD_DIM:(g + 1) * HEAD_DIM] = out[g * BLK:(g + 1) * BLK, :].astype(o_ref.dtype)


def _stack_scaled_queries(q_ref, qs_ref, group):
    for g in range(group):
        qs_ref[g * BLK:(g + 1) * BLK, :] = (q_ref[:, g * HEAD_DIM:(g + 1) * HEAD_DIM].astype(F32) * SCALE).astype(BF16)


SWA_G = SWA_HEADS // SWA_KV_HEADS


def _sg_swa_kernel(sink_ref, u_ref, v_ref, lng_ref, lnb_ref, w_ref, b_ref, q_ref, k_ref, vv_ref, bias_ref,
                   o_sg_ref, o_swa_ref, qs_ref):
    u = _gelu_tanh(u_ref[...].astype(F32))
    gv = _gelu_tanh(v_ref[...].astype(F32))
    mu = jnp.mean(gv, axis=-1, keepdims=True)
    cen = gv - mu
    var = jnp.mean(cen * cen, axis=-1, keepdims=True)
    gv = (cen * lax.rsqrt(var + RMS_EPS) * lng_ref[...] + lnb_ref[...]).astype(BF16)
    row = lax.broadcasted_iota(jnp.int32, (SG_CHUNK, SG_CHUNK), 0)
    col = lax.broadcasted_iota(jnp.int32, (SG_CHUNK, SG_CHUNK), 1)
    gw = u.shape[1] // SG_GROUPS
    for g in range(SG_GROUPS):
        sl = slice(g * gw, (g + 1) * gw)
        w = jnp.where(row >= col, w_ref[g], 0.0).astype(BF16)
        spatial = _dot(w, gv[:, sl]) + b_ref[:, g:g + 1]
        o_sg_ref[:, sl] = (u[:, sl] * spatial).astype(o_sg_ref.dtype)

    qw = SWA_G * HEAD_DIM
    for kv in range(SWA_KV_HEADS):
        heads = pl.ds(kv * qw, qw)
        _banded_body(q_ref.at[:, heads], k_ref.at[:, pl.ds(kv * HEAD_DIM, HEAD_DIM)],
                     vv_ref.at[:, pl.ds(kv * HEAD_DIM, HEAD_DIM)], bias_ref.at[:, pl.ds(kv * SWA_G, SWA_G)],
                     o_swa_ref.at[:, heads], qs_ref, lambda g, kv=kv: sink_ref[kv * SWA_G + g], pl.program_id(0),
                     group=SWA_G, n_prev=-(-SWA_WINDOW // BLK), window=SWA_WINDOW, scale_q=True)


def spatial_gating_swa(z1, ln_g, ln_b, sg_w, sg_b, bias_raw, sinks, t):
    width = ln_g.shape[0]
    assert width % (SG_GROUPS * LANE) == 0 and SG_CHUNK == BLK and CD_Q * LANE == 2 * width
    n_tiles = -(-SWA_WINDOW // BLK) + 1
    qw = SWA_HEADS * HEAD_DIM
    kvw = SWA_KV_HEADS * HEAD_DIM
    row = lambda w, blk: pl.BlockSpec((BLK, w), lambda i: (i, blk))
    full = lambda shape: pl.BlockSpec(shape, lambda i: (0,) * len(shape))
    return pl.pallas_call(
        _sg_swa_kernel,
        out_shape=(jax.ShapeDtypeStruct((t, width), BF16), jax.ShapeDtypeStruct((t, qw), BF16)),
        grid=(t // BLK,),
        in_specs=[pl.BlockSpec(memory_space=pltpu.SMEM),
                  row(width, 0), row(width, 1), full((1, width)), full((1, width)),
                  full((SG_GROUPS, SG_CHUNK, SG_CHUNK)), full((SG_CHUNK, SG_GROUPS)),
                  row(qw, CD_Q * LANE // qw),
                  pl.BlockSpec((t, kvw), lambda i: (0, CD_K * LANE // kvw)),
                  pl.BlockSpec((t, kvw), lambda i: (0, CD_V * LANE // kvw)),
                  full((n_tiles, REL_HEADS, BLK, BLK))],
        out_specs=(row(width, 0), row(qw, 0)),
        scratch_shapes=[pltpu.VMEM((SWA_G * BLK, HEAD_DIM), BF16)],
        compiler_params=_cparams(("arbitrary",)),
        name="spatial_gating_swa",
    )(sinks, z1, z1, ln_g.reshape(1, width), ln_b.reshape(1, width), sg_w, jnp.transpose(sg_b), z1, z1, z1, bias_raw)


MOE_TM = 512
MOE_TOP_K = 2


def moe_plan(idx, gate, t):
    n_rows = MOE_TOP_K * t + N_EXPERTS * MOE_TM
    n_blocks = n_rows // MOE_TM
    pair_e = idx[:, :MOE_TOP_K].reshape(-1)
    pair_w = gate[:, :MOE_TOP_K].reshape(-1)
    onehot = (pair_e[:, None] == jnp.arange(N_EXPERTS, dtype=jnp.int32)[None, :]).astype(jnp.int32)
    before = jnp.cumsum(onehot, axis=0) - onehot
    count = jnp.sum(onehot, axis=0)
    padded = (count + MOE_TM - 1) // MOE_TM * MOE_TM
    group_end = jnp.cumsum(padded)
    group_start = group_end - padded
    pair_row = jnp.sum(onehot * (before + group_start[None, :]), axis=1)
    row_pair = jnp.full((n_rows,), -1, jnp.int32).at[pair_row].set(jnp.arange(MOE_TOP_K * t, dtype=jnp.int32))
    row_tok = jnp.maximum(row_pair, 0) // MOE_TOP_K
    row_w = jnp.where(row_pair >= 0, pair_w[jnp.maximum(row_pair, 0)], 0.0)
    block_start = jnp.arange(n_blocks, dtype=jnp.int32) * MOE_TM
    block_e = jnp.minimum(jnp.sum(block_start[:, None] >= group_end[None, :], axis=1), N_EXPERTS - 1).astype(jnp.int32)
    block_rows = (group_start + count)[block_e] - block_start
    block_half = (block_rows <= MOE_TM // 2).astype(jnp.int32)
    n_used = (group_end[-1] // MOE_TM).astype(jnp.int32).reshape(1)
    nonempty = count > 0
    group_e = jnp.argsort(jnp.logical_not(nonempty), stable=True).astype(jnp.int32)
    block_group = (jnp.cumsum(nonempty.astype(jnp.int32)) - 1)[block_e].astype(jnp.int32)
    n_groups = jnp.sum(nonempty.astype(jnp.int32)).reshape(1)
    blocks = (block_e, block_half, n_used, block_group, group_e, n_groups)
    return row_tok, row_w.reshape(n_rows, 1), blocks, pair_row.astype(jnp.int32)


MOE_GATHER_ROWS = 256
DMA_BURST = 8


def _gather_rows_kernel(n_live_ref, tok_ref, tok_next_ref, src_ref, o_ref, buf, sem):
    b = pl.program_id(0)
    slot = b % 2
    n_live = n_live_ref[0]

    def row_copy(idx_ref, k, s):
        return pltpu.make_async_copy(src_ref.at[idx_ref[k]], buf.at[s, k], sem.at[s])

    def start_block(idx_ref, s):
        def start(k8, carry):
            for u in range(DMA_BURST):
                row_copy(idx_ref, k8 * DMA_BURST + u, s).start(priority=u % 2)
            return carry
        lax.fori_loop(0, MOE_GATHER_ROWS // DMA_BURST, start, 0)

    @pl.when(b == 0)
    def _():
        start_block(tok_ref, 0)

    @pl.when(b + 1 < n_live)
    def _():
        start_block(tok_next_ref, 1 - slot)

    @pl.when(b < n_live)
    def _():
        def wait(k, carry):
            row_copy(tok_ref, k, slot).wait()
            return carry

        lax.fori_loop(0, MOE_GATHER_ROWS, wait, 0, unroll=8)
        o_ref[...] = buf[slot].reshape(o_ref.shape)

    @pl.when(b >= n_live)
    def _():
        o_ref[...] = jnp.zeros(o_ref.shape, o_ref.dtype)


def gather_rows(src, row_tok, n_live_rows):
    n_rows = row_tok.shape[0]
    d = src.shape[1] * src.shape[2]
    n_steps = n_rows // MOE_GATHER_ROWS
    return pl.pallas_call(
        _gather_rows_kernel,
        out_shape=jax.ShapeDtypeStruct((n_rows, d), src.dtype),
        grid=(n_steps,),
        in_specs=[pl.BlockSpec(memory_space=pltpu.SMEM),
                  pl.BlockSpec((MOE_GATHER_ROWS,), lambda b: (b,), memory_space=pltpu.SMEM),
                  pl.BlockSpec((MOE_GATHER_ROWS,), lambda b: (jnp.minimum(b + 1, n_steps - 1),),
                               memory_space=pltpu.SMEM),
                  pl.BlockSpec(memory_space=pl.ANY)],
        out_specs=pl.BlockSpec((MOE_GATHER_ROWS, d), lambda b: (b, 0)),
        scratch_shapes=[pltpu.VMEM((2, MOE_GATHER_ROWS) + src.shape[1:], src.dtype), pltpu.SemaphoreType.DMA((2,))],
        compiler_params=_cparams(("arbitrary",)),
        name="moe_gather",
    )(n_live_rows // MOE_GATHER_ROWS, row_tok, row_tok, src)


def _moe_rows(b, half_ref, n_used_ref, compute, o_ref):
    live = b < n_used_ref[0]
    half = half_ref[jnp.minimum(b, n_used_ref[0] - 1)] == 1
    lower, upper = slice(0, MOE_TM // 2), slice(MOE_TM // 2, MOE_TM)

    @pl.when(jnp.logical_and(live, jnp.logical_not(half)))
    def _():
        compute(slice(0, MOE_TM))

    @pl.when(jnp.logical_and(live, half))
    def _():
        compute(lower)
        o_ref[upper, :] = jnp.zeros((MOE_TM // 2, o_ref.shape[1]), o_ref.dtype)

    @pl.when(jnp.logical_not(live))
    def _():
        o_ref[...] = jnp.zeros(o_ref.shape, o_ref.dtype)


def _moe_panels(tables, w_refs, wbf_refs, stage_refs, sem, tn):
    block_e_ref, _, n_used_ref, block_group_ref, group_e_ref, n_groups_ref = tables
    j, b = pl.program_id(0), pl.program_id(1)
    n_groups = n_groups_ref[0]
    group = block_group_ref[b]
    first_block = jnp.logical_or(b == 0, group != block_group_ref[jnp.maximum(b - 1, 0)])

    def panel(widx, expert, jj):
        return _panel_copy(w_refs[widx], (0, expert), jj * tn, tn, False, stage_refs[widx], sem, widx)

    @pl.when(jnp.logical_and(b < n_used_ref[0], first_block))
    def _():
        number = j * n_groups + group

        @pl.when(number == 0)
        def _():
            for widx in range(len(w_refs)):
                panel(widx, block_e_ref[b], 0).start()

        for widx in range(len(w_refs)):
            panel(widx, block_e_ref[b], j).wait()
            wbf_refs[widx][...] = stage_refs[widx][...].astype(BF16)

        @pl.when(number + 1 < pl.num_programs(0) * n_groups)
        def _():
            wraps = group + 1 == n_groups
            next_e = group_e_ref[jnp.where(wraps, 0, group + 1)]
            next_j = jnp.where(wraps, j + 1, j)
            for widx in range(len(w_refs)):
                panel(widx, next_e, next_j).start()


def _moe_up_kernel(*refs, tn):
    tables, (x_ref, wg_ref, wu_ref, o_ref, wg_bf, wu_bf, wg_st, wu_st, sem) = refs[:6], refs[6:]
    _moe_panels(tables, (wg_ref, wu_ref), (wg_bf, wu_bf), (wg_st, wu_st), sem, tn)

    def compute(rows):
        x = x_ref[rows, :]
        gate = _dot(x, wg_bf[...])
        o_ref[rows, :] = ((gate * _sigmoid(gate)) * _dot(x, wu_bf[...])).astype(o_ref.dtype)

    _moe_rows(pl.program_id(1), tables[1], tables[2], compute, o_ref)


def _moe_down_kernel(*refs, tn):
    tables, (h_ref, wd_ref, w_ref, o_ref, wd_bf, wd_st, sem) = refs[:6], refs[6:]
    _moe_panels(tables, (wd_ref,), (wd_bf,), (wd_st,), sem, tn)

    def compute(rows):
        o_ref[rows, :] = w_ref[rows, :] * _dot(h_ref[rows, :], wd_bf[...])

    _moe_rows(pl.program_id(1), tables[1], tables[2], compute, o_ref)


def moe_experts(xs, row_w, blocks, w_gate, w_up, w_down):
    n_rows, d = xs.shape
    d_ff = w_gate.shape[3]
    n_blocks = n_rows // MOE_TM
    tn_up, tn_down = 512, 1024
    row_block = lambda j, b, be, bh, nu, bg, ge, ng: (jnp.minimum(b, nu[0] - 1), 0)
    out_block = lambda j, b, be, bh, nu, bg, ge, ng: (b, j)
    in_hbm = pl.BlockSpec(memory_space=pl.ANY)
    hid = pl.pallas_call(
        functools.partial(_moe_up_kernel, tn=tn_up),
        out_shape=jax.ShapeDtypeStruct((n_rows, d_ff), BF16),
        grid_spec=pltpu.PrefetchScalarGridSpec(
            num_scalar_prefetch=len(blocks),
            grid=(d_ff // tn_up, n_blocks),
            in_specs=[pl.BlockSpec((MOE_TM, d), row_block), in_hbm, in_hbm],
            out_specs=pl.BlockSpec((MOE_TM, tn_up), out_block),
            scratch_shapes=[pltpu.VMEM((d, tn_up), BF16), pltpu.VMEM((d, tn_up), BF16),
                            pltpu.VMEM((d, tn_up), F32), pltpu.VMEM((d, tn_up), F32),
                            pltpu.SemaphoreType.DMA((2,))]),
        compiler_params=_cparams(("arbitrary", "arbitrary")),
        name="moe_up",
    )(*blocks, xs, w_gate, w_up)
    return pl.pallas_call(
        functools.partial(_moe_down_kernel, tn=tn_down),
        out_shape=jax.ShapeDtypeStruct((n_rows, d), F32),
        grid_spec=pltpu.PrefetchScalarGridSpec(
            num_scalar_prefetch=len(blocks),
            grid=(d // tn_down, n_blocks),
            in_specs=[pl.BlockSpec((MOE_TM, d_ff), row_block), in_hbm, pl.BlockSpec((MOE_TM, 1), row_block)],
            out_specs=pl.BlockSpec((MOE_TM, tn_down), out_block),
            scratch_shapes=[pltpu.VMEM((d_ff, tn_down), BF16), pltpu.VMEM((d_ff, tn_down), F32),
                            pltpu.SemaphoreType.DMA((1,))]),
        compiler_params=_cparams(("arbitrary", "arbitrary")),
        name="moe_down",
    )(*blocks, hid, w_down, row_w)


MOE_TC = 128


def _moe_combine_kernel(rows_ref, rows_next_ref, x_ref, g_ref, y_ref, o_ref, buf, sem):
    b = pl.program_id(0)
    slot = b % 2

    def row_copy(idx_ref, k, choice, s):
        row = idx_ref[MOE_TOP_K * k + choice]
        return pltpu.make_async_copy(y_ref.at[pl.ds(row, 1), :], buf.at[s, choice, pl.ds(k, 1), :], sem.at[s])

    def start_block(idx_ref, s):
        def start(k4, carry):
            for u in range(DMA_BURST // MOE_TOP_K):
                for choice in range(MOE_TOP_K):
                    row_copy(idx_ref, k4 * (DMA_BURST // MOE_TOP_K) + u, choice, s).start(priority=choice % 2)
            return carry
        lax.fori_loop(0, MOE_TC * MOE_TOP_K // DMA_BURST, start, 0)

    @pl.when(b == 0)
    def _():
        start_block(rows_ref, 0)

    @pl.when(b + 1 < pl.num_programs(0))
    def _():
        start_block(rows_next_ref, 1 - slot)

    def wait(k, carry):
        for choice in range(MOE_TOP_K):
            row_copy(rows_ref, k, choice, slot).wait()
        return carry

    lax.fori_loop(0, MOE_TC, wait, 0, unroll=4)
    x = x_ref[...]
    for choice in range(MOE_TOP_K):
        x = x + buf[slot, choice]
    y = x * lax.rsqrt(jnp.mean(x * x, axis=-1, keepdims=True) + RMS_EPS)
    o_ref[...] = (y * g_ref[...]).astype(o_ref.dtype)


def moe_combine_rmsnorm(x, ys, pair_row, g):
    t, d = x.shape
    n_steps = t // MOE_TC
    idx_block = lambda index_map: pl.BlockSpec((MOE_TOP_K * MOE_TC,), index_map, memory_space=pltpu.SMEM)
    return pl.pallas_call(
        _moe_combine_kernel,
        out_shape=jax.ShapeDtypeStruct((t, d), F32),
        grid=(n_steps,),
        in_specs=[idx_block(lambda i: (i,)),
                  idx_block(lambda i: (jnp.minimum(i + 1, n_steps - 1),)),
                  pl.BlockSpec((MOE_TC, d), lambda i: (i, 0)),
                  pl.BlockSpec((1, d), lambda i: (0, 0)),
                  pl.BlockSpec(memory_space=pl.ANY)],
        out_specs=pl.BlockSpec((MOE_TC, d), lambda i: (i, 0)),
        scratch_shapes=[pltpu.VMEM((2, MOE_TOP_K, MOE_TC, d), F32), pltpu.SemaphoreType.DMA((2,))],
        compiler_params=_cparams(("arbitrary",)),
        name="moe_combine",
    )(pair_row, pair_row, x, g.reshape(1, d), ys)


def kernel(x, rel_table, norm_mix, norm_ffn, norm_final, ab_w_in, ab_w_out, cmp_k_pe, cmp_k_w1, cmp_k_b1, cmp_k_w2, cmp_v_pe, cmp_v_w1, cmp_v_b1, cmp_v_w2, ffn_w_gate, ffn_w_up, ffn_w_down, cd_w_in, cd_w_out, sg_ln_g, sg_ln_b, sg_w, sg_b, swa_sinks, moe_w_router, moe_w_gate, moe_w_up, moe_w_down):
    b, t, d = x.shape
    assert b == 1
    xs = x.reshape(t, d)
    bias_raw, bias_sel, bias_cmp = bias_tiles(rel_table)

    h = rmsnorm(xs, norm_mix[0], BF16)
    ab_main = AB_MAIN_BLOCKS * LANE
    w_in_t = jnp.swapaxes(ab_w_in, 1, 2)
    z = matmul(h, [w_in_t], (0,), ab_main, out_dtype=BF16, w_transposed=True)
    w_gate_rows = jnp.pad(w_in_t[0, ab_main:], ((0, LANE - (w_in_t.shape[1] - ab_main)), (0, 0)))
    gate_logits = matmul(h, [w_gate_rows], (), LANE, out_dtype=F32, w_transposed=True)
    o_sb = sb_attention(z, t)
    kc = compress(z, AB_K_C, cmp_k_pe[0], cmp_k_w1[0], cmp_k_b1[0], cmp_k_w2[0], t)
    vc = compress(z, AB_V_C, cmp_v_pe[0], cmp_v_w1[0], cmp_v_b1[0], cmp_v_w2[0], t)
    o_cmp, pen, o_win = cmp_select_window(z, kc, vc, bias_cmp, bias_raw, t)
    o_nsa = sel_attention_mix(z, pen, bias_sel, gate_logits, o_cmp, o_win, t)
    xs = matmul([o_sb, o_nsa], [ab_w_out], (0,), d, out_dtype=F32, res=xs)
    h = rmsnorm(xs, norm_ffn[0], BF16)
    hid = matmul(h, [ffn_w_gate, ffn_w_up], (0,), ffn_w_gate.shape[2], out_dtype=BF16, act="swiglu")
    xs = matmul(hid, [ffn_w_down], (0,), d, out_dtype=F32, res=xs)

    h = rmsnorm(xs, norm_mix[1], BF16)
    z1 = matmul(h, [cd_w_in], (0,), cd_w_in.shape[2], out_dtype=BF16)
    o_sg, o_swa = spatial_gating_swa(z1, sg_ln_g[0], sg_ln_b[0], sg_w[0], sg_b[0], bias_raw, swa_sinks[0], t)
    xs = matmul([o_sg, o_swa], [cd_w_out], (0,), d, out_dtype=F32, res=xs)
    h, gate, idx = rmsnorm_router(xs, norm_ffn[1], moe_w_router[0])
    row_tok, row_w, blocks, pair_row = moe_plan(idx, gate, t)
    rows = gather_rows(h, row_tok, blocks[2] * MOE_TM)
    ys = moe_experts(rows, row_w, blocks, moe_w_gate, moe_w_up, moe_w_down)
    return moe_combine_rmsnorm(xs, ys, pair_row, norm_final).reshape(b, t, d)
```

```python
import functools
import math

import numpy as np
import jax
import jax.numpy as jnp
from jax import lax
from jax.experimental import pallas as pl
from jax.experimental.pallas import tpu as pltpu

F32 = jnp.float32
BF16 = jnp.bfloat16

LANE = 128
VMEM_LIMIT = 56 * 1024 * 1024

HEAD_DIM = 128
BLK = 128
NEG_INF = -1e30
RMS_EPS = 1e-6
SCALE = HEAD_DIM ** -0.5

SB_HEADS = 16
NSA_HEADS = 16
NSA_KV_HEADS = 4
NSA_G = NSA_HEADS // NSA_KV_HEADS
NSA_CMP_LEN = 32
NSA_CMP_STRIDE = 16
NSA_SEL_LEN = 64
NSA_SEL_TOPK = 16
NSA_WINDOW = 512
NSA_FORCE_SCORE = 1e6
SG_GROUPS = 8
SG_CHUNK = 128
SWA_HEADS = 16
SWA_KV_HEADS = 2
SWA_WINDOW = 128
REL_BUCKETS = 32
REL_MAX_DIST = 1024
REL_HEADS = 16
N_EXPERTS = 8

AB_Q_SB, AB_K_SB, AB_V_SB = 0, 16, 32
AB_Q_N = 48
AB_K_C, AB_V_C, AB_K_S, AB_V_S, AB_K_W, AB_V_W = 64, 68, 72, 76, 80, 84
AB_MAIN_BLOCKS = 88
CD_Q, CD_K, CD_V = 32, 48, 50

SEL_CHUNK_BLOCKS = 4
SEL_FAR_BLOCKS = 8
SEL_NEG_BLOCKS = SEL_CHUNK_BLOCKS - 1
SEL_TILES = SEL_NEG_BLOCKS + SEL_FAR_BLOCKS + SEL_CHUNK_BLOCKS - 1
RAW_TILES = NSA_WINDOW // BLK + 1
CMP_PAD = 120
SB_SKIP_LOG = -110.0
M_INIT = -5e29


def _cparams(sem):
    return pltpu.CompilerParams(dimension_semantics=sem, vmem_limit_bytes=VMEM_LIMIT)


def _dot(a, b):
    return jnp.dot(a, b, preferred_element_type=F32)


def _dot_nt(a, b):
    return lax.dot_general(a, b, (((1,), (1,)), ((), ())), preferred_element_type=F32)


def _dot_split(a, b):
    a_hi = a.astype(BF16)
    b_hi = b.astype(BF16)
    a_lo = (a - a_hi.astype(F32)).astype(BF16)
    b_lo = (b - b_hi.astype(F32)).astype(BF16)
    return _dot(a_hi, b_hi) + (_dot(a_hi, b_lo) + _dot(a_lo, b_hi))


def _sigmoid(x):
    return 1.0 / (1.0 + jnp.exp(-x))


def _gelu_tanh(x):
    return 0.5 * x * (1.0 + jnp.tanh(math.sqrt(2.0 / math.pi) * (x + 0.044715 * (x * x * x))))


def _rmsnorm_kernel(x_ref, g_ref, o_ref):
    x = x_ref[...]
    y = x * lax.rsqrt(jnp.mean(x * x, axis=-1, keepdims=True) + RMS_EPS)
    o_ref[...] = (y * g_ref[...]).astype(o_ref.dtype)


def rmsnorm(x, g, out_dtype):
    t, d = x.shape
    tm = 256
    return pl.pallas_call(
        _rmsnorm_kernel,
        out_shape=jax.ShapeDtypeStruct((t, d), out_dtype),
        grid=(t // tm,),
        in_specs=[pl.BlockSpec((tm, d), lambda i: (i, 0)), pl.BlockSpec((1, d), lambda i: (0, 0))],
        out_specs=pl.BlockSpec((tm, d), lambda i: (i, 0)),
        compiler_params=_cparams(("parallel",)),
        name="rmsnorm",
    )(x, g.reshape(1, d))


def _rmsnorm_router_kernel(x_ref, g_ref, wr_ref, o_ref, gate_ref, idx_ref):
    x = x_ref[...]
    y = x * lax.rsqrt(jnp.mean(x * x, axis=-1, keepdims=True) + RMS_EPS)
    y = y * g_ref[...]
    o_ref[...] = y.astype(o_ref.dtype).reshape(o_ref.shape)
    logits = _dot_split(y, wr_ref[...])
    lane = lax.broadcasted_iota(jnp.int32, logits.shape, 1)
    logits = jnp.where(lane < N_EXPERTS, logits, NEG_INF)
    m1 = jnp.max(logits, axis=-1, keepdims=True)
    i1 = jnp.min(jnp.where(logits == m1, lane, LANE), axis=-1, keepdims=True)
    rest = jnp.where(lane == i1, NEG_INF, logits)
    m2 = jnp.max(rest, axis=-1, keepdims=True)
    i2 = jnp.min(jnp.where(rest == m2, lane, LANE), axis=-1, keepdims=True)
    e2 = jnp.exp(m2 - m1)
    denom = 1.0 + e2
    gate_ref[...] = jnp.where(lane == 0, 1.0 / denom, jnp.where(lane == 1, e2 / denom, 0.0))
    idx_ref[...] = jnp.where(lane == 0, i1, jnp.where(lane == 1, i2, 0))


def rmsnorm_router(x, g, w_router):
    t, d = x.shape
    tm = 256
    wr = jnp.pad(w_router, ((0, 0), (0, LANE - w_router.shape[1])))
    return pl.pallas_call(
        _rmsnorm_router_kernel,
        out_shape=(jax.ShapeDtypeStruct((t, d // LANE, LANE), BF16), jax.ShapeDtypeStruct((t, LANE), F32),
                   jax.ShapeDtypeStruct((t, LANE), jnp.int32)),
        grid=(t // tm,),
        in_specs=[pl.BlockSpec((tm, d), lambda i: (i, 0)), pl.BlockSpec((1, d), lambda i: (0, 0)),
                  pl.BlockSpec((d, LANE), lambda i: (0, 0))],
        out_specs=(pl.BlockSpec((tm, d // LANE, LANE), lambda i: (i, 0, 0)), pl.BlockSpec((tm, LANE), lambda i: (i, 0)),
                   pl.BlockSpec((tm, LANE), lambda i: (i, 0))),
        compiler_params=_cparams(("parallel",)),
        name="rmsnorm_router",
    )(x, g.reshape(1, d), wr)


def _mm_epilogue(parts, res_ref, o_ref, act):
    r = parts[0]
    if act == "swiglu":
        r = (r * _sigmoid(r)) * parts[1]
    if res_ref is not None:
        r = res_ref[...] + r
    o_ref[...] = r.astype(o_ref.dtype)


def _panel_copy(w_ref, lead, col0, tn, transposed, stage_ref, sem_ref, widx):
    window = (pl.ds(col0, tn), slice(None)) if transposed else (slice(None), pl.ds(col0, tn))
    return pltpu.make_async_copy(w_ref.at[tuple(lead) + window], stage_ref, sem_ref.at[widx])


def _mm_resident_kernel(*refs, n_a, n_w, has_res, act, w_transposed, lead, tn):
    a_refs = refs[:n_a]
    w_refs = refs[n_a:n_a + n_w]
    res_ref = refs[n_a + n_w] if has_res else None
    o_ref = refs[n_a + n_w + int(has_res)]
    scratch = refs[n_a + n_w + int(has_res) + 1:]
    wbf_refs, stage_refs, sem = scratch[:n_w], scratch[n_w:2 * n_w], scratch[2 * n_w]
    j = pl.program_id(0)

    def panel(widx, jj):
        return _panel_copy(w_refs[widx], lead, jj * tn, tn, w_transposed, stage_refs[widx], sem, widx)

    @pl.when(pl.program_id(1) == 0)
    def _():
        @pl.when(j == 0)
        def _():
            for widx in range(n_w):
                panel(widx, 0).start()

        for widx in range(n_w):
            panel(widx, j).wait()
            wbf_refs[widx][...] = stage_refs[widx][...].astype(BF16)

        @pl.when(j + 1 < pl.num_programs(0))
        def _():
            for widx in range(n_w):
                panel(widx, j + 1).start()

    parts = []
    for wbf in wbf_refs:
        k0, part = 0, None
        for a_ref in a_refs:
            k1 = k0 + a_ref.shape[1]
            term = _dot_nt(a_ref[...], wbf[:, k0:k1]) if w_transposed else _dot(a_ref[...], wbf[k0:k1, :])
            part = term if part is None else part + term
            k0 = k1
        parts.append(part)
    _mm_epilogue(parts, res_ref, o_ref, act)


def _cast_kernel(x_ref, o_ref):
    o_ref[...] = x_ref[...].astype(o_ref.dtype)


def cast_bf16(w, lead):
    rows, cols = w.shape[-2:]
    lead = tuple(lead)
    tr = 1024
    assert rows % tr == 0
    return pl.pallas_call(
        _cast_kernel,
        out_shape=jax.ShapeDtypeStruct((rows, cols), BF16),
        grid=(rows // tr,),
        in_specs=[pl.BlockSpec((None,) * len(lead) + (tr, cols), lambda i: lead + (i, 0))],
        out_specs=pl.BlockSpec((tr, cols), lambda i: (i, 0)),
        compiler_params=_cparams(("parallel",)),
        name="cast_bf16",
    )(w)


def _mm_ktiled_kernel(a_ref, w_ref, res_ref, o_ref, acc_ref, *, nk):
    k = pl.program_id(2)
    part = _dot(a_ref[...], w_ref[...])

    @pl.when(k == 0)
    def _():
        acc_ref[...] = part

    @pl.when(k > 0)
    def _():
        acc_ref[...] += part

    @pl.when(k == nk - 1)
    def _():
        _mm_epilogue([acc_ref[...]], res_ref, o_ref, None)


MM_RESIDENT_MAX_K = 4096
MM_VMEM_BUDGET = 50 * 1024 * 1024


def _resident_tiles(m, kdim, n_cols, n_w, out_bytes, has_res):
    for tn in (1024, 512, 256, LANE):
        for tm in (1024, 512):
            if n_cols % tn or m % tm:
                continue
            a_buf = 2 * tm * kdim * 2
            w_buf = n_w * (kdim * tn * 4 + kdim * tn * 2)
            o_buf = 2 * tm * tn * out_bytes + (2 * tm * tn * 4 if has_res else 0)
            if a_buf + w_buf + o_buf <= MM_VMEM_BUDGET:
                return tm, tn
    raise ValueError("no resident-panel tiling fits VMEM")


def matmul(a, ws, lead, n_cols, *, out_dtype, res=None, act=None, w_transposed=False):
    a_parts = list(a) if isinstance(a, (list, tuple)) else [a]
    m = a_parts[0].shape[0]
    kdim = sum(part.shape[1] for part in a_parts)
    lead = tuple(lead)
    if kdim <= MM_RESIDENT_MAX_K:
        tm, tn = _resident_tiles(m, kdim, n_cols, len(ws), jnp.dtype(out_dtype).itemsize, res is not None)
        in_specs = [pl.BlockSpec((tm, part.shape[1]), lambda j, i: (i, 0)) for part in a_parts]
        panel = (tn, kdim) if w_transposed else (kdim, tn)
        in_specs += [pl.BlockSpec(memory_space=pl.ANY) for _ in ws]
        args = [*a_parts, *ws]
        if res is not None:
            in_specs.append(pl.BlockSpec((tm, tn), lambda j, i: (i, j)))
            args.append(res)
        return pl.pallas_call(
            functools.partial(_mm_resident_kernel, n_a=len(a_parts), n_w=len(ws), has_res=res is not None, act=act,
                              w_transposed=w_transposed, lead=lead, tn=tn),
            out_shape=jax.ShapeDtypeStruct((m, n_cols), out_dtype),
            grid=(n_cols // tn, m // tm),
            in_specs=in_specs,
            out_specs=pl.BlockSpec((tm, tn), lambda j, i: (i, j)),
            scratch_shapes=([pltpu.VMEM(panel, BF16) for _ in ws] + [pltpu.VMEM(panel, F32) for _ in ws]
                            + [pltpu.SemaphoreType.DMA((len(ws),))]),
            compiler_params=_cparams(("arbitrary", "arbitrary")),
            name="matmul_resident",
        )(*args)
    assert len(ws) == 1 and act is None and res is not None and len(a_parts) == 1 and not w_transposed
    a = a_parts[0]
    w_bf = cast_bf16(ws[0], lead)
    tm, tn, tk = 1024, 1024, 2048
    assert m % tm == 0 and n_cols % tn == 0 and kdim % tk == 0
    nk = kdim // tk
    return pl.pallas_call(
        functools.partial(_mm_ktiled_kernel, nk=nk),
        out_shape=jax.ShapeDtypeStruct((m, n_cols), out_dtype),
        grid=(n_cols // tn, m // tm, nk),
        in_specs=[pl.BlockSpec((tm, tk), lambda j, i, k: (i, k)),
                  pl.BlockSpec((tk, tn), lambda j, i, k: (k, j)),
                  pl.BlockSpec((tm, tn), lambda j, i, k: (i, j))],
        out_specs=pl.BlockSpec((tm, tn), lambda j, i, k: (i, j)),
        scratch_shapes=[pltpu.VMEM((tm, tn), F32)],
        compiler_params=_cparams(("parallel", "parallel", "arbitrary")),
        name="matmul_ktiled",
    )(a, w_bf, res)


def _bucket_thresholds():
    d = np.arange(0, 4 * REL_MAX_DIST)
    max_exact = REL_BUCKETS // 2
    log_ratio = np.log(np.maximum(d, 1).astype(np.float64) / max_exact) / math.log(REL_MAX_DIST / max_exact)
    far = np.minimum(max_exact + (log_ratio * (REL_BUCKETS - max_exact)).astype(np.int32), REL_BUCKETS - 1)
    bucket = np.where(d < max_exact, d, far)
    assert np.all(np.diff(bucket) >= 0) and bucket[-1] == REL_BUCKETS - 1
    return [int(np.argmax(bucket >= b)) for b in range(REL_BUCKETS)]


_BUCKET_THR = _bucket_thresholds()
REL_FAR_DIST = _BUCKET_THR[REL_BUCKETS - 1]
assert REL_FAR_DIST <= SEL_FAR_BLOCKS * BLK - (BLK - 1)
assert REL_FAR_DIST <= NSA_CMP_STRIDE * (CMP_PAD + 1) - (NSA_CMP_LEN - 1)


def _bias_kernel(tab_ref, raw_ref, sel_ref, cmp_ref):
    h = pl.program_id(0)
    row = lax.broadcasted_iota(jnp.int32, (BLK, BLK), 0)
    col = lax.broadcasted_iota(jnp.int32, (BLK, BLK), 1)

    def lookup(dist):
        dd = jnp.maximum(dist, 0)
        b = jnp.full(dist.shape, tab_ref[0, h], F32)
        for bk in range(1, REL_BUCKETS):
            b = jnp.where(dd >= _BUCKET_THR[bk], tab_ref[bk, h], b)
        return b

    far = tab_ref[REL_BUCKETS - 1, h]
    for d in range(RAW_TILES):
        raw_ref[d] = lookup(d * BLK + row - col)
    for step in range(SEL_TILES):
        dist = (step - SEL_NEG_BLOCKS) * BLK + row - col
        sel_ref[step] = jnp.where(dist < 0, NEG_INF, lookup(dist) - far)
    dist_c = row - NSA_CMP_STRIDE * (col - CMP_PAD) - (NSA_CMP_LEN - 1)
    cmp_ref[...] = lookup(dist_c) - far


def bias_tiles(rel_table):
    tiles = lambda n: pl.BlockSpec((n, None, BLK, BLK), lambda h: (0, h, 0, 0))
    return pl.pallas_call(
        _bias_kernel,
        out_shape=(jax.ShapeDtypeStruct((RAW_TILES, REL_HEADS, BLK, BLK), F32),
                   jax.ShapeDtypeStruct((SEL_TILES, REL_HEADS, BLK, BLK), F32),
                   jax.ShapeDtypeStruct((REL_HEADS, BLK, BLK), F32)),
        grid=(REL_HEADS,),
        in_specs=[pl.BlockSpec(memory_space=pltpu.SMEM)],
        out_specs=(tiles(RAW_TILES), tiles(SEL_TILES), pl.BlockSpec((None, BLK, BLK), lambda h: (h, 0, 0))),
        compiler_params=_cparams(("arbitrary",)),
        name="bias_tiles",
    )(rel_table)


SB_BQ = 256
SB_GROUP = 4


def _sb_kernel(q_ref, k_ref, v_ref, o_ref, c_ref, acc_ref):
    qb = pl.program_id(1)
    row = lax.broadcasted_iota(jnp.int32, (SB_BQ, SB_BQ), 0)
    col = lax.broadcasted_iota(jnp.int32, (SB_BQ, SB_BQ), 1)
    upper = (row > col).astype(BF16)

    def block(h, kb, past, live=None):
        cols = slice(h * HEAD_DIM, (h + 1) * HEAD_DIM)
        ks = pl.multiple_of(kb * SB_BQ, SB_BQ)
        z = _dot_nt(q_ref[:, cols], k_ref[pl.ds(ks, SB_BQ), cols]) * SCALE
        sp = jnp.maximum(z, 0.0) + jnp.log(1.0 + jnp.exp(-jnp.abs(z)))
        log_stay = -sp
        if past is not None:
            log_stay = jnp.where(past, log_stay, 0.0)
        if live is not None:
            log_stay = jnp.where(live, log_stay, 0.0)
        later = _dot(log_stay.astype(BF16), upper)
        tot = jnp.broadcast_to(jnp.sum(log_stay, axis=-1, keepdims=True), (SB_BQ, HEAD_DIM))
        if past is not None:
            w = jnp.where(past, jnp.exp((z - sp) + later), 0.0)
            c_new = tot
        else:
            carry = c_ref[:, cols]
            w = jnp.exp((z - sp) + later + jnp.concatenate([carry] * (SB_BQ // HEAD_DIM), axis=1))
            if live is not None:
                w = jnp.where(live, w, 0.0)
            c_new = carry + tot
        contrib = _dot(w.astype(BF16), v_ref[pl.ds(ks, SB_BQ), cols])
        if past is not None:
            acc_ref[:, cols] = contrib
        else:
            acc_ref[:, cols] += contrib
        c_ref[:, cols] = c_new
        return jnp.max(c_new)

    def sweep(kb, past, live=None):
        return functools.reduce(jnp.maximum, [block(h, kb, past, live) for h in range(SB_GROUP)])

    def cond(state):
        kb, c_max = state
        return jnp.logical_and(kb >= 0, c_max > SB_SKIP_LOG)

    def body(state):
        kb, _ = state
        return kb - 1, sweep(kb, None)

    sweep(qb, col < row)
    c_max = sweep(jnp.maximum(qb - 1, 0), None, qb > 0)
    lax.while_loop(cond, body, (qb - 2, c_max))
    o_ref[...] = acc_ref[...].astype(o_ref.dtype)


def sb_attention(z, t):
    width = SB_GROUP * HEAD_DIM
    assert SB_HEADS % SB_GROUP == 0 and AB_K_SB % SB_GROUP == 0 and AB_V_SB % SB_GROUP == 0
    return pl.pallas_call(
        _sb_kernel,
        out_shape=jax.ShapeDtypeStruct((t, SB_HEADS * HEAD_DIM), BF16),
        grid=(SB_HEADS // SB_GROUP, t // SB_BQ),
        in_specs=[pl.BlockSpec((SB_BQ, width), lambda h, i: (i, AB_Q_SB // SB_GROUP + h)),
                  pl.BlockSpec((t, width), lambda h, i: (0, AB_K_SB // SB_GROUP + h)),
                  pl.BlockSpec((t, width), lambda h, i: (0, AB_V_SB // SB_GROUP + h))],
        out_specs=pl.BlockSpec((SB_BQ, width), lambda h, i: (i, h)),
        scratch_shapes=[pltpu.VMEM((SB_BQ, width), F32), pltpu.VMEM((SB_BQ, width), F32)],
        compiler_params=_cparams(("parallel", "arbitrary")),
        name="sb_attention",
    )(z, z, z)


def _compress_kernel(a_ref, pe_ref, w1_ref, b1_ref, w2_ref, o_ref):
    n_chunk = a_ref.shape[0]
    half = NSA_CMP_STRIDE * HEAD_DIM
    chunks = a_ref[...].reshape(n_chunk, half).astype(F32)
    first = _dot((chunks + pe_ref[:, :half]).astype(BF16), w1_ref[:half, :].astype(BF16))
    second = _dot((chunks + pe_ref[:, half:]).astype(BF16), w1_ref[half:, :].astype(BF16))
    hid = _gelu_tanh(first + pltpu.roll(second, n_chunk - 1, 0) + b1_ref[...])
    out = _dot(hid.astype(BF16), w2_ref[...].astype(BF16))
    o_ref[0:CMP_PAD, :] = jnp.zeros((CMP_PAD, HEAD_DIM), o_ref.dtype)
    o_ref[CMP_PAD:CMP_PAD + n_chunk, :] = out.astype(o_ref.dtype)
    o_ref[CMP_PAD + n_chunk:, :] = jnp.zeros((o_ref.shape[0] - CMP_PAD - n_chunk, HEAD_DIM), o_ref.dtype)


def compress(z, col_block, pe, w1, b1, w2, t):
    assert NSA_CMP_LEN == 2 * NSA_CMP_STRIDE
    n_chunk = t // NSA_CMP_STRIDE
    width = NSA_CMP_LEN * HEAD_DIM
    hidden = w1.shape[-1]
    rows = CMP_PAD + n_chunk + 8
    return pl.pallas_call(
        _compress_kernel,
        out_shape=jax.ShapeDtypeStruct((NSA_KV_HEADS, rows, HEAD_DIM), BF16),
        grid=(NSA_KV_HEADS,),
        in_specs=[pl.BlockSpec((n_chunk, NSA_CMP_STRIDE, HEAD_DIM), lambda h: (0, 0, col_block + h)),
                  pl.BlockSpec((1, width), lambda h: (0, 0)),
                  pl.BlockSpec((width, hidden), lambda h: (0, 0)),
                  pl.BlockSpec((1, hidden), lambda h: (0, 0)),
                  pl.BlockSpec((hidden, HEAD_DIM), lambda h: (0, 0))],
        out_specs=pl.BlockSpec((None, rows, HEAD_DIM), lambda h: (h, 0, 0)),
        compiler_params=_cparams(("parallel",)),
        name="compress",
    )(z.reshape(n_chunk, NSA_CMP_STRIDE, z.shape[1]), pe.reshape(1, width), w1, b1.reshape(1, hidden), w2)


def _cmp_to_sel_padded(t):
    n_cmp = (t - NSA_CMP_LEN) // NSA_CMP_STRIDE + 1
    n_sel = t // NSA_SEL_LEN
    cmp_end = np.arange(n_cmp) * NSA_CMP_STRIDE + NSA_CMP_LEN - 1
    cmp_start = cmp_end - (NSA_CMP_LEN - 1)
    sel_start = np.arange(n_sel) * NSA_SEL_LEN
    overlap = (np.minimum(cmp_end[:, None], sel_start[None, :] + NSA_SEL_LEN - 1)
               - np.maximum(cmp_start[:, None], sel_start[None, :]) + 1)
    c2s = np.clip(overlap, 0, None).astype(np.float32) / NSA_CMP_LEN
    out = np.zeros((CMP_PAD + n_cmp + 9, n_sel), np.float32)
    out[CMP_PAD:CMP_PAD + n_cmp] = c2s
    return out


def _cmp_win_kernel(q_ref, kc_ref, vc_ref, c2s_ref, bias_ref, kw_ref, vw_ref, bias_win_ref, o_ref, pen_ref, ow_ref,
                    qs_ref, *, n_far):
    qb = pl.program_id(1)
    _stack_scaled_queries(q_ref, qs_ref, NSA_G)
    r0 = pl.multiple_of(qb * (BLK // NSA_CMP_STRIDE), 8)
    kc_far = kc_ref[0:n_far, :]
    kc_near = kc_ref[pl.ds(r0, BLK), :]
    vs_far = jnp.concatenate([vc_ref[0:n_far, :], c2s_ref[0:n_far, :]], axis=1)
    vs_near = jnp.concatenate([vc_ref[pl.ds(r0, BLK), :], c2s_ref[pl.ds(r0, BLK), :]], axis=1)
    r_far = lax.broadcasted_iota(jnp.int32, (BLK, n_far), 1)
    far_ok = jnp.logical_and(r_far >= CMP_PAD, r_far < r0)
    row = lax.broadcasted_iota(jnp.int32, (BLK, BLK), 0)
    col = lax.broadcasted_iota(jnp.int32, (BLK, BLK), 1)
    dist_c = row - NSA_CMP_STRIDE * (col - CMP_PAD) - (NSA_CMP_LEN - 1)
    near_ok = jnp.logical_and(dist_c >= 0, col + r0 >= CMP_PAD)

    far_mask = jnp.where(far_ok, 0.0, NEG_INF)

    n_sel = c2s_ref.shape[1]
    imp = jnp.zeros((BLK, n_sel), F32)
    for g in range(NSA_G):
        q = qs_ref[g * BLK:(g + 1) * BLK, :]
        s_far = _dot_nt(q, kc_far) + far_mask
        s_near = _dot_nt(q, kc_near) + jnp.where(near_ok, bias_ref[g], NEG_INF)
        m = jnp.maximum(jnp.maximum(jnp.max(s_far, axis=-1, keepdims=True), jnp.max(s_near, axis=-1, keepdims=True)),
                        M_INIT)
        e_far = jnp.exp(s_far - m)
        e_near = jnp.exp(s_near - m)
        l = jnp.sum(e_far, axis=-1, keepdims=True) + jnp.sum(e_near, axis=-1, keepdims=True)
        inv = 1.0 / jnp.where(l > 0.0, l, 1.0)
        res = (_dot(e_far.astype(BF16), vs_far) + _dot(e_near.astype(BF16), vs_near)) * inv
        o_ref[:, g * HEAD_DIM:(g + 1) * HEAD_DIM] = res[:, :HEAD_DIM].astype(o_ref.dtype)
        imp = imp + res[:, HEAD_DIM:]

    blk = lax.broadcasted_iota(jnp.int32, (n_sel, BLK), 0)
    t_pos = qb * BLK + lax.broadcasted_iota(jnp.int32, (n_sel, BLK), 1)
    cur = t_pos // NSA_SEL_LEN
    forced = jnp.logical_or(blk == 0, jnp.logical_or(blk == cur, blk == cur - 1))
    score = jnp.where(forced, NSA_FORCE_SCORE, jnp.where(blk <= cur, imp.T, -1.0))
    pen = jnp.full((n_sel, BLK), NEG_INF, F32)
    for _ in range(min(NSA_SEL_TOPK, n_sel)):
        top = jnp.max(score, axis=0, keepdims=True)
        first = jnp.min(jnp.where(score == top, blk, n_sel), axis=0, keepdims=True)
        hit = blk == first
        pen = jnp.where(hit, 0.0, pen)
        score = jnp.where(hit, -2.0, score)
    pen_ref[...] = pen.T.astype(pen_ref.dtype)

    _banded_body(q_ref, kw_ref, vw_ref, bias_win_ref, ow_ref, qs_ref, None, qb, group=NSA_G,
                 n_prev=NSA_WINDOW // BLK, window=NSA_WINDOW, scale_q=False)


def cmp_select_window(z, kc, vc, bias_cmp, bias_raw, t):
    assert NSA_WINDOW % BLK == 0
    n_sel = t // NSA_SEL_LEN
    assert n_sel == LANE, "selection mask rides in one 128-lane tile"
    c2s = jnp.asarray(_cmp_to_sel_padded(t), BF16)
    rows = kc.shape[1]
    assert c2s.shape[0] == rows
    n_far = t // NSA_CMP_STRIDE
    qw = NSA_G * HEAD_DIM
    return pl.pallas_call(
        functools.partial(_cmp_win_kernel, n_far=n_far),
        out_shape=(jax.ShapeDtypeStruct((t, NSA_HEADS * HEAD_DIM), BF16),
                   jax.ShapeDtypeStruct((NSA_KV_HEADS, t, n_sel), BF16),
                   jax.ShapeDtypeStruct((t, NSA_HEADS * HEAD_DIM), BF16)),
        grid=(NSA_KV_HEADS, t // BLK),
        in_specs=[pl.BlockSpec((BLK, qw), lambda h, i: (i, AB_Q_N // NSA_G + h)),
                  pl.BlockSpec((None, rows, HEAD_DIM), lambda h, i: (h, 0, 0)),
                  pl.BlockSpec((None, rows, HEAD_DIM), lambda h, i: (h, 0, 0)),
                  pl.BlockSpec((rows, n_sel), lambda h, i: (0, 0)),
                  pl.BlockSpec((NSA_G, BLK, BLK), lambda h, i: (h, 0, 0)),
                  pl.BlockSpec((t, HEAD_DIM), lambda h, i: (0, AB_K_W + h)),
                  pl.BlockSpec((t, HEAD_DIM), lambda h, i: (0, AB_V_W + h)),
                  pl.BlockSpec((RAW_TILES, NSA_G, BLK, BLK), lambda h, i: (0, h, 0, 0))],
        out_specs=(pl.BlockSpec((BLK, qw), lambda h, i: (i, h)),
                   pl.BlockSpec((None, BLK, n_sel), lambda h, i: (h, i, 0)),
                   pl.BlockSpec((BLK, qw), lambda h, i: (i, h))),
        scratch_shapes=[pltpu.VMEM((NSA_G * BLK, HEAD_DIM), BF16)],
        compiler_params=_cparams(("parallel", "arbitrary")),
        name="cmp_select_window",
    )(z, kc, vc, c2s, bias_cmp, z, z, bias_raw)


SEL_CHUNK = SEL_CHUNK_BLOCKS * BLK


def _sel_kernel(q_ref, pen_ref, kaug_ref, vaug_ref, bias_ref, gate_ref, ocmp_ref, owin_ref, o_ref,
                qa_ref, s_ref, cmax_ref, m_ref, acc_ref):
    qb = pl.program_id(1)
    pen = pen_ref[...]
    for g in range(NSA_G):
        rows = slice(g * BLK, (g + 1) * BLK)
        qa_ref[rows, :HEAD_DIM] = (q_ref[:, g * HEAD_DIM:(g + 1) * HEAD_DIM].astype(F32) * SCALE).astype(BF16)
        qa_ref[rows, HEAD_DIM:] = pen
    m_ref[...] = jnp.full(m_ref.shape, M_INIT, F32)
    acc_ref[...] = jnp.zeros(acc_ref.shape, F32)
    n_chunks = qb // SEL_CHUNK_BLOCKS + 1
    last_chunk = kaug_ref.shape[0] // SEL_CHUNK - 1

    def chunk_start(c):
        return pl.multiple_of(jnp.minimum(c, last_chunk) * SEL_CHUNK, SEL_CHUNK)

    def scores(c, slot):
        s_all = _dot_nt(qa_ref[...], kaug_ref[pl.ds(chunk_start(c), SEL_CHUNK), :])
        first = qb - c * SEL_CHUNK_BLOCKS + SEL_NEG_BLOCKS
        n_rows = NSA_G * BLK
        tiles = [s_all[:, j * BLK:(j + 1) * BLK] + bias_ref[jnp.clip(first - j, 0, SEL_TILES - 1)].reshape(n_rows, BLK)
                 for j in range(SEL_CHUNK_BLOCKS)]
        for j, tile in enumerate(tiles):
            s_ref[slot, :, j * BLK:(j + 1) * BLK] = tile
        cmax_ref[slot] = jnp.broadcast_to(
            jnp.max(functools.reduce(jnp.maximum, tiles), axis=-1, keepdims=True), (n_rows, LANE))

    def absorb(c, slot):
        va = vaug_ref[pl.ds(chunk_start(c), SEL_CHUNK), :]
        m_old = m_ref[...]
        m_new = jnp.maximum(m_old, cmax_ref[slot])
        alpha = jnp.exp(m_old - m_new)
        p = jnp.concatenate([jnp.exp(s_ref[slot, :, j * BLK:(j + 1) * BLK] - m_new)
                             for j in range(SEL_CHUNK_BLOCKS)], axis=1).astype(BF16)
        acc_ref[...] = jnp.concatenate([alpha, alpha], axis=1) * acc_ref[...] + _dot(p, va)
        m_ref[...] = m_new

    def pair(k, carry):
        scores(2 * k + 1, 1)
        absorb(2 * k, 0)
        scores(2 * k + 2, 0)
        absorb(2 * k + 1, 1)
        return carry

    scores(0, 0)
    lax.fori_loop(0, n_chunks // 2, pair, 0)

    @pl.when(n_chunks % 2 == 1)
    def _():
        absorb(n_chunks - 1, 0)

    gates = _sigmoid(gate_ref[...])
    for g in range(NSA_G):
        rows = slice(g * BLK, (g + 1) * BLK)
        cols = slice(g * HEAD_DIM, (g + 1) * HEAD_DIM)
        o_sel = acc_ref[rows, :HEAD_DIM] / acc_ref[rows, HEAD_DIM:]
        mixed = (gates[:, 3 * g:3 * g + 1] * ocmp_ref[:, cols].astype(F32)
                 + gates[:, 3 * g + 1:3 * g + 2] * o_sel
                 + gates[:, 3 * g + 2:3 * g + 3] * owin_ref[:, cols].astype(F32))
        o_ref[:, cols] = mixed.astype(o_ref.dtype)


def sel_attention_mix(z, pen, bias_sel, gate_logits, o_cmp, o_win, t):
    n_sel = t // NSA_SEL_LEN
    n_gate = 3 * NSA_G
    gates_kv = gate_logits[:, :NSA_KV_HEADS * n_gate].reshape(t, NSA_KV_HEADS, n_gate).transpose(1, 0, 2)
    gates_kv = jnp.pad(gates_kv, ((0, 0), (0, 0), (0, LANE - n_gate)))
    assert t % SEL_CHUNK == 0
    onehot = jnp.asarray(np.arange(t)[:, None] // NSA_SEL_LEN == np.arange(n_sel)[None, :], BF16)
    kv_cols = lambda blk: z[:, blk * LANE:(blk + NSA_KV_HEADS) * LANE].reshape(t, NSA_KV_HEADS, HEAD_DIM)
    side = lambda a: jnp.broadcast_to(a[:, None, :], (t, NSA_KV_HEADS, a.shape[1]))
    kaug = jnp.concatenate([kv_cols(AB_K_S), side(onehot)], axis=2).transpose(1, 0, 2)
    vaug = jnp.concatenate([kv_cols(AB_V_S), side(jnp.ones((t, HEAD_DIM), BF16))], axis=2).transpose(1, 0, 2)
    qw = NSA_G * HEAD_DIM
    rows = NSA_G * BLK
    return pl.pallas_call(
        _sel_kernel,
        out_shape=jax.ShapeDtypeStruct((t, NSA_HEADS * HEAD_DIM), BF16),
        grid=(NSA_KV_HEADS, t // BLK),
        in_specs=[pl.BlockSpec((BLK, qw), lambda h, i: (i, AB_Q_N // NSA_G + h)),
                  pl.BlockSpec((None, BLK, n_sel), lambda h, i: (h, i, 0)),
                  pl.BlockSpec((None, t, HEAD_DIM + n_sel), lambda h, i: (h, 0, 0)),
                  pl.BlockSpec((None, t, 2 * HEAD_DIM), lambda h, i: (h, 0, 0)),
                  pl.BlockSpec((SEL_TILES, NSA_G, BLK, BLK), lambda h, i: (0, h, 0, 0)),
                  pl.BlockSpec((None, BLK, LANE), lambda h, i: (h, i, 0)),
                  pl.BlockSpec((BLK, qw), lambda h, i: (i, h)),
                  pl.BlockSpec((BLK, qw), lambda h, i: (i, h))],
        out_specs=pl.BlockSpec((BLK, qw), lambda h, i: (i, h)),
        scratch_shapes=[pltpu.VMEM((rows, HEAD_DIM + n_sel), BF16), pltpu.VMEM((2, rows, SEL_CHUNK), F32),
                        pltpu.VMEM((2, rows, LANE), F32), pltpu.VMEM((rows, LANE), F32),
                        pltpu.VMEM((rows, 2 * HEAD_DIM), F32)],
        compiler_params=_cparams(("parallel", "arbitrary")),
        name="sel_attention",
    )(z, pen, kaug, vaug, bias_sel, gates_kv, o_cmp, o_win)


def _banded_body(q_ref, k_ref, v_ref, bias_ref, o_ref, qs_ref, sink_of, qb, *, group, n_prev, window, scale_q):
    if scale_q:
        _stack_scaled_queries(q_ref, qs_ref, group)
    qs = qs_ref[...]
    row = lax.broadcasted_iota(jnp.int32, (BLK, BLK), 0)
    col = lax.broadcasted_iota(jnp.int32, (BLK, BLK), 1)
    ones = jnp.ones((BLK, HEAD_DIM), BF16)
    tiles, vaugs = [], []
    for delta in range(n_prev + 1):
        kb = qb - delta
        start = pl.multiple_of(jnp.maximum(kb, 0) * BLK, BLK)
        dist = delta * BLK + row - col
        limit = jnp.where(kb >= 0, window, 0)
        valid = jnp.logical_and(dist >= 0, dist < limit)
        bias = jnp.concatenate([jnp.where(valid, bias_ref[delta, g], NEG_INF) for g in range(group)], axis=0)
        tiles.append(_dot_nt(qs, k_ref[pl.ds(start, BLK), :]) + bias)
        vaugs.append(jnp.concatenate([v_ref[pl.ds(start, BLK), :], ones], axis=1))
    m = jnp.max(functools.reduce(jnp.maximum, tiles), axis=-1, keepdims=True)
    if sink_of is not None:
        sink = jnp.concatenate([jnp.full((BLK, 1), sink_of(g), F32) for g in range(group)], axis=0)
        m = jnp.maximum(m, sink)
    acc = functools.reduce(jnp.add, [_dot(jnp.exp(tile - m).astype(BF16), vaug) for tile, vaug in zip(tiles, vaugs)])
    denom = acc[:, HEAD_DIM:]
    if sink_of is not None:
        denom = denom + jnp.exp(sink - m)
    out = acc[:, :HEAD_DIM] / denom
    for g in range(group):
        o_ref[:, g * HEAD_DIM:(g + 1) * HEAD_DIM] = out[g * BLK:(g + 1) * BLK, :].astype(o_ref.dtype)


def _stack_scaled_queries(q_ref, qs_ref, group):
    for g in range(group):
        qs_ref[g * BLK:(g + 1) * BLK, :] = (q_ref[:, g * HEAD_DIM:(g + 1) * HEAD_DIM].astype(F32) * SCALE).astype(BF16)


SWA_G = SWA_HEADS // SWA_KV_HEADS


def _sg_swa_kernel(sink_ref, u_ref, v_ref, lng_ref, lnb_ref, w_ref, b_ref, q_ref, k_ref, vv_ref, bias_ref,
                   o_sg_ref, o_swa_ref, qs_ref):
    u = _gelu_tanh(u_ref[...].astype(F32))
    gv = _gelu_tanh(v_ref[...].astype(F32))
    mu = jnp.mean(gv, axis=-1, keepdims=True)
    cen = gv - mu
    var = jnp.mean(cen * cen, axis=-1, keepdims=True)
    gv = (cen * lax.rsqrt(var + RMS_EPS) * lng_ref[...] + lnb_ref[...]).astype(BF16)
    row = lax.broadcasted_iota(jnp.int32, (SG_CHUNK, SG_CHUNK), 0)
    col = lax.broadcasted_iota(jnp.int32, (SG_CHUNK, SG_CHUNK), 1)
    gw = u.shape[1] // SG_GROUPS
    for g in range(SG_GROUPS):
        sl = slice(g * gw, (g + 1) * gw)
        w = jnp.where(row >= col, w_ref[g], 0.0).astype(BF16)
        spatial = _dot(w, gv[:, sl]) + b_ref[:, g:g + 1]
        o_sg_ref[:, sl] = (u[:, sl] * spatial).astype(o_sg_ref.dtype)

    qw = SWA_G * HEAD_DIM
    for kv in range(SWA_KV_HEADS):
        heads = pl.ds(kv * qw, qw)
        _banded_body(q_ref.at[:, heads], k_ref.at[:, pl.ds(kv * HEAD_DIM, HEAD_DIM)],
                     vv_ref.at[:, pl.ds(kv * HEAD_DIM, HEAD_DIM)], bias_ref.at[:, pl.ds(kv * SWA_G, SWA_G)],
                     o_swa_ref.at[:, heads], qs_ref, lambda g, kv=kv: sink_ref[kv * SWA_G + g], pl.program_id(0),
                     group=SWA_G, n_prev=-(-SWA_WINDOW // BLK), window=SWA_WINDOW, scale_q=True)


def spatial_gating_swa(z1, ln_g, ln_b, sg_w, sg_b, bias_raw, sinks, t):
    width = ln_g.shape[0]
    assert width % (SG_GROUPS * LANE) == 0 and SG_CHUNK == BLK and CD_Q * LANE == 2 * width
    n_tiles = -(-SWA_WINDOW // BLK) + 1
    qw = SWA_HEADS * HEAD_DIM
    kvw = SWA_KV_HEADS * HEAD_DIM
    row = lambda w, blk: pl.BlockSpec((BLK, w), lambda i: (i, blk))
    full = lambda shape: pl.BlockSpec(shape, lambda i: (0,) * len(shape))
    return pl.pallas_call(
        _sg_swa_kernel,
        out_shape=(jax.ShapeDtypeStruct((t, width), BF16), jax.ShapeDtypeStruct((t, qw), BF16)),
        grid=(t // BLK,),
        in_specs=[pl.BlockSpec(memory_space=pltpu.SMEM),
                  row(width, 0), row(width, 1), full((1, width)), full((1, width)),
                  full((SG_GROUPS, SG_CHUNK, SG_CHUNK)), full((SG_CHUNK, SG_GROUPS)),
                  row(qw, CD_Q * LANE // qw),
                  pl.BlockSpec((t, kvw), lambda i: (0, CD_K * LANE // kvw)),
                  pl.BlockSpec((t, kvw), lambda i: (0, CD_V * LANE // kvw)),
                  full((n_tiles, REL_HEADS, BLK, BLK))],
        out_specs=(row(width, 0), row(qw, 0)),
        scratch_shapes=[pltpu.VMEM((SWA_G * BLK, HEAD_DIM), BF16)],
        compiler_params=_cparams(("arbitrary",)),
        name="spatial_gating_swa",
    )(sinks, z1, z1, ln_g.reshape(1, width), ln_b.reshape(1, width), sg_w, jnp.transpose(sg_b), z1, z1, z1, bias_raw)


MOE_TM = 512
MOE_TOP_K = 2


def moe_plan(idx, gate, t):
    n_rows = MOE_TOP_K * t + N_EXPERTS * MOE_TM
    n_blocks = n_rows // MOE_TM
    pair_e = idx[:, :MOE_TOP_K].reshape(-1)
    pair_w = gate[:, :MOE_TOP_K].reshape(-1)
    onehot = (pair_e[:, None] == jnp.arange(N_EXPERTS, dtype=jnp.int32)[None, :]).astype(jnp.int32)
    before = jnp.cumsum(onehot, axis=0) - onehot
    count = jnp.sum(onehot, axis=0)
    padded = (count + MOE_TM - 1) // MOE_TM * MOE_TM
    group_end = jnp.cumsum(padded)
    group_start = group_end - padded
    pair_row = jnp.sum(onehot * (before + group_start[None, :]), axis=1)
    row_pair = jnp.full((n_rows,), -1, jnp.int32).at[pair_row].set(jnp.arange(MOE_TOP_K * t, dtype=jnp.int32))
    row_tok = jnp.maximum(row_pair, 0) // MOE_TOP_K
    row_w = jnp.where(row_pair >= 0, pair_w[jnp.maximum(row_pair, 0)], 0.0)
    block_start = jnp.arange(n_blocks, dtype=jnp.int32) * MOE_TM
    block_e = jnp.minimum(jnp.sum(block_start[:, None] >= group_end[None, :], axis=1), N_EXPERTS - 1).astype(jnp.int32)
    block_rows = (group_start + count)[block_e] - block_start
    block_half = (block_rows <= MOE_TM // 2).astype(jnp.int32)
    n_used = (group_end[-1] // MOE_TM).astype(jnp.int32).reshape(1)
    nonempty = count > 0
    group_e = jnp.argsort(jnp.logical_not(nonempty), stable=True).astype(jnp.int32)
    block_group = (jnp.cumsum(nonempty.astype(jnp.int32)) - 1)[block_e].astype(jnp.int32)
    n_groups = jnp.sum(nonempty.astype(jnp.int32)).reshape(1)
    blocks = (block_e, block_half, n_used, block_group, group_e, n_groups)
    return row_tok, row_w.reshape(n_rows, 1), blocks, pair_row.astype(jnp.int32)


MOE_GATHER_ROWS = 256
DMA_BURST = 8


def _gather_rows_kernel(n_live_ref, tok_ref, tok_next_ref, src_ref, o_ref, buf, sem):
    b = pl.program_id(0)
    slot = b % 2
    n_live = n_live_ref[0]

    def row_copy(idx_ref, k, s):
        return pltpu.make_async_copy(src_ref.at[idx_ref[k]], buf.at[s, k], sem.at[s])

    def start_block(idx_ref, s):
        def start(k8, carry):
            for u in range(DMA_BURST):
                row_copy(idx_ref, k8 * DMA_BURST + u, s).start(priority=u % 2)
            return carry
        lax.fori_loop(0, MOE_GATHER_ROWS // DMA_BURST, start, 0)

    @pl.when(b == 0)
    def _():
        start_block(tok_ref, 0)

    @pl.when(b + 1 < n_live)
    def _():
        start_block(tok_next_ref, 1 - slot)

    @pl.when(b < n_live)
    def _():
        def wait(k, carry):
            row_copy(tok_ref, k, slot).wait()
            return carry

        lax.fori_loop(0, MOE_GATHER_ROWS, wait, 0, unroll=8)
        o_ref[...] = buf[slot].reshape(o_ref.shape)

    @pl.when(b >= n_live)
    def _():
        o_ref[...] = jnp.zeros(o_ref.shape, o_ref.dtype)


def gather_rows(src, row_tok, n_live_rows):
    n_rows = row_tok.shape[0]
    d = src.shape[1] * src.shape[2]
    n_steps = n_rows // MOE_GATHER_ROWS
    return pl.pallas_call(
        _gather_rows_kernel,
        out_shape=jax.ShapeDtypeStruct((n_rows, d), src.dtype),
        grid=(n_steps,),
        in_specs=[pl.BlockSpec(memory_space=pltpu.SMEM),
                  pl.BlockSpec((MOE_GATHER_ROWS,), lambda b: (b,), memory_space=pltpu.SMEM),
                  pl.BlockSpec((MOE_GATHER_ROWS,), lambda b: (jnp.minimum(b + 1, n_steps - 1),),
                               memory_space=pltpu.SMEM),
                  pl.BlockSpec(memory_space=pl.ANY)],
        out_specs=pl.BlockSpec((MOE_GATHER_ROWS, d), lambda b: (b, 0)),
        scratch_shapes=[pltpu.VMEM((2, MOE_GATHER_ROWS) + src.shape[1:], src.dtype), pltpu.SemaphoreType.DMA((2,))],
        compiler_params=_cparams(("arbitrary",)),
        name="moe_gather",
    )(n_live_rows // MOE_GATHER_ROWS, row_tok, row_tok, src)


def _moe_rows(b, half_ref, n_used_ref, compute, o_ref):
    live = b < n_used_ref[0]
    half = half_ref[jnp.minimum(b, n_used_ref[0] - 1)] == 1
    lower, upper = slice(0, MOE_TM // 2), slice(MOE_TM // 2, MOE_TM)

    @pl.when(jnp.logical_and(live, jnp.logical_not(half)))
    def _():
        compute(slice(0, MOE_TM))

    @pl.when(jnp.logical_and(live, half))
    def _():
        compute(lower)
        o_ref[upper, :] = jnp.zeros((MOE_TM // 2, o_ref.shape[1]), o_ref.dtype)

    @pl.when(jnp.logical_not(live))
    def _():
        o_ref[...] = jnp.zeros(o_ref.shape, o_ref.dtype)


def _moe_panels(tables, w_refs, wbf_refs, stage_refs, sem, tn):
    block_e_ref, _, n_used_ref, block_group_ref, group_e_ref, n_groups_ref = tables
    j, b = pl.program_id(0), pl.program_id(1)
    n_groups = n_groups_ref[0]
    group = block_group_ref[b]
    first_block = jnp.logical_or(b == 0, group != block_group_ref[jnp.maximum(b - 1, 0)])

    def panel(widx, expert, jj):
        return _panel_copy(w_refs[widx], (0, expert), jj * tn, tn, False, stage_refs[widx], sem, widx)

    @pl.when(jnp.logical_and(b < n_used_ref[0], first_block))
    def _():
        number = j * n_groups + group

        @pl.when(number == 0)
        def _():
            for widx in range(len(w_refs)):
                panel(widx, block_e_ref[b], 0).start()

        for widx in range(len(w_refs)):
            panel(widx, block_e_ref[b], j).wait()
            wbf_refs[widx][...] = stage_refs[widx][...].astype(BF16)

        @pl.when(number + 1 < pl.num_programs(0) * n_groups)
        def _():
            wraps = group + 1 == n_groups
            next_e = group_e_ref[jnp.where(wraps, 0, group + 1)]
            next_j = jnp.where(wraps, j + 1, j)
            for widx in range(len(w_refs)):
                panel(widx, next_e, next_j).start()


def _moe_up_kernel(*refs, tn):
    tables, (x_ref, wg_ref, wu_ref, o_ref, wg_bf, wu_bf, wg_st, wu_st, sem) = refs[:6], refs[6:]
    _moe_panels(tables, (wg_ref, wu_ref), (wg_bf, wu_bf), (wg_st, wu_st), sem, tn)

    def compute(rows):
        x = x_ref[rows, :]
        gate = _dot(x, wg_bf[...])
        o_ref[rows, :] = ((gate * _sigmoid(gate)) * _dot(x, wu_bf[...])).astype(o_ref.dtype)

    _moe_rows(pl.program_id(1), tables[1], tables[2], compute, o_ref)


def _moe_down_kernel(*refs, tn):
    tables, (h_ref, wd_ref, w_ref, o_ref, wd_bf, wd_st, sem) = refs[:6], refs[6:]
    _moe_panels(tables, (wd_ref,), (wd_bf,), (wd_st,), sem, tn)

    def compute(rows):
        o_ref[rows, :] = w_ref[rows, :] * _dot(h_ref[rows, :], wd_bf[...])

    _moe_rows(pl.program_id(1), tables[1], tables[2], compute, o_ref)


def moe_experts(xs, row_w, blocks, w_gate, w_up, w_down):
    n_rows, d = xs.shape
    d_ff = w_gate.shape[3]
    n_blocks = n_rows // MOE_TM
    tn_up, tn_down = 512, 1024
    row_block = lambda j, b, be, bh, nu, bg, ge, ng: (jnp.minimum(b, nu[0] - 1), 0)
    out_block = lambda j, b, be, bh, nu, bg, ge, ng: (b, j)
    in_hbm = pl.BlockSpec(memory_space=pl.ANY)
    hid = pl.pallas_call(
        functools.partial(_moe_up_kernel, tn=tn_up),
        out_shape=jax.ShapeDtypeStruct((n_rows, d_ff), BF16),
        grid_spec=pltpu.PrefetchScalarGridSpec(
            num_scalar_prefetch=len(blocks),
            grid=(d_ff // tn_up, n_blocks),
            in_specs=[pl.BlockSpec((MOE_TM, d), row_block), in_hbm, in_hbm],
            out_specs=pl.BlockSpec((MOE_TM, tn_up), out_block),
            scratch_shapes=[pltpu.VMEM((d, tn_up), BF16), pltpu.VMEM((d, tn_up), BF16),
                            pltpu.VMEM((d, tn_up), F32), pltpu.VMEM((d, tn_up), F32),
                            pltpu.SemaphoreType.DMA((2,))]),
        compiler_params=_cparams(("arbitrary", "arbitrary")),
        name="moe_up",
    )(*blocks, xs, w_gate, w_up)
    return pl.pallas_call(
        functools.partial(_moe_down_kernel, tn=tn_down),
        out_shape=jax.ShapeDtypeStruct((n_rows, d), F32),
        grid_spec=pltpu.PrefetchScalarGridSpec(
            num_scalar_prefetch=len(blocks),
            grid=(d // tn_down, n_blocks),
            in_specs=[pl.BlockSpec((MOE_TM, d_ff), row_block), in_hbm, pl.BlockSpec((MOE_TM, 1), row_block)],
            out_specs=pl.BlockSpec((MOE_TM, tn_down), out_block),
            scratch_shapes=[pltpu.VMEM((d_ff, tn_down), BF16), pltpu.VMEM((d_ff, tn_down), F32),
                            pltpu.SemaphoreType.DMA((1,))]),
        compiler_params=_cparams(("arbitrary", "arbitrary")),
        name="moe_down",
    )(*blocks, hid, w_down, row_w)


MOE_TC = 256


def _moe_combine_kernel(rows_ref, rows_next_ref, x_ref, g_ref, y_ref, o_ref, buf, sem):
    b = pl.program_id(0)
    slot = b % 2

    def row_copy(idx_ref, k, choice, s):
        row = idx_ref[MOE_TOP_K * k + choice]
        return pltpu.make_async_copy(y_ref.at[pl.ds(row, 1), :], buf.at[s, choice, pl.ds(k, 1), :], sem.at[s])

    def start_block(idx_ref, s):
        def start(k4, carry):
            for u in range(DMA_BURST // MOE_TOP_K):
                for choice in range(MOE_TOP_K):
                    row_copy(idx_ref, k4 * (DMA_BURST // MOE_TOP_K) + u, choice, s).start(priority=choice % 2)
            return carry
        lax.fori_loop(0, MOE_TC * MOE_TOP_K // DMA_BURST, start, 0)

    @pl.when(b == 0)
    def _():
        start_block(rows_ref, 0)

    @pl.when(b + 1 < pl.num_programs(0))
    def _():
        start_block(rows_next_ref, 1 - slot)

    def wait(k, carry):
        for choice in range(MOE_TOP_K):
            row_copy(rows_ref, k, choice, slot).wait()
        return carry

    lax.fori_loop(0, MOE_TC, wait, 0, unroll=4)
    x = x_ref[...]
    for choice in range(MOE_TOP_K):
        x = x + buf[slot, choice]
    y = x * lax.rsqrt(jnp.mean(x * x, axis=-1, keepdims=True) + RMS_EPS)
    o_ref[...] = (y * g_ref[...]).astype(o_ref.dtype)


def moe_combine_rmsnorm(x, ys, pair_row, g):
    t, d = x.shape
    n_steps = t // MOE_TC
    idx_block = lambda index_map: pl.BlockSpec((MOE_TOP_K * MOE_TC,), index_map, memory_space=pltpu.SMEM)
    return pl.pallas_call(
        _moe_combine_kernel,
        out_shape=jax.ShapeDtypeStruct((t, d), F32),
        grid=(n_steps,),
        in_specs=[idx_block(lambda i: (i,)),
                  idx_block(lambda i: (jnp.minimum(i + 1, n_steps - 1),)),
                  pl.BlockSpec((MOE_TC, d), lambda i: (i, 0)),
                  pl.BlockSpec((1, d), lambda i: (0, 0)),
                  pl.BlockSpec(memory_space=pl.ANY)],
        out_specs=pl.BlockSpec((MOE_TC, d), lambda i: (i, 0)),
        scratch_shapes=[pltpu.VMEM((2, MOE_TOP_K, MOE_TC, d), F32), pltpu.SemaphoreType.DMA((2,))],
        compiler_params=_cparams(("arbitrary",)),
        name="moe_combine",
    )(pair_row, pair_row, x, g.reshape(1, d), ys)


def kernel(x, rel_table, norm_mix, norm_ffn, norm_final, ab_w_in, ab_w_out, cmp_k_pe, cmp_k_w1, cmp_k_b1, cmp_k_w2, cmp_v_pe, cmp_v_w1, cmp_v_b1, cmp_v_w2, ffn_w_gate, ffn_w_up, ffn_w_down, cd_w_in, cd_w_out, sg_ln_g, sg_ln_b, sg_w, sg_b, swa_sinks, moe_w_router, moe_w_gate, moe_w_up, moe_w_down):
    b, t, d = x.shape
    assert b == 1
    xs = x.reshape(t, d)
    bias_raw, bias_sel, bias_cmp = bias_tiles(rel_table)

    h = rmsnorm(xs, norm_mix[0], BF16)
    ab_main = AB_MAIN_BLOCKS * LANE
    w_in_t = jnp.swapaxes(ab_w_in, 1, 2)
    z = matmul(h, [w_in_t], (0,), ab_main, out_dtype=BF16, w_transposed=True)
    w_gate_rows = jnp.pad(w_in_t[0, ab_main:], ((0, LANE - (w_in_t.shape[1] - ab_main)), (0, 0)))
    gate_logits = matmul(h, [w_gate_rows], (), LANE, out_dtype=F32, w_transposed=True)
    o_sb = sb_attention(z, t)
    kc = compress(z, AB_K_C, cmp_k_pe[0], cmp_k_w1[0], cmp_k_b1[0], cmp_k_w2[0], t)
    vc = compress(z, AB_V_C, cmp_v_pe[0], cmp_v_w1[0], cmp_v_b1[0], cmp_v_w2[0], t)
    o_cmp, pen, o_win = cmp_select_window(z, kc, vc, bias_cmp, bias_raw, t)
    o_nsa = sel_attention_mix(z, pen, bias_sel, gate_logits, o_cmp, o_win, t)
    xs = matmul([o_sb, o_nsa], [ab_w_out], (0,), d, out_dtype=F32, res=xs)
    h = rmsnorm(xs, norm_ffn[0], BF16)
    hid = matmul(h, [ffn_w_gate, ffn_w_up], (0,), ffn_w_gate.shape[2], out_dtype=BF16, act="swiglu")
    xs = matmul(hid, [ffn_w_down], (0,), d, out_dtype=F32, res=xs)

    h = rmsnorm(xs, norm_mix[1], BF16)
    z1 = matmul(h, [cd_w_in], (0,), cd_w_in.shape[2], out_dtype=BF16)
    o_sg, o_swa = spatial_gating_swa(z1, sg_ln_g[0], sg_ln_b[0], sg_w[0], sg_b[0], bias_raw, swa_sinks[0], t)
    xs = matmul([o_sg, o_swa], [cd_w_out], (0,), d, out_dtype=F32, res=xs)
    h, gate, idx = rmsnorm_router(xs, norm_ffn[1], moe_w_router[0])
    row_tok, row_w, blocks, pair_row = moe_plan(idx, gate, t)
    rows = gather_rows(h, row_tok, blocks[2] * MOE_TM)
    ys = moe_experts(rows, row_w, blocks, moe_w_gate, moe_w_up, moe_w_down)
    return moe_combine_rmsnorm(xs, ys, pair_row, norm_final).reshape(b, t, d)
```

```python
import functools
import math

import numpy as np
import jax
import jax.numpy as jnp
from jax import lax
from jax.experimental import pallas as pl
from jax.experimental.pallas import tpu as pltpu

F32 = jnp.float32
BF16 = jnp.bfloat16

LANE = 128
VMEM_LIMIT = 56 * 1024 * 1024

HEAD_DIM = 128
BLK = 128
NEG_INF = -1e30
RMS_EPS = 1e-6
SCALE = HEAD_DIM ** -0.5

SB_HEADS = 16
NSA_HEADS = 16
NSA_KV_HEADS = 4
NSA_G = NSA_HEADS // NSA_KV_HEADS
NSA_CMP_LEN = 32
NSA_CMP_STRIDE = 16
NSA_SEL_LEN = 64
NSA_SEL_TOPK = 16
NSA_WINDOW = 512
NSA_FORCE_SCORE = 1e6
SG_GROUPS = 8
SG_CHUNK = 128
SWA_HEADS = 16
SWA_KV_HEADS = 2
SWA_WINDOW = 128
REL_BUCKETS = 32
REL_MAX_DIST = 1024
REL_HEADS = 16
N_EXPERTS = 8

AB_Q_SB, AB_K_SB, AB_V_SB = 0, 16, 32
AB_Q_N = 48
AB_K_C, AB_V_C, AB_K_S, AB_V_S, AB_K_W, AB_V_W = 64, 68, 72, 76, 80, 84
AB_MAIN_BLOCKS = 88
CD_Q, CD_K, CD_V = 32, 48, 50

SEL_CHUNK_BLOCKS = 4
SEL_FAR_BLOCKS = 8
SEL_NEG_BLOCKS = SEL_CHUNK_BLOCKS - 1
SEL_TILES = SEL_NEG_BLOCKS + SEL_FAR_BLOCKS + SEL_CHUNK_BLOCKS - 1
RAW_TILES = NSA_WINDOW // BLK + 1
CMP_PAD = 120
SB_SKIP_LOG = -110.0
M_INIT = -5e29


def _cparams(sem):
    return pltpu.CompilerParams(dimension_semantics=sem, vmem_limit_bytes=VMEM_LIMIT)


def _dot(a, b):
    return jnp.dot(a, b, preferred_element_type=F32)


def _dot_nt(a, b):
    return lax.dot_general(a, b, (((1,), (1,)), ((), ())), preferred_element_type=F32)


def _dot_split(a, b):
    a_hi = a.astype(BF16)
    b_hi = b.astype(BF16)
    a_lo = (a - a_hi.astype(F32)).astype(BF16)
    b_lo = (b - b_hi.astype(F32)).astype(BF16)
    return _dot(a_hi, b_hi) + (_dot(a_hi, b_lo) + _dot(a_lo, b_hi))


def _sigmoid(x):
    return 1.0 / (1.0 + jnp.exp(-x))


def _gelu_tanh(x):
    return 0.5 * x * (1.0 + jnp.tanh(math.sqrt(2.0 / math.pi) * (x + 0.044715 * (x * x * x))))


def _rmsnorm_kernel(x_ref, g_ref, o_ref):
    x = x_ref[...]
    y = x * lax.rsqrt(jnp.mean(x * x, axis=-1, keepdims=True) + RMS_EPS)
    o_ref[...] = (y * g_ref[...]).astype(o_ref.dtype)


def rmsnorm(x, g, out_dtype):
    t, d = x.shape
    tm = 256
    return pl.pallas_call(
        _rmsnorm_kernel,
        out_shape=jax.ShapeDtypeStruct((t, d), out_dtype),
        grid=(t // tm,),
        in_specs=[pl.BlockSpec((tm, d), lambda i: (i, 0)), pl.BlockSpec((1, d), lambda i: (0, 0))],
        out_specs=pl.BlockSpec((tm, d), lambda i: (i, 0)),
        compiler_params=_cparams(("parallel",)),
        name="rmsnorm",
    )(x, g.reshape(1, d))


def _rmsnorm_router_kernel(x_ref, g_ref, wr_ref, o_ref, gate_ref, idx_ref):
    x = x_ref[...]
    y = x * lax.rsqrt(jnp.mean(x * x, axis=-1, keepdims=True) + RMS_EPS)
    y = y * g_ref[...]
    o_ref[...] = y.astype(o_ref.dtype).reshape(o_ref.shape)
    logits = _dot_split(y, wr_ref[...])
    lane = lax.broadcasted_iota(jnp.int32, logits.shape, 1)
    logits = jnp.where(lane < N_EXPERTS, logits, NEG_INF)
    m1 = jnp.max(logits, axis=-1, keepdims=True)
    i1 = jnp.min(jnp.where(logits == m1, lane, LANE), axis=-1, keepdims=True)
    rest = jnp.where(lane == i1, NEG_INF, logits)
    m2 = jnp.max(rest, axis=-1, keepdims=True)
    i2 = jnp.min(jnp.where(rest == m2, lane, LANE), axis=-1, keepdims=True)
    e2 = jnp.exp(m2 - m1)
    denom = 1.0 + e2
    gate_ref[...] = jnp.where(lane == 0, 1.0 / denom, jnp.where(lane == 1, e2 / denom, 0.0))
    idx_ref[...] = jnp.where(lane == 0, i1, jnp.where(lane == 1, i2, 0))


def rmsnorm_router(x, g, w_router):
    t, d = x.shape
    tm = 256
    wr = jnp.pad(w_router, ((0, 0), (0, LANE - w_router.shape[1])))
    return pl.pallas_call(
        _rmsnorm_router_kernel,
        out_shape=(jax.ShapeDtypeStruct((t, d // LANE, LANE), BF16), jax.ShapeDtypeStruct((t, LANE), F32),
                   jax.ShapeDtypeStruct((t, LANE), jnp.int32)),
        grid=(t // tm,),
        in_specs=[pl.BlockSpec((tm, d), lambda i: (i, 0)), pl.BlockSpec((1, d), lambda i: (0, 0)),
                  pl.BlockSpec((d, LANE), lambda i: (0, 0))],
        out_specs=(pl.BlockSpec((tm, d // LANE, LANE), lambda i: (i, 0, 0)), pl.BlockSpec((tm, LANE), lambda i: (i, 0)),
                   pl.BlockSpec((tm, LANE), lambda i: (i, 0))),
        compiler_params=_cparams(("parallel",)),
        name="rmsnorm_router",
    )(x, g.reshape(1, d), wr)


def _mm_epilogue(parts, res_ref, o_ref, act):
    r = parts[0]
    if act == "swiglu":
        r = (r * _sigmoid(r)) * parts[1]
    if res_ref is not None:
        r = res_ref[...] + r
    o_ref[...] = r.astype(o_ref.dtype)


def _panel_copy(w_ref, lead, col0, tn, transposed, stage_ref, sem_ref, widx):
    window = (pl.ds(col0, tn), slice(None)) if transposed else (slice(None), pl.ds(col0, tn))
    return pltpu.make_async_copy(w_ref.at[tuple(lead) + window], stage_ref, sem_ref.at[widx])


def _mm_resident_kernel(*refs, n_a, n_w, has_res, act, w_transposed, lead, tn):
    a_refs = refs[:n_a]
    w_refs = refs[n_a:n_a + n_w]
    res_ref = refs[n_a + n_w] if has_res else None
    o_ref = refs[n_a + n_w + int(has_res)]
    scratch = refs[n_a + n_w + int(has_res) + 1:]
    wbf_refs, stage_refs, sem = scratch[:n_w], scratch[n_w:2 * n_w], scratch[2 * n_w]
    j = pl.program_id(0)

    def panel(widx, jj):
        return _panel_copy(w_refs[widx], lead, jj * tn, tn, w_transposed, stage_refs[widx], sem, widx)

    @pl.when(pl.program_id(1) == 0)
    def _():
        @pl.when(j == 0)
        def _():
            for widx in range(n_w):
                panel(widx, 0).start()

        for widx in range(n_w):
            panel(widx, j).wait()
            wbf_refs[widx][...] = stage_refs[widx][...].astype(BF16)

        @pl.when(j + 1 < pl.num_programs(0))
        def _():
            for widx in range(n_w):
                panel(widx, j + 1).start()

    parts = []
    for wbf in wbf_refs:
        k0, part = 0, None
        for a_ref in a_refs:
            k1 = k0 + a_ref.shape[1]
            term = _dot_nt(a_ref[...], wbf[:, k0:k1]) if w_transposed else _dot(a_ref[...], wbf[k0:k1, :])
            part = term if part is None else part + term
            k0 = k1
        parts.append(part)
    _mm_epilogue(parts, res_ref, o_ref, act)


def _mm_ktiled_kernel(a_ref, w_ref, res_ref, o_ref, acc_ref, *, nk):
    k = pl.program_id(2)

    @pl.when(k == 0)
    def _():
        acc_ref[...] = jnp.zeros(acc_ref.shape, F32)

    acc_ref[...] += _dot(a_ref[...], w_ref[...].astype(BF16))

    @pl.when(k == nk - 1)
    def _():
        _mm_epilogue([acc_ref[...]], res_ref, o_ref, None)


MM_RESIDENT_MAX_K = 4096
MM_VMEM_BUDGET = 50 * 1024 * 1024


def _resident_tiles(m, kdim, n_cols, n_w, out_bytes, has_res):
    for tn in (1024, 512, 256, LANE):
        for tm in (1024, 512):
            if n_cols % tn or m % tm:
                continue
            a_buf = 2 * tm * kdim * 2
            w_buf = n_w * (kdim * tn * 4 + kdim * tn * 2)
            o_buf = 2 * tm * tn * out_bytes + (2 * tm * tn * 4 if has_res else 0)
            if a_buf + w_buf + o_buf <= MM_VMEM_BUDGET:
                return tm, tn
    raise ValueError("no resident-panel tiling fits VMEM")


def matmul(a, ws, lead, n_cols, *, out_dtype, res=None, act=None, w_transposed=False):
    a_parts = list(a) if isinstance(a, (list, tuple)) else [a]
    m = a_parts[0].shape[0]
    kdim = sum(part.shape[1] for part in a_parts)
    lead = tuple(lead)
    squeeze = (None,) * len(lead)
    if kdim <= MM_RESIDENT_MAX_K:
        tm, tn = _resident_tiles(m, kdim, n_cols, len(ws), jnp.dtype(out_dtype).itemsize, res is not None)
        in_specs = [pl.BlockSpec((tm, part.shape[1]), lambda j, i: (i, 0)) for part in a_parts]
        panel = (tn, kdim) if w_transposed else (kdim, tn)
        in_specs += [pl.BlockSpec(memory_space=pl.ANY) for _ in ws]
        args = [*a_parts, *ws]
        if res is not None:
            in_specs.append(pl.BlockSpec((tm, tn), lambda j, i: (i, j)))
            args.append(res)
        return pl.pallas_call(
            functools.partial(_mm_resident_kernel, n_a=len(a_parts), n_w=len(ws), has_res=res is not None, act=act,
                              w_transposed=w_transposed, lead=lead, tn=tn),
            out_shape=jax.ShapeDtypeStruct((m, n_cols), out_dtype),
            grid=(n_cols // tn, m // tm),
            in_specs=in_specs,
            out_specs=pl.BlockSpec((tm, tn), lambda j, i: (i, j)),
            scratch_shapes=([pltpu.VMEM(panel, BF16) for _ in ws] + [pltpu.VMEM(panel, F32) for _ in ws]
                            + [pltpu.SemaphoreType.DMA((len(ws),))]),
            compiler_params=_cparams(("arbitrary", "arbitrary")),
            name="matmul_resident",
        )(*args)
    assert len(ws) == 1 and act is None and res is not None and len(a_parts) == 1 and not w_transposed
    a = a_parts[0]
    tm, tn, tk = 2048, 512, 2048
    assert m % tm == 0 and n_cols % tn == 0 and kdim % tk == 0
    nk = kdim // tk
    return pl.pallas_call(
        functools.partial(_mm_ktiled_kernel, nk=nk),
        out_shape=jax.ShapeDtypeStruct((m, n_cols), out_dtype),
        grid=(n_cols // tn, m // tm, nk),
        in_specs=[pl.BlockSpec((tm, tk), lambda j, i, k: (i, k)),
                  pl.BlockSpec(squeeze + (tk, tn), lambda j, i, k: lead + (k, j)),
                  pl.BlockSpec((tm, tn), lambda j, i, k: (i, j))],
        out_specs=pl.BlockSpec((tm, tn), lambda j, i, k: (i, j)),
        scratch_shapes=[pltpu.VMEM((tm, tn), F32)],
        compiler_params=_cparams(("parallel", "parallel", "arbitrary")),
        name="matmul_ktiled",
    )(a, ws[0], res)


def _bucket_thresholds():
    d = np.arange(0, 4 * REL_MAX_DIST)
    max_exact = REL_BUCKETS // 2
    log_ratio = np.log(np.maximum(d, 1).astype(np.float64) / max_exact) / math.log(REL_MAX_DIST / max_exact)
    far = np.minimum(max_exact + (log_ratio * (REL_BUCKETS - max_exact)).astype(np.int32), REL_BUCKETS - 1)
    bucket = np.where(d < max_exact, d, far)
    assert np.all(np.diff(bucket) >= 0) and bucket[-1] == REL_BUCKETS - 1
    return [int(np.argmax(bucket >= b)) for b in range(REL_BUCKETS)]


_BUCKET_THR = _bucket_thresholds()
REL_FAR_DIST = _BUCKET_THR[REL_BUCKETS - 1]
assert REL_FAR_DIST <= SEL_FAR_BLOCKS * BLK - (BLK - 1)
assert REL_FAR_DIST <= NSA_CMP_STRIDE * (CMP_PAD + 1) - (NSA_CMP_LEN - 1)


def _bias_kernel(tab_ref, raw_ref, sel_ref, cmp_ref):
    h = pl.program_id(0)
    row = lax.broadcasted_iota(jnp.int32, (BLK, BLK), 0)
    col = lax.broadcasted_iota(jnp.int32, (BLK, BLK), 1)

    def lookup(dist):
        dd = jnp.maximum(dist, 0)
        b = jnp.full(dist.shape, tab_ref[0, h], F32)
        for bk in range(1, REL_BUCKETS):
            b = jnp.where(dd >= _BUCKET_THR[bk], tab_ref[bk, h], b)
        return b

    far = tab_ref[REL_BUCKETS - 1, h]
    for d in range(RAW_TILES):
        raw_ref[d] = lookup(d * BLK + row - col)
    for step in range(SEL_TILES):
        dist = (step - SEL_NEG_BLOCKS) * BLK + row - col
        sel_ref[step] = jnp.where(dist < 0, NEG_INF, lookup(dist) - far)
    dist_c = row - NSA_CMP_STRIDE * (col - CMP_PAD) - (NSA_CMP_LEN - 1)
    cmp_ref[...] = lookup(dist_c) - far


def bias_tiles(rel_table):
    tiles = lambda n: pl.BlockSpec((n, None, BLK, BLK), lambda h: (0, h, 0, 0))
    return pl.pallas_call(
        _bias_kernel,
        out_shape=(jax.ShapeDtypeStruct((RAW_TILES, REL_HEADS, BLK, BLK), F32),
                   jax.ShapeDtypeStruct((SEL_TILES, REL_HEADS, BLK, BLK), F32),
                   jax.ShapeDtypeStruct((REL_HEADS, BLK, BLK), F32)),
        grid=(REL_HEADS,),
        in_specs=[pl.BlockSpec(memory_space=pltpu.SMEM)],
        out_specs=(tiles(RAW_TILES), tiles(SEL_TILES), pl.BlockSpec((None, BLK, BLK), lambda h: (h, 0, 0))),
        compiler_params=_cparams(("arbitrary",)),
        name="bias_tiles",
    )(rel_table)


SB_BQ = 256
SB_GROUP = 4


def _sb_kernel(q_ref, k_ref, v_ref, o_ref, c_ref, acc_ref):
    qb = pl.program_id(1)
    row = lax.broadcasted_iota(jnp.int32, (SB_BQ, SB_BQ), 0)
    col = lax.broadcasted_iota(jnp.int32, (SB_BQ, SB_BQ), 1)
    upper = (row > col).astype(BF16)

    def block(h, kb, past, live=None):
        cols = slice(h * HEAD_DIM, (h + 1) * HEAD_DIM)
        ks = pl.multiple_of(kb * SB_BQ, SB_BQ)
        z = _dot_nt(q_ref[:, cols], k_ref[pl.ds(ks, SB_BQ), cols]) * SCALE
        sp = jnp.maximum(z, 0.0) + jnp.log(1.0 + jnp.exp(-jnp.abs(z)))
        log_stay = -sp
        if past is not None:
            log_stay = jnp.where(past, log_stay, 0.0)
        if live is not None:
            log_stay = jnp.where(live, log_stay, 0.0)
        later = _dot(log_stay.astype(BF16), upper)
        tot = jnp.broadcast_to(jnp.sum(log_stay, axis=-1, keepdims=True), (SB_BQ, HEAD_DIM))
        if past is not None:
            w = jnp.where(past, jnp.exp((z - sp) + later), 0.0)
            c_new = tot
        else:
            carry = c_ref[:, cols]
            w = jnp.exp((z - sp) + later + jnp.concatenate([carry] * (SB_BQ // HEAD_DIM), axis=1))
            if live is not None:
                w = jnp.where(live, w, 0.0)
            c_new = carry + tot
        contrib = _dot(w.astype(BF16), v_ref[pl.ds(ks, SB_BQ), cols])
        if past is not None:
            acc_ref[:, cols] = contrib
        else:
            acc_ref[:, cols] += contrib
        c_ref[:, cols] = c_new
        return jnp.max(c_new)

    def sweep(kb, past, live=None):
        return functools.reduce(jnp.maximum, [block(h, kb, past, live) for h in range(SB_GROUP)])

    def cond(state):
        kb, c_max = state
        return jnp.logical_and(kb >= 0, c_max > SB_SKIP_LOG)

    def body(state):
        kb, _ = state
        return kb - 1, sweep(kb, None)

    sweep(qb, col < row)
    c_max = sweep(jnp.maximum(qb - 1, 0), None, qb > 0)
    lax.while_loop(cond, body, (qb - 2, c_max))
    o_ref[...] = acc_ref[...].astype(o_ref.dtype)


def sb_attention(z, t):
    width = SB_GROUP * HEAD_DIM
    assert SB_HEADS % SB_GROUP == 0 and AB_K_SB % SB_GROUP == 0 and AB_V_SB % SB_GROUP == 0
    return pl.pallas_call(
        _sb_kernel,
        out_shape=jax.ShapeDtypeStruct((t, SB_HEADS * HEAD_DIM), BF16),
        grid=(SB_HEADS // SB_GROUP, t // SB_BQ),
        in_specs=[pl.BlockSpec((SB_BQ, width), lambda h, i: (i, AB_Q_SB // SB_GROUP + h)),
                  pl.BlockSpec((t, width), lambda h, i: (0, AB_K_SB // SB_GROUP + h)),
                  pl.BlockSpec((t, width), lambda h, i: (0, AB_V_SB // SB_GROUP + h))],
        out_specs=pl.BlockSpec((SB_BQ, width), lambda h, i: (i, h)),
        scratch_shapes=[pltpu.VMEM((SB_BQ, width), F32), pltpu.VMEM((SB_BQ, width), F32)],
        compiler_params=_cparams(("parallel", "arbitrary")),
        name="sb_attention",
    )(z, z, z)


def _compress_kernel(a_ref, pe_ref, w1_ref, b1_ref, w2_ref, o_ref):
    n_chunk = a_ref.shape[0]
    half = NSA_CMP_STRIDE * HEAD_DIM
    chunks = a_ref[...].reshape(n_chunk, half).astype(F32)
    first = _dot((chunks + pe_ref[:, :half]).astype(BF16), w1_ref[:half, :].astype(BF16))
    second = _dot((chunks + pe_ref[:, half:]).astype(BF16), w1_ref[half:, :].astype(BF16))
    hid = _gelu_tanh(first + pltpu.roll(second, n_chunk - 1, 0) + b1_ref[...])
    out = _dot(hid.astype(BF16), w2_ref[...].astype(BF16))
    o_ref[0:CMP_PAD, :] = jnp.zeros((CMP_PAD, HEAD_DIM), o_ref.dtype)
    o_ref[CMP_PAD:CMP_PAD + n_chunk, :] = out.astype(o_ref.dtype)
    o_ref[CMP_PAD + n_chunk:, :] = jnp.zeros((o_ref.shape[0] - CMP_PAD - n_chunk, HEAD_DIM), o_ref.dtype)


def compress(z, col_block, pe, w1, b1, w2, t):
    assert NSA_CMP_LEN == 2 * NSA_CMP_STRIDE
    n_chunk = t // NSA_CMP_STRIDE
    width = NSA_CMP_LEN * HEAD_DIM
    hidden = w1.shape[-1]
    rows = CMP_PAD + n_chunk + 8
    return pl.pallas_call(
        _compress_kernel,
        out_shape=jax.ShapeDtypeStruct((NSA_KV_HEADS, rows, HEAD_DIM), BF16),
        grid=(NSA_KV_HEADS,),
        in_specs=[pl.BlockSpec((n_chunk, NSA_CMP_STRIDE, HEAD_DIM), lambda h: (0, 0, col_block + h)),
                  pl.BlockSpec((1, width), lambda h: (0, 0)),
                  pl.BlockSpec((width, hidden), lambda h: (0, 0)),
                  pl.BlockSpec((1, hidden), lambda h: (0, 0)),
                  pl.BlockSpec((hidden, HEAD_DIM), lambda h: (0, 0))],
        out_specs=pl.BlockSpec((None, rows, HEAD_DIM), lambda h: (h, 0, 0)),
        compiler_params=_cparams(("parallel",)),
        name="compress",
    )(z.reshape(n_chunk, NSA_CMP_STRIDE, z.shape[1]), pe.reshape(1, width), w1, b1.reshape(1, hidden), w2)


def _cmp_to_sel_padded(t):
    n_cmp = (t - NSA_CMP_LEN) // NSA_CMP_STRIDE + 1
    n_sel = t // NSA_SEL_LEN
    cmp_end = np.arange(n_cmp) * NSA_CMP_STRIDE + NSA_CMP_LEN - 1
    cmp_start = cmp_end - (NSA_CMP_LEN - 1)
    sel_start = np.arange(n_sel) * NSA_SEL_LEN
    overlap = (np.minimum(cmp_end[:, None], sel_start[None, :] + NSA_SEL_LEN - 1)
               - np.maximum(cmp_start[:, None], sel_start[None, :]) + 1)
    c2s = np.clip(overlap, 0, None).astype(np.float32) / NSA_CMP_LEN
    out = np.zeros((CMP_PAD + n_cmp + 9, n_sel), np.float32)
    out[CMP_PAD:CMP_PAD + n_cmp] = c2s
    return out


def _cmp_win_kernel(q_ref, kc_ref, vc_ref, c2s_ref, bias_ref, kw_ref, vw_ref, bias_win_ref, o_ref, pen_ref, ow_ref,
                    qs_ref, *, n_far):
    qb = pl.program_id(1)
    _stack_scaled_queries(q_ref, qs_ref, NSA_G)
    r0 = pl.multiple_of(qb * (BLK // NSA_CMP_STRIDE), 8)
    kc_far = kc_ref[0:n_far, :]
    kc_near = kc_ref[pl.ds(r0, BLK), :]
    vs_far = jnp.concatenate([vc_ref[0:n_far, :], c2s_ref[0:n_far, :]], axis=1)
    vs_near = jnp.concatenate([vc_ref[pl.ds(r0, BLK), :], c2s_ref[pl.ds(r0, BLK), :]], axis=1)
    r_far = lax.broadcasted_iota(jnp.int32, (BLK, n_far), 1)
    far_ok = jnp.logical_and(r_far >= CMP_PAD, r_far < r0)
    row = lax.broadcasted_iota(jnp.int32, (BLK, BLK), 0)
    col = lax.broadcasted_iota(jnp.int32, (BLK, BLK), 1)
    dist_c = row - NSA_CMP_STRIDE * (col - CMP_PAD) - (NSA_CMP_LEN - 1)
    near_ok = jnp.logical_and(dist_c >= 0, col + r0 >= CMP_PAD)

    far_mask = jnp.where(far_ok, 0.0, NEG_INF)

    n_sel = c2s_ref.shape[1]
    imp = jnp.zeros((BLK, n_sel), F32)
    for g in range(NSA_G):
        q = qs_ref[g * BLK:(g + 1) * BLK, :]
        s_far = _dot_nt(q, kc_far) + far_mask
        s_near = _dot_nt(q, kc_near) + jnp.where(near_ok, bias_ref[g], NEG_INF)
        m = jnp.maximum(jnp.maximum(jnp.max(s_far, axis=-1, keepdims=True), jnp.max(s_near, axis=-1, keepdims=True)),
                        M_INIT)
        e_far = jnp.exp(s_far - m)
        e_near = jnp.exp(s_near - m)
        l = jnp.sum(e_far, axis=-1, keepdims=True) + jnp.sum(e_near, axis=-1, keepdims=True)
        inv = 1.0 / jnp.where(l > 0.0, l, 1.0)
        res = (_dot(e_far.astype(BF16), vs_far) + _dot(e_near.astype(BF16), vs_near)) * inv
        o_ref[:, g * HEAD_DIM:(g + 1) * HEAD_DIM] = res[:, :HEAD_DIM].astype(o_ref.dtype)
        imp = imp + res[:, HEAD_DIM:]

    blk = lax.broadcasted_iota(jnp.int32, (n_sel, BLK), 0)
    t_pos = qb * BLK + lax.broadcasted_iota(jnp.int32, (n_sel, BLK), 1)
    cur = t_pos // NSA_SEL_LEN
    forced = jnp.logical_or(blk == 0, jnp.logical_or(blk == cur, blk == cur - 1))
    score = jnp.where(forced, NSA_FORCE_SCORE, jnp.where(blk <= cur, imp.T, -1.0))
    pen = jnp.full((n_sel, BLK), NEG_INF, F32)
    for _ in range(min(NSA_SEL_TOPK, n_sel)):
        top = jnp.max(score, axis=0, keepdims=True)
        first = jnp.min(jnp.where(score == top, blk, n_sel), axis=0, keepdims=True)
        hit = blk == first
        pen = jnp.where(hit, 0.0, pen)
        score = jnp.where(hit, -2.0, score)
    pen_ref[...] = pen.T.astype(pen_ref.dtype)

    _banded_body(q_ref, kw_ref, vw_ref, bias_win_ref, ow_ref, qs_ref, None, qb, group=NSA_G,
                 n_prev=NSA_WINDOW // BLK, window=NSA_WINDOW, scale_q=False)


def cmp_select_window(z, kc, vc, bias_cmp, bias_raw, t):
    assert NSA_WINDOW % BLK == 0
    n_sel = t // NSA_SEL_LEN
    assert n_sel == LANE, "selection mask rides in one 128-lane tile"
    c2s = jnp.asarray(_cmp_to_sel_padded(t), BF16)
    rows = kc.shape[1]
    assert c2s.shape[0] == rows
    n_far = t // NSA_CMP_STRIDE
    qw = NSA_G * HEAD_DIM
    return pl.pallas_call(
        functools.partial(_cmp_win_kernel, n_far=n_far),
        out_shape=(jax.ShapeDtypeStruct((t, NSA_HEADS * HEAD_DIM), BF16),
                   jax.ShapeDtypeStruct((NSA_KV_HEADS, t, n_sel), BF16),
                   jax.ShapeDtypeStruct((t, NSA_HEADS * HEAD_DIM), BF16)),
        grid=(NSA_KV_HEADS, t // BLK),
        in_specs=[pl.BlockSpec((BLK, qw), lambda h, i: (i, AB_Q_N // NSA_G + h)),
                  pl.BlockSpec((None, rows, HEAD_DIM), lambda h, i: (h, 0, 0)),
                  pl.BlockSpec((None, rows, HEAD_DIM), lambda h, i: (h, 0, 0)),
                  pl.BlockSpec((rows, n_sel), lambda h, i: (0, 0)),
                  pl.BlockSpec((NSA_G, BLK, BLK), lambda h, i: (h, 0, 0)),
                  pl.BlockSpec((t, HEAD_DIM), lambda h, i: (0, AB_K_W + h)),
                  pl.BlockSpec((t, HEAD_DIM), lambda h, i: (0, AB_V_W + h)),
                  pl.BlockSpec((RAW_TILES, NSA_G, BLK, BLK), lambda h, i: (0, h, 0, 0))],
        out_specs=(pl.BlockSpec((BLK, qw), lambda h, i: (i, h)),
                   pl.BlockSpec((None, BLK, n_sel), lambda h, i: (h, i, 0)),
                   pl.BlockSpec((BLK, qw), lambda h, i: (i, h))),
        scratch_shapes=[pltpu.VMEM((NSA_G * BLK, HEAD_DIM), BF16)],
        compiler_params=_cparams(("parallel", "arbitrary")),
        name="cmp_select_window",
    )(z, kc, vc, c2s, bias_cmp, z, z, bias_raw)


SEL_CHUNK = SEL_CHUNK_BLOCKS * BLK


def _sel_kernel(q_ref, pen_ref, kaug_ref, vaug_ref, bias_ref, gate_ref, ocmp_ref, owin_ref, o_ref,
                qa_ref, s_ref, cmax_ref, m_ref, acc_ref):
    qb = pl.program_id(1)
    pen = pen_ref[...]
    for g in range(NSA_G):
        rows = slice(g * BLK, (g + 1) * BLK)
        qa_ref[rows, :HEAD_DIM] = (q_ref[:, g * HEAD_DIM:(g + 1) * HEAD_DIM].astype(F32) * SCALE).astype(BF16)
        qa_ref[rows, HEAD_DIM:] = pen
    m_ref[...] = jnp.full(m_ref.shape, M_INIT, F32)
    acc_ref[...] = jnp.zeros(acc_ref.shape, F32)
    n_chunks = qb // SEL_CHUNK_BLOCKS + 1
    last_chunk = kaug_ref.shape[0] // SEL_CHUNK - 1

    def chunk_start(c):
        return pl.multiple_of(jnp.minimum(c, last_chunk) * SEL_CHUNK, SEL_CHUNK)

    def scores(c, slot):
        s_all = _dot_nt(qa_ref[...], kaug_ref[pl.ds(chunk_start(c), SEL_CHUNK), :])
        first = qb - c * SEL_CHUNK_BLOCKS + SEL_NEG_BLOCKS
        n_rows = NSA_G * BLK
        tiles = [s_all[:, j * BLK:(j + 1) * BLK] + bias_ref[jnp.clip(first - j, 0, SEL_TILES - 1)].reshape(n_rows, BLK)
                 for j in range(SEL_CHUNK_BLOCKS)]
        for j, tile in enumerate(tiles):
            s_ref[slot, :, j * BLK:(j + 1) * BLK] = tile
        cmax_ref[slot] = jnp.broadcast_to(
            jnp.max(functools.reduce(jnp.maximum, tiles), axis=-1, keepdims=True), (n_rows, LANE))

    def absorb(c, slot):
        va = vaug_ref[pl.ds(chunk_start(c), SEL_CHUNK), :]
        m_old = m_ref[...]
        m_new = jnp.maximum(m_old, cmax_ref[slot])
        alpha = jnp.exp(m_old - m_new)
        p = jnp.concatenate([jnp.exp(s_ref[slot, :, j * BLK:(j + 1) * BLK] - m_new)
                             for j in range(SEL_CHUNK_BLOCKS)], axis=1).astype(BF16)
        acc_ref[...] = jnp.concatenate([alpha, alpha], axis=1) * acc_ref[...] + _dot(p, va)
        m_ref[...] = m_new

    def pair(k, carry):
        scores(2 * k + 1, 1)
        absorb(2 * k, 0)
        scores(2 * k + 2, 0)
        absorb(2 * k + 1, 1)
        return carry

    scores(0, 0)
    lax.fori_loop(0, n_chunks // 2, pair, 0)

    @pl.when(n_chunks % 2 == 1)
    def _():
        absorb(n_chunks - 1, 0)

    gates = _sigmoid(gate_ref[...])
    for g in range(NSA_G):
        rows = slice(g * BLK, (g + 1) * BLK)
        cols = slice(g * HEAD_DIM, (g + 1) * HEAD_DIM)
        o_sel = acc_ref[rows, :HEAD_DIM] / acc_ref[rows, HEAD_DIM:]
        mixed = (gates[:, 3 * g:3 * g + 1] * ocmp_ref[:, cols].astype(F32)
                 + gates[:, 3 * g + 1:3 * g + 2] * o_sel
                 + gates[:, 3 * g + 2:3 * g + 3] * owin_ref[:, cols].astype(F32))
        o_ref[:, cols] = mixed.astype(o_ref.dtype)


def sel_attention_mix(z, pen, bias_sel, gate_logits, o_cmp, o_win, t):
    n_sel = t // NSA_SEL_LEN
    n_gate = 3 * NSA_G
    gates_kv = gate_logits[:, :NSA_KV_HEADS * n_gate].reshape(t, NSA_KV_HEADS, n_gate).transpose(1, 0, 2)
    gates_kv = jnp.pad(gates_kv, ((0, 0), (0, 0), (0, LANE - n_gate)))
    assert t % SEL_CHUNK == 0
    onehot = jnp.asarray(np.arange(t)[:, None] // NSA_SEL_LEN == np.arange(n_sel)[None, :], BF16)
    kv_cols = lambda blk: z[:, blk * LANE:(blk + NSA_KV_HEADS) * LANE].reshape(t, NSA_KV_HEADS, HEAD_DIM)
    side = lambda a: jnp.broadcast_to(a[:, None, :], (t, NSA_KV_HEADS, a.shape[1]))
    kaug = jnp.concatenate([kv_cols(AB_K_S), side(onehot)], axis=2).transpose(1, 0, 2)
    vaug = jnp.concatenate([kv_cols(AB_V_S), side(jnp.ones((t, HEAD_DIM), BF16))], axis=2).transpose(1, 0, 2)
    qw = NSA_G * HEAD_DIM
    rows = NSA_G * BLK
    return pl.pallas_call(
        _sel_kernel,
        out_shape=jax.ShapeDtypeStruct((t, NSA_HEADS * HEAD_DIM), BF16),
        grid=(NSA_KV_HEADS, t // BLK),
        in_specs=[pl.BlockSpec((BLK, qw), lambda h, i: (i, AB_Q_N // NSA_G + h)),
                  pl.BlockSpec((None, BLK, n_sel), lambda h, i: (h, i, 0)),
                  pl.BlockSpec((None, t, HEAD_DIM + n_sel), lambda h, i: (h, 0, 0)),
                  pl.BlockSpec((None, t, 2 * HEAD_DIM), lambda h, i: (h, 0, 0)),
                  pl.BlockSpec((SEL_TILES, NSA_G, BLK, BLK), lambda h, i: (0, h, 0, 0)),
                  pl.BlockSpec((None, BLK, LANE), lambda h, i: (h, i, 0)),
                  pl.BlockSpec((BLK, qw), lambda h, i: (i, h)),
                  pl.BlockSpec((BLK, qw), lambda h, i: (i, h))],
        out_specs=pl.BlockSpec((BLK, qw), lambda h, i: (i, h)),
        scratch_shapes=[pltpu.VMEM((rows, HEAD_DIM + n_sel), BF16), pltpu.VMEM((2, rows, SEL_CHUNK), F32),
                        pltpu.VMEM((2, rows, LANE), F32), pltpu.VMEM((rows, LANE), F32),
                        pltpu.VMEM((rows, 2 * HEAD_DIM), F32)],
        compiler_params=_cparams(("parallel", "arbitrary")),
        name="sel_attention",
    )(z, pen, kaug, vaug, bias_sel, gates_kv, o_cmp, o_win)


def _banded_body(q_ref, k_ref, v_ref, bias_ref, o_ref, qs_ref, sink_of, qb, *, group, n_prev, window, scale_q):
    if scale_q:
        _stack_scaled_queries(q_ref, qs_ref, group)
    qs = qs_ref[...]
    row = lax.broadcasted_iota(jnp.int32, (BLK, BLK), 0)
    col = lax.broadcasted_iota(jnp.int32, (BLK, BLK), 1)
    ones = jnp.ones((BLK, HEAD_DIM), BF16)
    tiles, vaugs = [], []
    for delta in range(n_prev + 1):
        kb = qb - delta
        start = pl.multiple_of(jnp.maximum(kb, 0) * BLK, BLK)
        dist = delta * BLK + row - col
        limit = jnp.where(kb >= 0, window, 0)
        valid = jnp.logical_and(dist >= 0, dist < limit)
        bias = jnp.concatenate([jnp.where(valid, bias_ref[delta, g], NEG_INF) for g in range(group)], axis=0)
        tiles.append(_dot_nt(qs, k_ref[pl.ds(start, BLK), :]) + bias)
        vaugs.append(jnp.concatenate([v_ref[pl.ds(start, BLK), :], ones], axis=1))
    m = jnp.max(functools.reduce(jnp.maximum, tiles), axis=-1, keepdims=True)
    if sink_of is not None:
        sink = jnp.concatenate([jnp.full((BLK, 1), sink_of(g), F32) for g in range(group)], axis=0)
        m = jnp.maximum(m, sink)
    acc = functools.reduce(jnp.add, [_dot(jnp.exp(tile - m).astype(BF16), vaug) for tile, vaug in zip(tiles, vaugs)])
    denom = acc[:, HEAD_DIM:]
    if sink_of is not None:
        denom = denom + jnp.exp(sink - m)
    out = acc[:, :HEAD_DIM] / denom
    for g in range(group):
        o_ref[:, g * HEAD_DIM:(g + 1) * HEAD_DIM] = out[g * BLK:(g + 1) * BLK, :].astype(o_ref.dtype)


def _stack_scaled_queries(q_ref, qs_ref, group):
    for g in range(group):
        qs_ref[g * BLK:(g + 1) * BLK, :] = (q_ref[:, g * HEAD_DIM:(g + 1) * HEAD_DIM].astype(F32) * SCALE).astype(BF16)


SWA_G = SWA_HEADS // SWA_KV_HEADS


def _sg_swa_kernel(sink_ref, u_ref, v_ref, lng_ref, lnb_ref, w_ref, b_ref, q_ref, k_ref, vv_ref, bias_ref,
                   o_sg_ref, o_swa_ref, qs_ref):
    u = _gelu_tanh(u_ref[...].astype(F32))
    gv = _gelu_tanh(v_ref[...].astype(F32))
    mu = jnp.mean(gv, axis=-1, keepdims=True)
    cen = gv - mu
    var = jnp.mean(cen * cen, axis=-1, keepdims=True)
    gv = (cen * lax.rsqrt(var + RMS_EPS) * lng_ref[...] + lnb_ref[...]).astype(BF16)
    row = lax.broadcasted_iota(jnp.int32, (SG_CHUNK, SG_CHUNK), 0)
    col = lax.broadcasted_iota(jnp.int32, (SG_CHUNK, SG_CHUNK), 1)
    gw = u.shape[1] // SG_GROUPS
    for g in range(SG_GROUPS):
        sl = slice(g * gw, (g + 1) * gw)
        w = jnp.where(row >= col, w_ref[g], 0.0).astype(BF16)
        spatial = _dot(w, gv[:, sl]) + b_ref[:, g:g + 1]
        o_sg_ref[:, sl] = (u[:, sl] * spatial).astype(o_sg_ref.dtype)

    qw = SWA_G * HEAD_DIM
    for kv in range(SWA_KV_HEADS):
        heads = pl.ds(kv * qw, qw)
        _banded_body(q_ref.at[:, heads], k_ref.at[:, pl.ds(kv * HEAD_DIM, HEAD_DIM)],
                     vv_ref.at[:, pl.ds(kv * HEAD_DIM, HEAD_DIM)], bias_ref.at[:, pl.ds(kv * SWA_G, SWA_G)],
                     o_swa_ref.at[:, heads], qs_ref, lambda g, kv=kv: sink_ref[kv * SWA_G + g], pl.program_id(0),
                     group=SWA_G, n_prev=-(-SWA_WINDOW // BLK), window=SWA_WINDOW, scale_q=True)


def spatial_gating_swa(z1, ln_g, ln_b, sg_w, sg_b, bias_raw, sinks, t):
    width = ln_g.shape[0]
    assert width % (SG_GROUPS * LANE) == 0 and SG_CHUNK == BLK and CD_Q * LANE == 2 * width
    n_tiles = -(-SWA_WINDOW // BLK) + 1
    qw = SWA_HEADS * HEAD_DIM
    kvw = SWA_KV_HEADS * HEAD_DIM
    row = lambda w, blk: pl.BlockSpec((BLK, w), lambda i: (i, blk))
    full = lambda shape: pl.BlockSpec(shape, lambda i: (0,) * len(shape))
    return pl.pallas_call(
        _sg_swa_kernel,
        out_shape=(jax.ShapeDtypeStruct((t, width), BF16), jax.ShapeDtypeStruct((t, qw), BF16)),
        grid=(t // BLK,),
        in_specs=[pl.BlockSpec(memory_space=pltpu.SMEM),
                  row(width, 0), row(width, 1), full((1, width)), full((1, width)),
                  full((SG_GROUPS, SG_CHUNK, SG_CHUNK)), full((SG_CHUNK, SG_GROUPS)),
                  row(qw, CD_Q * LANE // qw),
                  pl.BlockSpec((t, kvw), lambda i: (0, CD_K * LANE // kvw)),
                  pl.BlockSpec((t, kvw), lambda i: (0, CD_V * LANE // kvw)),
                  full((n_tiles, REL_HEADS, BLK, BLK))],
        out_specs=(row(width, 0), row(qw, 0)),
        scratch_shapes=[pltpu.VMEM((SWA_G * BLK, HEAD_DIM), BF16)],
        compiler_params=_cparams(("arbitrary",)),
        name="spatial_gating_swa",
    )(sinks, z1, z1, ln_g.reshape(1, width), ln_b.reshape(1, width), sg_w, jnp.transpose(sg_b), z1, z1, z1, bias_raw)


MOE_TM = 512
MOE_TOP_K = 2


def moe_plan(idx, gate, t):
    n_rows = MOE_TOP_K * t + N_EXPERTS * MOE_TM
    n_blocks = n_rows // MOE_TM
    pair_e = idx[:, :MOE_TOP_K].reshape(-1)
    pair_w = gate[:, :MOE_TOP_K].reshape(-1)
    onehot = (pair_e[:, None] == jnp.arange(N_EXPERTS, dtype=jnp.int32)[None, :]).astype(jnp.int32)
    before = jnp.cumsum(onehot, axis=0) - onehot
    count = jnp.sum(onehot, axis=0)
    padded = (count + MOE_TM - 1) // MOE_TM * MOE_TM
    group_end = jnp.cumsum(padded)
    group_start = group_end - padded
    pair_row = jnp.sum(onehot * (before + group_start[None, :]), axis=1)
    row_pair = jnp.full((n_rows,), -1, jnp.int32).at[pair_row].set(jnp.arange(MOE_TOP_K * t, dtype=jnp.int32))
    row_tok = jnp.maximum(row_pair, 0) // MOE_TOP_K
    row_w = jnp.where(row_pair >= 0, pair_w[jnp.maximum(row_pair, 0)], 0.0)
    block_start = jnp.arange(n_blocks, dtype=jnp.int32) * MOE_TM
    block_e = jnp.minimum(jnp.sum(block_start[:, None] >= group_end[None, :], axis=1), N_EXPERTS - 1).astype(jnp.int32)
    block_rows = (group_start + count)[block_e] - block_start
    block_half = (block_rows <= MOE_TM // 2).astype(jnp.int32)
    n_used = (group_end[-1] // MOE_TM).astype(jnp.int32).reshape(1)
    nonempty = count > 0
    group_e = jnp.argsort(jnp.logical_not(nonempty), stable=True).astype(jnp.int32)
    block_group = (jnp.cumsum(nonempty.astype(jnp.int32)) - 1)[block_e].astype(jnp.int32)
    n_groups = jnp.sum(nonempty.astype(jnp.int32)).reshape(1)
    blocks = (block_e, block_half, n_used, block_group, group_e, n_groups)
    return row_tok, row_w.reshape(n_rows, 1), blocks, pair_row.astype(jnp.int32)


MOE_GATHER_ROWS = 256
DMA_BURST = 8


def _gather_rows_kernel(n_live_ref, tok_ref, tok_next_ref, src_ref, o_ref, buf, sem):
    b = pl.program_id(0)
    slot = b % 2
    n_live = n_live_ref[0]

    def row_copy(idx_ref, k, s):
        return pltpu.make_async_copy(src_ref.at[idx_ref[k]], buf.at[s, k], sem.at[s])

    def start_block(idx_ref, s):
        def start(k8, carry):
            for u in range(DMA_BURST):
                row_copy(idx_ref, k8 * DMA_BURST + u, s).start(priority=u % 2)
            return carry
        lax.fori_loop(0, MOE_GATHER_ROWS // DMA_BURST, start, 0)

    @pl.when(b == 0)
    def _():
        start_block(tok_ref, 0)

    @pl.when(b + 1 < n_live)
    def _():
        start_block(tok_next_ref, 1 - slot)

    @pl.when(b < n_live)
    def _():
        def wait(k, carry):
            row_copy(tok_ref, k, slot).wait()
            return carry

        lax.fori_loop(0, MOE_GATHER_ROWS, wait, 0, unroll=8)
        o_ref[...] = buf[slot].reshape(o_ref.shape)

    @pl.when(b >= n_live)
    def _():
        o_ref[...] = jnp.zeros(o_ref.shape, o_ref.dtype)


def gather_rows(src, row_tok, n_live_rows):
    n_rows = row_tok.shape[0]
    d = src.shape[1] * src.shape[2]
    n_steps = n_rows // MOE_GATHER_ROWS
    return pl.pallas_call(
        _gather_rows_kernel,
        out_shape=jax.ShapeDtypeStruct((n_rows, d), src.dtype),
        grid=(n_steps,),
        in_specs=[pl.BlockSpec(memory_space=pltpu.SMEM),
                  pl.BlockSpec((MOE_GATHER_ROWS,), lambda b: (b,), memory_space=pltpu.SMEM),
                  pl.BlockSpec((MOE_GATHER_ROWS,), lambda b: (jnp.minimum(b + 1, n_steps - 1),),
                               memory_space=pltpu.SMEM),
                  pl.BlockSpec(memory_space=pl.ANY)],
        out_specs=pl.BlockSpec((MOE_GATHER_ROWS, d), lambda b: (b, 0)),
        scratch_shapes=[pltpu.VMEM((2, MOE_GATHER_ROWS) + src.shape[1:], src.dtype), pltpu.SemaphoreType.DMA((2,))],
        compiler_params=_cparams(("arbitrary",)),
        name="moe_gather",
    )(n_live_rows // MOE_GATHER_ROWS, row_tok, row_tok, src)


def _moe_rows(b, half_ref, n_used_ref, compute, o_ref):
    live = b < n_used_ref[0]
    half = half_ref[jnp.minimum(b, n_used_ref[0] - 1)] == 1
    lower, upper = slice(0, MOE_TM // 2), slice(MOE_TM // 2, MOE_TM)

    @pl.when(jnp.logical_and(live, jnp.logical_not(half)))
    def _():
        compute(slice(0, MOE_TM))

    @pl.when(jnp.logical_and(live, half))
    def _():
        compute(lower)
        o_ref[upper, :] = jnp.zeros((MOE_TM // 2, o_ref.shape[1]), o_ref.dtype)

    @pl.when(jnp.logical_not(live))
    def _():
        o_ref[...] = jnp.zeros(o_ref.shape, o_ref.dtype)


def _moe_panels(tables, w_refs, wbf_refs, stage_refs, sem, tn):
    block_e_ref, _, n_used_ref, block_group_ref, group_e_ref, n_groups_ref = tables
    j, b = pl.program_id(0), pl.program_id(1)
    n_groups = n_groups_ref[0]
    group = block_group_ref[b]
    first_block = jnp.logical_or(b == 0, group != block_group_ref[jnp.maximum(b - 1, 0)])

    def panel(widx, expert, jj):
        return _panel_copy(w_refs[widx], (0, expert), jj * tn, tn, False, stage_refs[widx], sem, widx)

    @pl.when(jnp.logical_and(b < n_used_ref[0], first_block))
    def _():
        number = j * n_groups + group

        @pl.when(number == 0)
        def _():
            for widx in range(len(w_refs)):
                panel(widx, block_e_ref[b], 0).start()

        for widx in range(len(w_refs)):
            panel(widx, block_e_ref[b], j).wait()
            wbf_refs[widx][...] = stage_refs[widx][...].astype(BF16)

        @pl.when(number + 1 < pl.num_programs(0) * n_groups)
        def _():
            wraps = group + 1 == n_groups
            next_e = group_e_ref[jnp.where(wraps, 0, group + 1)]
            next_j = jnp.where(wraps, j + 1, j)
            for widx in range(len(w_refs)):
                panel(widx, next_e, next_j).start()


def _moe_up_kernel(*refs, tn):
    tables, (x_ref, wg_ref, wu_ref, o_ref, wg_bf, wu_bf, wg_st, wu_st, sem) = refs[:6], refs[6:]
    _moe_panels(tables, (wg_ref, wu_ref), (wg_bf, wu_bf), (wg_st, wu_st), sem, tn)

    def compute(rows):
        x = x_ref[rows, :]
        gate = _dot(x, wg_bf[...])
        o_ref[rows, :] = ((gate * _sigmoid(gate)) * _dot(x, wu_bf[...])).astype(o_ref.dtype)

    _moe_rows(pl.program_id(1), tables[1], tables[2], compute, o_ref)


def _moe_down_kernel(*refs, tn):
    tables, (h_ref, wd_ref, w_ref, o_ref, wd_bf, wd_st, sem) = refs[:6], refs[6:]
    _moe_panels(tables, (wd_ref,), (wd_bf,), (wd_st,), sem, tn)

    def compute(rows):
        o_ref[rows, :] = w_ref[rows, :] * _dot(h_ref[rows, :], wd_bf[...])

    _moe_rows(pl.program_id(1), tables[1], tables[2], compute, o_ref)


def moe_experts(xs, row_w, blocks, w_gate, w_up, w_down):
    n_rows, d = xs.shape
    d_ff = w_gate.shape[3]
    n_blocks = n_rows // MOE_TM
    tn_up, tn_down = 512, 1024
    row_block = lambda j, b, be, bh, nu, bg, ge, ng: (jnp.minimum(b, nu[0] - 1), 0)
    out_block = lambda j, b, be, bh, nu, bg, ge, ng: (b, j)
    in_hbm = pl.BlockSpec(memory_space=pl.ANY)
    hid = pl.pallas_call(
        functools.partial(_moe_up_kernel, tn=tn_up),
        out_shape=jax.ShapeDtypeStruct((n_rows, d_ff), BF16),
        grid_spec=pltpu.PrefetchScalarGridSpec(
            num_scalar_prefetch=len(blocks),
            grid=(d_ff // tn_up, n_blocks),
            in_specs=[pl.BlockSpec((MOE_TM, d), row_block), in_hbm, in_hbm],
            out_specs=pl.BlockSpec((MOE_TM, tn_up), out_block),
            scratch_shapes=[pltpu.VMEM((d, tn_up), BF16), pltpu.VMEM((d, tn_up), BF16),
                            pltpu.VMEM((d, tn_up), F32), pltpu.VMEM((d, tn_up), F32),
                            pltpu.SemaphoreType.DMA((2,))]),
        compiler_params=_cparams(("arbitrary", "arbitrary")),
        name="moe_up",
    )(*blocks, xs, w_gate, w_up)
    return pl.pallas_call(
        functools.partial(_moe_down_kernel, tn=tn_down),
        out_shape=jax.ShapeDtypeStruct((n_rows, d), F32),
        grid_spec=pltpu.PrefetchScalarGridSpec(
            num_scalar_prefetch=len(blocks),
            grid=(d // tn_down, n_blocks),
            in_specs=[pl.BlockSpec((MOE_TM, d_ff), row_block), in_hbm, pl.BlockSpec((MOE_TM, 1), row_block)],
            out_specs=pl.BlockSpec((MOE_TM, tn_down), out_block),
            scratch_shapes=[pltpu.VMEM((d_ff, tn_down), BF16), pltpu.VMEM((d_ff, tn_down), F32),
                            pltpu.SemaphoreType.DMA((1,))]),
        compiler_params=_cparams(("arbitrary", "arbitrary")),
        name="moe_down",
    )(*blocks, hid, w_down, row_w)


MOE_TC = 128


def _moe_combine_kernel(rows_ref, rows_next_ref, x_ref, g_ref, y_ref, o_ref, buf, sem):
    b = pl.program_id(0)
    slot = b % 2

    def row_copy(idx_ref, k, choice, s):
        row = idx_ref[MOE_TOP_K * k + choice]
        return pltpu.make_async_copy(y_ref.at[pl.ds(row, 1), :], buf.at[s, choice, pl.ds(k, 1), :], sem.at[s])

    def start_block(idx_ref, s):
        def start(k4, carry):
            for u in range(DMA_BURST // MOE_TOP_K):
                for choice in range(MOE_TOP_K):
                    row_copy(idx_ref, k4 * (DMA_BURST // MOE_TOP_K) + u, choice, s).start(priority=choice % 2)
            return carry
        lax.fori_loop(0, MOE_TC * MOE_TOP_K // DMA_BURST, start, 0)

    @pl.when(b == 0)
    def _():
        start_block(rows_ref, 0)

    @pl.when(b + 1 < pl.num_programs(0))
    def _():
        start_block(rows_next_ref, 1 - slot)

    def wait(k, carry):
        for choice in range(MOE_TOP_K):
            row_copy(rows_ref, k, choice, slot).wait()
        return carry

    lax.fori_loop(0, MOE_TC, wait, 0, unroll=4)
    x = x_ref[...]
    for choice in range(MOE_TOP_K):
        x = x + buf[slot, choice]
    y = x * lax.rsqrt(jnp.mean(x * x, axis=-1, keepdims=True) + RMS_EPS)
    o_ref[...] = (y * g_ref[...]).astype(o_ref.dtype)


def moe_combine_rmsnorm(x, ys, pair_row, g):
    t, d = x.shape
    n_steps = t // MOE_TC
    idx_block = lambda index_map: pl.BlockSpec((MOE_TOP_K * MOE_TC,), index_map, memory_space=pltpu.SMEM)
    return pl.pallas_call(
        _moe_combine_kernel,
        out_shape=jax.ShapeDtypeStruct((t, d), F32),
        grid=(n_steps,),
        in_specs=[idx_block(lambda i: (i,)),
                  idx_block(lambda i: (jnp.minimum(i + 1, n_steps - 1),)),
                  pl.BlockSpec((MOE_TC, d), lambda i: (i, 0)),
                  pl.BlockSpec((1, d), lambda i: (0, 0)),
                  pl.BlockSpec(memory_space=pl.ANY)],
        out_specs=pl.BlockSpec((MOE_TC, d), lambda i: (i, 0)),
        scratch_shapes=[pltpu.VMEM((2, MOE_TOP_K, MOE_TC, d), F32), pltpu.SemaphoreType.DMA((2,))],
        compiler_params=_cparams(("arbitrary",)),
        name="moe_combine",
    )(pair_row, pair_row, x, g.reshape(1, d), ys)


def kernel(x, rel_table, norm_mix, norm_ffn, norm_final, ab_w_in, ab_w_out, cmp_k_pe, cmp_k_w1, cmp_k_b1, cmp_k_w2, cmp_v_pe, cmp_v_w1, cmp_v_b1, cmp_v_w2, ffn_w_gate, ffn_w_up, ffn_w_down, cd_w_in, cd_w_out, sg_ln_g, sg_ln_b, sg_w, sg_b, swa_sinks, moe_w_router, moe_w_gate, moe_w_up, moe_w_down):
    b, t, d = x.shape
    assert b == 1
    xs = x.reshape(t, d)
    bias_raw, bias_sel, bias_cmp = bias_tiles(rel_table)

    h = rmsnorm(xs, norm_mix[0], BF16)
    ab_main = AB_MAIN_BLOCKS * LANE
    w_in_t = jnp.swapaxes(ab_w_in, 1, 2)
    z = matmul(h, [w_in_t], (0,), ab_main, out_dtype=BF16, w_transposed=True)
    w_gate_rows = jnp.pad(w_in_t[0, ab_main:], ((0, LANE - (w_in_t.shape[1] - ab_main)), (0, 0)))
    gate_logits = matmul(h, [w_gate_rows], (), LANE, out_dtype=F32, w_transposed=True)
    o_sb = sb_attention(z, t)
    kc = compress(z, AB_K_C, cmp_k_pe[0], cmp_k_w1[0], cmp_k_b1[0], cmp_k_w2[0], t)
    vc = compress(z, AB_V_C, cmp_v_pe[0], cmp_v_w1[0], cmp_v_b1[0], cmp_v_w2[0], t)
    o_cmp, pen, o_win = cmp_select_window(z, kc, vc, bias_cmp, bias_raw, t)
    o_nsa = sel_attention_mix(z, pen, bias_sel, gate_logits, o_cmp, o_win, t)
    xs = matmul([o_sb, o_nsa], [ab_w_out], (0,), d, out_dtype=F32, res=xs)
    h = rmsnorm(xs, norm_ffn[0], BF16)
    hid = matmul(h, [ffn_w_gate, ffn_w_up], (0,), ffn_w_gate.shape[2], out_dtype=BF16, act="swiglu")
    xs = matmul(hid, [ffn_w_down], (0,), d, out_dtype=F32, res=xs)

    h = rmsnorm(xs, norm_mix[1], BF16)
    z1 = matmul(h, [cd_w_in], (0,), cd_w_in.shape[2], out_dtype=BF16)
    o_sg, o_swa = spatial_gating_swa(z1, sg_ln_g[0], sg_ln_b[0], sg_w[0], sg_b[0], bias_raw, swa_sinks[0], t)
    xs = matmul([o_sg, o_swa], [cd_w_out], (0,), d, out_dtype=F32, res=xs)
    h, gate, idx = rmsnorm_router(xs, norm_ffn[1], moe_w_router[0])
    row_tok, row_w, blocks, pair_row = moe_plan(idx, gate, t)
    rows = gather_rows(h, row_tok, blocks[2] * MOE_TM)
    ys = moe_experts(rows, row_w, blocks, moe_w_gate, moe_w_up, moe_w_down)
    return moe_combine_rmsnorm(xs, ys, pair_row, norm_final).reshape(b, t, d)
```
